```python
import jax, jax.numpy as jnp
from jax import lax
import numpy as np

D_MODEL = 1024
BATCH = 8
SEQ = 8192
DEPTH = 4

HEAD_DIM = 64
SB_WIDTH = D_MODEL // 2
N_SB_HEADS = SB_WIDTH // HEAD_DIM
SG_WIDTH = D_MODEL // 2
SG_GROUP_DIM = 64
SG_GROUPS = SG_WIDTH // SG_GROUP_DIM
MIX_WIDTH = SB_WIDTH + SG_WIDTH
IN_WIDTH = 3 * SB_WIDTH + 2 * SG_WIDTH
CHUNK = 128
Q_BLOCK = 128
CONV_K = 31
CONV_WIDTH = D_MODEL
D_FF = ((8 * D_MODEL // 3 + 127) // 128) * 128
FFN_K = 3
N_EVEN = (DEPTH + 1) // 2
N_ODD = DEPTH // 2
EPS = 1e-6

kernel_name = "stickbreak_sgu_conformer_convglu_hybrid"


def rms_norm(x, g):
    xf = x.astype(jnp.float32)
    y = xf * lax.rsqrt(jnp.mean(xf * xf, axis=-1, keepdims=True) + EPS)
    return (y * g.astype(jnp.float32)).astype(x.dtype)


def layer_norm(x, g, b):
    xf = x.astype(jnp.float32)
    mu = jnp.mean(xf, axis=-1, keepdims=True)
    xc = xf - mu
    y = xc * lax.rsqrt(jnp.mean(xc * xc, axis=-1, keepdims=True) + EPS)
    return (y * g.astype(jnp.float32) + b.astype(jnp.float32)).astype(x.dtype)


def causal_depthwise_conv(x, w, b):
    k, c = w.shape
    y = lax.conv_general_dilated(
        x, w[:, None, :].astype(x.dtype), window_strides=(1,),
        padding=[(k - 1, 0)], dimension_numbers=("NWC", "WIO", "NWC"),
        feature_group_count=c)
    return y + b.astype(x.dtype)


def stick_breaking_attention(q, k, v):
    b_, s_, h_, dh = q.shape
    n_blk = s_ // Q_BLOCK
    scale = dh ** -0.5
    qb = q.reshape(b_, n_blk, Q_BLOCK, h_, dh).transpose(1, 0, 3, 2, 4)
    kt = k.transpose(0, 2, 1, 3).astype(jnp.float32)
    vt = v.transpose(0, 2, 1, 3).astype(jnp.float32)
    key_pos = jnp.arange(s_)

    def one_block(args):
        q_blk, blk = args
        z = jnp.einsum("bhqd,bhkd->bhqk", q_blk.astype(jnp.float32), kt) * scale
        q_pos = blk * Q_BLOCK + jnp.arange(Q_BLOCK)
        causal = key_pos[None, :] < q_pos[:, None]
        log_keep = jnp.where(causal, jax.nn.log_sigmoid(-z), 0.0)
        rev = lax.cumsum(log_keep, axis=3, reverse=True)
        tail = jnp.concatenate([rev[..., 1:], jnp.zeros_like(rev[..., :1])], axis=-1)
        w = jnp.where(causal, jnp.exp(jax.nn.log_sigmoid(z) + tail), 0.0)
        return jnp.einsum("bhqk,bhkd->bhqd", w, vt)

    out = lax.map(one_block, (qb, jnp.arange(n_blk)))
    return out.transpose(1, 0, 3, 2, 4).reshape(b_, s_, h_ * dh).astype(q.dtype)


def spatial_gating(u, z, g_z, w_s, b_s):
    b_, s_, _ = u.shape
    u = jax.nn.gelu(u, approximate=False)
    z = jax.nn.gelu(z, approximate=False)
    zg = z.reshape(b_, s_, SG_GROUPS, SG_GROUP_DIM)
    zf = zg.astype(jnp.float32)
    zg = (zf * lax.rsqrt(jnp.mean(zf * zf, axis=-1, keepdims=True) + EPS)
          * g_z.reshape(SG_GROUPS, SG_GROUP_DIM).astype(jnp.float32)).astype(z.dtype)
    zc = zg.reshape(b_, s_ // CHUNK, CHUNK, SG_GROUPS, SG_GROUP_DIM)
    mask = jnp.tril(jnp.ones((CHUNK, CHUNK), dtype=bool))
    wm = jnp.where(mask[None], w_s, 0.0).astype(z.dtype)
    s = jnp.einsum("gts,bcsgd->bctgd", wm, zc) + b_s.T[:, :, None].astype(z.dtype)
    return u * s.reshape(b_, s_, SG_WIDTH)


def attn_sgu_mixer(h, w_in, q_g, k_g, z_g, w_s, b_s, w_out):
    b_, s_, _ = h.shape
    proj = h @ w_in
    q, k, v, u, z = jnp.split(
        proj, [SB_WIDTH, 2 * SB_WIDTH, 3 * SB_WIDTH, 3 * SB_WIDTH + SG_WIDTH], axis=-1)
    q = rms_norm(q.reshape(b_, s_, N_SB_HEADS, HEAD_DIM), q_g)
    k = rms_norm(k.reshape(b_, s_, N_SB_HEADS, HEAD_DIM), k_g)
    v = v.reshape(b_, s_, N_SB_HEADS, HEAD_DIM)
    a = stick_breaking_attention(q, k, v)
    g = spatial_gating(u, z, z_g, w_s, b_s)
    return jnp.concatenate([a, g], axis=-1) @ w_out


def conformer_conv(h, w1, b1, w_dw, b_dw, ln_g, ln_b, w2, b2):
    a, gate = jnp.split(h @ w1 + b1, 2, axis=-1)
    y = a * jax.nn.sigmoid(gate)
    y = causal_depthwise_conv(y, w_dw, b_dw)
    y = jax.nn.silu(layer_norm(y, ln_g, ln_b))
    return y @ w2 + b2


def conv_glu_ffn(h, w_up, w_dw, b_dw, w_down):
    gate, val = jnp.split(h @ w_up, 2, axis=-1)
    gate = causal_depthwise_conv(gate, w_dw, b_dw)
    return (jax.nn.silu(gate) * val) @ w_down


def _fwd_setup_inputs(seed: int = 0) -> dict:
    key = jax.random.key(seed)
    ks = jax.random.split(key, 24)
    f32 = jnp.float32

    def nrm(k, shape, scale):
        return jax.random.normal(k, shape, f32) * scale

    def gain(k, shape):
        return 1.0 + 0.02 * jax.random.normal(k, shape, f32)

    return {
        "x": jax.random.normal(ks[0], (BATCH, SEQ, D_MODEL), f32),
        "mix_norm_g": gain(ks[1], (DEPTH, D_MODEL)),
        "sb_w_in": nrm(ks[2], (N_EVEN, D_MODEL, IN_WIDTH), D_MODEL ** -0.5),
        "sb_q_norm_g": gain(ks[3], (N_EVEN, HEAD_DIM)),
        "sb_k_norm_g": gain(ks[4], (N_EVEN, HEAD_DIM)),
        "sg_z_norm_g": gain(ks[5], (N_EVEN, SG_WIDTH)),
        "sg_w_spatial": nrm(ks[6], (N_EVEN, SG_GROUPS, CHUNK, CHUNK), CHUNK ** -0.5),
        "sg_b_spatial": gain(ks[7], (N_EVEN, SG_GROUPS, CHUNK)),
        "hyb_w_out": nrm(ks[8], (N_EVEN, MIX_WIDTH, D_MODEL), MIX_WIDTH ** -0.5),
        "cv_w_pw1": nrm(ks[9], (N_ODD, D_MODEL, 2 * CONV_WIDTH), D_MODEL ** -0.5),
        "cv_b_pw1": nrm(ks[10], (N_ODD, 2 * CONV_WIDTH), 0.02),
        "cv_w_dw": nrm(ks[11], (N_ODD, CONV_K, CONV_WIDTH), CONV_K ** -0.5),
        "cv_b_dw": nrm(ks[12], (N_ODD, CONV_WIDTH), 0.02),
        "cv_ln_g": gain(ks[13], (N_ODD, CONV_WIDTH)),
        "cv_ln_b": nrm(ks[14], (N_ODD, CONV_WIDTH), 0.02),
        "cv_w_pw2": nrm(ks[15], (N_ODD, CONV_WIDTH, D_MODEL), CONV_WIDTH ** -0.5),
        "cv_b_pw2": nrm(ks[16], (N_ODD, D_MODEL), 0.02),
        "ffn_norm_g": gain(ks[17], (DEPTH, D_MODEL)),
        "ffn_w_up": nrm(ks[18], (DEPTH, D_MODEL, 2 * D_FF), D_MODEL ** -0.5),
        "ffn_w_dw": nrm(ks[19], (DEPTH, FFN_K, D_FF), FFN_K ** -0.5),
        "ffn_b_dw": nrm(ks[20], (DEPTH, D_FF), 0.02),
        "ffn_w_down": nrm(ks[21], (DEPTH, D_FF, D_MODEL), D_FF ** -0.5),
    }


def _fwd_reference(x, mix_norm_g, sb_w_in, sb_q_norm_g, sb_k_norm_g, sg_z_norm_g,
              sg_w_spatial, sg_b_spatial, hyb_w_out, cv_w_pw1, cv_b_pw1, cv_w_dw,
              cv_b_dw, cv_ln_g, cv_ln_b, cv_w_pw2, cv_b_pw2, ffn_norm_g, ffn_w_up,
              ffn_w_dw, ffn_b_dw, ffn_w_down):
    for i in range(DEPTH):
        h = rms_norm(x, mix_norm_g[i])
        j = i // 2
        if i % 2 == 0:
            x = x + attn_sgu_mixer(h, sb_w_in[j], sb_q_norm_g[j], sb_k_norm_g[j],
                                   sg_z_norm_g[j], sg_w_spatial[j], sg_b_spatial[j],
                                   hyb_w_out[j])
        else:
            x = x + conformer_conv(h, cv_w_pw1[j], cv_b_pw1[j], cv_w_dw[j], cv_b_dw[j],
                                   cv_ln_g[j], cv_ln_b[j], cv_w_pw2[j], cv_b_pw2[j])
        h = rms_norm(x, ffn_norm_g[i])
        x = x + conv_glu_ffn(h, ffn_w_up[i], ffn_w_dw[i], ffn_b_dw[i], ffn_w_down[i])
    return x


import jax as _jax
import jax.numpy as _jnp

TWIN_FORMAT = 'train_step'
FWD_PARAMS = ['x', 'mix_norm_g', 'sb_w_in', 'sb_q_norm_g', 'sb_k_norm_g', 'sg_z_norm_g', 'sg_w_spatial', 'sg_b_spatial', 'hyb_w_out', 'cv_w_pw1', 'cv_b_pw1', 'cv_w_dw', 'cv_b_dw', 'cv_ln_g', 'cv_ln_b', 'cv_w_pw2', 'cv_b_pw2', 'ffn_norm_g', 'ffn_w_up', 'ffn_w_dw', 'ffn_b_dw', 'ffn_w_down']
TWIN_WEIGHTS = ['mix_norm_g', 'sb_w_in', 'sb_q_norm_g', 'sb_k_norm_g', 'sg_z_norm_g', 'sg_w_spatial', 'sg_b_spatial', 'hyb_w_out', 'cv_w_pw1', 'cv_b_pw1', 'cv_w_dw', 'cv_b_dw', 'cv_ln_g', 'cv_ln_b', 'cv_w_pw2', 'cv_b_pw2', 'ffn_norm_g', 'ffn_w_up', 'ffn_w_dw', 'ffn_b_dw', 'ffn_w_down']
TWIN_DIFF_INPUT = 'x'
TWIN_INPUTS = ['x', 'mix_norm_g', 'sb_w_in', 'sb_q_norm_g', 'sb_k_norm_g', 'sg_z_norm_g', 'sg_w_spatial', 'sg_b_spatial', 'hyb_w_out', 'cv_w_pw1', 'cv_b_pw1', 'cv_w_dw', 'cv_b_dw', 'cv_ln_g', 'cv_ln_b', 'cv_w_pw2', 'cv_b_pw2', 'ffn_norm_g', 'ffn_w_up', 'ffn_w_dw', 'ffn_b_dw', 'ffn_w_down', 'loss_target', 'm_mix_norm_g', 'm_sb_w_in', 'm_sb_q_norm_g', 'm_sb_k_norm_g', 'm_sg_z_norm_g', 'm_sg_w_spatial', 'm_sg_b_spatial', 'm_hyb_w_out', 'm_cv_w_pw1', 'm_cv_b_pw1', 'm_cv_w_dw', 'm_cv_b_dw', 'm_cv_ln_g', 'm_cv_ln_b', 'm_cv_w_pw2', 'm_cv_b_pw2', 'm_ffn_norm_g', 'm_ffn_w_up', 'm_ffn_w_dw', 'm_ffn_b_dw', 'm_ffn_w_down', 'v_mix_norm_g', 'v_sb_w_in', 'v_sb_q_norm_g', 'v_sb_k_norm_g', 'v_sg_z_norm_g', 'v_sg_w_spatial', 'v_sg_b_spatial', 'v_hyb_w_out', 'v_cv_w_pw1', 'v_cv_b_pw1', 'v_cv_w_dw', 'v_cv_b_dw', 'v_cv_ln_g', 'v_cv_ln_b', 'v_cv_w_pw2', 'v_cv_b_pw2', 'v_ffn_norm_g', 'v_ffn_w_up', 'v_ffn_w_dw', 'v_ffn_b_dw', 'v_ffn_w_down']
TWIN_OUTPUTS = ['loss', 'grad_x', 'grad_mix_norm_g', 'grad_sb_w_in', 'grad_sb_q_norm_g', 'grad_sb_k_norm_g', 'grad_sg_z_norm_g', 'grad_sg_w_spatial', 'grad_sg_b_spatial', 'grad_hyb_w_out', 'grad_cv_w_pw1', 'grad_cv_b_pw1', 'grad_cv_w_dw', 'grad_cv_b_dw', 'grad_cv_ln_g', 'grad_cv_ln_b', 'grad_cv_w_pw2', 'grad_cv_b_pw2', 'grad_ffn_norm_g', 'grad_ffn_w_up', 'grad_ffn_w_dw', 'grad_ffn_b_dw', 'grad_ffn_w_down', 'delta_mix_norm_g', 'delta_sb_w_in', 'delta_sb_q_norm_g', 'delta_sb_k_norm_g', 'delta_sg_z_norm_g', 'delta_sg_w_spatial', 'delta_sg_b_spatial', 'delta_hyb_w_out', 'delta_cv_w_pw1', 'delta_cv_b_pw1', 'delta_cv_w_dw', 'delta_cv_b_dw', 'delta_cv_ln_g', 'delta_cv_ln_b', 'delta_cv_w_pw2', 'delta_cv_b_pw2', 'delta_ffn_norm_g', 'delta_ffn_w_up', 'delta_ffn_w_dw', 'delta_ffn_b_dw', 'delta_ffn_w_down', 'new_m_mix_norm_g', 'new_m_sb_w_in', 'new_m_sb_q_norm_g', 'new_m_sb_k_norm_g', 'new_m_sg_z_norm_g', 'new_m_sg_w_spatial', 'new_m_sg_b_spatial', 'new_m_hyb_w_out', 'new_m_cv_w_pw1', 'new_m_cv_b_pw1', 'new_m_cv_w_dw', 'new_m_cv_b_dw', 'new_m_cv_ln_g', 'new_m_cv_ln_b', 'new_m_cv_w_pw2', 'new_m_cv_b_pw2', 'new_m_ffn_norm_g', 'new_m_ffn_w_up', 'new_m_ffn_w_dw', 'new_m_ffn_b_dw', 'new_m_ffn_w_down', 'new_v_mix_norm_g', 'new_v_sb_w_in', 'new_v_sb_q_norm_g', 'new_v_sb_k_norm_g', 'new_v_sg_z_norm_g', 'new_v_sg_w_spatial', 'new_v_sg_b_spatial', 'new_v_hyb_w_out', 'new_v_cv_w_pw1', 'new_v_cv_b_pw1', 'new_v_cv_w_dw', 'new_v_cv_b_dw', 'new_v_cv_ln_g', 'new_v_cv_ln_b', 'new_v_cv_w_pw2', 'new_v_cv_b_pw2', 'new_v_ffn_norm_g', 'new_v_ffn_w_up', 'new_v_ffn_w_dw', 'new_v_ffn_b_dw', 'new_v_ffn_w_down']
TWIN_LEAF_KINDS = {'loss': 'loss', 'grad_x': 'grad_x', 'grad_mix_norm_g': 'grad_w', 'grad_sb_w_in': 'grad_w', 'grad_sb_q_norm_g': 'grad_w', 'grad_sb_k_norm_g': 'grad_w', 'grad_sg_z_norm_g': 'grad_w', 'grad_sg_w_spatial': 'grad_w', 'grad_sg_b_spatial': 'grad_w', 'grad_hyb_w_out': 'grad_w', 'grad_cv_w_pw1': 'grad_w', 'grad_cv_b_pw1': 'grad_w', 'grad_cv_w_dw': 'grad_w', 'grad_cv_b_dw': 'grad_w', 'grad_cv_ln_g': 'grad_w', 'grad_cv_ln_b': 'grad_w', 'grad_cv_w_pw2': 'grad_w', 'grad_cv_b_pw2': 'grad_w', 'grad_ffn_norm_g': 'grad_w', 'grad_ffn_w_up': 'grad_w', 'grad_ffn_w_dw': 'grad_w', 'grad_ffn_b_dw': 'grad_w', 'grad_ffn_w_down': 'grad_w', 'delta_mix_norm_g': 'delta_w', 'delta_sb_w_in': 'delta_w', 'delta_sb_q_norm_g': 'delta_w', 'delta_sb_k_norm_g': 'delta_w', 'delta_sg_z_norm_g': 'delta_w', 'delta_sg_w_spatial': 'delta_w', 'delta_sg_b_spatial': 'delta_w', 'delta_hyb_w_out': 'delta_w', 'delta_cv_w_pw1': 'delta_w', 'delta_cv_b_pw1': 'delta_w', 'delta_cv_w_dw': 'delta_w', 'delta_cv_b_dw': 'delta_w', 'delta_cv_ln_g': 'delta_w', 'delta_cv_ln_b': 'delta_w', 'delta_cv_w_pw2': 'delta_w', 'delta_cv_b_pw2': 'delta_w', 'delta_ffn_norm_g': 'delta_w', 'delta_ffn_w_up': 'delta_w', 'delta_ffn_w_dw': 'delta_w', 'delta_ffn_b_dw': 'delta_w', 'delta_ffn_w_down': 'delta_w', 'new_m_mix_norm_g': 'new_m', 'new_m_sb_w_in': 'new_m', 'new_m_sb_q_norm_g': 'new_m', 'new_m_sb_k_norm_g': 'new_m', 'new_m_sg_z_norm_g': 'new_m', 'new_m_sg_w_spatial': 'new_m', 'new_m_sg_b_spatial': 'new_m', 'new_m_hyb_w_out': 'new_m', 'new_m_cv_w_pw1': 'new_m', 'new_m_cv_b_pw1': 'new_m', 'new_m_cv_w_dw': 'new_m', 'new_m_cv_b_dw': 'new_m', 'new_m_cv_ln_g': 'new_m', 'new_m_cv_ln_b': 'new_m', 'new_m_cv_w_pw2': 'new_m', 'new_m_cv_b_pw2': 'new_m', 'new_m_ffn_norm_g': 'new_m', 'new_m_ffn_w_up': 'new_m', 'new_m_ffn_w_dw': 'new_m', 'new_m_ffn_b_dw': 'new_m', 'new_m_ffn_w_down': 'new_m', 'new_v_mix_norm_g': 'new_v', 'new_v_sb_w_in': 'new_v', 'new_v_sb_q_norm_g': 'new_v', 'new_v_sb_k_norm_g': 'new_v', 'new_v_sg_z_norm_g': 'new_v', 'new_v_sg_w_spatial': 'new_v', 'new_v_sg_b_spatial': 'new_v', 'new_v_hyb_w_out': 'new_v', 'new_v_cv_w_pw1': 'new_v', 'new_v_cv_b_pw1': 'new_v', 'new_v_cv_w_dw': 'new_v', 'new_v_cv_b_dw': 'new_v', 'new_v_cv_ln_g': 'new_v', 'new_v_cv_ln_b': 'new_v', 'new_v_cv_w_pw2': 'new_v', 'new_v_cv_b_pw2': 'new_v', 'new_v_ffn_norm_g': 'new_v', 'new_v_ffn_w_up': 'new_v', 'new_v_ffn_w_dw': 'new_v', 'new_v_ffn_b_dw': 'new_v', 'new_v_ffn_w_down': 'new_v'}


def _forward(args):
    return _fwd_reference(*[args[k] for k in FWD_PARAMS])


def _output_shape():
    out = _jax.eval_shape(lambda: _forward(_fwd_setup_inputs(0)))
    return out.shape, out.dtype

N_MICROBATCH = 1
ADAM_LR = 0.001
ADAM_B1 = 0.9
ADAM_B2 = 0.999
ADAM_EPS = 1e-08
ADAM_WD = 0.01
ADAM_STEP = 10
PER_EXAMPLE_BATCH_AXIS = {'x': 0, 'loss_target': 0}
SHARED_INPUTS = []
_WEIGHT_DTYPES = {'mix_norm_g': _jnp.float32, 'sb_w_in': _jnp.float32, 'sb_q_norm_g': _jnp.float32, 'sb_k_norm_g': _jnp.float32, 'sg_z_norm_g': _jnp.float32, 'sg_w_spatial': _jnp.float32, 'sg_b_spatial': _jnp.float32, 'hyb_w_out': _jnp.float32, 'cv_w_pw1': _jnp.float32, 'cv_b_pw1': _jnp.float32, 'cv_w_dw': _jnp.float32, 'cv_b_dw': _jnp.float32, 'cv_ln_g': _jnp.float32, 'cv_ln_b': _jnp.float32, 'cv_w_pw2': _jnp.float32, 'cv_b_pw2': _jnp.float32, 'ffn_norm_g': _jnp.float32, 'ffn_w_up': _jnp.float32, 'ffn_w_dw': _jnp.float32, 'ffn_b_dw': _jnp.float32, 'ffn_w_down': _jnp.float32}
MOMENT_SCALE = {'mix_norm_g': 2.568375e+01, 'sb_w_in': 2.074057e+00, 'sb_q_norm_g': 3.115255e+01, 'sb_k_norm_g': 3.120912e+01, 'sg_z_norm_g': 1.257128e+01, 'sg_w_spatial': 3.466409e+00, 'sg_b_spatial': 1.322406e+01, 'hyb_w_out': 7.661691e+00, 'cv_w_pw1': 1.598005e+00, 'cv_b_pw1': 1.422273e+01, 'cv_w_dw': 3.276601e+00, 'cv_b_dw': 3.216764e+01, 'cv_ln_g': 3.401972e+01, 'cv_ln_b': 2.705248e+01, 'cv_w_pw2': 7.946049e+00, 'cv_b_pw2': 3.843625e+01, 'ffn_norm_g': 5.064223e+01, 'ffn_w_up': 1.245106e+00, 'ffn_w_dw': 5.871367e+00, 'ffn_b_dw': 7.241511e+00, 'ffn_w_down': 1.359510e+00}


def _to_microbatches(a, axis):
    t = _jnp.moveaxis(a, axis, 0)
    t = t.reshape((N_MICROBATCH, t.shape[0] // N_MICROBATCH) + t.shape[1:])
    return _jnp.moveaxis(t, 1, axis + 1)


def setup_inputs(seed: int = 0) -> dict:
    inp = _fwd_setup_inputs(seed)
    key = _jax.random.fold_in(_jax.random.key(seed), 7919)
    shape, _ = _output_shape()
    out = dict(inp)
    out["loss_target"] = _jax.random.normal(_jax.random.fold_in(key, 0), shape, _jnp.float32)
    for i, name in enumerate(TWIN_WEIGHTS):
        w = inp[name].astype(_jnp.float32)
        if MOMENT_SCALE is None:
            s = _jnp.sqrt(_jnp.mean(_jnp.square(w)) + 1e-30)
        else:
            s = MOMENT_SCALE[name]
        km, kv = _jax.random.split(_jax.random.fold_in(key, i + 1))
        out[name] = w
        out["m_" + name] = s * _jax.random.normal(km, w.shape, _jnp.float32)
        out["v_" + name] = (s * s) * _jax.random.uniform(kv, w.shape, _jnp.float32, 0.5, 1.5)
    if N_MICROBATCH > 1:
        for name, axis in PER_EXAMPLE_BATCH_AXIS.items():
            out[name] = _to_microbatches(out[name], axis)
    return {'x': out['x'], 'mix_norm_g': out['mix_norm_g'], 'sb_w_in': out['sb_w_in'], 'sb_q_norm_g': out['sb_q_norm_g'], 'sb_k_norm_g': out['sb_k_norm_g'], 'sg_z_norm_g': out['sg_z_norm_g'], 'sg_w_spatial': out['sg_w_spatial'], 'sg_b_spatial': out['sg_b_spatial'], 'hyb_w_out': out['hyb_w_out'], 'cv_w_pw1': out['cv_w_pw1'], 'cv_b_pw1': out['cv_b_pw1'], 'cv_w_dw': out['cv_w_dw'], 'cv_b_dw': out['cv_b_dw'], 'cv_ln_g': out['cv_ln_g'], 'cv_ln_b': out['cv_ln_b'], 'cv_w_pw2': out['cv_w_pw2'], 'cv_b_pw2': out['cv_b_pw2'], 'ffn_norm_g': out['ffn_norm_g'], 'ffn_w_up': out['ffn_w_up'], 'ffn_w_dw': out['ffn_w_dw'], 'ffn_b_dw': out['ffn_b_dw'], 'ffn_w_down': out['ffn_w_down'], 'loss_target': out['loss_target'], 'm_mix_norm_g': out['m_mix_norm_g'], 'm_sb_w_in': out['m_sb_w_in'], 'm_sb_q_norm_g': out['m_sb_q_norm_g'], 'm_sb_k_norm_g': out['m_sb_k_norm_g'], 'm_sg_z_norm_g': out['m_sg_z_norm_g'], 'm_sg_w_spatial': out['m_sg_w_spatial'], 'm_sg_b_spatial': out['m_sg_b_spatial'], 'm_hyb_w_out': out['m_hyb_w_out'], 'm_cv_w_pw1': out['m_cv_w_pw1'], 'm_cv_b_pw1': out['m_cv_b_pw1'], 'm_cv_w_dw': out['m_cv_w_dw'], 'm_cv_b_dw': out['m_cv_b_dw'], 'm_cv_ln_g': out['m_cv_ln_g'], 'm_cv_ln_b': out['m_cv_ln_b'], 'm_cv_w_pw2': out['m_cv_w_pw2'], 'm_cv_b_pw2': out['m_cv_b_pw2'], 'm_ffn_norm_g': out['m_ffn_norm_g'], 'm_ffn_w_up': out['m_ffn_w_up'], 'm_ffn_w_dw': out['m_ffn_w_dw'], 'm_ffn_b_dw': out['m_ffn_b_dw'], 'm_ffn_w_down': out['m_ffn_w_down'], 'v_mix_norm_g': out['v_mix_norm_g'], 'v_sb_w_in': out['v_sb_w_in'], 'v_sb_q_norm_g': out['v_sb_q_norm_g'], 'v_sb_k_norm_g': out['v_sb_k_norm_g'], 'v_sg_z_norm_g': out['v_sg_z_norm_g'], 'v_sg_w_spatial': out['v_sg_w_spatial'], 'v_sg_b_spatial': out['v_sg_b_spatial'], 'v_hyb_w_out': out['v_hyb_w_out'], 'v_cv_w_pw1': out['v_cv_w_pw1'], 'v_cv_b_pw1': out['v_cv_b_pw1'], 'v_cv_w_dw': out['v_cv_w_dw'], 'v_cv_b_dw': out['v_cv_b_dw'], 'v_cv_ln_g': out['v_cv_ln_g'], 'v_cv_ln_b': out['v_cv_ln_b'], 'v_cv_w_pw2': out['v_cv_w_pw2'], 'v_cv_b_pw2': out['v_cv_b_pw2'], 'v_ffn_norm_g': out['v_ffn_norm_g'], 'v_ffn_w_up': out['v_ffn_w_up'], 'v_ffn_w_dw': out['v_ffn_w_dw'], 'v_ffn_b_dw': out['v_ffn_b_dw'], 'v_ffn_w_down': out['v_ffn_w_down']}


def _loss(weights, diff, rest, loss_target):
    with _jax.named_scope("forward"):
        args = {**rest, TWIN_DIFF_INPUT: diff, **{k: w.astype(_WEIGHT_DTYPES[k]) for k, w in weights.items()}}
        y = _forward(args)
    with _jax.named_scope("loss_head"):
        err = _jnp.square(y.astype(_jnp.float32) - loss_target)
        return 0.5 * _jnp.sum(_jnp.mean(err, axis=-1)) if err.ndim else 0.5 * err


def _adamw(w, g, m, v):
    m = ADAM_B1 * m + (1.0 - ADAM_B1) * g
    v = ADAM_B2 * v + (1.0 - ADAM_B2) * _jnp.square(g)
    m_hat = m / (1.0 - ADAM_B1 ** ADAM_STEP)
    v_hat = v / (1.0 - ADAM_B2 ** ADAM_STEP)
    delta = -ADAM_LR * (m_hat / (_jnp.sqrt(v_hat) + ADAM_EPS) + ADAM_WD * w)
    return delta, m, v


def reference(x, mix_norm_g, sb_w_in, sb_q_norm_g, sb_k_norm_g, sg_z_norm_g, sg_w_spatial, sg_b_spatial, hyb_w_out, cv_w_pw1, cv_b_pw1, cv_w_dw, cv_b_dw, cv_ln_g, cv_ln_b, cv_w_pw2, cv_b_pw2, ffn_norm_g, ffn_w_up, ffn_w_dw, ffn_b_dw, ffn_w_down, loss_target, m_mix_norm_g, m_sb_w_in, m_sb_q_norm_g, m_sb_k_norm_g, m_sg_z_norm_g, m_sg_w_spatial, m_sg_b_spatial, m_hyb_w_out, m_cv_w_pw1, m_cv_b_pw1, m_cv_w_dw, m_cv_b_dw, m_cv_ln_g, m_cv_ln_b, m_cv_w_pw2, m_cv_b_pw2, m_ffn_norm_g, m_ffn_w_up, m_ffn_w_dw, m_ffn_b_dw, m_ffn_w_down, v_mix_norm_g, v_sb_w_in, v_sb_q_norm_g, v_sb_k_norm_g, v_sg_z_norm_g, v_sg_w_spatial, v_sg_b_spatial, v_hyb_w_out, v_cv_w_pw1, v_cv_b_pw1, v_cv_w_dw, v_cv_b_dw, v_cv_ln_g, v_cv_ln_b, v_cv_w_pw2, v_cv_b_pw2, v_ffn_norm_g, v_ffn_w_up, v_ffn_w_dw, v_ffn_b_dw, v_ffn_w_down):
    given = dict(x=x, mix_norm_g=mix_norm_g, sb_w_in=sb_w_in, sb_q_norm_g=sb_q_norm_g, sb_k_norm_g=sb_k_norm_g, sg_z_norm_g=sg_z_norm_g, sg_w_spatial=sg_w_spatial, sg_b_spatial=sg_b_spatial, hyb_w_out=hyb_w_out, cv_w_pw1=cv_w_pw1, cv_b_pw1=cv_b_pw1, cv_w_dw=cv_w_dw, cv_b_dw=cv_b_dw, cv_ln_g=cv_ln_g, cv_ln_b=cv_ln_b, cv_w_pw2=cv_w_pw2, cv_b_pw2=cv_b_pw2, ffn_norm_g=ffn_norm_g, ffn_w_up=ffn_w_up, ffn_w_dw=ffn_w_dw, ffn_b_dw=ffn_b_dw, ffn_w_down=ffn_w_down, loss_target=loss_target, m_mix_norm_g=m_mix_norm_g, m_sb_w_in=m_sb_w_in, m_sb_q_norm_g=m_sb_q_norm_g, m_sb_k_norm_g=m_sb_k_norm_g, m_sg_z_norm_g=m_sg_z_norm_g, m_sg_w_spatial=m_sg_w_spatial, m_sg_b_spatial=m_sg_b_spatial, m_hyb_w_out=m_hyb_w_out, m_cv_w_pw1=m_cv_w_pw1, m_cv_b_pw1=m_cv_b_pw1, m_cv_w_dw=m_cv_w_dw, m_cv_b_dw=m_cv_b_dw, m_cv_ln_g=m_cv_ln_g, m_cv_ln_b=m_cv_ln_b, m_cv_w_pw2=m_cv_w_pw2, m_cv_b_pw2=m_cv_b_pw2, m_ffn_norm_g=m_ffn_norm_g, m_ffn_w_up=m_ffn_w_up, m_ffn_w_dw=m_ffn_w_dw, m_ffn_b_dw=m_ffn_b_dw, m_ffn_w_down=m_ffn_w_down, v_mix_norm_g=v_mix_norm_g, v_sb_w_in=v_sb_w_in, v_sb_q_norm_g=v_sb_q_norm_g, v_sb_k_norm_g=v_sb_k_norm_g, v_sg_z_norm_g=v_sg_z_norm_g, v_sg_w_spatial=v_sg_w_spatial, v_sg_b_spatial=v_sg_b_spatial, v_hyb_w_out=v_hyb_w_out, v_cv_w_pw1=v_cv_w_pw1, v_cv_b_pw1=v_cv_b_pw1, v_cv_w_dw=v_cv_w_dw, v_cv_b_dw=v_cv_b_dw, v_cv_ln_g=v_cv_ln_g, v_cv_ln_b=v_cv_ln_b, v_cv_w_pw2=v_cv_w_pw2, v_cv_b_pw2=v_cv_b_pw2, v_ffn_norm_g=v_ffn_norm_g, v_ffn_w_up=v_ffn_w_up, v_ffn_w_dw=v_ffn_w_dw, v_ffn_b_dw=v_ffn_b_dw, v_ffn_w_down=v_ffn_w_down)
    weights = {n: given[n] for n in TWIN_WEIGHTS}
    shared = {n: given[n] for n in SHARED_INPUTS}
    per_example = {n: given[n] for n in ['x']}
    grad_fn = _jax.value_and_grad(_loss, argnums=(0, 1))

    def one_microbatch(ex, loss_target):
        ex = dict(ex)
        diff = ex.pop(TWIN_DIFF_INPUT)
        return grad_fn(weights, diff, {**shared, **ex}, loss_target)

    if N_MICROBATCH == 1:
        loss, (grad_w, grad_x) = one_microbatch(per_example, given["loss_target"])
    else:
        def body(carry, xs):
            loss_sum, grad_sum = carry
            l_k, (gw_k, gx_k) = one_microbatch(xs[0], xs[1])
            with _jax.named_scope("update"):
                return (loss_sum + l_k, _jax.tree.map(_jnp.add, grad_sum, gw_k)), gx_k

        init = (_jnp.zeros((), _jnp.float32), _jax.tree.map(_jnp.zeros_like, weights))
        (loss, grad_w), grad_x = _jax.lax.scan(body, init, (per_example, given["loss_target"]))
    with _jax.named_scope("update"):
        delta_w, new_m, new_v = {}, {}, {}
        for n in TWIN_WEIGHTS:
            delta_w[n], new_m[n], new_v[n] = _adamw(weights[n], grad_w[n], given["m_" + n], given["v_" + n])
    return (loss, grad_x, *[grad_w[n] for n in TWIN_WEIGHTS], *[delta_w[n] for n in TWIN_WEIGHTS],
            *[new_m[n] for n in TWIN_WEIGHTS], *[new_v[n] for n in TWIN_WEIGHTS])
```

```python
import functools

import jax
import jax.numpy as jnp
from jax import lax
from jax.experimental import pallas as pl
from jax.experimental.pallas import tpu as pltpu

F32 = jnp.float32
BF16 = jnp.bfloat16

D_MODEL = 1024
HEAD_DIM = 64
SB_WIDTH = 512
SG_WIDTH = 512
SG_GROUPS = 8
IN_WIDTH = 3 * SB_WIDTH + 2 * SG_WIDTH
CHUNK = 128
CONV_K = 31
D_FF = 2816
FFN_K = 3
DEPTH = 4
EPS = 1e-6
N_DEV = 8
LANE = 128
HALO = 32
ATT_BLOCK = 256
FF_CHUNK = 256
MIB = 2 ** 20

ADAM_LR = 0.001
ADAM_B1 = 0.9
ADAM_B2 = 0.999
ADAM_EPS = 1e-08
ADAM_WD = 0.01
ADAM_STEP = 10

NT_DIMS = (((1,), (1,)), ((), ()))
TN_DIMS = (((0,), (0,)), ((), ()))


def _params(semantics, vmem_mib):
    return pltpu.CompilerParams(dimension_semantics=semantics, vmem_limit_bytes=vmem_mib * MIB)


def _full(shape):
    nd = len(shape)
    return pl.BlockSpec(shape, lambda *_: (0,) * nd)


def _sigmoid(x):
    return 1.0 / (1.0 + jnp.exp(-x))


def _gelu(x):
    return 0.5 * x * (1.0 + lax.erf(x * 0.7071067811865476))


def _gelu_grad(x):
    return 0.5 * (1.0 + lax.erf(x * 0.7071067811865476)) + x * jnp.exp(-0.5 * x * x) * 0.3989422804014327


def _silu_grad(x, s):
    return s * (1.0 + x * (1.0 - s))


def _dot(a, b):
    return jnp.dot(a, b, preferred_element_type=F32)


def _dot2(a, b):
    hi = a.astype(BF16)
    lo = (a - hi.astype(F32)).astype(BF16)
    return _dot(hi, b) + _dot(lo, b)


def _group_mean(t, bd):
    return _dot2(t, bd) * (1.0 / HEAD_DIM)


def _shift_down(v, prev8, s):
    top = pltpu.roll(jnp.concatenate([prev8, v[:8]], axis=0), s, 0)[8:16]
    return jnp.concatenate([top, pltpu.roll(v, s, 0)[8:]], axis=0)


def _shift_up(v, next8, s):
    n = v.shape[0]
    bottom = pltpu.roll(jnp.concatenate([v[n - 8:], next8], axis=0), 16 - s, 0)[0:8]
    return jnp.concatenate([pltpu.roll(v, n - s, 0)[: n - 8], bottom], axis=0)


def _rms_matmul(x, g, w, b, name):
    s, d = x.shape
    n = w.shape[1]
    tm = min(512, s)

    def body(x_ref, g_ref, w_ref, b_ref, y_ref, h_ref):
        xv = x_ref[...]
        r = lax.rsqrt(jnp.mean(xv * xv, axis=-1, keepdims=True) + EPS)
        h = (xv * r * g_ref[...]).astype(BF16)
        h_ref[...] = h
        for c in range(0, n, 512):
            y_ref[:, c:c + 512] = _dot(h, w_ref[:, c:c + 512]) + b_ref[:, c:c + 512]

    return pl.pallas_call(
        body, name=name, grid=(s // tm,),
        in_specs=[pl.BlockSpec((tm, d), lambda i: (i, 0)), _full((1, d)), _full((d, n)), _full((1, n))],
        out_specs=[pl.BlockSpec((tm, n), lambda i: (i, 0)), pl.BlockSpec((tm, d), lambda i: (i, 0))],
        out_shape=[jax.ShapeDtypeStruct((s, n), F32), jax.ShapeDtypeStruct((s, d), BF16)],
        compiler_params=_params(("arbitrary",), 48),
    )(x, g, w, b)


def _matmul_nt(a, w, name):
    s, n = a.shape
    k = w.shape[0]
    tm = min(512, s)

    def body(a_ref, w_ref, o_ref):
        o_ref[...] = lax.dot_general(a_ref[...].astype(BF16), w_ref[...], NT_DIMS, preferred_element_type=F32)

    return pl.pallas_call(
        body, name=name, grid=(s // tm,),
        in_specs=[pl.BlockSpec((tm, n), lambda i: (i, 0)), _full((k, n))],
        out_specs=pl.BlockSpec((tm, k), lambda i: (i, 0)),
        out_shape=jax.ShapeDtypeStruct((s, k), F32),
        compiler_params=_params(("arbitrary",), 40),
    )(a, w)


def _matmul_tn(a, b, name):
    s, k = a.shape
    n = b.shape[1]
    ts = min(512, s)
    tn = 512
    steps = s // ts

    def body(a_ref, b_ref, o_ref):
        @pl.when(pl.program_id(1) == 0)
        def _():
            o_ref[...] = jnp.zeros_like(o_ref)

        o_ref[...] += lax.dot_general(a_ref[...].astype(BF16), b_ref[...].astype(BF16), TN_DIMS,
                                      preferred_element_type=F32)

    return pl.pallas_call(
        body, name=name, grid=(n // tn, steps),
        in_specs=[pl.BlockSpec((ts, k), lambda j, t: (t, 0)), pl.BlockSpec((ts, tn), lambda j, t: (t, j))],
        out_specs=pl.BlockSpec((k, tn), lambda j, t: (0, j)),
        out_shape=jax.ShapeDtypeStruct((k, n), F32),
        compiler_params=_params(("arbitrary", "arbitrary"), 48),
    )(a, b)


def _nt_rms_bwd(dp, w, x, g, dres, name):
    s, n = dp.shape
    d = x.shape[1]
    tm = min(256, s)

    def body(dp_ref, w_ref, x_ref, g_ref, dres_ref, dx_ref, dg_ref):
        @pl.when(pl.program_id(0) == 0)
        def _():
            dg_ref[...] = jnp.zeros_like(dg_ref)

        dh = lax.dot_general(dp_ref[...], w_ref[...], NT_DIMS, preferred_element_type=F32)
        xv = x_ref[...]
        r = lax.rsqrt(jnp.mean(xv * xv, axis=-1, keepdims=True) + EPS)
        xh = xv * r
        dg_ref[...] += jnp.sum(dh * xh, axis=0, keepdims=True)
        dn = dh * g_ref[...]
        dx_ref[...] = dres_ref[...] + r * (dn - xh * jnp.mean(dn * xh, axis=-1, keepdims=True))

    return pl.pallas_call(
        body, name=name, grid=(s // tm,),
        in_specs=[pl.BlockSpec((tm, n), lambda i: (i, 0)), _full((d, n)), pl.BlockSpec((tm, d), lambda i: (i, 0)),
                  _full((1, d)), pl.BlockSpec((tm, d), lambda i: (i, 0))],
        out_specs=[pl.BlockSpec((tm, d), lambda i: (i, 0)), _full((1, d))],
        out_shape=[jax.ShapeDtypeStruct((s, d), F32), jax.ShapeDtypeStruct((1, d), F32)],
        compiler_params=_params(("arbitrary",), 52),
    )(dp, w, x, g, dres)


def _res_matmul(x, a, w, b, name):
    s, d = x.shape
    k = a.shape[1]
    tm = min(512, s)

    def body(x_ref, a_ref, w_ref, b_ref, o_ref):
        o_ref[...] = x_ref[...] + _dot(a_ref[...], w_ref[...]) + b_ref[...]

    return pl.pallas_call(
        body, name=name, grid=(s // tm,),
        in_specs=[pl.BlockSpec((tm, d), lambda i: (i, 0)), pl.BlockSpec((tm, k), lambda i: (i, 0)), _full((k, d)),
                  _full((1, d))],
        out_specs=pl.BlockSpec((tm, d), lambda i: (i, 0)),
        out_shape=jax.ShapeDtypeStruct((s, d), F32),
        compiler_params=_params(("arbitrary",), 32),
    )(x, a, w, b)


def _out_proj(x, o, gg, w, name):
    s, d = x.shape
    tm = min(512, s)

    def body(x_ref, o_ref, gg_ref, w_ref, y_ref, mix_ref):
        mix = jnp.concatenate([o_ref[...], gg_ref[...]], axis=1).astype(BF16)
        mix_ref[...] = mix
        y_ref[...] = x_ref[...] + _dot(mix, w_ref[...])

    return pl.pallas_call(
        body, name=name, grid=(s // tm,),
        in_specs=[pl.BlockSpec((tm, d), lambda i: (i, 0)), pl.BlockSpec((tm, SB_WIDTH), lambda i: (i, 0)),
                  pl.BlockSpec((tm, SG_WIDTH), lambda i: (i, 0)), _full((d, d))],
        out_specs=[pl.BlockSpec((tm, d), lambda i: (i, 0)), pl.BlockSpec((tm, d), lambda i: (i, 0))],
        out_shape=[jax.ShapeDtypeStruct((s, d), F32), jax.ShapeDtypeStruct((s, d), BF16)],
        compiler_params=_params(("arbitrary",), 32),
    )(x, o, gg, w)


def _qk_prep(proj, gq, gk, bd):
    s = proj.shape[0]
    tm = min(512, s)

    def body(q_ref, k_ref, v_ref, gq_ref, gk_ref, bd_ref, qn_ref, kn_ref, vb_ref):
        bdv = bd_ref[...]
        q = q_ref[...]
        k = k_ref[...]
        qn_ref[...] = (q * lax.rsqrt(_group_mean(q * q, bdv) + EPS) * gq_ref[...]).astype(BF16)
        kn_ref[...] = (k * lax.rsqrt(_group_mean(k * k, bdv) + EPS) * gk_ref[...]).astype(BF16)
        vb_ref[...] = v_ref[...].astype(BF16)

    col = lambda c: pl.BlockSpec((tm, SB_WIDTH), lambda i: (i, c))
    out = pl.BlockSpec((tm, SB_WIDTH), lambda i: (i, 0))
    return pl.pallas_call(
        body, name="qk_prep", grid=(s // tm,),
        in_specs=[col(0), col(1), col(2), _full((1, SB_WIDTH)), _full((1, SB_WIDTH)), _full((SB_WIDTH, SB_WIDTH))],
        out_specs=[out, out, out],
        out_shape=[jax.ShapeDtypeStruct((s, SB_WIDTH), BF16)] * 3,
        compiler_params=_params(("arbitrary",), 32),
    )(proj, proj, proj, gq, gk, bd)


def _qk_bwd(proj, dqn, dkn, gq, gk, bd):
    s = proj.shape[0]
    tm = min(512, s)

    def body(q_ref, k_ref, dq_ref, dk_ref, gq_ref, gk_ref, bd_ref, o_ref, dgq_ref, dgk_ref):
        @pl.when(pl.program_id(0) == 0)
        def _():
            dgq_ref[...] = jnp.zeros_like(dgq_ref)
            dgk_ref[...] = jnp.zeros_like(dgk_ref)

        bdv = bd_ref[...]

        def back(t, gain, dout, dg_ref):
            r = lax.rsqrt(_group_mean(t * t, bdv) + EPS)
            th = t * r
            dg_ref[...] += jnp.sum(dout * th, axis=0, keepdims=True)
            dn = dout * gain
            return r * (dn - th * _group_mean(dn * th, bdv))

        o_ref[:, 0:SB_WIDTH] = back(q_ref[...], gq_ref[...], dq_ref[...], dgq_ref)
        o_ref[:, SB_WIDTH:2 * SB_WIDTH] = back(k_ref[...], gk_ref[...], dk_ref[...], dgk_ref)

    col = lambda c: pl.BlockSpec((tm, SB_WIDTH), lambda i: (i, c))
    row = pl.BlockSpec((tm, SB_WIDTH), lambda i: (i, 0))
    return pl.pallas_call(
        body, name="qk_bwd", grid=(s // tm,),
        in_specs=[col(0), col(1), row, row, _full((1, SB_WIDTH)), _full((1, SB_WIDTH)), _full((SB_WIDTH, SB_WIDTH))],
        out_specs=[pl.BlockSpec((tm, 2 * SB_WIDTH), lambda i: (i, 0)), _full((1, SB_WIDTH)), _full((1, SB_WIDTH))],
        out_shape=[jax.ShapeDtypeStruct((s, 2 * SB_WIDTH), F32), jax.ShapeDtypeStruct((1, SB_WIDTH), F32),
                   jax.ShapeDtypeStruct((1, SB_WIDTH), F32)],
        compiler_params=_params(("arbitrary",), 40),
    )(proj, proj, dqn, dkn, gq, gk, bd)


def _softplus(z):
    return jnp.maximum(z, 0.0) + jnp.log(1.0 + jnp.exp(-jnp.abs(z)))


def _attn_consts(tq):
    row = lax.broadcasted_iota(jnp.int32, (tq, tq), 0)
    col = lax.broadcasted_iota(jnp.int32, (tq, tq), 1)
    lane = lax.broadcasted_iota(jnp.int32, (tq, LANE), 1)
    return row, col, lane


def _sb_attn_fwd(qn, kn, vb):
    s = qn.shape[0]
    tq = min(ATT_BLOCK, s)
    nq = s // tq
    assert nq <= LANE

    def body(q_ref, k_ref, v_ref, o_ref, ls_ref):
        i = pl.program_id(1)
        row, col, lane = _attn_consts(tq)
        strict = col < row
        u_incl = (row >= col).astype(BF16)
        q = q_ref[...]
        acc = jnp.zeros((tq, LANE), F32)
        for hh in range(2):
            hmask = (lane >= hh * HEAD_DIM) & (lane < (hh + 1) * HEAD_DIM)

            def block(kb, carry, acc, lsum, diag):
                off = pl.multiple_of(kb * tq, tq)
                kblk = jnp.where(hmask, k_ref[pl.ds(off, tq), :], jnp.zeros((), BF16))
                vblk = jnp.where(hmask, v_ref[pl.ds(off, tq), :], jnp.zeros((), BF16))
                z = lax.dot_general(q, kblk, NT_DIMS, preferred_element_type=F32)
                lk = -_softplus(z)
                if diag:
                    lk = jnp.where(strict, lk, 0.0)
                rin = carry + _dot2(lk, u_incl)
                a = jnp.exp(z + rin)
                if diag:
                    a = jnp.where(strict, a, 0.0)
                acc = acc + _dot(a.astype(BF16), vblk)
                rs = jnp.sum(lk, axis=1, keepdims=True)
                lsum = lsum + jnp.where(lane == kb, rs, 0.0)
                return carry + rs, acc, lsum

            carry, acc, lsum = block(i, jnp.zeros((tq, 1), F32), acc, jnp.zeros((tq, LANE), F32), True)
            carry, acc, lsum = lax.fori_loop(
                0, i, lambda n, c: block(i - 1 - n, c[0], c[1], c[2], False), (carry, acc, lsum))
            ls_ref[:, hh * LANE:(hh + 1) * LANE] = lsum
        o_ref[...] = acc

    return pl.pallas_call(
        body, name="sb_attn_fwd", grid=(SB_WIDTH // LANE, nq),
        in_specs=[pl.BlockSpec((tq, LANE), lambda p, i: (i, p)), pl.BlockSpec((s, LANE), lambda p, i: (0, p)),
                  pl.BlockSpec((s, LANE), lambda p, i: (0, p))],
        out_specs=[pl.BlockSpec((tq, LANE), lambda p, i: (i, p)), pl.BlockSpec((tq, 2 * LANE), lambda p, i: (i, p))],
        out_shape=[jax.ShapeDtypeStruct((s, SB_WIDTH), F32), jax.ShapeDtypeStruct((s, 2 * SB_WIDTH), F32)],
        compiler_params=_params(("arbitrary", "arbitrary"), 40),
    )(qn, kn, vb)


def _sb_attn_bwd(qn, kn, vb, lsum, dmix):
    s = qn.shape[0]
    tq = min(ATT_BLOCK, s)
    nq = s // tq

    def body(q_ref, k_ref, v_ref, ls_ref, do_ref, dq_ref, dk_ref, dv_ref):
        i = pl.program_id(1)

        @pl.when(i == 0)
        def _():
            dk_ref[...] = jnp.zeros_like(dk_ref)
            dv_ref[...] = jnp.zeros_like(dv_ref)

        row, col, lane = _attn_consts(tq)
        strict = col < row
        u_incl = (row >= col).astype(BF16)
        u_pre = (row <= col).astype(BF16)
        lrow = lax.broadcasted_iota(jnp.int32, (LANE, LANE), 0)
        lcol = lax.broadcasted_iota(jnp.int32, (LANE, LANE), 1)
        u_after = (lrow > lcol).astype(BF16)
        q = q_ref[...]
        dob = do_ref[...].astype(BF16)
        dq = jnp.zeros((tq, LANE), F32)
        for hh in range(2):
            hmask = (lane >= hh * HEAD_DIM) & (lane < (hh + 1) * HEAD_DIM)
            qm = jnp.where(hmask, q, jnp.zeros((), BF16))
            dom = jnp.where(hmask, dob, jnp.zeros((), BF16))
            ls = ls_ref[:, hh * LANE:(hh + 1) * LANE]
            hi = ls.astype(BF16)
            mid = (ls - hi.astype(F32)).astype(BF16)
            lo = (ls - hi.astype(F32) - mid.astype(F32)).astype(BF16)
            after = _dot(hi, u_after) + _dot(mid, u_after) + _dot(lo, u_after)

            def block(kb, cp, dq, diag):
                off = pl.multiple_of(kb * tq, tq)
                kblk = jnp.where(hmask, k_ref[pl.ds(off, tq), :], jnp.zeros((), BF16))
                vblk = v_ref[pl.ds(off, tq), :]
                z = lax.dot_general(q, kblk, NT_DIMS, preferred_element_type=F32)
                sp = _softplus(z)
                lk = -sp
                if diag:
                    lk = jnp.where(strict, lk, 0.0)
                cr = jnp.sum(jnp.where(lane == kb, after, 0.0), axis=1, keepdims=True)
                rin = cr + _dot2(lk, u_incl)
                a = jnp.exp(z + rin)
                if diag:
                    a = jnp.where(strict, a, 0.0)
                da = lax.dot_general(dom, vblk, NT_DIMS, preferred_element_type=F32)
                g = da * a
                pre = cp + _dot(g.astype(BF16), u_pre)
                dz = g - jnp.exp(z - sp) * pre
                if diag:
                    dz = jnp.where(strict, dz, 0.0)
                dzb = dz.astype(BF16)
                dv_ref[pl.ds(off, tq), :] += lax.dot_general(a.astype(BF16), dom, TN_DIMS, preferred_element_type=F32)
                dk_ref[pl.ds(off, tq), :] += lax.dot_general(dzb, qm, TN_DIMS, preferred_element_type=F32)
                dq = dq + _dot(dzb, kblk)
                return cp + jnp.sum(g, axis=1, keepdims=True), dq

            cp, dq = lax.fori_loop(0, i, lambda n, c: block(n, c[0], c[1], False), (jnp.zeros((tq, 1), F32), dq))
            cp, dq = block(i, cp, dq, True)
        dq_ref[...] = dq

    blk = pl.BlockSpec((tq, LANE), lambda p, i: (i, p))
    whole = pl.BlockSpec((s, LANE), lambda p, i: (0, p))
    return pl.pallas_call(
        body, name="sb_attn_bwd", grid=(SB_WIDTH // LANE, nq),
        in_specs=[blk, whole, whole, pl.BlockSpec((tq, 2 * LANE), lambda p, i: (i, p)), blk],
        out_specs=[blk, whole, whole],
        out_shape=[jax.ShapeDtypeStruct((s, SB_WIDTH), F32)] * 3,
        compiler_params=_params(("arbitrary", "arbitrary"), 48),
    )(qn, kn, vb, lsum, dmix)


def _sgu_spatial(zn, wm_ref, lane, c):
    parts = []
    for p in range(SG_WIDTH // LANE):
        blk = zn[c * CHUNK:(c + 1) * CHUNK, p * LANE:(p + 1) * LANE].astype(BF16)
        lo = jnp.where(lane < HEAD_DIM, blk, jnp.zeros((), BF16))
        hi = jnp.where(lane >= HEAD_DIM, blk, jnp.zeros((), BF16))
        parts.append(_dot(wm_ref[2 * p], lo) + _dot(wm_ref[2 * p + 1], hi))
    return jnp.concatenate(parts, axis=1)


def _sgu_fwd(proj, gz, wm, bt, bd):
    s = proj.shape[0]
    tm = min(512, s)

    def body(u_ref, z_ref, gz_ref, wm_ref, bt_ref, bd_ref, o_ref):
        lane = lax.broadcasted_iota(jnp.int32, (CHUNK, LANE), 1)
        ug = _gelu(u_ref[...])
        zg = _gelu(z_ref[...])
        zn = zg * lax.rsqrt(_group_mean(zg * zg, bd_ref[...]) + EPS) * gz_ref[...]
        for c in range(tm // CHUNK):
            sp = _sgu_spatial(zn, wm_ref, lane, c) + bt_ref[...]
            o_ref[c * CHUNK:(c + 1) * CHUNK, :] = ug[c * CHUNK:(c + 1) * CHUNK, :] * sp

    col = lambda c: pl.BlockSpec((tm, SG_WIDTH), lambda i: (i, c))
    return pl.pallas_call(
        body, name="sgu_fwd", grid=(s // tm,),
        in_specs=[col(3), col(4), _full((1, SG_WIDTH)), _full((SG_GROUPS, CHUNK, CHUNK)), _full((CHUNK, SG_WIDTH)),
                  _full((SG_WIDTH, SG_WIDTH))],
        out_specs=pl.BlockSpec((tm, SG_WIDTH), lambda i: (i, 0)),
        out_shape=jax.ShapeDtypeStruct((s, SG_WIDTH), F32),
        compiler_params=_params(("arbitrary",), 32),
    )(proj, proj, gz, wm, bt, bd)


def _sgu_bwd(proj, dmix, gz, wm, wmt, bt, bd):
    s = proj.shape[0]
    tm = min(512, s)

    def body(u_ref, z_ref, dg_ref, gz_ref, wm_ref, wmt_ref, bt_ref, bd_ref, o_ref, dwm_ref, dbt_ref, dgz_ref):
        @pl.when(pl.program_id(0) == 0)
        def _():
            dwm_ref[...] = jnp.zeros_like(dwm_ref)
            dbt_ref[...] = jnp.zeros_like(dbt_ref)
            dgz_ref[...] = jnp.zeros_like(dgz_ref)

        lane = lax.broadcasted_iota(jnp.int32, (CHUNK, LANE), 1)
        bdv = bd_ref[...]
        u = u_ref[...]
        z = z_ref[...]
        ug = _gelu(u)
        zg = _gelu(z)
        r = lax.rsqrt(_group_mean(zg * zg, bdv) + EPS)
        zh = zg * r
        zn = zh * gz_ref[...]
        dzn_rows = []
        for c in range(tm // CHUNK):
            rows = slice(c * CHUNK, (c + 1) * CHUNK)
            sp = _sgu_spatial(zn, wm_ref, lane, c) + bt_ref[...]
            dgg = dg_ref[rows, :]
            ds = dgg * ug[rows, :]
            o_ref[rows, 0:SG_WIDTH] = dgg * sp * _gelu_grad(u[rows, :])
            dbt_ref[...] += ds
            parts = []
            for p in range(SG_WIDTH // LANE):
                dsb = ds[:, p * LANE:(p + 1) * LANE].astype(BF16)
                znb = zn[rows, p * LANE:(p + 1) * LANE].astype(BF16)
                acc = jnp.zeros((CHUNK, LANE), F32)
                for hh in range(2):
                    hm = (lane >= hh * HEAD_DIM) & (lane < (hh + 1) * HEAD_DIM)
                    dsm = jnp.where(hm, dsb, jnp.zeros((), BF16))
                    znm = jnp.where(hm, znb, jnp.zeros((), BF16))
                    acc = acc + _dot(wmt_ref[2 * p + hh], dsm)
                    dwm_ref[2 * p + hh] += lax.dot_general(dsm, znm, NT_DIMS, preferred_element_type=F32)
                parts.append(acc)
            dzn_rows.append(jnp.concatenate(parts, axis=1))
        dzn = jnp.concatenate(dzn_rows, axis=0)
        dgz_ref[...] += jnp.sum(dzn * zh, axis=0, keepdims=True)
        dn = dzn * gz_ref[...]
        o_ref[:, SG_WIDTH:2 * SG_WIDTH] = r * (dn - zh * _group_mean(dn * zh, bdv)) * _gelu_grad(z)

    col = lambda c: pl.BlockSpec((tm, SG_WIDTH), lambda i: (i, c))
    wspec = _full((SG_GROUPS, CHUNK, CHUNK))
    return pl.pallas_call(
        body, name="sgu_bwd", grid=(s // tm,),
        in_specs=[col(3), col(4), pl.BlockSpec((tm, SG_WIDTH), lambda i: (i, 1)), _full((1, SG_WIDTH)), wspec, wspec,
                  _full((CHUNK, SG_WIDTH)), _full((SG_WIDTH, SG_WIDTH))],
        out_specs=[pl.BlockSpec((tm, 2 * SG_WIDTH), lambda i: (i, 0)), wspec, _full((CHUNK, SG_WIDTH)),
                   _full((1, SG_WIDTH))],
        out_shape=[jax.ShapeDtypeStruct((s, 2 * SG_WIDTH), F32), jax.ShapeDtypeStruct((SG_GROUPS, CHUNK, CHUNK), F32),
                   jax.ShapeDtypeStruct((CHUNK, SG_WIDTH), F32), jax.ShapeDtypeStruct((1, SG_WIDTH), F32)],
        compiler_params=_params(("arbitrary",), 40),
    )(proj, proj, dmix, gz, wm, wmt, bt, bd)


def _conf_mid_fwd(p, wdw, bdw, lng, lnb):
    s = p.shape[0]
    c = p.shape[1] // 2
    tm = min(256, s)

    def body(a_ref, gt_ref, w_ref, b_ref, g_ref, beta_ref, yc_ref, y2_ref, ext):
        i = pl.program_id(0)

        @pl.when(i == 0)
        def _():
            ext[0:HALO, :] = jnp.zeros((HALO, c), F32)

        @pl.when(i > 0)
        def _():
            ext[0:HALO, :] = ext[tm:tm + HALO, :]

        ext[HALO:HALO + tm, :] = a_ref[...] * _sigmoid(gt_ref[...])
        acc = jnp.zeros((tm, c), F32) + b_ref[...]
        for j in range(CONV_K):
            acc = acc + w_ref[j:j + 1, :] * ext[pl.ds(HALO - CONV_K + 1 + j, tm), :]
        yc_ref[...] = acc
        xc = acc - jnp.mean(acc, axis=-1, keepdims=True)
        ln = xc * lax.rsqrt(jnp.mean(xc * xc, axis=-1, keepdims=True) + EPS) * g_ref[...] + beta_ref[...]
        y2_ref[...] = (ln * _sigmoid(ln)).astype(BF16)

    vec = _full((1, c))
    return pl.pallas_call(
        body, name="conf_mid_fwd", grid=(s // tm,),
        in_specs=[pl.BlockSpec((tm, c), lambda i: (i, 0)), pl.BlockSpec((tm, c), lambda i: (i, 1)), _full((HALO, c)), vec,
                  vec, vec],
        out_specs=[pl.BlockSpec((tm, c), lambda i: (i, 0)), pl.BlockSpec((tm, c), lambda i: (i, 0))],
        out_shape=[jax.ShapeDtypeStruct((s, c), F32), jax.ShapeDtypeStruct((s, c), BF16)],
        scratch_shapes=[pltpu.VMEM((HALO + tm, c), F32)],
        compiler_params=_params(("arbitrary",), 32),
    )(p, p, wdw, bdw, lng, lnb)


def _conf_mid_bwd(p, yc, dy2, dout, wdw, lng, lnb):
    s = p.shape[0]
    c = p.shape[1] // 2
    tm = min(256, s)
    n = s // tm
    hb = tm // HALO

    def body(a_ref, gt_ref, ah_ref, gh_ref, yc_ref, dy2_ref, dout_ref, w_ref, g_ref, beta_ref,
             dp_ref, dw_ref, dbdw_ref, dlg_ref, dlb_ref, db1_ref, db2_ref, exty, extd):
        i = pl.program_id(0)

        @pl.when(i == 0)
        def _():
            extd[tm:tm + HALO, :] = jnp.zeros((HALO, c), F32)
            for ref in (dw_ref, dbdw_ref, dlg_ref, dlb_ref, db1_ref, db2_ref):
                ref[...] = jnp.zeros_like(ref)

        @pl.when(i > 0)
        def _():
            extd[tm:tm + HALO, :] = extd[0:HALO, :]

        a = a_ref[...]
        sg = _sigmoid(gt_ref[...])
        exty[HALO:HALO + tm, :] = a * sg
        exty[0:HALO, :] = jnp.where(i < n - 1, ah_ref[...] * _sigmoid(gh_ref[...]), 0.0)
        ycv = yc_ref[...]
        xc = ycv - jnp.mean(ycv, axis=-1, keepdims=True)
        rstd = lax.rsqrt(jnp.mean(xc * xc, axis=-1, keepdims=True) + EPS)
        xh = xc * rstd
        ln = xh * g_ref[...] + beta_ref[...]
        dln = dy2_ref[...] * _silu_grad(ln, _sigmoid(ln))
        dlg_ref[...] += jnp.sum(dln * xh, axis=0, keepdims=True)
        dlb_ref[...] += jnp.sum(dln, axis=0, keepdims=True)
        dxh = dln * g_ref[...]
        dyc = rstd * (dxh - jnp.mean(dxh, axis=-1, keepdims=True) - xh * jnp.mean(dxh * xh, axis=-1, keepdims=True))
        extd[0:tm, :] = dyc
        dbdw_ref[...] += jnp.sum(dyc, axis=0, keepdims=True)
        db2_ref[...] += jnp.sum(dout_ref[...], axis=0, keepdims=True)
        dy = jnp.zeros((tm, c), F32)
        rows = []
        for j in range(CONV_K):
            dy = dy + w_ref[j:j + 1, :] * extd[pl.ds(CONV_K - 1 - j, tm), :]
            rows.append(jnp.sum(dyc * exty[pl.ds(HALO - CONV_K + 1 + j, tm), :], axis=0, keepdims=True))
        rows.append(jnp.zeros((HALO - CONV_K, c), F32))
        dw_ref[...] += jnp.concatenate(rows, axis=0)
        da = dy * sg
        dgt = dy * a * sg * (1.0 - sg)
        dp_ref[:, 0:c] = da.astype(BF16)
        dp_ref[:, c:2 * c] = dgt.astype(BF16)
        db1_ref[:, 0:c] += jnp.sum(da, axis=0, keepdims=True)
        db1_ref[:, c:2 * c] += jnp.sum(dgt, axis=0, keepdims=True)

    rev = lambda col: pl.BlockSpec((tm, c), lambda i: (n - 1 - i, col))
    halo = lambda col: pl.BlockSpec((HALO, c), lambda i: (jnp.maximum((n - 1 - i) * hb - 1, 0), col))
    vec = _full((1, c))
    return pl.pallas_call(
        body, name="conf_mid_bwd", grid=(n,),
        in_specs=[rev(0), rev(1), halo(0), halo(1), rev(0), rev(0), rev(0), _full((HALO, c)), vec, vec],
        out_specs=[pl.BlockSpec((tm, 2 * c), lambda i: (n - 1 - i, 0)), _full((HALO, c)), vec, vec, vec,
                   _full((1, 2 * c)), vec],
        out_shape=[jax.ShapeDtypeStruct((s, 2 * c), BF16), jax.ShapeDtypeStruct((HALO, c), F32),
                   jax.ShapeDtypeStruct((1, c), F32), jax.ShapeDtypeStruct((1, c), F32), jax.ShapeDtypeStruct((1, c), F32),
                   jax.ShapeDtypeStruct((1, 2 * c), F32), jax.ShapeDtypeStruct((1, c), F32)],
        scratch_shapes=[pltpu.VMEM((HALO + tm, c), F32), pltpu.VMEM((tm + HALO, c), F32)],
        compiler_params=_params(("arbitrary",), 40),
    )(p, p, p, p, yc, dy2, dout, wdw, lng, lnb)


def _ffn_fwd(x, g, wup, wdw, bdw, wdn):
    s, d = x.shape
    ff = wdn.shape[0]
    tm = min(256, s)

    def body(x_ref, g_ref, wup_ref, wdw_ref, bdw_ref, wdn_ref, y_ref, h_ref, u_ref, carry):
        @pl.when(pl.program_id(0) == 0)
        def _():
            carry[...] = jnp.zeros_like(carry)

        xv = x_ref[...]
        r = lax.rsqrt(jnp.mean(xv * xv, axis=-1, keepdims=True) + EPS)
        h = (xv * r * g_ref[...]).astype(BF16)
        h_ref[...] = h
        acc = xv
        for c in range(0, ff, FF_CHUNK):
            cs = slice(c, c + FF_CHUNK)
            gp = _dot(h, wup_ref[:, cs])
            val = _dot(h, wup_ref[:, ff + c:ff + c + FF_CHUNK])
            u_ref[:, cs] = gp.astype(BF16)
            u_ref[:, ff + c:ff + c + FF_CHUNK] = val.astype(BF16)
            prev = carry[:, cs]
            gate = (wdw_ref[0:1, cs] * _shift_down(gp, prev, 2) + wdw_ref[1:2, cs] * _shift_down(gp, prev, 1)
                    + wdw_ref[2:3, cs] * gp + bdw_ref[:, cs])
            act = gate * _sigmoid(gate) * val
            acc = acc + _dot(act.astype(BF16), wdn_ref[cs, :])
            carry[:, cs] = gp[tm - 8:tm, :]
        y_ref[...] = acc

    return pl.pallas_call(
        body, name="ffn_fwd", grid=(s // tm,),
        in_specs=[pl.BlockSpec((tm, d), lambda i: (i, 0)), _full((1, d)), _full((d, 2 * ff)), _full((8, ff)),
                  _full((1, ff)), _full((ff, d))],
        out_specs=[pl.BlockSpec((tm, d), lambda i: (i, 0)), pl.BlockSpec((tm, d), lambda i: (i, 0)),
                   pl.BlockSpec((tm, 2 * ff), lambda i: (i, 0))],
        out_shape=[jax.ShapeDtypeStruct((s, d), F32), jax.ShapeDtypeStruct((s, d), BF16),
                   jax.ShapeDtypeStruct((s, 2 * ff), BF16)],
        scratch_shapes=[pltpu.VMEM((8, ff), F32)],
        compiler_params=_params(("arbitrary",), 56),
    )(x, g, wup, wdw, bdw, wdn)


def _ffn_bwd(dy, u, wdw, bdw, wdn):
    s, d = dy.shape
    ff = wdn.shape[0]
    tm = min(256, s)
    n = s // tm
    hb = tm // 16

    def body(dy_ref, u_ref, uh_ref, wdw_ref, bdw_ref, wdn_ref, du_ref, act_ref, dw_ref, db_ref, carry):
        i = pl.program_id(0)

        @pl.when(i == 0)
        def _():
            carry[...] = jnp.zeros_like(carry)
            dw_ref[...] = jnp.zeros_like(dw_ref)
            db_ref[...] = jnp.zeros_like(db_ref)

        dyb = dy_ref[...].astype(BF16)
        for c in range(0, ff, FF_CHUNK):
            cs = slice(c, c + FF_CHUNK)
            vs = slice(ff + c, ff + c + FF_CHUNK)
            gp = u_ref[:, cs].astype(F32)
            val = u_ref[:, vs].astype(F32)
            prev = jnp.where(i < n - 1, uh_ref[:, cs].astype(F32)[8:16], 0.0)
            g1 = _shift_down(gp, prev, 1)
            g2 = _shift_down(gp, prev, 2)
            gate = wdw_ref[0:1, cs] * g2 + wdw_ref[1:2, cs] * g1 + wdw_ref[2:3, cs] * gp + bdw_ref[:, cs]
            sg = _sigmoid(gate)
            si = gate * sg
            act_ref[:, cs] = (si * val).astype(BF16)
            da = lax.dot_general(dyb, wdn_ref[cs, :], NT_DIMS, preferred_element_type=F32)
            dgate = da * val * _silu_grad(gate, sg)
            nxt = carry[:, cs]
            du_ref[:, cs] = (wdw_ref[2:3, cs] * dgate + wdw_ref[1:2, cs] * _shift_up(dgate, nxt, 1)
                             + wdw_ref[0:1, cs] * _shift_up(dgate, nxt, 2)).astype(BF16)
            du_ref[:, vs] = (da * si).astype(BF16)
            dw_ref[:, cs] += jnp.concatenate(
                [jnp.sum(dgate * g2, axis=0, keepdims=True), jnp.sum(dgate * g1, axis=0, keepdims=True),
                 jnp.sum(dgate * gp, axis=0, keepdims=True), jnp.zeros((5, FF_CHUNK), F32)], axis=0)
            db_ref[:, cs] += jnp.sum(dgate, axis=0, keepdims=True)
            carry[:, cs] = dgate[0:8, :]

    return pl.pallas_call(
        body, name="ffn_bwd", grid=(n,),
        in_specs=[pl.BlockSpec((tm, d), lambda i: (n - 1 - i, 0)), pl.BlockSpec((tm, 2 * ff), lambda i: (n - 1 - i, 0)),
                  pl.BlockSpec((16, 2 * ff), lambda i: (jnp.maximum((n - 1 - i) * hb - 1, 0), 0)),
                  _full((8, ff)), _full((1, ff)), _full((ff, d))],
        out_specs=[pl.BlockSpec((tm, 2 * ff), lambda i: (n - 1 - i, 0)), pl.BlockSpec((tm, ff), lambda i: (n - 1 - i, 0)),
                   _full((8, ff)), _full((1, ff))],
        out_shape=[jax.ShapeDtypeStruct((s, 2 * ff), BF16), jax.ShapeDtypeStruct((s, ff), BF16),
                   jax.ShapeDtypeStruct((8, ff), F32), jax.ShapeDtypeStruct((1, ff), F32)],
        scratch_shapes=[pltpu.VMEM((8, ff), F32)],
        compiler_params=_params(("arbitrary",), 48),
    )(dy, u, u, wdw, bdw, wdn)


def _loss_head(y, target):
    s, d = y.shape
    tm = min(512, s)

    def body(y_ref, t_ref, l_ref, dy_ref):
        @pl.when(pl.program_id(0) == 0)
        def _():
            l_ref[...] = jnp.zeros_like(l_ref)

        err = y_ref[...] - t_ref[...]
        dy_ref[...] = err * (1.0 / d)
        l_ref[...] += 0.5 * jnp.sum(jnp.mean(err * err, axis=-1, keepdims=True), axis=0, keepdims=True)

    return pl.pallas_call(
        body, name="loss_head", grid=(s // tm,),
        in_specs=[pl.BlockSpec((tm, d), lambda i: (i, 0)), pl.BlockSpec((tm, d), lambda i: (i, 0))],
        out_specs=[_full((8, LANE)), pl.BlockSpec((tm, d), lambda i: (i, 0))],
        out_shape=[jax.ShapeDtypeStruct((8, LANE), F32), jax.ShapeDtypeStruct((s, d), F32)],
        compiler_params=_params(("arbitrary",), 32),
    )(y, target)


def _reduce_adamw(parts, w, m, v, name):
    _, r, c = parts.shape
    tr = 8
    for cand in (512, 256, 184, 128, 64, 56, 32, 24, 16, 8):
        if r % cand == 0:
            tr = cand
            break

    def body(p_ref, w_ref, m_ref, v_ref, g_ref, d_ref, mo_ref, vo_ref):
        g = p_ref[0].astype(F32)
        for k in range(1, N_DEV):
            g = g + p_ref[k].astype(F32)
        g_ref[...] = g
        mn = ADAM_B1 * m_ref[...] + (1.0 - ADAM_B1) * g
        vn = ADAM_B2 * v_ref[...] + (1.0 - ADAM_B2) * (g * g)
        mo_ref[...] = mn
        vo_ref[...] = vn
        m_hat = mn / (1.0 - ADAM_B1 ** ADAM_STEP)
        v_hat = vn / (1.0 - ADAM_B2 ** ADAM_STEP)
        d_ref[...] = -ADAM_LR * (m_hat / (jnp.sqrt(v_hat) + ADAM_EPS) + ADAM_WD * w_ref[...])

    blk = pl.BlockSpec((tr, c), lambda i: (i, 0))
    return pl.pallas_call(
        body, name=name, grid=(r // tr,),
        in_specs=[pl.BlockSpec((N_DEV, tr, c), lambda i: (0, i, 0)), blk, blk, blk],
        out_specs=[blk, blk, blk, blk],
        out_shape=[jax.ShapeDtypeStruct((r, c), F32)] * 4,
        compiler_params=_params(("arbitrary",), 48),
    )(parts, w, m, v)


def _mesh_pos():
    return lax.axis_index("x"), lax.axis_index("y"), lax.axis_index("c")


def _all_gather(shard, name):
    r, c = shard.shape

    def body(x_ref, out_ref, send_sems, recv_sems, local_sem):
        x, y, cc = _mesh_pos()
        me, sibling = (x, y, cc), (x, y, 1 - cc)
        chips = [(1 - x, y), (x, 1 - y), (1 - x, 1 - y)]

        def slot(px, py, pc):
            return out_ref.at[4 * px + 2 * py + pc]

        def copy(k, block, to, src=None):
            return pltpu.make_async_remote_copy(
                src_ref=slot(*block) if src is None else src, dst_ref=slot(*block),
                send_sem=send_sems.at[k], recv_sem=recv_sems.at[k],
                device_id=to, device_id_type=pl.DeviceIdType.MESH)

        mine = pltpu.make_async_copy(x_ref, slot(*me), local_sem)
        mine.start()
        first = [copy(0, me, sibling, src=x_ref)]
        first += [copy(1 + j, me, (*chip, cc), src=x_ref) for j, chip in enumerate(chips)]
        for cp in first:
            cp.start()
        passed = [copy(4 + j, (*chip, cc), sibling) for j, chip in enumerate(chips)]
        for j, chip in enumerate(chips):
            copy(1 + j, (*chip, cc), me).wait_recv()
            passed[j].start()
        copy(0, sibling, me).wait_recv()
        for j, chip in enumerate(chips):
            copy(4 + j, (*chip, 1 - cc), me).wait_recv()
        for cp in first + passed:
            cp.wait_send()
        mine.wait()

    return pl.pallas_call(
        body, name=name,
        in_specs=[pl.BlockSpec(memory_space=pl.ANY)],
        out_specs=pl.BlockSpec(memory_space=pl.ANY),
        out_shape=jax.ShapeDtypeStruct((N_DEV, r, c), shard.dtype),
        scratch_shapes=[pltpu.SemaphoreType.DMA((7,)), pltpu.SemaphoreType.DMA((7,)), pltpu.SemaphoreType.DMA],
    )(shard)


def _all_to_all(blocks, name):
    _, r, c = blocks.shape

    def body(g_ref, out_ref, send_sems, recv_sems, local_sem):
        x, y, cc = _mesh_pos()
        me = 4 * x + 2 * y + cc
        mine = pltpu.make_async_copy(g_ref.at[me], out_ref.at[me], local_sem)
        mine.start()
        copies = []
        for k in range(1, N_DEV):
            px = 1 - x if k & 4 else x
            py = 1 - y if k & 2 else y
            pc = 1 - cc if k & 1 else cc
            copies.append(pltpu.make_async_remote_copy(
                src_ref=g_ref.at[4 * px + 2 * py + pc], dst_ref=out_ref.at[me],
                send_sem=send_sems.at[k - 1], recv_sem=recv_sems.at[k - 1],
                device_id=(px, py, pc), device_id_type=pl.DeviceIdType.MESH))
        for cp in copies:
            cp.start()
        for cp in copies:
            cp.wait()
        mine.wait()

    return pl.pallas_call(
        body, name=name,
        in_specs=[pl.BlockSpec(memory_space=pl.ANY)],
        out_specs=pl.BlockSpec(memory_space=pl.ANY),
        out_shape=jax.ShapeDtypeStruct(blocks.shape, blocks.dtype),
        scratch_shapes=[pltpu.SemaphoreType.DMA((7,)), pltpu.SemaphoreType.DMA((7,)), pltpu.SemaphoreType.DMA],
    )(blocks)


def _row(v):
    return v.reshape(1, -1)


def _group_ones():
    idx = jnp.arange(SB_WIDTH) // HEAD_DIM
    return (idx[:, None] == idx[None, :]).astype(BF16)


def _pad_rows(w, rows):
    return jnp.concatenate([w, jnp.zeros((rows - w.shape[0], w.shape[1]), w.dtype)], axis=0)


def _local_step(x, target, wt):
    scale = HEAD_DIM ** -0.5
    bd = _group_ones()
    tril = jnp.tril(jnp.ones((CHUNK, CHUNK), dtype=bool))
    saved = []
    for i in range(DEPTH):
        j = i // 2
        lay = {"x_mix": x}
        if i % 2 == 0:
            proj, h = _rms_matmul(x, _row(wt["mix_norm_g"][i]), wt["sb_w_in"][j], jnp.zeros((1, IN_WIDTH), F32), "in_proj")
            gq = _row(jnp.tile(wt["sb_q_norm_g"][j], SB_WIDTH // HEAD_DIM)) * scale
            gk = _row(jnp.tile(wt["sb_k_norm_g"][j], SB_WIDTH // HEAD_DIM))
            qn, kn, vb = _qk_prep(proj, gq, gk, bd)
            o, lsum = _sb_attn_fwd(qn, kn, vb)
            wm = jnp.where(tril[None], wt["sg_w_spatial"][j], 0.0)
            wmb = wm.astype(BF16)
            wmt = jnp.swapaxes(wm, 1, 2).astype(BF16)
            bt = jnp.repeat(wt["sg_b_spatial"][j].T, HEAD_DIM, axis=1)
            gz = _row(wt["sg_z_norm_g"][j])
            gg = _sgu_fwd(proj, gz, wmb, bt, bd)
            x, mix = _out_proj(x, o, gg, wt["hyb_w_out"][j], "out_proj")
            lay.update(proj=proj, h=h, gq=gq, gk=gk, qn=qn, kn=kn, vb=vb, lsum=lsum, wmb=wmb, wmt=wmt, bt=bt, gz=gz, mix=mix)
        else:
            p, h = _rms_matmul(x, _row(wt["mix_norm_g"][i]), wt["cv_w_pw1"][j], _row(wt["cv_b_pw1"][j]), "conf_pw1")
            wdw = _pad_rows(wt["cv_w_dw"][j], HALO)
            yc, y2 = _conf_mid_fwd(p, wdw, _row(wt["cv_b_dw"][j]), _row(wt["cv_ln_g"][j]), _row(wt["cv_ln_b"][j]))
            x = _res_matmul(x, y2, wt["cv_w_pw2"][j], _row(wt["cv_b_pw2"][j]), "conf_pw2")
            lay.update(p=p, h=h, wdw=wdw, yc=yc, y2=y2)
        lay["x_ffn"] = x
        fdw = _pad_rows(wt["ffn_w_dw"][i], 8)
        x, hf, u = _ffn_fwd(x, _row(wt["ffn_norm_g"][i]), wt["ffn_w_up"][i], fdw, _row(wt["ffn_b_dw"][i]),
                            wt["ffn_w_down"][i])
        lay.update(hf=hf, u=u, fdw=fdw)
        saved.append(lay)

    lpart, dy = _loss_head(x, target)
    loss = lpart[0, 0]

    gr = {k: [None] * v.shape[0] for k, v in wt.items()}
    for i in reversed(range(DEPTH)):
        j = i // 2
        lay = saved[i]
        du, act, dfdw, dfb = _ffn_bwd(dy, lay["u"], lay["fdw"], _row(wt["ffn_b_dw"][i]), wt["ffn_w_down"][i])
        gr["ffn_w_down"][i] = _matmul_tn(act, dy, "ffn_dw_down")
        gr["ffn_w_up"][i] = _matmul_tn(lay["hf"], du, "ffn_dw_up")
        gr["ffn_w_dw"][i] = dfdw[:FFN_K]
        gr["ffn_b_dw"][i] = dfb[0]
        dy, dgf = _nt_rms_bwd(du, wt["ffn_w_up"][i], lay["x_ffn"], _row(wt["ffn_norm_g"][i]), dy, "ffn_dx")
        gr["ffn_norm_g"][i] = dgf[0]
        if i % 2 == 0:
            dmix = _matmul_nt(dy, wt["hyb_w_out"][j], "out_proj_dx")
            gr["hyb_w_out"][j] = _matmul_tn(lay["mix"], dy, "out_proj_dw")
            dqn, dkn, dv = _sb_attn_bwd(lay["qn"], lay["kn"], lay["vb"], lay["lsum"], dmix)
            dqk, dgq, dgk = _qk_bwd(lay["proj"], dqn, dkn, lay["gq"], lay["gk"], bd)
            duz, dwm, dbt, dgz = _sgu_bwd(lay["proj"], dmix, lay["gz"], lay["wmb"], lay["wmt"], lay["bt"], bd)
            dproj = jnp.concatenate([dqk.astype(BF16), dv.astype(BF16), duz.astype(BF16)], axis=1)
            gr["sb_w_in"][j] = _matmul_tn(lay["h"], dproj, "in_proj_dw")
            gr["sb_q_norm_g"][j] = dgq.reshape(SB_WIDTH // HEAD_DIM, HEAD_DIM).sum(0) * scale
            gr["sb_k_norm_g"][j] = dgk.reshape(SB_WIDTH // HEAD_DIM, HEAD_DIM).sum(0)
            gr["sg_z_norm_g"][j] = dgz[0]
            gr["sg_w_spatial"][j] = jnp.where(tril[None], dwm, 0.0)
            gr["sg_b_spatial"][j] = dbt.reshape(CHUNK, SG_GROUPS, HEAD_DIM).sum(-1).T
            dy, dgm = _nt_rms_bwd(dproj, wt["sb_w_in"][j], lay["x_mix"], _row(wt["mix_norm_g"][i]), dy, "in_proj_dx")
        else:
            dy2 = _matmul_nt(dy, wt["cv_w_pw2"][j], "conf_pw2_dx")
            gr["cv_w_pw2"][j] = _matmul_tn(lay["y2"], dy, "conf_pw2_dw")
            dp, dwdw, dbdw, dlg, dlb, db1, db2 = _conf_mid_bwd(lay["p"], lay["yc"], dy2, dy, lay["wdw"],
                                                             _row(wt["cv_ln_g"][j]), _row(wt["cv_ln_b"][j]))
            gr["cv_w_pw1"][j] = _matmul_tn(lay["h"], dp, "conf_pw1_dw")
            gr["cv_w_dw"][j] = dwdw[:CONV_K]
            gr["cv_b_dw"][j] = dbdw[0]
            gr["cv_ln_g"][j] = dlg[0]
            gr["cv_ln_b"][j] = dlb[0]
            gr["cv_b_pw1"][j] = db1[0]
            gr["cv_b_pw2"][j] = db2[0]
            dy, dgm = _nt_rms_bwd(dp, wt["cv_w_pw1"][j], lay["x_mix"], _row(wt["mix_norm_g"][i]), dy, "conf_pw1_dx")
        gr["mix_norm_g"][i] = dgm[0]
    grads = {k: jnp.stack(v) for k, v in gr.items()}
    return loss, dy, grads


WEIGHTS = ["mix_norm_g", "sb_w_in", "sb_q_norm_g", "sb_k_norm_g", "sg_z_norm_g", "sg_w_spatial", "sg_b_spatial",
           "hyb_w_out", "cv_w_pw1", "cv_b_pw1", "cv_w_dw", "cv_b_dw", "cv_ln_g", "cv_ln_b", "cv_w_pw2", "cv_b_pw2",
           "ffn_norm_g", "ffn_w_up", "ffn_w_dw", "ffn_b_dw", "ffn_w_down"]
BIG = [("sb_w_in", "col"), ("hyb_w_out", "row"), ("cv_w_pw1", "col"), ("cv_w_pw2", "row"), ("ffn_w_up", "col"),
       ("ffn_w_down", "row")]
SMALL = [("cv_b_pw1", "col"), ("cv_w_dw", "col"), ("cv_b_dw", "col"), ("cv_ln_g", "col"), ("cv_ln_b", "col"),
         ("cv_b_pw2", "col"), ("ffn_w_dw", "col")]
REPLICATED = ["mix_norm_g", "sb_q_norm_g", "sb_k_norm_g", "sg_z_norm_g", "sg_w_spatial", "sg_b_spatial", "ffn_norm_g",
              "ffn_b_dw"]
BIG_COLS = 1024
SMALL_COLS = LANE


def _to_blocks(full, kind):
    if kind == "col":
        t = jnp.moveaxis(full.reshape(full.shape[:-1] + (N_DEV, full.shape[-1] // N_DEV)), -2, 0)
    else:
        t = jnp.moveaxis(full.reshape((full.shape[0], N_DEV, full.shape[1] // N_DEV) + full.shape[2:]), 1, 0)
    return t.reshape(N_DEV, -1)


def _from_blocks(blocks, shard_shape, kind):
    t = blocks.reshape((N_DEV,) + tuple(shard_shape))
    if kind == "col":
        return jnp.moveaxis(t, 0, -2).reshape(tuple(shard_shape[:-1]) + (N_DEV * shard_shape[-1],))
    return jnp.moveaxis(t, 0, 1).reshape((shard_shape[0], N_DEV * shard_shape[1]) + tuple(shard_shape[2:]))


def _padded_rows(count, cols):
    rows = -(-count // cols)
    return -(-rows // 16) * 16


def _pack(arrays, cols, dtype):
    lead = arrays[0].shape[:-1] if arrays[0].ndim > 1 else ()
    flat = jnp.concatenate([a.astype(dtype) for a in arrays], axis=-1)
    rows = _padded_rows(flat.shape[-1], cols)
    pad = rows * cols - flat.shape[-1]
    if pad:
        flat = jnp.concatenate([flat, jnp.zeros(lead + (pad,), dtype)], axis=-1)
    return flat.reshape(lead + (rows, cols))


def _unpack(packed, shapes):
    flat = packed.reshape(-1)
    out, off = [], 0
    for shp in shapes:
        size = 1
        for dim in shp:
            size *= dim
        out.append(flat[off:off + size].reshape(shp))
        off += size
    return out


def kernel(x, mix_norm_g, sb_w_in, sb_q_norm_g, sb_k_norm_g, sg_z_norm_g, sg_w_spatial, sg_b_spatial, hyb_w_out, cv_w_pw1, cv_b_pw1, cv_w_dw, cv_b_dw, cv_ln_g, cv_ln_b, cv_w_pw2, cv_b_pw2, ffn_norm_g, ffn_w_up, ffn_w_dw, ffn_b_dw, ffn_w_down, loss_target, m_mix_norm_g, m_sb_w_in, m_sb_q_norm_g, m_sb_k_norm_g, m_sg_z_norm_g, m_sg_w_spatial, m_sg_b_spatial, m_hyb_w_out, m_cv_w_pw1, m_cv_b_pw1, m_cv_w_dw, m_cv_b_dw, m_cv_ln_g, m_cv_ln_b, m_cv_w_pw2, m_cv_b_pw2, m_ffn_norm_g, m_ffn_w_up, m_ffn_w_dw, m_ffn_b_dw, m_ffn_w_down, v_mix_norm_g, v_sb_w_in, v_sb_q_norm_g, v_sb_k_norm_g, v_sg_z_norm_g, v_sg_w_spatial, v_sg_b_spatial, v_hyb_w_out, v_cv_w_pw1, v_cv_b_pw1, v_cv_w_dw, v_cv_b_dw, v_cv_ln_g, v_cv_ln_b, v_cv_w_pw2, v_cv_b_pw2, v_ffn_norm_g, v_ffn_w_up, v_ffn_w_dw, v_ffn_b_dw, v_ffn_w_down):
    w = dict(zip(WEIGHTS, (mix_norm_g, sb_w_in, sb_q_norm_g, sb_k_norm_g, sg_z_norm_g, sg_w_spatial, sg_b_spatial,
                           hyb_w_out, cv_w_pw1, cv_b_pw1, cv_w_dw, cv_b_dw, cv_ln_g, cv_ln_b, cv_w_pw2, cv_b_pw2,
                           ffn_norm_g, ffn_w_up, ffn_w_dw, ffn_b_dw, ffn_w_down)))
    m = dict(zip(WEIGHTS, (m_mix_norm_g, m_sb_w_in, m_sb_q_norm_g, m_sb_k_norm_g, m_sg_z_norm_g, m_sg_w_spatial,
                           m_sg_b_spatial, m_hyb_w_out, m_cv_w_pw1, m_cv_b_pw1, m_cv_w_dw, m_cv_b_dw, m_cv_ln_g,
                           m_cv_ln_b, m_cv_w_pw2, m_cv_b_pw2, m_ffn_norm_g, m_ffn_w_up, m_ffn_w_dw, m_ffn_b_dw,
                           m_ffn_w_down)))
    v = dict(zip(WEIGHTS, (v_mix_norm_g, v_sb_w_in, v_sb_q_norm_g, v_sb_k_norm_g, v_sg_z_norm_g, v_sg_w_spatial,
                           v_sg_b_spatial, v_hyb_w_out, v_cv_w_pw1, v_cv_b_pw1, v_cv_w_dw, v_cv_b_dw, v_cv_ln_g,
                           v_cv_ln_b, v_cv_w_pw2, v_cv_b_pw2, v_ffn_norm_g, v_ffn_w_up, v_ffn_w_dw, v_ffn_b_dw,
                           v_ffn_w_down)))
    big_names = [n for n, _ in BIG]
    small_names = [n for n, _ in SMALL]
    flat = lambda t, names: [t[n].reshape(-1) for n in names]

    big_all = _all_gather(_pack(flat(w, big_names), BIG_COLS, BF16), "gather_weights")
    small_all = _all_gather(_pack(flat(w, small_names), SMALL_COLS, F32), "gather_small_weights")
    big_parts = zip(*[_unpack(big_all[d], [(w[n].size,) for n in big_names]) for d in range(N_DEV)])
    small_parts = zip(*[_unpack(small_all[d], [(w[n].size,) for n in small_names]) for d in range(N_DEV)])
    wt = {n: w[n] for n in REPLICATED}
    for (n, kind), parts in zip(BIG, big_parts):
        wt[n] = _from_blocks(jnp.stack(parts), w[n].shape, kind)
    for (n, kind), parts in zip(SMALL, small_parts):
        wt[n] = _from_blocks(jnp.stack(parts), w[n].shape, kind)

    loss, gx, grads = _local_step(x[0], loss_target[0], wt)

    gbig = _pack([_to_blocks(grads[n], kind) for n, kind in BIG], BIG_COLS, BF16)
    recv_big = _all_to_all(gbig, "exchange_grads")
    grep = _pack(flat(grads, REPLICATED), SMALL_COLS, F32)
    gsmall = _pack([_to_blocks(grads[n], kind) for n, kind in SMALL], SMALL_COLS, F32)
    rep_rows, small_rows = grep.shape[0], gsmall.shape[1]
    vec = jnp.concatenate([grep, gsmall.reshape(N_DEV * small_rows, SMALL_COLS)], axis=0)
    vec_all = _all_gather(vec, "gather_small_grads")
    me = 4 * lax.axis_index("x") + 2 * lax.axis_index("y") + lax.axis_index("c")
    recv_rep = vec_all[:, :rep_rows]
    recv_small = lax.dynamic_slice(vec_all, (0, rep_rows + small_rows * me, 0), (N_DEV, small_rows, SMALL_COLS))

    out = {}
    for names, recv, cols, tag in ((big_names, recv_big, BIG_COLS, "adamw_big"),
                                   (small_names, recv_small, SMALL_COLS, "adamw_small"),
                                   (REPLICATED, recv_rep, SMALL_COLS, "adamw_replicated")):
        res = _reduce_adamw(recv, _pack(flat(w, names), cols, F32), _pack(flat(m, names), cols, F32),
                            _pack(flat(v, names), cols, F32), tag)
        shapes = [w[n].shape for n in names]
        for kind, packed in zip(("grad", "delta", "new_m", "new_v"), res):
            for n, arr in zip(names, _unpack(packed, shapes)):
                out[kind, n] = arr

    loss = lax.psum(loss, ("x", "y", "c"))
    return (loss, gx[None], *[out[kind, n] for kind in ("grad", "delta", "new_m", "new_v") for n in WEIGHTS])
```

```python
import functools

import jax
import jax.numpy as jnp
from jax import lax
from jax.experimental import pallas as pl
from jax.experimental.pallas import tpu as pltpu

F32 = jnp.float32
BF16 = jnp.bfloat16

D_MODEL = 1024
HEAD_DIM = 64
SB_WIDTH = 512
SG_WIDTH = 512
SG_GROUPS = 8
IN_WIDTH = 3 * SB_WIDTH + 2 * SG_WIDTH
CHUNK = 128
CONV_K = 31
D_FF = 2816
FFN_K = 3
DEPTH = 4
EPS = 1e-6
N_DEV = 8
LANE = 128
HALO = 32
ATT_BLOCK = 256
FF_CHUNK = 256
MIB = 2 ** 20

ADAM_LR = 0.001
ADAM_B1 = 0.9
ADAM_B2 = 0.999
ADAM_EPS = 1e-08
ADAM_WD = 0.01
ADAM_STEP = 10

LOG2E = 1.4426950408889634
LN2 = 0.6931471805599453

NT_DIMS = (((1,), (1,)), ((), ()))
TN_DIMS = (((0,), (0,)), ((), ()))


def _params(semantics, vmem_mib):
    return pltpu.CompilerParams(dimension_semantics=semantics, vmem_limit_bytes=vmem_mib * MIB)


def _full(shape):
    nd = len(shape)
    return pl.BlockSpec(shape, lambda *_: (0,) * nd)


def _sigmoid(x):
    return 1.0 / (1.0 + jnp.exp(-x))


def _gelu(x):
    return 0.5 * x * (1.0 + lax.erf(x * 0.7071067811865476))


def _gelu_grad(x):
    return 0.5 * (1.0 + lax.erf(x * 0.7071067811865476)) + x * jnp.exp(-0.5 * x * x) * 0.3989422804014327


def _silu_grad(x, s):
    return s * (1.0 + x * (1.0 - s))


def _dot(a, b):
    return jnp.dot(a, b, preferred_element_type=F32)


def _dot2(a, b):
    hi = a.astype(BF16)
    lo = (a - hi.astype(F32)).astype(BF16)
    return _dot(hi, b) + _dot(lo, b)


def _group_mean(t, bd):
    return _dot2(t, bd) * (1.0 / HEAD_DIM)


def _shift_down(v, prev8, s):
    top = pltpu.roll(jnp.concatenate([prev8, v[:8]], axis=0), s, 0)[8:16]
    return jnp.concatenate([top, pltpu.roll(v, s, 0)[8:]], axis=0)


def _shift_up(v, next8, s):
    n = v.shape[0]
    bottom = pltpu.roll(jnp.concatenate([v[n - 8:], next8], axis=0), 16 - s, 0)[0:8]
    return jnp.concatenate([pltpu.roll(v, n - s, 0)[: n - 8], bottom], axis=0)


def _rms_matmul(x, g, w, b, name):
    s, d = x.shape
    n = w.shape[1]
    tm = min(512, s)

    def body(x_ref, g_ref, w_ref, b_ref, y_ref, h_ref):
        xv = x_ref[...]
        r = lax.rsqrt(jnp.mean(xv * xv, axis=-1, keepdims=True) + EPS)
        h = (xv * r * g_ref[...]).astype(BF16)
        h_ref[...] = h
        for c in range(0, n, 512):
            y_ref[:, c:c + 512] = _dot(h, w_ref[:, c:c + 512]) + b_ref[:, c:c + 512]

    return pl.pallas_call(
        body, name=name, grid=(s // tm,),
        in_specs=[pl.BlockSpec((tm, d), lambda i: (i, 0)), _full((1, d)), _full((d, n)), _full((1, n))],
        out_specs=[pl.BlockSpec((tm, n), lambda i: (i, 0)), pl.BlockSpec((tm, d), lambda i: (i, 0))],
        out_shape=[jax.ShapeDtypeStruct((s, n), F32), jax.ShapeDtypeStruct((s, d), BF16)],
        compiler_params=_params(("arbitrary",), 48),
    )(x, g, w, b)


def _matmul_nt(a, w, name):
    s, n = a.shape
    k = w.shape[0]
    tm = min(512, s)

    def body(a_ref, w_ref, o_ref):
        o_ref[...] = lax.dot_general(a_ref[...].astype(BF16), w_ref[...], NT_DIMS, preferred_element_type=F32)

    return pl.pallas_call(
        body, name=name, grid=(s // tm,),
        in_specs=[pl.BlockSpec((tm, n), lambda i: (i, 0)), _full((k, n))],
        out_specs=pl.BlockSpec((tm, k), lambda i: (i, 0)),
        out_shape=jax.ShapeDtypeStruct((s, k), F32),
        compiler_params=_params(("arbitrary",), 40),
    )(a, w)


def _matmul_tn(a, b, name):
    s, k = a.shape
    n = b.shape[1]
    ts = min(512, s)
    tn = 512
    steps = s // ts

    def body(a_ref, b_ref, o_ref):
        @pl.when(pl.program_id(1) == 0)
        def _():
            o_ref[...] = jnp.zeros_like(o_ref)

        o_ref[...] += lax.dot_general(a_ref[...].astype(BF16), b_ref[...].astype(BF16), TN_DIMS,
                                      preferred_element_type=F32)

    return pl.pallas_call(
        body, name=name, grid=(n // tn, steps),
        in_specs=[pl.BlockSpec((ts, k), lambda j, t: (t, 0)), pl.BlockSpec((ts, tn), lambda j, t: (t, j))],
        out_specs=pl.BlockSpec((k, tn), lambda j, t: (0, j)),
        out_shape=jax.ShapeDtypeStruct((k, n), F32),
        compiler_params=_params(("arbitrary", "arbitrary"), 48),
    )(a, b)


def _nt_rms_bwd(dp, w, x, g, dres, name):
    s, n = dp.shape
    d = x.shape[1]
    tm = min(256, s)

    def body(dp_ref, w_ref, x_ref, g_ref, dres_ref, dx_ref, dg_ref):
        @pl.when(pl.program_id(0) == 0)
        def _():
            dg_ref[...] = jnp.zeros_like(dg_ref)

        dh = lax.dot_general(dp_ref[...], w_ref[...], NT_DIMS, preferred_element_type=F32)
        xv = x_ref[...]
        r = lax.rsqrt(jnp.mean(xv * xv, axis=-1, keepdims=True) + EPS)
        xh = xv * r
        dg_ref[...] += jnp.sum(dh * xh, axis=0, keepdims=True)
        dn = dh * g_ref[...]
        dx_ref[...] = dres_ref[...] + r * (dn - xh * jnp.mean(dn * xh, axis=-1, keepdims=True))

    return pl.pallas_call(
        body, name=name, grid=(s // tm,),
        in_specs=[pl.BlockSpec((tm, n), lambda i: (i, 0)), _full((d, n)), pl.BlockSpec((tm, d), lambda i: (i, 0)),
                  _full((1, d)), pl.BlockSpec((tm, d), lambda i: (i, 0))],
        out_specs=[pl.BlockSpec((tm, d), lambda i: (i, 0)), _full((1, d))],
        out_shape=[jax.ShapeDtypeStruct((s, d), F32), jax.ShapeDtypeStruct((1, d), F32)],
        compiler_params=_params(("arbitrary",), 52),
    )(dp, w, x, g, dres)


def _res_matmul(x, a, w, b, name):
    s, d = x.shape
    k = a.shape[1]
    tm = min(512, s)

    def body(x_ref, a_ref, w_ref, b_ref, o_ref):
        o_ref[...] = x_ref[...] + _dot(a_ref[...], w_ref[...]) + b_ref[...]

    return pl.pallas_call(
        body, name=name, grid=(s // tm,),
        in_specs=[pl.BlockSpec((tm, d), lambda i: (i, 0)), pl.BlockSpec((tm, k), lambda i: (i, 0)), _full((k, d)),
                  _full((1, d))],
        out_specs=pl.BlockSpec((tm, d), lambda i: (i, 0)),
        out_shape=jax.ShapeDtypeStruct((s, d), F32),
        compiler_params=_params(("arbitrary",), 32),
    )(x, a, w, b)


def _out_proj(x, o, gg, w, name):
    s, d = x.shape
    tm = min(512, s)

    def body(x_ref, o_ref, gg_ref, w_ref, y_ref, mix_ref):
        mix = jnp.concatenate([o_ref[...], gg_ref[...]], axis=1).astype(BF16)
        mix_ref[...] = mix
        y_ref[...] = x_ref[...] + _dot(mix, w_ref[...])

    return pl.pallas_call(
        body, name=name, grid=(s // tm,),
        in_specs=[pl.BlockSpec((tm, d), lambda i: (i, 0)), pl.BlockSpec((tm, SB_WIDTH), lambda i: (i, 0)),
                  pl.BlockSpec((tm, SG_WIDTH), lambda i: (i, 0)), _full((d, d))],
        out_specs=[pl.BlockSpec((tm, d), lambda i: (i, 0)), pl.BlockSpec((tm, d), lambda i: (i, 0))],
        out_shape=[jax.ShapeDtypeStruct((s, d), F32), jax.ShapeDtypeStruct((s, d), BF16)],
        compiler_params=_params(("arbitrary",), 32),
    )(x, o, gg, w)


def _qk_prep(proj, gq, gk, bd):
    s = proj.shape[0]
    tm = min(512, s)

    def body(q_ref, k_ref, v_ref, gq_ref, gk_ref, bd_ref, qn_ref, kn_ref, vb_ref):
        bdv = bd_ref[...]
        q = q_ref[...]
        k = k_ref[...]
        qn_ref[...] = (q * lax.rsqrt(_group_mean(q * q, bdv) + EPS) * gq_ref[...]).astype(BF16)
        kn_ref[...] = (k * lax.rsqrt(_group_mean(k * k, bdv) + EPS) * gk_ref[...]).astype(BF16)
        vb_ref[...] = v_ref[...].astype(BF16)

    col = lambda c: pl.BlockSpec((tm, SB_WIDTH), lambda i: (i, c))
    out = pl.BlockSpec((tm, SB_WIDTH), lambda i: (i, 0))
    return pl.pallas_call(
        body, name="qk_prep", grid=(s // tm,),
        in_specs=[col(0), col(1), col(2), _full((1, SB_WIDTH)), _full((1, SB_WIDTH)), _full((SB_WIDTH, SB_WIDTH))],
        out_specs=[out, out, out],
        out_shape=[jax.ShapeDtypeStruct((s, SB_WIDTH), BF16)] * 3,
        compiler_params=_params(("arbitrary",), 32),
    )(proj, proj, proj, gq, gk, bd)


def _qk_bwd(proj, dqn, dkn, gq, gk, bd):
    s = proj.shape[0]
    tm = min(512, s)

    def body(q_ref, k_ref, dq_ref, dk_ref, gq_ref, gk_ref, bd_ref, o_ref, dgq_ref, dgk_ref):
        @pl.when(pl.program_id(0) == 0)
        def _():
            dgq_ref[...] = jnp.zeros_like(dgq_ref)
            dgk_ref[...] = jnp.zeros_like(dgk_ref)

        bdv = bd_ref[...]

        def back(t, gain, dout, dg_ref):
            r = lax.rsqrt(_group_mean(t * t, bdv) + EPS)
            th = t * r
            dg_ref[...] += jnp.sum(dout * th, axis=0, keepdims=True)
            dn = dout * gain
            return r * (dn - th * _group_mean(dn * th, bdv))

        o_ref[:, 0:SB_WIDTH] = back(q_ref[...], gq_ref[...], dq_ref[...], dgq_ref)
        o_ref[:, SB_WIDTH:2 * SB_WIDTH] = back(k_ref[...], gk_ref[...], dk_ref[...], dgk_ref)

    col = lambda c: pl.BlockSpec((tm, SB_WIDTH), lambda i: (i, c))
    row = pl.BlockSpec((tm, SB_WIDTH), lambda i: (i, 0))
    return pl.pallas_call(
        body, name="qk_bwd", grid=(s // tm,),
        in_specs=[col(0), col(1), row, row, _full((1, SB_WIDTH)), _full((1, SB_WIDTH)), _full((SB_WIDTH, SB_WIDTH))],
        out_specs=[pl.BlockSpec((tm, 2 * SB_WIDTH), lambda i: (i, 0)), _full((1, SB_WIDTH)), _full((1, SB_WIDTH))],
        out_shape=[jax.ShapeDtypeStruct((s, 2 * SB_WIDTH), F32), jax.ShapeDtypeStruct((1, SB_WIDTH), F32),
                   jax.ShapeDtypeStruct((1, SB_WIDTH), F32)],
        compiler_params=_params(("arbitrary",), 40),
    )(proj, proj, dqn, dkn, gq, gk, bd)


def _attn_masks(tq, tk):
    lane = lax.broadcasted_iota(jnp.int32, (tk, LANE), 1)
    heads = [(lane >= hh * HEAD_DIM) & (lane < (hh + 1) * HEAD_DIM) for hh in range(2)]
    urow = lax.broadcasted_iota(jnp.int32, (tk, tk), 0)
    ucol = lax.broadcasted_iota(jnp.int32, (tk, tk), 1)
    return heads, urow, ucol


def _keep_cost(z2):
    return jnp.log(1.0 + jnp.exp2(jnp.minimum(z2, 126.0)))


def _sb_attn_fwd(q2, kn, vb):
    s = q2.shape[0]
    tk = min(ATT_BLOCK, s)
    tq = min(2 * ATT_BLOCK, s)
    r = tq // tk
    nq = s // tq
    assert s // tk <= LANE

    def body(q_ref, k_ref, v_ref, o_ref, ls_ref):
        i = pl.program_id(1)
        heads, urow, ucol = _attn_masks(tq, tk)
        u_incl = (urow >= ucol).astype(BF16)
        row = lax.broadcasted_iota(jnp.int32, (tq, tk), 0)
        col = lax.broadcasted_iota(jnp.int32, (tq, tk), 1)
        qlane = lax.broadcasted_iota(jnp.int32, (tq, LANE), 1)
        q = q_ref[...]

        def block(kb, state, masked):
            carry, acc, ls = list(state[0:2]), state[2], list(state[3:5])
            off = pl.multiple_of(kb * tk, tk)
            kfull = k_ref[pl.ds(off, tk), :]
            vfull = v_ref[pl.ds(off, tk), :]
            if masked:
                valid = (kb * tk + col) < (i * tq + row)
            for hh in range(2):
                kblk = jnp.where(heads[hh], kfull, jnp.zeros((), BF16))
                vblk = jnp.where(heads[hh], vfull, jnp.zeros((), BF16))
                z2 = lax.dot_general(q, kblk, NT_DIMS, preferred_element_type=F32)
                cost = _keep_cost(z2)
                if masked:
                    cost = jnp.where(valid, cost, 0.0)
                rin = carry[hh] + _dot2(cost, u_incl)
                a = jnp.exp2(z2 - rin * LOG2E)
                if masked:
                    a = jnp.where(valid, a, 0.0)
                acc = acc + _dot(a.astype(BF16), vblk)
                rs = jnp.sum(cost, axis=1, keepdims=True)
                ls[hh] = ls[hh] + jnp.where(qlane == kb, rs, 0.0)
                carry[hh] = carry[hh] + rs
            return carry[0], carry[1], acc, ls[0], ls[1]

        zc = jnp.zeros((tq, 1), F32)
        zt = jnp.zeros((tq, LANE), F32)
        state = (zc, zc, zt, zt, zt)
        for d in range(r):
            state = block(r * i + r - 1 - d, state, True)
        def step(n, st):
            for d in range(r):
                st = block(r * (i - 1 - n) + r - 1 - d, st, False)
            return st

        state = lax.fori_loop(0, i, step, state)
        o_ref[...] = state[2]
        ls_ref[:, 0:LANE] = state[3]
        ls_ref[:, LANE:2 * LANE] = state[4]

    return pl.pallas_call(
        body, name="sb_attn_fwd", grid=(SB_WIDTH // LANE, nq),
        in_specs=[pl.BlockSpec((tq, LANE), lambda p, i: (i, p)), pl.BlockSpec((s, LANE), lambda p, i: (0, p)),
                  pl.BlockSpec((s, LANE), lambda p, i: (0, p))],
        out_specs=[pl.BlockSpec((tq, LANE), lambda p, i: (i, p)), pl.BlockSpec((tq, 2 * LANE), lambda p, i: (i, p))],
        out_shape=[jax.ShapeDtypeStruct((s, SB_WIDTH), F32), jax.ShapeDtypeStruct((s, 2 * SB_WIDTH), F32)],
        compiler_params=_params(("arbitrary", "arbitrary"), 48),
    )(q2, kn, vb)


def _sb_attn_bwd(q2, kn, vb, lsum, dmix):
    s = q2.shape[0]
    tk = min(ATT_BLOCK, s)
    tq = min(2 * ATT_BLOCK, s)
    r = tq // tk
    nq = s // tq

    def body(q_ref, k_ref, v_ref, ls_ref, do_ref, dq_ref, dk_ref, dv_ref):
        i = pl.program_id(1)

        @pl.when(i == 0)
        def _():
            dk_ref[...] = jnp.zeros_like(dk_ref)
            dv_ref[...] = jnp.zeros_like(dv_ref)

        heads, urow, ucol = _attn_masks(tq, tk)
        u_incl = (urow >= ucol).astype(BF16)
        u_pre = (urow <= ucol).astype(BF16)
        lrow = lax.broadcasted_iota(jnp.int32, (LANE, LANE), 0)
        lcol = lax.broadcasted_iota(jnp.int32, (LANE, LANE), 1)
        u_after = (lrow > lcol).astype(BF16)
        row = lax.broadcasted_iota(jnp.int32, (tq, tk), 0)
        col = lax.broadcasted_iota(jnp.int32, (tq, tk), 1)
        qlane = lax.broadcasted_iota(jnp.int32, (tq, LANE), 1)
        qheads = [(qlane >= hh * HEAD_DIM) & (qlane < (hh + 1) * HEAD_DIM) for hh in range(2)]
        q = q_ref[...]
        dob = do_ref[...].astype(BF16)
        qm = [jnp.where(qheads[hh], q, jnp.zeros((), BF16)) for hh in range(2)]
        dom = [jnp.where(qheads[hh], dob, jnp.zeros((), BF16)) for hh in range(2)]
        after = []
        for hh in range(2):
            ls = ls_ref[:, hh * LANE:(hh + 1) * LANE]
            hi = ls.astype(BF16)
            mid = (ls - hi.astype(F32)).astype(BF16)
            lo = (ls - hi.astype(F32) - mid.astype(F32)).astype(BF16)
            after.append(_dot(hi, u_after) + _dot(mid, u_after) + _dot(lo, u_after))

        def block(kb, state, masked):
            cp, dq = list(state[0:2]), state[2]
            off = pl.multiple_of(kb * tk, tk)
            kfull = k_ref[pl.ds(off, tk), :]
            vblk = v_ref[pl.ds(off, tk), :]
            if masked:
                valid = (kb * tk + col) < (i * tq + row)
            dk_acc = jnp.zeros((tk, LANE), F32)
            dv_acc = jnp.zeros((tk, LANE), F32)
            for hh in range(2):
                kblk = jnp.where(heads[hh], kfull, jnp.zeros((), BF16))
                z2 = lax.dot_general(q, kblk, NT_DIMS, preferred_element_type=F32)
                cost = _keep_cost(z2)
                sig = jnp.exp2(z2 - cost * LOG2E)
                if masked:
                    cost = jnp.where(valid, cost, 0.0)
                cr = jnp.sum(jnp.where(qlane == kb, after[hh], 0.0), axis=1, keepdims=True)
                rin = cr + _dot2(cost, u_incl)
                a = jnp.exp2(z2 - rin * LOG2E)
                if masked:
                    a = jnp.where(valid, a, 0.0)
                da = lax.dot_general(dom[hh], vblk, NT_DIMS, preferred_element_type=F32)
                g = da * a
                pre = cp[hh] + _dot(g.astype(BF16), u_pre)
                dz = g - sig * pre
                if masked:
                    dz = jnp.where(valid, dz, 0.0)
                dzb = dz.astype(BF16)
                dv_acc = dv_acc + lax.dot_general(a.astype(BF16), dom[hh], TN_DIMS, preferred_element_type=F32)
                dk_acc = dk_acc + lax.dot_general(dzb, qm[hh], TN_DIMS, preferred_element_type=F32)
                dq = dq + _dot(dzb, kblk)
                cp[hh] = cp[hh] + jnp.sum(g, axis=1, keepdims=True)
            dv_ref[pl.ds(off, tk), :] += dv_acc
            dk_ref[pl.ds(off, tk), :] += dk_acc
            return cp[0], cp[1], dq

        zc = jnp.zeros((tq, 1), F32)
        def step(n, st):
            for d in range(r):
                st = block(r * n + d, st, False)
            return st

        state = lax.fori_loop(0, i, step, (zc, zc, jnp.zeros((tq, LANE), F32)))
        for d in range(r):
            state = block(r * i + d, state, True)
        dq_ref[...] = state[2]

    blk = pl.BlockSpec((tq, LANE), lambda p, i: (i, p))
    whole = pl.BlockSpec((s, LANE), lambda p, i: (0, p))
    return pl.pallas_call(
        body, name="sb_attn_bwd", grid=(SB_WIDTH // LANE, nq),
        in_specs=[blk, whole, whole, pl.BlockSpec((tq, 2 * LANE), lambda p, i: (i, p)), blk],
        out_specs=[blk, whole, whole],
        out_shape=[jax.ShapeDtypeStruct((s, SB_WIDTH), F32)] * 3,
        compiler_params=_params(("arbitrary", "arbitrary"), 56),
    )(q2, kn, vb, lsum, dmix)


def _sgu_spatial(zn, wm_ref, lane, c):
    parts = []
    for p in range(SG_WIDTH // LANE):
        blk = zn[c * CHUNK:(c + 1) * CHUNK, p * LANE:(p + 1) * LANE].astype(BF16)
        lo = jnp.where(lane < HEAD_DIM, blk, jnp.zeros((), BF16))
        hi = jnp.where(lane >= HEAD_DIM, blk, jnp.zeros((), BF16))
        parts.append(_dot(wm_ref[2 * p], lo) + _dot(wm_ref[2 * p + 1], hi))
    return jnp.concatenate(parts, axis=1)


def _sgu_fwd(proj, gz, wm, bt, bd):
    s = proj.shape[0]
    tm = min(512, s)

    def body(u_ref, z_ref, gz_ref, wm_ref, bt_ref, bd_ref, o_ref):
        lane = lax.broadcasted_iota(jnp.int32, (CHUNK, LANE), 1)
        ug = _gelu(u_ref[...])
        zg = _gelu(z_ref[...])
        zn = zg * lax.rsqrt(_group_mean(zg * zg, bd_ref[...]) + EPS) * gz_ref[...]
        for c in range(tm // CHUNK):
            sp = _sgu_spatial(zn, wm_ref, lane, c) + bt_ref[...]
            o_ref[c * CHUNK:(c + 1) * CHUNK, :] = ug[c * CHUNK:(c + 1) * CHUNK, :] * sp

    col = lambda c: pl.BlockSpec((tm, SG_WIDTH), lambda i: (i, c))
    return pl.pallas_call(
        body, name="sgu_fwd", grid=(s // tm,),
        in_specs=[col(3), col(4), _full((1, SG_WIDTH)), _full((SG_GROUPS, CHUNK, CHUNK)), _full((CHUNK, SG_WIDTH)),
                  _full((SG_WIDTH, SG_WIDTH))],
        out_specs=pl.BlockSpec((tm, SG_WIDTH), lambda i: (i, 0)),
        out_shape=jax.ShapeDtypeStruct((s, SG_WIDTH), F32),
        compiler_params=_params(("arbitrary",), 32),
    )(proj, proj, gz, wm, bt, bd)


def _sgu_bwd(proj, dmix, gz, wm, wmt, bt, bd):
    s = proj.shape[0]
    tm = min(512, s)

    def body(u_ref, z_ref, dg_ref, gz_ref, wm_ref, wmt_ref, bt_ref, bd_ref, o_ref, dwm_ref, dbt_ref, dgz_ref):
        @pl.when(pl.program_id(0) == 0)
        def _():
            dwm_ref[...] = jnp.zeros_like(dwm_ref)
            dbt_ref[...] = jnp.zeros_like(dbt_ref)
            dgz_ref[...] = jnp.zeros_like(dgz_ref)

        lane = lax.broadcasted_iota(jnp.int32, (CHUNK, LANE), 1)
        bdv = bd_ref[...]
        u = u_ref[...]
        z = z_ref[...]
        ug = _gelu(u)
        zg = _gelu(z)
        r = lax.rsqrt(_group_mean(zg * zg, bdv) + EPS)
        zh = zg * r
        zn = zh * gz_ref[...]
        dzn_rows = []
        for c in range(tm // CHUNK):
            rows = slice(c * CHUNK, (c + 1) * CHUNK)
            sp = _sgu_spatial(zn, wm_ref, lane, c) + bt_ref[...]
            dgg = dg_ref[rows, :]
            ds = dgg * ug[rows, :]
            o_ref[rows, 0:SG_WIDTH] = dgg * sp * _gelu_grad(u[rows, :])
            dbt_ref[...] += ds
            parts = []
            for p in range(SG_WIDTH // LANE):
                dsb = ds[:, p * LANE:(p + 1) * LANE].astype(BF16)
                znb = zn[rows, p * LANE:(p + 1) * LANE].astype(BF16)
                acc = jnp.zeros((CHUNK, LANE), F32)
                for hh in range(2):
                    hm = (lane >= hh * HEAD_DIM) & (lane < (hh + 1) * HEAD_DIM)
                    dsm = jnp.where(hm, dsb, jnp.zeros((), BF16))
                    znm = jnp.where(hm, znb, jnp.zeros((), BF16))
                    acc = acc + _dot(wmt_ref[2 * p + hh], dsm)
                    dwm_ref[2 * p + hh] += lax.dot_general(dsm, znm, NT_DIMS, preferred_element_type=F32)
                parts.append(acc)
            dzn_rows.append(jnp.concatenate(parts, axis=1))
        dzn = jnp.concatenate(dzn_rows, axis=0)
        dgz_ref[...] += jnp.sum(dzn * zh, axis=0, keepdims=True)
        dn = dzn * gz_ref[...]
        o_ref[:, SG_WIDTH:2 * SG_WIDTH] = r * (dn - zh * _group_mean(dn * zh, bdv)) * _gelu_grad(z)

    col = lambda c: pl.BlockSpec((tm, SG_WIDTH), lambda i: (i, c))
    wspec = _full((SG_GROUPS, CHUNK, CHUNK))
    return pl.pallas_call(
        body, name="sgu_bwd", grid=(s // tm,),
        in_specs=[col(3), col(4), pl.BlockSpec((tm, SG_WIDTH), lambda i: (i, 1)), _full((1, SG_WIDTH)), wspec, wspec,
                  _full((CHUNK, SG_WIDTH)), _full((SG_WIDTH, SG_WIDTH))],
        out_specs=[pl.BlockSpec((tm, 2 * SG_WIDTH), lambda i: (i, 0)), wspec, _full((CHUNK, SG_WIDTH)),
                   _full((1, SG_WIDTH))],
        out_shape=[jax.ShapeDtypeStruct((s, 2 * SG_WIDTH), F32), jax.ShapeDtypeStruct((SG_GROUPS, CHUNK, CHUNK), F32),
                   jax.ShapeDtypeStruct((CHUNK, SG_WIDTH), F32), jax.ShapeDtypeStruct((1, SG_WIDTH), F32)],
        compiler_params=_params(("arbitrary",), 40),
    )(proj, proj, dmix, gz, wm, wmt, bt, bd)


def _conf_mid_fwd(p, wdw, bdw, lng, lnb):
    s = p.shape[0]
    c = p.shape[1] // 2
    tm = min(256, s)

    def body(a_ref, gt_ref, w_ref, b_ref, g_ref, beta_ref, yc_ref, y2_ref, ext):
        i = pl.program_id(0)

        @pl.when(i == 0)
        def _():
            ext[0:HALO, :] = jnp.zeros((HALO, c), F32)

        @pl.when(i > 0)
        def _():
            ext[0:HALO, :] = ext[tm:tm + HALO, :]

        ext[HALO:HALO + tm, :] = a_ref[...] * _sigmoid(gt_ref[...])
        acc = jnp.zeros((tm, c), F32) + b_ref[...]
        for j in range(CONV_K):
            acc = acc + w_ref[j:j + 1, :] * ext[pl.ds(HALO - CONV_K + 1 + j, tm), :]
        yc_ref[...] = acc
        xc = acc - jnp.mean(acc, axis=-1, keepdims=True)
        ln = xc * lax.rsqrt(jnp.mean(xc * xc, axis=-1, keepdims=True) + EPS) * g_ref[...] + beta_ref[...]
        y2_ref[...] = (ln * _sigmoid(ln)).astype(BF16)

    vec = _full((1, c))
    return pl.pallas_call(
        body, name="conf_mid_fwd", grid=(s // tm,),
        in_specs=[pl.BlockSpec((tm, c), lambda i: (i, 0)), pl.BlockSpec((tm, c), lambda i: (i, 1)), _full((HALO, c)), vec,
                  vec, vec],
        out_specs=[pl.BlockSpec((tm, c), lambda i: (i, 0)), pl.BlockSpec((tm, c), lambda i: (i, 0))],
        out_shape=[jax.ShapeDtypeStruct((s, c), F32), jax.ShapeDtypeStruct((s, c), BF16)],
        scratch_shapes=[pltpu.VMEM((HALO + tm, c), F32)],
        compiler_params=_params(("arbitrary",), 32),
    )(p, p, wdw, bdw, lng, lnb)


def _conf_mid_bwd(p, yc, dy2, dout, wdw, lng, lnb):
    s = p.shape[0]
    c = p.shape[1] // 2
    tm = min(256, s)
    n = s // tm
    hb = tm // HALO

    def body(a_ref, gt_ref, ah_ref, gh_ref, yc_ref, dy2_ref, dout_ref, w_ref, g_ref, beta_ref,
             dp_ref, dw_ref, dbdw_ref, dlg_ref, dlb_ref, db1_ref, db2_ref, exty, extd):
        i = pl.program_id(0)

        @pl.when(i == 0)
        def _():
            extd[tm:tm + HALO, :] = jnp.zeros((HALO, c), F32)
            for ref in (dw_ref, dbdw_ref, dlg_ref, dlb_ref, db1_ref, db2_ref):
                ref[...] = jnp.zeros_like(ref)

        @pl.when(i > 0)
        def _():
            extd[tm:tm + HALO, :] = extd[0:HALO, :]

        a = a_ref[...]
        sg = _sigmoid(gt_ref[...])
        exty[HALO:HALO + tm, :] = a * sg
        exty[0:HALO, :] = jnp.where(i < n - 1, ah_ref[...] * _sigmoid(gh_ref[...]), 0.0)
        ycv = yc_ref[...]
        xc = ycv - jnp.mean(ycv, axis=-1, keepdims=True)
        rstd = lax.rsqrt(jnp.mean(xc * xc, axis=-1, keepdims=True) + EPS)
        xh = xc * rstd
        ln = xh * g_ref[...] + beta_ref[...]
        dln = dy2_ref[...] * _silu_grad(ln, _sigmoid(ln))
        dlg_ref[...] += jnp.sum(dln * xh, axis=0, keepdims=True)
        dlb_ref[...] += jnp.sum(dln, axis=0, keepdims=True)
        dxh = dln * g_ref[...]
        dyc = rstd * (dxh - jnp.mean(dxh, axis=-1, keepdims=True) - xh * jnp.mean(dxh * xh, axis=-1, keepdims=True))
        extd[0:tm, :] = dyc
        dbdw_ref[...] += jnp.sum(dyc, axis=0, keepdims=True)
        db2_ref[...] += jnp.sum(dout_ref[...], axis=0, keepdims=True)
        dy = jnp.zeros((tm, c), F32)
        rows = []
        for j in range(CONV_K):
            dy = dy + w_ref[j:j + 1, :] * extd[pl.ds(CONV_K - 1 - j, tm), :]
            rows.append(jnp.sum(dyc * exty[pl.ds(HALO - CONV_K + 1 + j, tm), :], axis=0, keepdims=True))
        rows.append(jnp.zeros((HALO - CONV_K, c), F32))
        dw_ref[...] += jnp.concatenate(rows, axis=0)
        da = dy * sg
        dgt = dy * a * sg * (1.0 - sg)
        dp_ref[:, 0:c] = da.astype(BF16)
        dp_ref[:, c:2 * c] = dgt.astype(BF16)
        db1_ref[:, 0:c] += jnp.sum(da, axis=0, keepdims=True)
        db1_ref[:, c:2 * c] += jnp.sum(dgt, axis=0, keepdims=True)

    rev = lambda col: pl.BlockSpec((tm, c), lambda i: (n - 1 - i, col))
    halo = lambda col: pl.BlockSpec((HALO, c), lambda i: (jnp.maximum((n - 1 - i) * hb - 1, 0), col))
    vec = _full((1, c))
    return pl.pallas_call(
        body, name="conf_mid_bwd", grid=(n,),
        in_specs=[rev(0), rev(1), halo(0), halo(1), rev(0), rev(0), rev(0), _full((HALO, c)), vec, vec],
        out_specs=[pl.BlockSpec((tm, 2 * c), lambda i: (n - 1 - i, 0)), _full((HALO, c)), vec, vec, vec,
                   _full((1, 2 * c)), vec],
        out_shape=[jax.ShapeDtypeStruct((s, 2 * c), BF16), jax.ShapeDtypeStruct((HALO, c), F32),
                   jax.ShapeDtypeStruct((1, c), F32), jax.ShapeDtypeStruct((1, c), F32), jax.ShapeDtypeStruct((1, c), F32),
                   jax.ShapeDtypeStruct((1, 2 * c), F32), jax.ShapeDtypeStruct((1, c), F32)],
        scratch_shapes=[pltpu.VMEM((HALO + tm, c), F32), pltpu.VMEM((tm + HALO, c), F32)],
        compiler_params=_params(("arbitrary",), 40),
    )(p, p, p, p, yc, dy2, dout, wdw, lng, lnb)


def _ffn_fwd(x, g, wup, wdw, bdw, wdn):
    s, d = x.shape
    ff = wdn.shape[0]
    tm = min(256, s)

    def body(x_ref, g_ref, wup_ref, wdw_ref, bdw_ref, wdn_ref, y_ref, h_ref, u_ref, carry):
        @pl.when(pl.program_id(0) == 0)
        def _():
            carry[...] = jnp.zeros_like(carry)

        xv = x_ref[...]
        r = lax.rsqrt(jnp.mean(xv * xv, axis=-1, keepdims=True) + EPS)
        h = (xv * r * g_ref[...]).astype(BF16)
        h_ref[...] = h
        acc = xv
        for c in range(0, ff, FF_CHUNK):
            cs = slice(c, c + FF_CHUNK)
            gp = _dot(h, wup_ref[:, cs])
            val = _dot(h, wup_ref[:, ff + c:ff + c + FF_CHUNK])
            u_ref[:, cs] = gp.astype(BF16)
            u_ref[:, ff + c:ff + c + FF_CHUNK] = val.astype(BF16)
            prev = carry[:, cs]
            gate = (wdw_ref[0:1, cs] * _shift_down(gp, prev, 2) + wdw_ref[1:2, cs] * _shift_down(gp, prev, 1)
                    + wdw_ref[2:3, cs] * gp + bdw_ref[:, cs])
            act = gate * _sigmoid(gate) * val
            acc = acc + _dot(act.astype(BF16), wdn_ref[cs, :])
            carry[:, cs] = gp[tm - 8:tm, :]
        y_ref[...] = acc

    return pl.pallas_call(
        body, name="ffn_fwd", grid=(s // tm,),
        in_specs=[pl.BlockSpec((tm, d), lambda i: (i, 0)), _full((1, d)), _full((d, 2 * ff)), _full((8, ff)),
                  _full((1, ff)), _full((ff, d))],
        out_specs=[pl.BlockSpec((tm, d), lambda i: (i, 0)), pl.BlockSpec((tm, d), lambda i: (i, 0)),
                   pl.BlockSpec((tm, 2 * ff), lambda i: (i, 0))],
        out_shape=[jax.ShapeDtypeStruct((s, d), F32), jax.ShapeDtypeStruct((s, d), BF16),
                   jax.ShapeDtypeStruct((s, 2 * ff), BF16)],
        scratch_shapes=[pltpu.VMEM((8, ff), F32)],
        compiler_params=_params(("arbitrary",), 56),
    )(x, g, wup, wdw, bdw, wdn)


def _ffn_bwd(dy, u, wdw, bdw, wdn):
    s, d = dy.shape
    ff = wdn.shape[0]
    tm = min(256, s)
    n = s // tm
    hb = tm // 16

    def body(dy_ref, u_ref, uh_ref, wdw_ref, bdw_ref, wdn_ref, du_ref, act_ref, dw_ref, db_ref, carry):
        i = pl.program_id(0)

        @pl.when(i == 0)
        def _():
            carry[...] = jnp.zeros_like(carry)
            dw_ref[...] = jnp.zeros_like(dw_ref)
            db_ref[...] = jnp.zeros_like(db_ref)

        dyb = dy_ref[...].astype(BF16)
        for c in range(0, ff, FF_CHUNK):
            cs = slice(c, c + FF_CHUNK)
            vs = slice(ff + c, ff + c + FF_CHUNK)
            gp = u_ref[:, cs].astype(F32)
            val = u_ref[:, vs].astype(F32)
            prev = jnp.where(i < n - 1, uh_ref[:, cs].astype(F32)[8:16], 0.0)
            g1 = _shift_down(gp, prev, 1)
            g2 = _shift_down(gp, prev, 2)
            gate = wdw_ref[0:1, cs] * g2 + wdw_ref[1:2, cs] * g1 + wdw_ref[2:3, cs] * gp + bdw_ref[:, cs]
            sg = _sigmoid(gate)
            si = gate * sg
            act_ref[:, cs] = (si * val).astype(BF16)
            da = lax.dot_general(dyb, wdn_ref[cs, :], NT_DIMS, preferred_element_type=F32)
            dgate = da * val * _silu_grad(gate, sg)
            nxt = carry[:, cs]
            du_ref[:, cs] = (wdw_ref[2:3, cs] * dgate + wdw_ref[1:2, cs] * _shift_up(dgate, nxt, 1)
                             + wdw_ref[0:1, cs] * _shift_up(dgate, nxt, 2)).astype(BF16)
            du_ref[:, vs] = (da * si).astype(BF16)
            dw_ref[:, cs] += jnp.concatenate(
                [jnp.sum(dgate * g2, axis=0, keepdims=True), jnp.sum(dgate * g1, axis=0, keepdims=True),
                 jnp.sum(dgate * gp, axis=0, keepdims=True), jnp.zeros((5, FF_CHUNK), F32)], axis=0)
            db_ref[:, cs] += jnp.sum(dgate, axis=0, keepdims=True)
            carry[:, cs] = dgate[0:8, :]

    return pl.pallas_call(
        body, name="ffn_bwd", grid=(n,),
        in_specs=[pl.BlockSpec((tm, d), lambda i: (n - 1 - i, 0)), pl.BlockSpec((tm, 2 * ff), lambda i: (n - 1 - i, 0)),
                  pl.BlockSpec((16, 2 * ff), lambda i: (jnp.maximum((n - 1 - i) * hb - 1, 0), 0)),
                  _full((8, ff)), _full((1, ff)), _full((ff, d))],
        out_specs=[pl.BlockSpec((tm, 2 * ff), lambda i: (n - 1 - i, 0)), pl.BlockSpec((tm, ff), lambda i: (n - 1 - i, 0)),
                   _full((8, ff)), _full((1, ff))],
        out_shape=[jax.ShapeDtypeStruct((s, 2 * ff), BF16), jax.ShapeDtypeStruct((s, ff), BF16),
                   jax.ShapeDtypeStruct((8, ff), F32), jax.ShapeDtypeStruct((1, ff), F32)],
        scratch_shapes=[pltpu.VMEM((8, ff), F32)],
        compiler_params=_params(("arbitrary",), 48),
    )(dy, u, u, wdw, bdw, wdn)


def _loss_head(y, target):
    s, d = y.shape
    tm = min(512, s)

    def body(y_ref, t_ref, l_ref, dy_ref):
        @pl.when(pl.program_id(0) == 0)
        def _():
            l_ref[...] = jnp.zeros_like(l_ref)

        err = y_ref[...] - t_ref[...]
        dy_ref[...] = err * (1.0 / d)
        l_ref[...] += 0.5 * jnp.sum(jnp.mean(err * err, axis=-1, keepdims=True), axis=0, keepdims=True)

    return pl.pallas_call(
        body, name="loss_head", grid=(s // tm,),
        in_specs=[pl.BlockSpec((tm, d), lambda i: (i, 0)), pl.BlockSpec((tm, d), lambda i: (i, 0))],
        out_specs=[_full((8, LANE)), pl.BlockSpec((tm, d), lambda i: (i, 0))],
        out_shape=[jax.ShapeDtypeStruct((8, LANE), F32), jax.ShapeDtypeStruct((s, d), F32)],
        compiler_params=_params(("arbitrary",), 32),
    )(y, target)


def _row_tile(rows, limit=512):
    for cand in range(min(limit, rows) // 16 * 16, 0, -16):
        if rows % cand == 0:
            return cand
    return rows


def _reduce_adamw(parts, w, m, v, name):
    _, nl, a, b = parts.shape
    ta = _row_tile(a, 256)

    def body(p_ref, w_ref, m_ref, v_ref, g_ref, d_ref, mo_ref, vo_ref):
        g = p_ref[0].astype(F32)
        for k in range(1, N_DEV):
            g = g + p_ref[k].astype(F32)
        g_ref[...] = g
        mn = ADAM_B1 * m_ref[...] + (1.0 - ADAM_B1) * g
        vn = ADAM_B2 * v_ref[...] + (1.0 - ADAM_B2) * (g * g)
        mo_ref[...] = mn
        vo_ref[...] = vn
        m_hat = mn / (1.0 - ADAM_B1 ** ADAM_STEP)
        v_hat = vn / (1.0 - ADAM_B2 ** ADAM_STEP)
        d_ref[...] = -ADAM_LR * (m_hat / (jnp.sqrt(v_hat) + ADAM_EPS) + ADAM_WD * w_ref[...])

    blk = pl.BlockSpec((1, ta, b), lambda l, i: (l, i, 0))
    return pl.pallas_call(
        body, name=name, grid=(nl, a // ta),
        in_specs=[pl.BlockSpec((N_DEV, 1, ta, b), lambda l, i: (0, l, i, 0)), blk, blk, blk],
        out_specs=[blk, blk, blk, blk],
        out_shape=[jax.ShapeDtypeStruct((nl, a, b), F32)] * 4,
        compiler_params=_params(("arbitrary", "arbitrary"), 48),
    )(parts, w, m, v)


def _unshard(w8, layer, name):
    _, _, k, n = w8.shape
    tk = _row_tile(k, 256)

    def body(w_ref, o_ref):
        for d in range(N_DEV):
            o_ref[:, d * n:(d + 1) * n] = w_ref[d, 0]

    return pl.pallas_call(
        body, name=name, grid=(k // tk,),
        in_specs=[pl.BlockSpec((N_DEV, 1, tk, n), lambda i: (0, layer, i, 0))],
        out_specs=pl.BlockSpec((tk, N_DEV * n), lambda i: (i, 0)),
        out_shape=jax.ShapeDtypeStruct((k, N_DEV * n), w8.dtype),
        compiler_params=_params(("arbitrary",), 32),
    )(w8)


def _shard_cast(g, name):
    k, n8 = g.shape
    n = n8 // N_DEV
    tk = _row_tile(k, 256)

    def body(g_ref, o_ref):
        for d in range(N_DEV):
            o_ref[d] = g_ref[:, d * n:(d + 1) * n].astype(BF16)

    return pl.pallas_call(
        body, name=name, grid=(k // tk,),
        in_specs=[pl.BlockSpec((tk, n8), lambda i: (i, 0))],
        out_specs=pl.BlockSpec((N_DEV, tk, n), lambda i: (0, i, 0)),
        out_shape=jax.ShapeDtypeStruct((N_DEV, k, n), BF16),
        compiler_params=_params(("arbitrary",), 32),
    )(g)


def _mesh_pos():
    return lax.axis_index("x"), lax.axis_index("y"), lax.axis_index("c")


def _all_gather(shards, name):
    n = len(shards)

    def body(*refs):
        x_refs, out_refs = refs[:n], refs[n:2 * n]
        send_sems, recv_sems, local_sems = refs[2 * n:]
        x, y, cc = _mesh_pos()
        me, sibling = (x, y, cc), (x, y, 1 - cc)
        chips = [(1 - x, y), (x, 1 - y), (1 - x, 1 - y)]

        def copy(a, k, block, to, own=False):
            slot = out_refs[a].at[4 * block[0] + 2 * block[1] + block[2]]
            return pltpu.make_async_remote_copy(
                src_ref=x_refs[a] if own else slot, dst_ref=slot,
                send_sem=send_sems.at[7 * a + k], recv_sem=recv_sems.at[7 * a + k],
                device_id=to, device_id_type=pl.DeviceIdType.MESH)

        mine = [pltpu.make_async_copy(x_refs[a], out_refs[a].at[4 * x + 2 * y + cc], local_sems.at[a]) for a in range(n)]
        first = []
        for a in range(n):
            mine[a].start()
            first.append(copy(a, 0, me, sibling, own=True))
            first += [copy(a, 1 + j, me, (*chip, cc), own=True) for j, chip in enumerate(chips)]
        for cp in first:
            cp.start()
        passed = []
        for a in range(n):
            for j, chip in enumerate(chips):
                copy(a, 1 + j, (*chip, cc), me).wait_recv()
                passed.append(copy(a, 4 + j, (*chip, cc), sibling))
                passed[-1].start()
        for a in range(n):
            copy(a, 0, sibling, me).wait_recv()
            for j, chip in enumerate(chips):
                copy(a, 4 + j, (*chip, 1 - cc), me).wait_recv()
        for cp in first + passed:
            cp.wait_send()
        for cp in mine:
            cp.wait()

    any_spec = pl.BlockSpec(memory_space=pl.ANY)
    return pl.pallas_call(
        body, name=name,
        in_specs=[any_spec] * n,
        out_specs=[any_spec] * n,
        out_shape=[jax.ShapeDtypeStruct((N_DEV,) + s.shape, s.dtype) for s in shards],
        scratch_shapes=[pltpu.SemaphoreType.DMA((7 * n,)), pltpu.SemaphoreType.DMA((7 * n,)),
                        pltpu.SemaphoreType.DMA((n,))],
    )(*shards)


def _all_to_all(groups, name):
    counts = [len(g) for g in groups]
    flat = [a for g in groups for a in g]
    n = len(flat)

    def body(*refs):
        src, outs = refs[:n], refs[n:n + len(groups)]
        send_sems, recv_sems, local_sems = refs[n + len(groups):]
        x, y, cc = _mesh_pos()
        me = 4 * x + 2 * y + cc
        copies, mine = [], []
        a = 0
        for gi, cnt in enumerate(counts):
            for layer in range(cnt):
                dst = outs[gi].at[me, layer]
                mine.append(pltpu.make_async_copy(src[a].at[me], dst, local_sems.at[a]))
                for k in range(1, N_DEV):
                    px = 1 - x if k & 4 else x
                    py = 1 - y if k & 2 else y
                    pc = 1 - cc if k & 1 else cc
                    copies.append(pltpu.make_async_remote_copy(
                        src_ref=src[a].at[4 * px + 2 * py + pc], dst_ref=dst,
                        send_sem=send_sems.at[7 * a + k - 1], recv_sem=recv_sems.at[7 * a + k - 1],
                        device_id=(px, py, pc), device_id_type=pl.DeviceIdType.MESH))
                a += 1
        for cp in mine + copies:
            cp.start()
        for cp in copies + mine:
            cp.wait()

    any_spec = pl.BlockSpec(memory_space=pl.ANY)
    return pl.pallas_call(
        body, name=name,
        in_specs=[any_spec] * n,
        out_specs=[any_spec] * len(groups),
        out_shape=[jax.ShapeDtypeStruct((N_DEV, len(g)) + g[0].shape[1:], g[0].dtype) for g in groups],
        scratch_shapes=[pltpu.SemaphoreType.DMA((7 * n,)), pltpu.SemaphoreType.DMA((7 * n,)),
                        pltpu.SemaphoreType.DMA((n,))],
    )(*flat)


def _row(v):
    return v.reshape(1, -1)


def _group_ones():
    idx = jnp.arange(SB_WIDTH) // HEAD_DIM
    return (idx[:, None] == idx[None, :]).astype(BF16)


def _pad_rows(w, rows):
    return jnp.concatenate([w, jnp.zeros((rows - w.shape[0], w.shape[1]), w.dtype)], axis=0)


def _local_step(x, target, wt):
    scale = HEAD_DIM ** -0.5
    bd = _group_ones()
    tril = jnp.tril(jnp.ones((CHUNK, CHUNK), dtype=bool))
    saved = []
    for i in range(DEPTH):
        j = i // 2
        lay = {"x_mix": x}
        if i % 2 == 0:
            proj, h = _rms_matmul(x, _row(wt["mix_norm_g"][i]), wt["sb_w_in"][j], jnp.zeros((1, IN_WIDTH), F32), "in_proj")
            gq = _row(jnp.tile(wt["sb_q_norm_g"][j], SB_WIDTH // HEAD_DIM)) * scale
            gk = _row(jnp.tile(wt["sb_k_norm_g"][j], SB_WIDTH // HEAD_DIM))
            qn, kn, vb = _qk_prep(proj, gq * LOG2E, gk, bd)
            o, lsum = _sb_attn_fwd(qn, kn, vb)
            wm = jnp.where(tril[None], wt["sg_w_spatial"][j], 0.0)
            wmb = wm.astype(BF16)
            wmt = jnp.swapaxes(wm, 1, 2).astype(BF16)
            bt = jnp.repeat(wt["sg_b_spatial"][j].T, HEAD_DIM, axis=1)
            gz = _row(wt["sg_z_norm_g"][j])
            gg = _sgu_fwd(proj, gz, wmb, bt, bd)
            x, mix = _out_proj(x, o, gg, wt["hyb_w_out"][j], "out_proj")
            lay.update(proj=proj, h=h, gq=gq, gk=gk, qn=qn, kn=kn, vb=vb, lsum=lsum, wmb=wmb, wmt=wmt, bt=bt, gz=gz, mix=mix)
        else:
            p, h = _rms_matmul(x, _row(wt["mix_norm_g"][i]), wt["cv_w_pw1"][j], _row(wt["cv_b_pw1"][j]), "conf_pw1")
            wdw = _pad_rows(wt["cv_w_dw"][j], HALO)
            yc, y2 = _conf_mid_fwd(p, wdw, _row(wt["cv_b_dw"][j]), _row(wt["cv_ln_g"][j]), _row(wt["cv_ln_b"][j]))
            x = _res_matmul(x, y2, wt["cv_w_pw2"][j], _row(wt["cv_b_pw2"][j]), "conf_pw2")
            lay.update(p=p, h=h, wdw=wdw, yc=yc, y2=y2)
        lay["x_ffn"] = x
        fdw = _pad_rows(wt["ffn_w_dw"][i], 8)
        x, hf, u = _ffn_fwd(x, _row(wt["ffn_norm_g"][i]), wt["ffn_w_up"][i], fdw, _row(wt["ffn_b_dw"][i]),
                            wt["ffn_w_down"][i])
        lay.update(hf=hf, u=u, fdw=fdw)
        saved.append(lay)

    lpart, dy = _loss_head(x, target)
    loss = lpart[0, 0]

    gr = {k: [None] * len(v) for k, v in wt.items()}
    for i in reversed(range(DEPTH)):
        j = i // 2
        lay = saved[i]
        du, act, dfdw, dfb = _ffn_bwd(dy, lay["u"], lay["fdw"], _row(wt["ffn_b_dw"][i]), wt["ffn_w_down"][i])
        gr["ffn_w_down"][i] = _matmul_tn(act, dy, "ffn_dw_down")
        gr["ffn_w_up"][i] = _matmul_tn(lay["hf"], du, "ffn_dw_up")
        gr["ffn_w_dw"][i] = dfdw[:FFN_K]
        gr["ffn_b_dw"][i] = dfb[0]
        dy, dgf = _nt_rms_bwd(du, wt["ffn_w_up"][i], lay["x_ffn"], _row(wt["ffn_norm_g"][i]), dy, "ffn_dx")
        gr["ffn_norm_g"][i] = dgf[0]
        if i % 2 == 0:
            dmix = _matmul_nt(dy, wt["hyb_w_out"][j], "out_proj_dx")
            gr["hyb_w_out"][j] = _matmul_tn(lay["mix"], dy, "out_proj_dw")
            dqn, dkn, dv = _sb_attn_bwd(lay["qn"], lay["kn"], lay["vb"], lay["lsum"], dmix)
            dqk, dgq, dgk = _qk_bwd(lay["proj"], dqn, dkn, lay["gq"], lay["gk"] * LN2, bd)
            duz, dwm, dbt, dgz = _sgu_bwd(lay["proj"], dmix, lay["gz"], lay["wmb"], lay["wmt"], lay["bt"], bd)
            dproj = jnp.concatenate([dqk.astype(BF16), dv.astype(BF16), duz.astype(BF16)], axis=1)
            gr["sb_w_in"][j] = _matmul_tn(lay["h"], dproj, "in_proj_dw")
            gr["sb_q_norm_g"][j] = dgq.reshape(SB_WIDTH // HEAD_DIM, HEAD_DIM).sum(0) * scale
            gr["sb_k_norm_g"][j] = dgk.reshape(SB_WIDTH // HEAD_DIM, HEAD_DIM).sum(0) * LN2
            gr["sg_z_norm_g"][j] = dgz[0]
            gr["sg_w_spatial"][j] = jnp.where(tril[None], dwm, 0.0)
            gr["sg_b_spatial"][j] = dbt.reshape(CHUNK, SG_GROUPS, HEAD_DIM).sum(-1).T
            dy, dgm = _nt_rms_bwd(dproj, wt["sb_w_in"][j], lay["x_mix"], _row(wt["mix_norm_g"][i]), dy, "in_proj_dx")
        else:
            dy2 = _matmul_nt(dy, wt["cv_w_pw2"][j], "conf_pw2_dx")
            gr["cv_w_pw2"][j] = _matmul_tn(lay["y2"], dy, "conf_pw2_dw")
            dp, dwdw, dbdw, dlg, dlb, db1, db2 = _conf_mid_bwd(lay["p"], lay["yc"], dy2, dy, lay["wdw"],
                                                             _row(wt["cv_ln_g"][j]), _row(wt["cv_ln_b"][j]))
            gr["cv_w_pw1"][j] = _matmul_tn(lay["h"], dp, "conf_pw1_dw")
            gr["cv_w_dw"][j] = dwdw[:CONV_K]
            gr["cv_b_dw"][j] = dbdw[0]
            gr["cv_ln_g"][j] = dlg[0]
            gr["cv_ln_b"][j] = dlb[0]
            gr["cv_b_pw1"][j] = db1[0]
            gr["cv_b_pw2"][j] = db2[0]
            dy, dgm = _nt_rms_bwd(dp, wt["cv_w_pw1"][j], lay["x_mix"], _row(wt["mix_norm_g"][i]), dy, "conf_pw1_dx")
        gr["mix_norm_g"][i] = dgm[0]
    matmul_weights = ("sb_w_in", "hyb_w_out", "cv_w_pw1", "cv_w_pw2", "ffn_w_up", "ffn_w_down")
    grads = {k: (v if k in matmul_weights else jnp.stack(v)) for k, v in gr.items()}
    return loss, dy, grads


WEIGHTS = ["mix_norm_g", "sb_w_in", "sb_q_norm_g", "sb_k_norm_g", "sg_z_norm_g", "sg_w_spatial", "sg_b_spatial",
           "hyb_w_out", "cv_w_pw1", "cv_b_pw1", "cv_w_dw", "cv_b_dw", "cv_ln_g", "cv_ln_b", "cv_w_pw2", "cv_b_pw2",
           "ffn_norm_g", "ffn_w_up", "ffn_w_dw", "ffn_b_dw", "ffn_w_down"]
BIG = [("sb_w_in", "col"), ("hyb_w_out", "row"), ("cv_w_pw1", "col"), ("cv_w_pw2", "row"), ("ffn_w_up", "col"),
       ("ffn_w_down", "row")]
SMALL = ["cv_b_pw1", "cv_w_dw", "cv_b_dw", "cv_ln_g", "cv_ln_b", "cv_b_pw2", "ffn_w_dw"]
REPLICATED = ["mix_norm_g", "sb_q_norm_g", "sb_k_norm_g", "sg_z_norm_g", "sg_w_spatial", "sg_b_spatial", "ffn_norm_g",
              "ffn_b_dw"]


def _last_dim_blocks(full):
    t = jnp.moveaxis(full.reshape(full.shape[:-1] + (N_DEV, full.shape[-1] // N_DEV)), -2, 0)
    return t.reshape(N_DEV, -1)


def _from_last_dim_blocks(blocks, shard_shape):
    t = jnp.moveaxis(blocks.reshape((N_DEV,) + tuple(shard_shape)), 0, -2)
    return t.reshape(tuple(shard_shape[:-1]) + (N_DEV * shard_shape[-1],))


def _pack(arrays):
    lead = arrays[0].shape[:-1]
    flat = jnp.concatenate([a.astype(F32) for a in arrays], axis=-1)
    rows = -(-flat.shape[-1] // (16 * LANE)) * 16
    pad = rows * LANE - flat.shape[-1]
    if pad:
        flat = jnp.concatenate([flat, jnp.zeros(lead + (pad,), F32)], axis=-1)
    return flat.reshape(lead + (rows, LANE))


def _unpack(packed, shapes):
    flat = packed.reshape(-1)
    out, off = [], 0
    for shp in shapes:
        size = 1
        for dim in shp:
            size *= dim
        out.append(flat[off:off + size].reshape(shp))
        off += size
    return out


def kernel(x, mix_norm_g, sb_w_in, sb_q_norm_g, sb_k_norm_g, sg_z_norm_g, sg_w_spatial, sg_b_spatial, hyb_w_out, cv_w_pw1, cv_b_pw1, cv_w_dw, cv_b_dw, cv_ln_g, cv_ln_b, cv_w_pw2, cv_b_pw2, ffn_norm_g, ffn_w_up, ffn_w_dw, ffn_b_dw, ffn_w_down, loss_target, m_mix_norm_g, m_sb_w_in, m_sb_q_norm_g, m_sb_k_norm_g, m_sg_z_norm_g, m_sg_w_spatial, m_sg_b_spatial, m_hyb_w_out, m_cv_w_pw1, m_cv_b_pw1, m_cv_w_dw, m_cv_b_dw, m_cv_ln_g, m_cv_ln_b, m_cv_w_pw2, m_cv_b_pw2, m_ffn_norm_g, m_ffn_w_up, m_ffn_w_dw, m_ffn_b_dw, m_ffn_w_down, v_mix_norm_g, v_sb_w_in, v_sb_q_norm_g, v_sb_k_norm_g, v_sg_z_norm_g, v_sg_w_spatial, v_sg_b_spatial, v_hyb_w_out, v_cv_w_pw1, v_cv_b_pw1, v_cv_w_dw, v_cv_b_dw, v_cv_ln_g, v_cv_ln_b, v_cv_w_pw2, v_cv_b_pw2, v_ffn_norm_g, v_ffn_w_up, v_ffn_w_dw, v_ffn_b_dw, v_ffn_w_down):
    w = dict(zip(WEIGHTS, (mix_norm_g, sb_w_in, sb_q_norm_g, sb_k_norm_g, sg_z_norm_g, sg_w_spatial, sg_b_spatial,
                           hyb_w_out, cv_w_pw1, cv_b_pw1, cv_w_dw, cv_b_dw, cv_ln_g, cv_ln_b, cv_w_pw2, cv_b_pw2,
                           ffn_norm_g, ffn_w_up, ffn_w_dw, ffn_b_dw, ffn_w_down)))
    m = dict(zip(WEIGHTS, (m_mix_norm_g, m_sb_w_in, m_sb_q_norm_g, m_sb_k_norm_g, m_sg_z_norm_g, m_sg_w_spatial,
                           m_sg_b_spatial, m_hyb_w_out, m_cv_w_pw1, m_cv_b_pw1, m_cv_w_dw, m_cv_b_dw, m_cv_ln_g,
                           m_cv_ln_b, m_cv_w_pw2, m_cv_b_pw2, m_ffn_norm_g, m_ffn_w_up, m_ffn_w_dw, m_ffn_b_dw,
                           m_ffn_w_down)))
    v = dict(zip(WEIGHTS, (v_mix_norm_g, v_sb_w_in, v_sb_q_norm_g, v_sb_k_norm_g, v_sg_z_norm_g, v_sg_w_spatial,
                           v_sg_b_spatial, v_hyb_w_out, v_cv_w_pw1, v_cv_b_pw1, v_cv_w_dw, v_cv_b_dw, v_cv_ln_g,
                           v_cv_ln_b, v_cv_w_pw2, v_cv_b_pw2, v_ffn_norm_g, v_ffn_w_up, v_ffn_w_dw, v_ffn_b_dw,
                           v_ffn_w_down)))
    big_names = [n for n, _ in BIG]
    flat = lambda t, names: [t[n].reshape(-1) for n in names]

    gathered = _all_gather([w[n].astype(BF16) for n in big_names] + [_pack(flat(w, SMALL))], "gather_weights")
    wt = {n: w[n] for n in REPLICATED}
    for (n, kind), w8 in zip(BIG, gathered):
        layers = w[n].shape[0]
        if kind == "col":
            wt[n] = [_unshard(w8, l, "unshard_" + n) for l in range(layers)]
        else:
            wt[n] = [w8[:, l].reshape((N_DEV * w8.shape[2],) + w8.shape[3:]) for l in range(layers)]
    small_parts = zip(*[_unpack(gathered[-1][d], [(w[n].size,) for n in SMALL]) for d in range(N_DEV)])
    for n, parts in zip(SMALL, small_parts):
        wt[n] = _from_last_dim_blocks(jnp.stack(parts), w[n].shape)

    loss, gx, grads = _local_step(x[0], loss_target[0], wt)

    groups = []
    for n, kind in BIG:
        if kind == "col":
            groups.append([_shard_cast(g, "shard_" + n) for g in grads[n]])
        else:
            groups.append([g.astype(BF16).reshape((N_DEV, g.shape[0] // N_DEV) + g.shape[1:]) for g in grads[n]])
    recv_big = _all_to_all(groups, "exchange_grads")
    grep = _pack(flat(grads, REPLICATED))
    gsmall = _pack([_last_dim_blocks(grads[n]) for n in SMALL])
    rep_rows, small_rows = grep.shape[0], gsmall.shape[1]
    vec = jnp.concatenate([grep, gsmall.reshape(N_DEV * small_rows, LANE)], axis=0)
    vec_all = _all_gather([vec], "gather_small_grads")[0]
    me = 4 * lax.axis_index("x") + 2 * lax.axis_index("y") + lax.axis_index("c")
    recv_rep = vec_all[:, :rep_rows]
    recv_small = lax.dynamic_slice(vec_all, (0, rep_rows + small_rows * me, 0), (N_DEV, small_rows, LANE))

    out = {}
    kinds = ("grad", "delta", "new_m", "new_v")
    for n, parts in zip(big_names, recv_big):
        for kind, arr in zip(kinds, _reduce_adamw(parts, w[n], m[n], v[n], "adamw_" + n)):
            out[kind, n] = arr
    for names, recv, tag in ((SMALL, recv_small, "adamw_small"), (REPLICATED, recv_rep, "adamw_replicated")):
        res = _reduce_adamw(recv[:, None], _pack(flat(w, names))[None], _pack(flat(m, names))[None],
                            _pack(flat(v, names))[None], tag)
        shapes = [w[n].shape for n in names]
        for kind, packed in zip(kinds, res):
            for n, arr in zip(names, _unpack(packed[0], shapes)):
                out[kind, n] = arr

    loss = lax.psum(loss, ("x", "y", "c"))
    return (loss, gx[None], *[out[kind, n] for kind in kinds for n in WEIGHTS])
```

```python
import functools

import jax
import jax.numpy as jnp
from jax import lax
from jax.experimental import pallas as pl
from jax.experimental.pallas import tpu as pltpu

F32 = jnp.float32
BF16 = jnp.bfloat16

D_MODEL = 1024
HEAD_DIM = 64
SB_WIDTH = 512
SG_WIDTH = 512
SG_GROUPS = 8
IN_WIDTH = 3 * SB_WIDTH + 2 * SG_WIDTH
CHUNK = 128
CONV_K = 31
D_FF = 2816
FFN_K = 3
DEPTH = 4
EPS = 1e-6
N_DEV = 8
LANE = 128
HALO = 32
ATT_BLOCK = 256
FF_CHUNK = 256
CONV_ROWS = 32
CONV_LANES = 512
MIB = 2 ** 20

ADAM_LR = 0.001
ADAM_B1 = 0.9
ADAM_B2 = 0.999
ADAM_EPS = 1e-08
ADAM_WD = 0.01
ADAM_STEP = 10

LOG2E = 1.4426950408889634
LN2 = 0.6931471805599453

NT_DIMS = (((1,), (1,)), ((), ()))
TN_DIMS = (((0,), (0,)), ((), ()))


def _params(semantics, vmem_mib):
    return pltpu.CompilerParams(dimension_semantics=semantics, vmem_limit_bytes=vmem_mib * MIB)


def _full(shape):
    nd = len(shape)
    return pl.BlockSpec(shape, lambda *_: (0,) * nd)


def _sigmoid(x):
    return 1.0 / (1.0 + jnp.exp(-x))


def _gelu(x):
    return 0.5 * x * (1.0 + lax.erf(x * 0.7071067811865476))


def _gelu_grad(x):
    return 0.5 * (1.0 + lax.erf(x * 0.7071067811865476)) + x * jnp.exp(-0.5 * x * x) * 0.3989422804014327


def _silu_grad(x, s):
    return s * (1.0 + x * (1.0 - s))


def _dot(a, b):
    return jnp.dot(a, b, preferred_element_type=F32)


def _dot2(a, b):
    hi = a.astype(BF16)
    lo = (a - hi.astype(F32)).astype(BF16)
    return _dot(hi, b) + _dot(lo, b)


def _group_mean(t, bd):
    return _dot2(t, bd) * (1.0 / HEAD_DIM)


def _shift_down(v, prev8, s):
    top = pltpu.roll(jnp.concatenate([prev8, v[:8]], axis=0), s, 0)[8:16]
    return jnp.concatenate([top, pltpu.roll(v, s, 0)[8:]], axis=0)


def _shift_up(v, next8, s):
    n = v.shape[0]
    bottom = pltpu.roll(jnp.concatenate([v[n - 8:], next8], axis=0), 16 - s, 0)[0:8]
    return jnp.concatenate([pltpu.roll(v, n - s, 0)[: n - 8], bottom], axis=0)


def _mesh_pos():
    return lax.axis_index("x"), lax.axis_index("y"), lax.axis_index("c")


def _comm_scratch(n):
    return [pltpu.SemaphoreType.DMA((7 * n,)), pltpu.SemaphoreType.DMA((7 * n,)), pltpu.SemaphoreType.DMA((n,))]


class _Scatter:
    def __init__(self, src_refs, out_refs, send_sems, recv_sems, local_sems):
        x, y, cc = _mesh_pos()
        me = 4 * x + 2 * y + cc
        self.copies, self.mine = [], []
        for a, (src, out) in enumerate(zip(src_refs, out_refs)):
            self.mine.append(pltpu.make_async_copy(src.at[me], out.at[me], local_sems.at[a]))
            for k in range(1, N_DEV):
                px = 1 - x if k & 4 else x
                py = 1 - y if k & 2 else y
                pc = 1 - cc if k & 1 else cc
                self.copies.append(pltpu.make_async_remote_copy(
                    src_ref=src.at[4 * px + 2 * py + pc], dst_ref=out.at[me],
                    send_sem=send_sems.at[7 * a + k - 1], recv_sem=recv_sems.at[7 * a + k - 1],
                    device_id=(px, py, pc), device_id_type=pl.DeviceIdType.MESH))

    def start(self):
        for cp in self.mine + self.copies:
            cp.start()

    def finish(self):
        for cp in self.copies + self.mine:
            cp.wait()


class _Gather:
    def __init__(self, x_refs, out_refs, send_sems, recv_sems, local_sems):
        x, y, cc = _mesh_pos()
        self.n = len(x_refs)
        self.me, self.sibling, self.cc = (x, y, cc), (x, y, 1 - cc), cc
        self.chips = [(1 - x, y), (x, 1 - y), (1 - x, 1 - y)]
        self.x_refs, self.out_refs, self.send_sems, self.recv_sems = x_refs, out_refs, send_sems, recv_sems
        self.mine = [pltpu.make_async_copy(x_refs[a], out_refs[a].at[4 * x + 2 * y + cc], local_sems.at[a])
                     for a in range(self.n)]

    def copy(self, a, k, block, to, own=False):
        slot = self.out_refs[a].at[4 * block[0] + 2 * block[1] + block[2]]
        return pltpu.make_async_remote_copy(
            src_ref=self.x_refs[a] if own else slot, dst_ref=slot,
            send_sem=self.send_sems.at[7 * a + k], recv_sem=self.recv_sems.at[7 * a + k],
            device_id=to, device_id_type=pl.DeviceIdType.MESH)

    def first_hop(self, a):
        return [self.copy(a, 0, self.me, self.sibling, own=True)] + [
            self.copy(a, 1 + j, self.me, (*chip, self.cc), own=True) for j, chip in enumerate(self.chips)]

    def passed_on(self, a):
        return [self.copy(a, 4 + j, (*chip, self.cc), self.sibling) for j, chip in enumerate(self.chips)]

    def start(self):
        for a in range(self.n):
            self.mine[a].start()
        for a in range(self.n):
            for cp in self.first_hop(a):
                cp.start()

    def forward(self):
        for a in range(self.n):
            for j, chip in enumerate(self.chips):
                self.copy(a, 1 + j, (*chip, self.cc), self.me).wait_recv()
                self.copy(a, 4 + j, (*chip, self.cc), self.sibling).start()

    def finish(self):
        for a in range(self.n):
            self.copy(a, 0, self.sibling, self.me).wait_recv()
            for j, chip in enumerate(self.chips):
                self.copy(a, 4 + j, (*chip, 1 - self.cc), self.me).wait_recv()
        for a in range(self.n):
            for cp in self.first_hop(a) + self.passed_on(a):
                cp.wait_send()
        for cp in self.mine:
            cp.wait()


def _carried(body, n_in, n_out, arrays, kind, when):
    n = len(arrays)
    if n == 0:
        return body, [], [], [], []

    def wrapped(*refs):
        ins, srcs = refs[:n_in], refs[n_in:n_in + n]
        outs, landed = refs[n_in + n:n_in + n + n_out], refs[n_in + n + n_out:n_in + 2 * n + n_out]
        scratch = refs[n_in + 2 * n + n_out:]
        comm = (_Scatter if kind == "scatter" else _Gather)(srcs, landed, *scratch[-3:])
        pl.when(when["start"]())(comm.start)
        if kind == "gather":
            pl.when(when["forward"]())(comm.forward)
        body(*ins, *outs, *scratch[:-3])
        pl.when(when["finish"]())(comm.finish)

    any_spec = pl.BlockSpec(memory_space=pl.ANY)
    if kind == "scatter":
        shapes = [jax.ShapeDtypeStruct(a.shape, a.dtype) for a in arrays]
    else:
        shapes = [jax.ShapeDtypeStruct((N_DEV,) + a.shape, a.dtype) for a in arrays]
    return wrapped, [any_spec] * n, [any_spec] * n, shapes, _comm_scratch(n)


def _all_gather(shards, name):
    n = len(shards)

    def body(*refs):
        comm = _Gather(refs[:n], refs[n:2 * n], *refs[2 * n:])
        comm.start()
        comm.forward()
        comm.finish()

    any_spec = pl.BlockSpec(memory_space=pl.ANY)
    return pl.pallas_call(
        body, name=name, in_specs=[any_spec] * n, out_specs=[any_spec] * n,
        out_shape=[jax.ShapeDtypeStruct((N_DEV,) + s.shape, s.dtype) for s in shards],
        scratch_shapes=_comm_scratch(n),
    )(*shards)


def _all_to_all(arrays, name):
    n = len(arrays)

    def body(*refs):
        comm = _Scatter(refs[:n], refs[n:2 * n], *refs[2 * n:])
        comm.start()
        comm.finish()

    any_spec = pl.BlockSpec(memory_space=pl.ANY)
    return pl.pallas_call(
        body, name=name, in_specs=[any_spec] * n, out_specs=[any_spec] * n,
        out_shape=[jax.ShapeDtypeStruct(a.shape, a.dtype) for a in arrays],
        scratch_shapes=_comm_scratch(n),
    )(*arrays)


def _rms_matmul(x, g, w, b, name):
    s, d = x.shape
    n = w.shape[1]
    tm = min(512, s)

    def body(x_ref, g_ref, w_ref, b_ref, y_ref, h_ref):
        xv = x_ref[...]
        r = lax.rsqrt(jnp.mean(xv * xv, axis=-1, keepdims=True) + EPS)
        h = (xv * r * g_ref[...]).astype(BF16)
        h_ref[...] = h
        for c in range(0, n, 512):
            y_ref[:, c:c + 512] = _dot(h, w_ref[:, c:c + 512]) + b_ref[:, c:c + 512]

    return pl.pallas_call(
        body, name=name, grid=(s // tm,),
        in_specs=[pl.BlockSpec((tm, d), lambda i: (i, 0)), _full((1, d)), _full((d, n)), _full((1, n))],
        out_specs=[pl.BlockSpec((tm, n), lambda i: (i, 0)), pl.BlockSpec((tm, d), lambda i: (i, 0))],
        out_shape=[jax.ShapeDtypeStruct((s, n), F32), jax.ShapeDtypeStruct((s, d), BF16)],
        compiler_params=_params(("arbitrary",), 48),
    )(x, g, w, b)


def _matmul_nt(a, w, name):
    s, n = a.shape
    k = w.shape[0]
    tm = min(512, s)

    def body(a_ref, w_ref, o_ref):
        o_ref[...] = lax.dot_general(a_ref[...].astype(BF16), w_ref[...], NT_DIMS, preferred_element_type=F32)

    return pl.pallas_call(
        body, name=name, grid=(s // tm,),
        in_specs=[pl.BlockSpec((tm, n), lambda i: (i, 0)), _full((k, n))],
        out_specs=pl.BlockSpec((tm, k), lambda i: (i, 0)),
        out_shape=jax.ShapeDtypeStruct((s, k), F32),
        compiler_params=_params(("arbitrary",), 40),
    )(a, w)


def _matmul_tn(a, b, name):
    s, k = a.shape
    n = b.shape[1]
    ts = min(512, s)
    tn = 512
    steps = s // ts

    def body(a_ref, b_ref, o_ref):
        @pl.when(pl.program_id(1) == 0)
        def _():
            o_ref[...] = jnp.zeros_like(o_ref)

        o_ref[...] += lax.dot_general(a_ref[...].astype(BF16), b_ref[...].astype(BF16), TN_DIMS,
                                      preferred_element_type=F32)

    return pl.pallas_call(
        body, name=name, grid=(n // tn, steps),
        in_specs=[pl.BlockSpec((ts, k), lambda j, t: (t, 0)), pl.BlockSpec((ts, tn), lambda j, t: (t, j))],
        out_specs=pl.BlockSpec((k, tn), lambda j, t: (0, j)),
        out_shape=jax.ShapeDtypeStruct((k, n), F32),
        compiler_params=_params(("arbitrary", "arbitrary"), 48),
    )(a, b)


def _nt_rms_bwd(dp, w, x, g, dres, name):
    s, n = dp.shape
    d = x.shape[1]
    tm = min(256, s)

    def body(dp_ref, w_ref, x_ref, g_ref, dres_ref, dx_ref, dg_ref):
        @pl.when(pl.program_id(0) == 0)
        def _():
            dg_ref[...] = jnp.zeros_like(dg_ref)

        dh = lax.dot_general(dp_ref[...], w_ref[...], NT_DIMS, preferred_element_type=F32)
        xv = x_ref[...]
        r = lax.rsqrt(jnp.mean(xv * xv, axis=-1, keepdims=True) + EPS)
        xh = xv * r
        dg_ref[...] += jnp.sum(dh * xh, axis=0, keepdims=True)
        dn = dh * g_ref[...]
        dx_ref[...] = dres_ref[...] + r * (dn - xh * jnp.mean(dn * xh, axis=-1, keepdims=True))

    return pl.pallas_call(
        body, name=name, grid=(s // tm,),
        in_specs=[pl.BlockSpec((tm, n), lambda i: (i, 0)), _full((d, n)), pl.BlockSpec((tm, d), lambda i: (i, 0)),
                  _full((1, d)), pl.BlockSpec((tm, d), lambda i: (i, 0))],
        out_specs=[pl.BlockSpec((tm, d), lambda i: (i, 0)), _full((1, d))],
        out_shape=[jax.ShapeDtypeStruct((s, d), F32), jax.ShapeDtypeStruct((1, d), F32)],
        compiler_params=_params(("arbitrary",), 52),
    )(dp, w, x, g, dres)


def _res_matmul(x, a, w, b, name):
    s, d = x.shape
    k = a.shape[1]
    tm = min(512, s)

    def body(x_ref, a_ref, w_ref, b_ref, o_ref):
        o_ref[...] = x_ref[...] + _dot(a_ref[...], w_ref[...]) + b_ref[...]

    return pl.pallas_call(
        body, name=name, grid=(s // tm,),
        in_specs=[pl.BlockSpec((tm, d), lambda i: (i, 0)), pl.BlockSpec((tm, k), lambda i: (i, 0)), _full((k, d)),
                  _full((1, d))],
        out_specs=pl.BlockSpec((tm, d), lambda i: (i, 0)),
        out_shape=jax.ShapeDtypeStruct((s, d), F32),
        compiler_params=_params(("arbitrary",), 32),
    )(x, a, w, b)


def _out_proj(x, o, gg, w, name):
    s, d = x.shape
    tm = min(512, s)

    def body(x_ref, o_ref, gg_ref, w_ref, y_ref, mix_ref):
        mix = jnp.concatenate([o_ref[...], gg_ref[...]], axis=1).astype(BF16)
        mix_ref[...] = mix
        y_ref[...] = x_ref[...] + _dot(mix, w_ref[...])

    return pl.pallas_call(
        body, name=name, grid=(s // tm,),
        in_specs=[pl.BlockSpec((tm, d), lambda i: (i, 0)), pl.BlockSpec((tm, SB_WIDTH), lambda i: (i, 0)),
                  pl.BlockSpec((tm, SG_WIDTH), lambda i: (i, 0)), _full((d, d))],
        out_specs=[pl.BlockSpec((tm, d), lambda i: (i, 0)), pl.BlockSpec((tm, d), lambda i: (i, 0))],
        out_shape=[jax.ShapeDtypeStruct((s, d), F32), jax.ShapeDtypeStruct((s, d), BF16)],
        compiler_params=_params(("arbitrary",), 32),
    )(x, o, gg, w)


def _qk_prep(proj, gq, gk, bd):
    s = proj.shape[0]
    tm = min(512, s)

    def body(q_ref, k_ref, v_ref, gq_ref, gk_ref, bd_ref, qn_ref, kn_ref, vb_ref):
        bdv = bd_ref[...]
        q = q_ref[...]
        k = k_ref[...]
        qn_ref[...] = (q * lax.rsqrt(_group_mean(q * q, bdv) + EPS) * gq_ref[...]).astype(BF16)
        kn_ref[...] = (k * lax.rsqrt(_group_mean(k * k, bdv) + EPS) * gk_ref[...]).astype(BF16)
        vb_ref[...] = v_ref[...].astype(BF16)

    col = lambda c: pl.BlockSpec((tm, SB_WIDTH), lambda i: (i, c))
    out = pl.BlockSpec((tm, SB_WIDTH), lambda i: (i, 0))
    return pl.pallas_call(
        body, name="qk_prep", grid=(s // tm,),
        in_specs=[col(0), col(1), col(2), _full((1, SB_WIDTH)), _full((1, SB_WIDTH)), _full((SB_WIDTH, SB_WIDTH))],
        out_specs=[out, out, out],
        out_shape=[jax.ShapeDtypeStruct((s, SB_WIDTH), BF16)] * 3,
        compiler_params=_params(("arbitrary",), 32),
    )(proj, proj, proj, gq, gk, bd)


def _qk_bwd(proj, dqn, dkn, gq, gk, bd):
    s = proj.shape[0]
    tm = min(512, s)

    def body(q_ref, k_ref, dq_ref, dk_ref, gq_ref, gk_ref, bd_ref, o_ref, dgq_ref, dgk_ref):
        @pl.when(pl.program_id(0) == 0)
        def _():
            dgq_ref[...] = jnp.zeros_like(dgq_ref)
            dgk_ref[...] = jnp.zeros_like(dgk_ref)

        bdv = bd_ref[...]

        def back(t, gain, dout, dg_ref):
            r = lax.rsqrt(_group_mean(t * t, bdv) + EPS)
            th = t * r
            dg_ref[...] += jnp.sum(dout * th, axis=0, keepdims=True)
            dn = dout * gain
            return r * (dn - th * _group_mean(dn * th, bdv))

        o_ref[:, 0:SB_WIDTH] = back(q_ref[...], gq_ref[...], dq_ref[...], dgq_ref)
        o_ref[:, SB_WIDTH:2 * SB_WIDTH] = back(k_ref[...], gk_ref[...], dk_ref[...], dgk_ref)

    col = lambda c: pl.BlockSpec((tm, SB_WIDTH), lambda i: (i, c))
    row = pl.BlockSpec((tm, SB_WIDTH), lambda i: (i, 0))
    return pl.pallas_call(
        body, name="qk_bwd", grid=(s // tm,),
        in_specs=[col(0), col(1), row, row, _full((1, SB_WIDTH)), _full((1, SB_WIDTH)), _full((SB_WIDTH, SB_WIDTH))],
        out_specs=[pl.BlockSpec((tm, 2 * SB_WIDTH), lambda i: (i, 0)), _full((1, SB_WIDTH)), _full((1, SB_WIDTH))],
        out_shape=[jax.ShapeDtypeStruct((s, 2 * SB_WIDTH), F32), jax.ShapeDtypeStruct((1, SB_WIDTH), F32),
                   jax.ShapeDtypeStruct((1, SB_WIDTH), F32)],
        compiler_params=_params(("arbitrary",), 40),
    )(proj, proj, dqn, dkn, gq, gk, bd)


def _attn_masks(tq, tk):
    lane = lax.broadcasted_iota(jnp.int32, (tk, LANE), 1)
    heads = [(lane >= hh * HEAD_DIM) & (lane < (hh + 1) * HEAD_DIM) for hh in range(2)]
    urow = lax.broadcasted_iota(jnp.int32, (tk, tk), 0)
    ucol = lax.broadcasted_iota(jnp.int32, (tk, tk), 1)
    return heads, urow, ucol


def _keep_cost(z2):
    return jnp.log(1.0 + jnp.exp2(jnp.minimum(z2, 126.0)))


def _sb_attn_fwd(q2, kn, vb, gather=()):
    s = q2.shape[0]
    tk = min(ATT_BLOCK, s)
    tq = min(2 * ATT_BLOCK, s)
    r = tq // tk
    nq = s // tq
    assert s // tk <= LANE

    def body(q_ref, k_ref, v_ref, o_ref, ls_ref):
        i = pl.program_id(1)
        heads, urow, ucol = _attn_masks(tq, tk)
        u_incl = (urow >= ucol).astype(BF16)
        row = lax.broadcasted_iota(jnp.int32, (tq, tk), 0)
        col = lax.broadcasted_iota(jnp.int32, (tq, tk), 1)
        qlane = lax.broadcasted_iota(jnp.int32, (tq, LANE), 1)
        q = q_ref[...]

        def blocks(kbs, state, masked):
            carry, acc, ls = list(state[0:2]), state[2], list(state[3:5])
            chains = [(d, hh) for d in range(len(kbs)) for hh in range(2)]
            kblk, vblk, valid = {}, {}, {}
            for d, kb in enumerate(kbs):
                off = pl.multiple_of(kb * tk, tk)
                kfull = k_ref[pl.ds(off, tk), :]
                vfull = v_ref[pl.ds(off, tk), :]
                if masked:
                    valid[d] = (kb * tk + col) < (i * tq + row)
                for hh in range(2):
                    kblk[d, hh] = jnp.where(heads[hh], kfull, jnp.zeros((), BF16))
                    vblk[d, hh] = jnp.where(heads[hh], vfull, jnp.zeros((), BF16))
            z2 = {c: lax.dot_general(q, kblk[c], NT_DIMS, preferred_element_type=F32) for c in chains}
            cost = {}
            for c in chains:
                cost[c] = _keep_cost(z2[c])
                if masked:
                    cost[c] = jnp.where(valid[c[0]], cost[c], 0.0)
            sums = {c: _dot(cost[c].astype(BF16), u_incl) for c in chains}
            a = {}
            for d, hh in chains:
                rin = carry[hh] + sums[d, hh]
                a[d, hh] = jnp.exp2(z2[d, hh] - rin * LOG2E)
                if masked:
                    a[d, hh] = jnp.where(valid[d], a[d, hh], 0.0)
                rs = jnp.sum(cost[d, hh], axis=1, keepdims=True)
                ls[hh] = ls[hh] + jnp.where(qlane == kbs[d], rs, 0.0)
                carry[hh] = carry[hh] + rs
            for c in chains:
                acc = acc + _dot(a[c].astype(BF16), vblk[c])
            return carry[0], carry[1], acc, ls[0], ls[1]

        zc = jnp.zeros((tq, 1), F32)
        zt = jnp.zeros((tq, LANE), F32)
        state = blocks([r * i + r - 1 - d for d in range(r)], (zc, zc, zt, zt, zt), True)
        state = lax.fori_loop(
            0, i, lambda n, st: blocks([r * (i - 1 - n) + r - 1 - d for d in range(r)], st, False), state)
        o_ref[...] = state[2]
        ls_ref[:, 0:LANE] = state[3]
        ls_ref[:, LANE:2 * LANE] = state[4]

    last_pair = SB_WIDTH // LANE - 1
    step_is = lambda p, i: lambda: (pl.program_id(0) == p) & (pl.program_id(1) == i)
    when = {"start": step_is(0, 0), "forward": step_is(last_pair, 0), "finish": step_is(last_pair, nq - 1)}
    kern, ins, outs, shapes, sems = _carried(body, 3, 2, gather, "gather", when)
    res = pl.pallas_call(
        kern, name="sb_attn_fwd", grid=(SB_WIDTH // LANE, nq),
        in_specs=[pl.BlockSpec((tq, LANE), lambda p, i: (i, p)), pl.BlockSpec((s, LANE), lambda p, i: (0, p)),
                  pl.BlockSpec((s, LANE), lambda p, i: (0, p))] + ins,
        out_specs=[pl.BlockSpec((tq, LANE), lambda p, i: (i, p)),
                   pl.BlockSpec((tq, 2 * LANE), lambda p, i: (i, p))] + outs,
        out_shape=[jax.ShapeDtypeStruct((s, SB_WIDTH), F32), jax.ShapeDtypeStruct((s, 2 * SB_WIDTH), F32)] + shapes,
        scratch_shapes=sems,
        compiler_params=_params(("arbitrary", "arbitrary"), 48),
    )(q2, kn, vb, *gather)
    return res[0], res[1], list(res[2:])


def _sb_attn_bwd(q2, kn, vb, lsum, dmix, send=()):
    s = q2.shape[0]
    tk = min(ATT_BLOCK, s)
    tq = min(2 * ATT_BLOCK, s)
    r = tq // tk
    nq = s // tq

    def body(q_ref, k_ref, v_ref, ls_ref, do_ref, dq_ref, dk_ref, dv_ref):
        i = pl.program_id(1)

        @pl.when(i == 0)
        def _():
            dk_ref[...] = jnp.zeros_like(dk_ref)
            dv_ref[...] = jnp.zeros_like(dv_ref)

        heads, urow, ucol = _attn_masks(tq, tk)
        u_incl = (urow >= ucol).astype(BF16)
        u_pre = (urow <= ucol).astype(BF16)
        lrow = lax.broadcasted_iota(jnp.int32, (LANE, LANE), 0)
        lcol = lax.broadcasted_iota(jnp.int32, (LANE, LANE), 1)
        u_after = (lrow > lcol).astype(BF16)
        row = lax.broadcasted_iota(jnp.int32, (tq, tk), 0)
        col = lax.broadcasted_iota(jnp.int32, (tq, tk), 1)
        qlane = lax.broadcasted_iota(jnp.int32, (tq, LANE), 1)
        qheads = [(qlane >= hh * HEAD_DIM) & (qlane < (hh + 1) * HEAD_DIM) for hh in range(2)]
        q = q_ref[...]
        dob = do_ref[...].astype(BF16)
        qm = [jnp.where(qheads[hh], q, jnp.zeros((), BF16)) for hh in range(2)]
        dom = [jnp.where(qheads[hh], dob, jnp.zeros((), BF16)) for hh in range(2)]
        after = []
        for hh in range(2):
            ls = ls_ref[:, hh * LANE:(hh + 1) * LANE]
            hi = ls.astype(BF16)
            mid = (ls - hi.astype(F32)).astype(BF16)
            lo = (ls - hi.astype(F32) - mid.astype(F32)).astype(BF16)
            after.append(_dot(hi, u_after) + _dot(mid, u_after) + _dot(lo, u_after))

        def blocks(kbs, state, masked):
            cp, dq = list(state[0:2]), state[2]
            chains = [(d, hh) for d in range(len(kbs)) for hh in range(2)]
            offs, kblk, vblk, valid = {}, {}, {}, {}
            for d, kb in enumerate(kbs):
                offs[d] = pl.multiple_of(kb * tk, tk)
                kfull = k_ref[pl.ds(offs[d], tk), :]
                vblk[d] = v_ref[pl.ds(offs[d], tk), :]
                if masked:
                    valid[d] = (kb * tk + col) < (i * tq + row)
                for hh in range(2):
                    kblk[d, hh] = jnp.where(heads[hh], kfull, jnp.zeros((), BF16))
            z2 = {c: lax.dot_general(q, kblk[c], NT_DIMS, preferred_element_type=F32) for c in chains}
            da = {(d, hh): lax.dot_general(dom[hh], vblk[d], NT_DIMS, preferred_element_type=F32) for d, hh in chains}
            cost, sig = {}, {}
            for c in chains:
                cost[c] = _keep_cost(z2[c])
                sig[c] = jnp.exp2(z2[c] - cost[c] * LOG2E)
                if masked:
                    cost[c] = jnp.where(valid[c[0]], cost[c], 0.0)
            sums = {c: _dot(cost[c].astype(BF16), u_incl) for c in chains}
            a, g = {}, {}
            for d, hh in chains:
                cr = jnp.sum(jnp.where(qlane == kbs[d], after[hh], 0.0), axis=1, keepdims=True)
                a[d, hh] = jnp.exp2(z2[d, hh] - (cr + sums[d, hh]) * LOG2E)
                if masked:
                    a[d, hh] = jnp.where(valid[d], a[d, hh], 0.0)
                g[d, hh] = da[d, hh] * a[d, hh]
            pre = {c: _dot(g[c].astype(BF16), u_pre) for c in chains}
            dzb = {}
            for d, hh in chains:
                dz = g[d, hh] - sig[d, hh] * (cp[hh] + pre[d, hh])
                if masked:
                    dz = jnp.where(valid[d], dz, 0.0)
                dzb[d, hh] = dz.astype(BF16)
                cp[hh] = cp[hh] + jnp.sum(g[d, hh], axis=1, keepdims=True)
            for d in range(len(kbs)):
                dv_ref[pl.ds(offs[d], tk), :] += sum(
                    lax.dot_general(a[d, hh].astype(BF16), dom[hh], TN_DIMS, preferred_element_type=F32) for hh in range(2))
                dk_ref[pl.ds(offs[d], tk), :] += sum(
                    lax.dot_general(dzb[d, hh], qm[hh], TN_DIMS, preferred_element_type=F32) for hh in range(2))
            for c in chains:
                dq = dq + _dot(dzb[c], kblk[c])
            return cp[0], cp[1], dq

        zc = jnp.zeros((tq, 1), F32)
        state = lax.fori_loop(0, i, lambda n, st: blocks([r * n + d for d in range(r)], st, False),
                              (zc, zc, jnp.zeros((tq, LANE), F32)))
        state = blocks([r * i + d for d in range(r)], state, True)
        dq_ref[...] = state[2]

    blk = pl.BlockSpec((tq, LANE), lambda p, i: (i, p))
    whole = pl.BlockSpec((s, LANE), lambda p, i: (0, p))
    last_pair = SB_WIDTH // LANE - 1
    step_is = lambda p, i: lambda: (pl.program_id(0) == p) & (pl.program_id(1) == i)
    kern, ins, outs, shapes, sems = _carried(body, 5, 3, send, "scatter",
                                             {"start": step_is(0, 0), "finish": step_is(last_pair, nq - 1)})
    res = pl.pallas_call(
        kern, name="sb_attn_bwd", grid=(SB_WIDTH // LANE, nq),
        in_specs=[blk, whole, whole, pl.BlockSpec((tq, 2 * LANE), lambda p, i: (i, p)), blk] + ins,
        out_specs=[blk, whole, whole] + outs,
        out_shape=[jax.ShapeDtypeStruct((s, SB_WIDTH), F32)] * 3 + shapes,
        scratch_shapes=sems,
        compiler_params=_params(("arbitrary", "arbitrary"), 56),
    )(q2, kn, vb, lsum, dmix, *send)
    return res[0], res[1], res[2], list(res[3:])


def _sgu_spatial(zn, wm_ref, lane, c):
    parts = []
    for p in range(SG_WIDTH // LANE):
        blk = zn[c * CHUNK:(c + 1) * CHUNK, p * LANE:(p + 1) * LANE].astype(BF16)
        lo = jnp.where(lane < HEAD_DIM, blk, jnp.zeros((), BF16))
        hi = jnp.where(lane >= HEAD_DIM, blk, jnp.zeros((), BF16))
        parts.append(_dot(wm_ref[2 * p], lo) + _dot(wm_ref[2 * p + 1], hi))
    return jnp.concatenate(parts, axis=1)


def _sgu_fwd(proj, gz, wm, bt, bd):
    s = proj.shape[0]
    tm = min(512, s)

    def body(u_ref, z_ref, gz_ref, wm_ref, bt_ref, bd_ref, o_ref):
        lane = lax.broadcasted_iota(jnp.int32, (CHUNK, LANE), 1)
        ug = _gelu(u_ref[...])
        zg = _gelu(z_ref[...])
        zn = zg * lax.rsqrt(_group_mean(zg * zg, bd_ref[...]) + EPS) * gz_ref[...]
        for c in range(tm // CHUNK):
            sp = _sgu_spatial(zn, wm_ref, lane, c) + bt_ref[...]
            o_ref[c * CHUNK:(c + 1) * CHUNK, :] = ug[c * CHUNK:(c + 1) * CHUNK, :] * sp

    col = lambda c: pl.BlockSpec((tm, SG_WIDTH), lambda i: (i, c))
    return pl.pallas_call(
        body, name="sgu_fwd", grid=(s // tm,),
        in_specs=[col(3), col(4), _full((1, SG_WIDTH)), _full((SG_GROUPS, CHUNK, CHUNK)), _full((CHUNK, SG_WIDTH)),
                  _full((SG_WIDTH, SG_WIDTH))],
        out_specs=pl.BlockSpec((tm, SG_WIDTH), lambda i: (i, 0)),
        out_shape=jax.ShapeDtypeStruct((s, SG_WIDTH), F32),
        compiler_params=_params(("arbitrary",), 32),
    )(proj, proj, gz, wm, bt, bd)


def _sgu_bwd(proj, dmix, gz, wm, wmt, bt, bd):
    s = proj.shape[0]
    tm = min(512, s)

    def body(u_ref, z_ref, dg_ref, gz_ref, wm_ref, wmt_ref, bt_ref, bd_ref, o_ref, dwm_ref, dbt_ref, dgz_ref):
        @pl.when(pl.program_id(0) == 0)
        def _():
            dwm_ref[...] = jnp.zeros_like(dwm_ref)
            dbt_ref[...] = jnp.zeros_like(dbt_ref)
            dgz_ref[...] = jnp.zeros_like(dgz_ref)

        lane = lax.broadcasted_iota(jnp.int32, (CHUNK, LANE), 1)
        bdv = bd_ref[...]
        u = u_ref[...]
        z = z_ref[...]
        ug = _gelu(u)
        zg = _gelu(z)
        r = lax.rsqrt(_group_mean(zg * zg, bdv) + EPS)
        zh = zg * r
        zn = zh * gz_ref[...]
        dzn_rows = []
        for c in range(tm // CHUNK):
            rows = slice(c * CHUNK, (c + 1) * CHUNK)
            sp = _sgu_spatial(zn, wm_ref, lane, c) + bt_ref[...]
            dgg = dg_ref[rows, :]
            ds = dgg * ug[rows, :]
            o_ref[rows, 0:SG_WIDTH] = dgg * sp * _gelu_grad(u[rows, :])
            dbt_ref[...] += ds
            parts = []
            for p in range(SG_WIDTH // LANE):
                dsb = ds[:, p * LANE:(p + 1) * LANE].astype(BF16)
                znb = zn[rows, p * LANE:(p + 1) * LANE].astype(BF16)
                acc = jnp.zeros((CHUNK, LANE), F32)
                for hh in range(2):
                    hm = (lane >= hh * HEAD_DIM) & (lane < (hh + 1) * HEAD_DIM)
                    dsm = jnp.where(hm, dsb, jnp.zeros((), BF16))
                    znm = jnp.where(hm, znb, jnp.zeros((), BF16))
                    acc = acc + _dot(wmt_ref[2 * p + hh], dsm)
                    dwm_ref[2 * p + hh] += lax.dot_general(dsm, znm, NT_DIMS, preferred_element_type=F32)
                parts.append(acc)
            dzn_rows.append(jnp.concatenate(parts, axis=1))
        dzn = jnp.concatenate(dzn_rows, axis=0)
        dgz_ref[...] += jnp.sum(dzn * zh, axis=0, keepdims=True)
        dn = dzn * gz_ref[...]
        o_ref[:, SG_WIDTH:2 * SG_WIDTH] = r * (dn - zh * _group_mean(dn * zh, bdv)) * _gelu_grad(z)

    col = lambda c: pl.BlockSpec((tm, SG_WIDTH), lambda i: (i, c))
    wspec = _full((SG_GROUPS, CHUNK, CHUNK))
    return pl.pallas_call(
        body, name="sgu_bwd", grid=(s // tm,),
        in_specs=[col(3), col(4), pl.BlockSpec((tm, SG_WIDTH), lambda i: (i, 1)), _full((1, SG_WIDTH)), wspec, wspec,
                  _full((CHUNK, SG_WIDTH)), _full((SG_WIDTH, SG_WIDTH))],
        out_specs=[pl.BlockSpec((tm, 2 * SG_WIDTH), lambda i: (i, 0)), wspec, _full((CHUNK, SG_WIDTH)),
                   _full((1, SG_WIDTH))],
        out_shape=[jax.ShapeDtypeStruct((s, 2 * SG_WIDTH), F32), jax.ShapeDtypeStruct((SG_GROUPS, CHUNK, CHUNK), F32),
                   jax.ShapeDtypeStruct((CHUNK, SG_WIDTH), F32), jax.ShapeDtypeStruct((1, SG_WIDTH), F32)],
        compiler_params=_params(("arbitrary",), 40),
    )(proj, proj, dmix, gz, wm, wmt, bt, bd)


def _shifted_copies(ext, sh):
    e = ext[...]
    sh[0] = e
    for b in range(1, 8):
        sh[b] = pltpu.roll(e, e.shape[0] - b, 0)


def _shifted_rows(sh, off, r0, cols):
    start = r0 + off - off % 8
    return sh[off % 8, start:start + CONV_ROWS, cols]


def _dwconv(sh, w_ref, offsets, rows, store):
    for r0 in range(0, rows, CONV_ROWS):
        for c0 in range(0, sh.shape[2], CONV_LANES):
            cols = slice(c0, c0 + CONV_LANES)
            acc = jnp.zeros((CONV_ROWS, CONV_LANES), F32)
            for j, off in enumerate(offsets):
                acc = acc + w_ref[j:j + 1, cols] * _shifted_rows(sh, off, r0, cols)
            store(slice(r0, r0 + CONV_ROWS), cols, acc)


def _conf_mid_fwd(p, wdw, bdw, lng, lnb):
    s = p.shape[0]
    c = p.shape[1] // 2
    tm = min(256, s)

    def body(a_ref, gt_ref, w_ref, b_ref, g_ref, beta_ref, yc_ref, y2_ref, ext, sh):
        i = pl.program_id(0)

        @pl.when(i == 0)
        def _():
            ext[0:HALO, :] = jnp.zeros((HALO, c), F32)

        @pl.when(i > 0)
        def _():
            ext[0:HALO, :] = ext[tm:tm + HALO, :]

        ext[HALO:HALO + tm, :] = a_ref[...] * _sigmoid(gt_ref[...])

        def store(rows, cols, block):
            yc_ref[rows, cols] = block + b_ref[:, cols]

        _shifted_copies(ext, sh)
        _dwconv(sh, w_ref, [HALO - CONV_K + 1 + j for j in range(CONV_K)], tm, store)
        acc = yc_ref[...]
        xc = acc - jnp.mean(acc, axis=-1, keepdims=True)
        ln = xc * lax.rsqrt(jnp.mean(xc * xc, axis=-1, keepdims=True) + EPS) * g_ref[...] + beta_ref[...]
        y2_ref[...] = (ln * _sigmoid(ln)).astype(BF16)

    vec = _full((1, c))
    return pl.pallas_call(
        body, name="conf_mid_fwd", grid=(s // tm,),
        in_specs=[pl.BlockSpec((tm, c), lambda i: (i, 0)), pl.BlockSpec((tm, c), lambda i: (i, 1)), _full((HALO, c)), vec,
                  vec, vec],
        out_specs=[pl.BlockSpec((tm, c), lambda i: (i, 0)), pl.BlockSpec((tm, c), lambda i: (i, 0))],
        out_shape=[jax.ShapeDtypeStruct((s, c), F32), jax.ShapeDtypeStruct((s, c), BF16)],
        scratch_shapes=[pltpu.VMEM((HALO + tm, c), F32), pltpu.VMEM((8, HALO + tm, c), F32)],
        compiler_params=_params(("arbitrary",), 40),
    )(p, p, wdw, bdw, lng, lnb)


def _conf_mid_bwd(p, yc, dy2, dout, wdw, lng, lnb, send=()):
    s = p.shape[0]
    c = p.shape[1] // 2
    tm = min(256, s)
    n = s // tm
    hb = tm // HALO

    def body(a_ref, gt_ref, ah_ref, gh_ref, yc_ref, dy2_ref, dout_ref, w_ref, g_ref, beta_ref,
             dp_ref, dw_ref, dbdw_ref, dlg_ref, dlb_ref, db1_ref, db2_ref, exty, extd, dyv, dwacc, shy, shd):
        i = pl.program_id(0)

        @pl.when(i == 0)
        def _():
            extd[tm:tm + HALO, :] = jnp.zeros((HALO, c), F32)
            for ref in (dw_ref, dbdw_ref, dlg_ref, dlb_ref, db1_ref, db2_ref, dwacc):
                ref[...] = jnp.zeros_like(ref)

        @pl.when(i > 0)
        def _():
            extd[tm:tm + HALO, :] = extd[0:HALO, :]

        a = a_ref[...]
        sg = _sigmoid(gt_ref[...])
        exty[HALO:HALO + tm, :] = a * sg
        exty[0:HALO, :] = jnp.where(i < n - 1, ah_ref[...] * _sigmoid(gh_ref[...]), 0.0)
        ycv = yc_ref[...]
        xc = ycv - jnp.mean(ycv, axis=-1, keepdims=True)
        rstd = lax.rsqrt(jnp.mean(xc * xc, axis=-1, keepdims=True) + EPS)
        xh = xc * rstd
        ln = xh * g_ref[...] + beta_ref[...]
        dln = dy2_ref[...] * _silu_grad(ln, _sigmoid(ln))
        dlg_ref[...] += jnp.sum(dln * xh, axis=0, keepdims=True)
        dlb_ref[...] += jnp.sum(dln, axis=0, keepdims=True)
        dxh = dln * g_ref[...]
        dyc = rstd * (dxh - jnp.mean(dxh, axis=-1, keepdims=True) - xh * jnp.mean(dxh * xh, axis=-1, keepdims=True))
        extd[0:tm, :] = dyc
        dbdw_ref[...] += jnp.sum(dyc, axis=0, keepdims=True)
        db2_ref[...] += jnp.sum(dout_ref[...], axis=0, keepdims=True)

        def store(rows, cols, block):
            dyv[rows, cols] = block

        _shifted_copies(extd, shd)
        _shifted_copies(exty, shy)
        _dwconv(shd, w_ref, [CONV_K - 1 - j for j in range(CONV_K)], tm, store)
        for r0 in range(0, tm, CONV_ROWS):
            for c0 in range(0, c, CONV_LANES):
                cols = slice(c0, c0 + CONV_LANES)
                dsub = extd[r0:r0 + CONV_ROWS, cols]
                for j in range(CONV_K):
                    prod = dsub * _shifted_rows(shy, HALO - CONV_K + 1 + j, r0, cols)
                    dwacc[8 * j:8 * j + 8, cols] += prod.reshape(CONV_ROWS // 8, 8, CONV_LANES).sum(axis=0)

        @pl.when(i == n - 1)
        def _():
            dw_ref[...] = dwacc[...].reshape(HALO, 8, c).sum(axis=1)

        dy = dyv[...]
        da = dy * sg
        dgt = dy * a * sg * (1.0 - sg)
        dp_ref[:, 0:c] = da.astype(BF16)
        dp_ref[:, c:2 * c] = dgt.astype(BF16)
        db1_ref[:, 0:c] += jnp.sum(da, axis=0, keepdims=True)
        db1_ref[:, c:2 * c] += jnp.sum(dgt, axis=0, keepdims=True)

    rev = lambda col: pl.BlockSpec((tm, c), lambda i: (n - 1 - i, col))
    halo = lambda col: pl.BlockSpec((HALO, c), lambda i: (jnp.maximum((n - 1 - i) * hb - 1, 0), col))
    vec = _full((1, c))
    step_is = lambda i: lambda: pl.program_id(0) == i
    kern, ins, outs, shapes, sems = _carried(body, 10, 7, send, "scatter",
                                             {"start": step_is(0), "finish": step_is(n - 1)})
    res = pl.pallas_call(
        kern, name="conf_mid_bwd", grid=(n,),
        in_specs=[rev(0), rev(1), halo(0), halo(1), rev(0), rev(0), rev(0), _full((HALO, c)), vec, vec] + ins,
        out_specs=[pl.BlockSpec((tm, 2 * c), lambda i: (n - 1 - i, 0)), _full((HALO, c)), vec, vec, vec,
                   _full((1, 2 * c)), vec] + outs,
        out_shape=[jax.ShapeDtypeStruct((s, 2 * c), BF16), jax.ShapeDtypeStruct((HALO, c), F32),
                   jax.ShapeDtypeStruct((1, c), F32), jax.ShapeDtypeStruct((1, c), F32), jax.ShapeDtypeStruct((1, c), F32),
                   jax.ShapeDtypeStruct((1, 2 * c), F32), jax.ShapeDtypeStruct((1, c), F32)] + shapes,
        scratch_shapes=[pltpu.VMEM((HALO + tm, c), F32), pltpu.VMEM((tm + HALO, c), F32), pltpu.VMEM((tm, c), F32),
                        pltpu.VMEM((8 * HALO, c), F32), pltpu.VMEM((8, HALO + tm, c), F32),
                        pltpu.VMEM((8, tm + HALO, c), F32)] + sems,
        compiler_params=_params(("arbitrary",), 56),
    )(p, p, p, p, yc, dy2, dout, wdw, lng, lnb, *send)
    return tuple(res[:7]) + (list(res[7:]),)


def _ffn_fwd(x, g, wup, wdw, bdw, wdn):
    s, d = x.shape
    ff = wdn.shape[0]
    tm = min(256, s)

    def body(x_ref, g_ref, wup_ref, wdw_ref, bdw_ref, wdn_ref, y_ref, h_ref, u_ref, carry):
        @pl.when(pl.program_id(0) == 0)
        def _():
            carry[...] = jnp.zeros_like(carry)

        xv = x_ref[...]
        r = lax.rsqrt(jnp.mean(xv * xv, axis=-1, keepdims=True) + EPS)
        h = (xv * r * g_ref[...]).astype(BF16)
        h_ref[...] = h
        acc = xv
        for c in range(0, ff, FF_CHUNK):
            cs = slice(c, c + FF_CHUNK)
            gp = _dot(h, wup_ref[:, cs])
            val = _dot(h, wup_ref[:, ff + c:ff + c + FF_CHUNK])
            u_ref[:, cs] = gp.astype(BF16)
            u_ref[:, ff + c:ff + c + FF_CHUNK] = val.astype(BF16)
            prev = carry[:, cs]
            gate = (wdw_ref[0:1, cs] * _shift_down(gp, prev, 2) + wdw_ref[1:2, cs] * _shift_down(gp, prev, 1)
                    + wdw_ref[2:3, cs] * gp + bdw_ref[:, cs])
            act = gate * _sigmoid(gate) * val
            acc = acc + _dot(act.astype(BF16), wdn_ref[cs, :])
            carry[:, cs] = gp[tm - 8:tm, :]
        y_ref[...] = acc

    return pl.pallas_call(
        body, name="ffn_fwd", grid=(s // tm,),
        in_specs=[pl.BlockSpec((tm, d), lambda i: (i, 0)), _full((1, d)), _full((d, 2 * ff)), _full((8, ff)),
                  _full((1, ff)), _full((ff, d))],
        out_specs=[pl.BlockSpec((tm, d), lambda i: (i, 0)), pl.BlockSpec((tm, d), lambda i: (i, 0)),
                   pl.BlockSpec((tm, 2 * ff), lambda i: (i, 0))],
        out_shape=[jax.ShapeDtypeStruct((s, d), F32), jax.ShapeDtypeStruct((s, d), BF16),
                   jax.ShapeDtypeStruct((s, 2 * ff), BF16)],
        scratch_shapes=[pltpu.VMEM((8, ff), F32)],
        compiler_params=_params(("arbitrary",), 56),
    )(x, g, wup, wdw, bdw, wdn)


def _ffn_bwd(dy, u, wdw, bdw, wdn, send=()):
    s, d = dy.shape
    ff = wdn.shape[0]
    tm = min(256, s)
    n = s // tm
    hb = tm // 16

    def body(dy_ref, u_ref, uh_ref, wdw_ref, bdw_ref, wdn_ref, du_ref, act_ref, dw_ref, db_ref, carry):
        i = pl.program_id(0)

        @pl.when(i == 0)
        def _():
            carry[...] = jnp.zeros_like(carry)
            dw_ref[...] = jnp.zeros_like(dw_ref)
            db_ref[...] = jnp.zeros_like(db_ref)

        dyb = dy_ref[...].astype(BF16)
        for c in range(0, ff, FF_CHUNK):
            cs = slice(c, c + FF_CHUNK)
            vs = slice(ff + c, ff + c + FF_CHUNK)
            gp = u_ref[:, cs].astype(F32)
            val = u_ref[:, vs].astype(F32)
            prev = jnp.where(i < n - 1, uh_ref[:, cs].astype(F32)[8:16], 0.0)
            g1 = _shift_down(gp, prev, 1)
            g2 = _shift_down(gp, prev, 2)
            gate = wdw_ref[0:1, cs] * g2 + wdw_ref[1:2, cs] * g1 + wdw_ref[2:3, cs] * gp + bdw_ref[:, cs]
            sg = _sigmoid(gate)
            si = gate * sg
            act_ref[:, cs] = (si * val).astype(BF16)
            da = lax.dot_general(dyb, wdn_ref[cs, :], NT_DIMS, preferred_element_type=F32)
            dgate = da * val * _silu_grad(gate, sg)
            nxt = carry[:, cs]
            du_ref[:, cs] = (wdw_ref[2:3, cs] * dgate + wdw_ref[1:2, cs] * _shift_up(dgate, nxt, 1)
                             + wdw_ref[0:1, cs] * _shift_up(dgate, nxt, 2)).astype(BF16)
            du_ref[:, vs] = (da * si).astype(BF16)
            dw_ref[:, cs] += jnp.concatenate(
                [jnp.sum(dgate * g2, axis=0, keepdims=True), jnp.sum(dgate * g1, axis=0, keepdims=True),
                 jnp.sum(dgate * gp, axis=0, keepdims=True), jnp.zeros((5, FF_CHUNK), F32)], axis=0)
            db_ref[:, cs] += jnp.sum(dgate, axis=0, keepdims=True)
            carry[:, cs] = dgate[0:8, :]

    step_is = lambda i: lambda: pl.program_id(0) == i
    kern, ins, outs, shapes, sems = _carried(body, 6, 4, send, "scatter",
                                             {"start": step_is(0), "finish": step_is(n - 1)})
    res = pl.pallas_call(
        kern, name="ffn_bwd", grid=(n,),
        in_specs=[pl.BlockSpec((tm, d), lambda i: (n - 1 - i, 0)), pl.BlockSpec((tm, 2 * ff), lambda i: (n - 1 - i, 0)),
                  pl.BlockSpec((16, 2 * ff), lambda i: (jnp.maximum((n - 1 - i) * hb - 1, 0), 0)),
                  _full((8, ff)), _full((1, ff)), _full((ff, d))] + ins,
        out_specs=[pl.BlockSpec((tm, 2 * ff), lambda i: (n - 1 - i, 0)), pl.BlockSpec((tm, ff), lambda i: (n - 1 - i, 0)),
                   _full((8, ff)), _full((1, ff))] + outs,
        out_shape=[jax.ShapeDtypeStruct((s, 2 * ff), BF16), jax.ShapeDtypeStruct((s, ff), BF16),
                   jax.ShapeDtypeStruct((8, ff), F32), jax.ShapeDtypeStruct((1, ff), F32)] + shapes,
        scratch_shapes=[pltpu.VMEM((8, ff), F32)] + sems,
        compiler_params=_params(("arbitrary",), 48),
    )(dy, u, u, wdw, bdw, wdn, *send)
    return res[0], res[1], res[2], res[3], list(res[4:])


def _loss_head(y, target):
    s, d = y.shape
    tm = min(512, s)

    def body(y_ref, t_ref, l_ref, dy_ref):
        @pl.when(pl.program_id(0) == 0)
        def _():
            l_ref[...] = jnp.zeros_like(l_ref)

        err = y_ref[...] - t_ref[...]
        dy_ref[...] = err * (1.0 / d)
        l_ref[...] += 0.5 * jnp.sum(jnp.mean(err * err, axis=-1, keepdims=True), axis=0, keepdims=True)

    return pl.pallas_call(
        body, name="loss_head", grid=(s // tm,),
        in_specs=[pl.BlockSpec((tm, d), lambda i: (i, 0)), pl.BlockSpec((tm, d), lambda i: (i, 0))],
        out_specs=[_full((8, LANE)), pl.BlockSpec((tm, d), lambda i: (i, 0))],
        out_shape=[jax.ShapeDtypeStruct((8, LANE), F32), jax.ShapeDtypeStruct((s, d), F32)],
        compiler_params=_params(("arbitrary",), 32),
    )(y, target)


def _row_tile(rows, limit=512):
    for cand in range(min(limit, rows) // 16 * 16, 0, -16):
        if rows % cand == 0:
            return cand
    return rows


def _reduce_adamw(parts, w, m, v, name):
    _, nl, a, b = parts.shape
    ta = _row_tile(a, 256)

    def body(p_ref, w_ref, m_ref, v_ref, g_ref, d_ref, mo_ref, vo_ref):
        g = p_ref[0].astype(F32)
        for k in range(1, N_DEV):
            g = g + p_ref[k].astype(F32)
        g_ref[...] = g
        mn = ADAM_B1 * m_ref[...] + (1.0 - ADAM_B1) * g
        vn = ADAM_B2 * v_ref[...] + (1.0 - ADAM_B2) * (g * g)
        mo_ref[...] = mn
        vo_ref[...] = vn
        m_hat = mn / (1.0 - ADAM_B1 ** ADAM_STEP)
        v_hat = vn / (1.0 - ADAM_B2 ** ADAM_STEP)
        d_ref[...] = -ADAM_LR * (m_hat / (jnp.sqrt(v_hat) + ADAM_EPS) + ADAM_WD * w_ref[...])

    blk = pl.BlockSpec((1, ta, b), lambda l, i: (l, i, 0))
    return pl.pallas_call(
        body, name=name, grid=(nl, a // ta),
        in_specs=[pl.BlockSpec((N_DEV, 1, ta, b), lambda l, i: (0, l, i, 0)), blk, blk, blk],
        out_specs=[blk, blk, blk, blk],
        out_shape=[jax.ShapeDtypeStruct((nl, a, b), F32)] * 4,
        compiler_params=_params(("arbitrary", "arbitrary"), 48),
    )(parts, w, m, v)


def _unshard(w8, layer, name):
    _, _, k, n = w8.shape
    tk = _row_tile(k, 256)

    def body(w_ref, o_ref):
        for d in range(N_DEV):
            o_ref[:, d * n:(d + 1) * n] = w_ref[d, 0]

    return pl.pallas_call(
        body, name=name, grid=(k // tk,),
        in_specs=[pl.BlockSpec((N_DEV, 1, tk, n), lambda i: (0, layer, i, 0))],
        out_specs=pl.BlockSpec((tk, N_DEV * n), lambda i: (i, 0)),
        out_shape=jax.ShapeDtypeStruct((k, N_DEV * n), w8.dtype),
        compiler_params=_params(("arbitrary",), 32),
    )(w8)


def _shard_cast(g, name):
    k, n8 = g.shape
    n = n8 // N_DEV
    tk = _row_tile(k, 256)

    def body(g_ref, o_ref):
        for d in range(N_DEV):
            o_ref[d] = g_ref[:, d * n:(d + 1) * n].astype(BF16)

    return pl.pallas_call(
        body, name=name, grid=(k // tk,),
        in_specs=[pl.BlockSpec((tk, n8), lambda i: (i, 0))],
        out_specs=pl.BlockSpec((N_DEV, tk, n), lambda i: (0, i, 0)),
        out_shape=jax.ShapeDtypeStruct((N_DEV, k, n), BF16),
        compiler_params=_params(("arbitrary",), 32),
    )(g)


def _row(v):
    return v.reshape(1, -1)


def _group_ones():
    idx = jnp.arange(SB_WIDTH) // HEAD_DIM
    return (idx[:, None] == idx[None, :]).astype(BF16)


def _pad_rows(w, rows):
    return jnp.concatenate([w, jnp.zeros((rows - w.shape[0], w.shape[1]), w.dtype)], axis=0)


class _NoTraffic:
    def rest(self):
        return ()

    def install(self, wt, gathered):
        pass

    def ready(self, name, layer, grad):
        pass

    def take(self):
        return ()

    def landed(self, received):
        pass


def _local_step(x, target, wt, traffic):
    scale = HEAD_DIM ** -0.5
    bd = _group_ones()
    tril = jnp.tril(jnp.ones((CHUNK, CHUNK), dtype=bool))
    saved = []
    for i in range(DEPTH):
        j = i // 2
        lay = {"x_mix": x}
        if i % 2 == 0:
            proj, h = _rms_matmul(x, _row(wt["mix_norm_g"][i]), wt["sb_w_in"][j], jnp.zeros((1, IN_WIDTH), F32), "in_proj")
            gq = _row(jnp.tile(wt["sb_q_norm_g"][j], SB_WIDTH // HEAD_DIM)) * scale
            gk = _row(jnp.tile(wt["sb_k_norm_g"][j], SB_WIDTH // HEAD_DIM))
            qn, kn, vb = _qk_prep(proj, gq * LOG2E, gk, bd)
            o, lsum, gathered = _sb_attn_fwd(qn, kn, vb, gather=traffic.rest() if i == 0 else ())
            if i == 0:
                traffic.install(wt, gathered)
            wm = jnp.where(tril[None], wt["sg_w_spatial"][j], 0.0)
            wmb = wm.astype(BF16)
            wmt = jnp.swapaxes(wm, 1, 2).astype(BF16)
            bt = jnp.repeat(wt["sg_b_spatial"][j].T, HEAD_DIM, axis=1)
            gz = _row(wt["sg_z_norm_g"][j])
            gg = _sgu_fwd(proj, gz, wmb, bt, bd)
            x, mix = _out_proj(x, o, gg, wt["hyb_w_out"][j], "out_proj")
            lay.update(proj=proj, h=h, gq=gq, gk=gk, qn=qn, kn=kn, vb=vb, lsum=lsum, wmb=wmb, wmt=wmt, bt=bt, gz=gz, mix=mix)
        else:
            p, h = _rms_matmul(x, _row(wt["mix_norm_g"][i]), wt["cv_w_pw1"][j], _row(wt["cv_b_pw1"][j]), "conf_pw1")
            wdw = _pad_rows(wt["cv_w_dw"][j], HALO)
            yc, y2 = _conf_mid_fwd(p, wdw, _row(wt["cv_b_dw"][j]), _row(wt["cv_ln_g"][j]), _row(wt["cv_ln_b"][j]))
            x = _res_matmul(x, y2, wt["cv_w_pw2"][j], _row(wt["cv_b_pw2"][j]), "conf_pw2")
            lay.update(p=p, h=h, wdw=wdw, yc=yc, y2=y2)
        lay["x_ffn"] = x
        fdw = _pad_rows(wt["ffn_w_dw"][i], 8)
        x, hf, u = _ffn_fwd(x, _row(wt["ffn_norm_g"][i]), wt["ffn_w_up"][i], fdw, _row(wt["ffn_b_dw"][i]),
                            wt["ffn_w_down"][i])
        lay.update(hf=hf, u=u, fdw=fdw)
        saved.append(lay)

    lpart, dy = _loss_head(x, target)
    loss = lpart[0, 0]

    gr = {k: [None] * len(v) for k, v in wt.items()}

    def made(name, layer, grad):
        gr[name][layer] = grad
        traffic.ready(name, layer, grad)

    for i in reversed(range(DEPTH)):
        j = i // 2
        lay = saved[i]
        du, act, dfdw, dfb, got = _ffn_bwd(dy, lay["u"], lay["fdw"], _row(wt["ffn_b_dw"][i]), wt["ffn_w_down"][i],
                                           send=traffic.take())
        traffic.landed(got)
        made("ffn_w_down", i, _matmul_tn(act, dy, "ffn_dw_down"))
        made("ffn_w_up", i, _matmul_tn(lay["hf"], du, "ffn_dw_up"))
        gr["ffn_w_dw"][i] = dfdw[:FFN_K]
        gr["ffn_b_dw"][i] = dfb[0]
        dy, dgf = _nt_rms_bwd(du, wt["ffn_w_up"][i], lay["x_ffn"], _row(wt["ffn_norm_g"][i]), dy, "ffn_dx")
        gr["ffn_norm_g"][i] = dgf[0]
        if i % 2 == 0:
            dmix = _matmul_nt(dy, wt["hyb_w_out"][j], "out_proj_dx")
            made("hyb_w_out", j, _matmul_tn(lay["mix"], dy, "out_proj_dw"))
            dqn, dkn, dv, got = _sb_attn_bwd(lay["qn"], lay["kn"], lay["vb"], lay["lsum"], dmix, send=traffic.take())
            traffic.landed(got)
            dqk, dgq, dgk = _qk_bwd(lay["proj"], dqn, dkn, lay["gq"], lay["gk"] * LN2, bd)
            duz, dwm, dbt, dgz = _sgu_bwd(lay["proj"], dmix, lay["gz"], lay["wmb"], lay["wmt"], lay["bt"], bd)
            dproj = jnp.concatenate([dqk.astype(BF16), dv.astype(BF16), duz.astype(BF16)], axis=1)
            made("sb_w_in", j, _matmul_tn(lay["h"], dproj, "in_proj_dw"))
            gr["sb_q_norm_g"][j] = dgq.reshape(SB_WIDTH // HEAD_DIM, HEAD_DIM).sum(0) * scale
            gr["sb_k_norm_g"][j] = dgk.reshape(SB_WIDTH // HEAD_DIM, HEAD_DIM).sum(0) * LN2
            gr["sg_z_norm_g"][j] = dgz[0]
            gr["sg_w_spatial"][j] = jnp.where(tril[None], dwm, 0.0)
            gr["sg_b_spatial"][j] = dbt.reshape(CHUNK, SG_GROUPS, HEAD_DIM).sum(-1).T
            dy, dgm = _nt_rms_bwd(dproj, wt["sb_w_in"][j], lay["x_mix"], _row(wt["mix_norm_g"][i]), dy, "in_proj_dx")
        else:
            dy2 = _matmul_nt(dy, wt["cv_w_pw2"][j], "conf_pw2_dx")
            made("cv_w_pw2", j, _matmul_tn(lay["y2"], dy, "conf_pw2_dw"))
            dp, dwdw, dbdw, dlg, dlb, db1, db2, got = _conf_mid_bwd(
                lay["p"], lay["yc"], dy2, dy, lay["wdw"], _row(wt["cv_ln_g"][j]), _row(wt["cv_ln_b"][j]),
                send=traffic.take())
            traffic.landed(got)
            made("cv_w_pw1", j, _matmul_tn(lay["h"], dp, "conf_pw1_dw"))
            gr["cv_w_dw"][j] = dwdw[:CONV_K]
            gr["cv_b_dw"][j] = dbdw[0]
            gr["cv_ln_g"][j] = dlg[0]
            gr["cv_ln_b"][j] = dlb[0]
            gr["cv_b_pw1"][j] = db1[0]
            gr["cv_b_pw2"][j] = db2[0]
            dy, dgm = _nt_rms_bwd(dp, wt["cv_w_pw1"][j], lay["x_mix"], _row(wt["mix_norm_g"][i]), dy, "conf_pw1_dx")
        gr["mix_norm_g"][i] = dgm[0]
    matmul_weights = ("sb_w_in", "hyb_w_out", "cv_w_pw1", "cv_w_pw2", "ffn_w_up", "ffn_w_down")
    grads = {k: (v if k in matmul_weights else jnp.stack(v)) for k, v in gr.items()}
    return loss, dy, grads


WEIGHTS = ["mix_norm_g", "sb_w_in", "sb_q_norm_g", "sb_k_norm_g", "sg_z_norm_g", "sg_w_spatial", "sg_b_spatial",
           "hyb_w_out", "cv_w_pw1", "cv_b_pw1", "cv_w_dw", "cv_b_dw", "cv_ln_g", "cv_ln_b", "cv_w_pw2", "cv_b_pw2",
           "ffn_norm_g", "ffn_w_up", "ffn_w_dw", "ffn_b_dw", "ffn_w_down"]
BIG = [("sb_w_in", "col"), ("hyb_w_out", "row"), ("cv_w_pw1", "col"), ("cv_w_pw2", "row"), ("ffn_w_up", "col"),
       ("ffn_w_down", "row")]
SMALL = ["cv_b_pw1", "cv_w_dw", "cv_b_dw", "cv_ln_g", "cv_ln_b", "cv_b_pw2", "ffn_w_dw"]
REPLICATED = ["mix_norm_g", "sb_q_norm_g", "sb_k_norm_g", "sg_z_norm_g", "sg_w_spatial", "sg_b_spatial", "ffn_norm_g",
              "ffn_b_dw"]


def _last_dim_blocks(full):
    t = jnp.moveaxis(full.reshape(full.shape[:-1] + (N_DEV, full.shape[-1] // N_DEV)), -2, 0)
    return t.reshape(N_DEV, -1)


def _from_last_dim_blocks(blocks, shard_shape):
    t = jnp.moveaxis(blocks.reshape((N_DEV,) + tuple(shard_shape)), 0, -2)
    return t.reshape(tuple(shard_shape[:-1]) + (N_DEV * shard_shape[-1],))


def _pack(arrays):
    lead = arrays[0].shape[:-1]
    flat = jnp.concatenate([a.astype(F32) for a in arrays], axis=-1)
    rows = -(-flat.shape[-1] // (16 * LANE)) * 16
    pad = rows * LANE - flat.shape[-1]
    if pad:
        flat = jnp.concatenate([flat, jnp.zeros(lead + (pad,), F32)], axis=-1)
    return flat.reshape(lead + (rows, LANE))


def _unpack(packed, shapes):
    flat = packed.reshape(-1)
    out, off = [], 0
    for shp in shapes:
        size = 1
        for dim in shp:
            size *= dim
        out.append(flat[off:off + size].reshape(shp))
        off += size
    return out


class _ShardTraffic:
    def __init__(self, w):
        self.w = w
        self.queue, self.flying, self.received = [], [], {}

    def rest(self):
        return [self.w["sb_w_in"][1:].astype(BF16)] + [self.w[n].astype(BF16) for n, _ in BIG[1:]]

    def install(self, wt, gathered):
        wt["sb_w_in"].append(_unshard(gathered[0], 0, "unshard_sb_w_in"))
        for (n, kind), w8 in zip(BIG[1:], gathered[1:]):
            if kind == "col":
                wt[n] = [_unshard(w8, l, "unshard_" + n) for l in range(w8.shape[1])]
            else:
                wt[n] = [w8[:, l].reshape((N_DEV * w8.shape[2],) + w8.shape[3:]) for l in range(w8.shape[1])]

    def ready(self, name, layer, grad):
        if dict(BIG)[name] == "col":
            blocks = _shard_cast(grad, "shard_" + name)
        else:
            blocks = grad.astype(BF16).reshape((N_DEV, grad.shape[0] // N_DEV) + grad.shape[1:])
        self.queue.append(((name, layer), blocks))

    def take(self):
        self.flying = [key for key, _ in self.queue]
        arrays = [blocks for _, blocks in self.queue]
        self.queue = []
        return arrays

    def landed(self, received):
        for key, blocks in zip(self.flying, received):
            self.received[key] = blocks
        self.flying = []


def kernel(x, mix_norm_g, sb_w_in, sb_q_norm_g, sb_k_norm_g, sg_z_norm_g, sg_w_spatial, sg_b_spatial, hyb_w_out, cv_w_pw1, cv_b_pw1, cv_w_dw, cv_b_dw, cv_ln_g, cv_ln_b, cv_w_pw2, cv_b_pw2, ffn_norm_g, ffn_w_up, ffn_w_dw, ffn_b_dw, ffn_w_down, loss_target, m_mix_norm_g, m_sb_w_in, m_sb_q_norm_g, m_sb_k_norm_g, m_sg_z_norm_g, m_sg_w_spatial, m_sg_b_spatial, m_hyb_w_out, m_cv_w_pw1, m_cv_b_pw1, m_cv_w_dw, m_cv_b_dw, m_cv_ln_g, m_cv_ln_b, m_cv_w_pw2, m_cv_b_pw2, m_ffn_norm_g, m_ffn_w_up, m_ffn_w_dw, m_ffn_b_dw, m_ffn_w_down, v_mix_norm_g, v_sb_w_in, v_sb_q_norm_g, v_sb_k_norm_g, v_sg_z_norm_g, v_sg_w_spatial, v_sg_b_spatial, v_hyb_w_out, v_cv_w_pw1, v_cv_b_pw1, v_cv_w_dw, v_cv_b_dw, v_cv_ln_g, v_cv_ln_b, v_cv_w_pw2, v_cv_b_pw2, v_ffn_norm_g, v_ffn_w_up, v_ffn_w_dw, v_ffn_b_dw, v_ffn_w_down):
    w = dict(zip(WEIGHTS, (mix_norm_g, sb_w_in, sb_q_norm_g, sb_k_norm_g, sg_z_norm_g, sg_w_spatial, sg_b_spatial,
                           hyb_w_out, cv_w_pw1, cv_b_pw1, cv_w_dw, cv_b_dw, cv_ln_g, cv_ln_b, cv_w_pw2, cv_b_pw2,
                           ffn_norm_g, ffn_w_up, ffn_w_dw, ffn_b_dw, ffn_w_down)))
    m = dict(zip(WEIGHTS, (m_mix_norm_g, m_sb_w_in, m_sb_q_norm_g, m_sb_k_norm_g, m_sg_z_norm_g, m_sg_w_spatial,
                           m_sg_b_spatial, m_hyb_w_out, m_cv_w_pw1, m_cv_b_pw1, m_cv_w_dw, m_cv_b_dw, m_cv_ln_g,
                           m_cv_ln_b, m_cv_w_pw2, m_cv_b_pw2, m_ffn_norm_g, m_ffn_w_up, m_ffn_w_dw, m_ffn_b_dw,
                           m_ffn_w_down)))
    v = dict(zip(WEIGHTS, (v_mix_norm_g, v_sb_w_in, v_sb_q_norm_g, v_sb_k_norm_g, v_sg_z_norm_g, v_sg_w_spatial,
                           v_sg_b_spatial, v_hyb_w_out, v_cv_w_pw1, v_cv_b_pw1, v_cv_w_dw, v_cv_b_dw, v_cv_ln_g,
                           v_cv_ln_b, v_cv_w_pw2, v_cv_b_pw2, v_ffn_norm_g, v_ffn_w_up, v_ffn_w_dw, v_ffn_b_dw,
                           v_ffn_w_down)))
    big_names = [n for n, _ in BIG]
    flat = lambda t, names: [t[n].reshape(-1) for n in names]

    first_in, small_all = _all_gather([w["sb_w_in"][0:1].astype(BF16), _pack(flat(w, SMALL))], "gather_first")
    wt = {n: w[n] for n in REPLICATED}
    wt["sb_w_in"] = [_unshard(first_in, 0, "unshard_sb_w_in")]
    small_parts = zip(*[_unpack(small_all[d], [(w[n].size,) for n in SMALL]) for d in range(N_DEV)])
    for n, parts in zip(SMALL, small_parts):
        wt[n] = _from_last_dim_blocks(jnp.stack(parts), w[n].shape)

    traffic = _ShardTraffic(w)
    loss, gx, grads = _local_step(x[0], loss_target[0], wt, traffic)
    traffic.landed(_all_to_all(traffic.take(), "exchange_last"))
    recv_big = [jnp.stack([traffic.received[n, l] for l in range(w[n].shape[0])], axis=1) for n in big_names]

    grep = _pack(flat(grads, REPLICATED))
    gsmall = _pack([_last_dim_blocks(grads[n]) for n in SMALL])
    rep_rows, small_rows = grep.shape[0], gsmall.shape[1]
    vec = jnp.concatenate([grep, gsmall.reshape(N_DEV * small_rows, LANE)], axis=0)
    vec_all = _all_gather([vec], "gather_small_grads")[0]
    me = 4 * lax.axis_index("x") + 2 * lax.axis_index("y") + lax.axis_index("c")
    recv_rep = vec_all[:, :rep_rows]
    recv_small = lax.dynamic_slice(vec_all, (0, rep_rows + small_rows * me, 0), (N_DEV, small_rows, LANE))

    out = {}
    kinds = ("grad", "delta", "new_m", "new_v")
    for n, parts in zip(big_names, recv_big):
        for kind, arr in zip(kinds, _reduce_adamw(parts, w[n], m[n], v[n], "adamw_" + n)):
            out[kind, n] = arr
    for names, recv, tag in ((SMALL, recv_small, "adamw_small"), (REPLICATED, recv_rep, "adamw_replicated")):
        res = _reduce_adamw(recv[:, None], _pack(flat(w, names))[None], _pack(flat(m, names))[None],
                            _pack(flat(v, names))[None], tag)
        shapes = [w[n].shape for n in names]
        for kind, packed in zip(kinds, res):
            for n, arr in zip(names, _unpack(packed[0], shapes)):
                out[kind, n] = arr

    loss = lax.psum(loss, ("x", "y", "c"))
    return (loss, gx[None], *[out[kind, n] for kind in kinds for n in WEIGHTS])
```

```python
import functools

import jax
import jax.numpy as jnp
from jax import lax
from jax.experimental import pallas as pl
from jax.experimental.pallas import tpu as pltpu

F32 = jnp.float32
BF16 = jnp.bfloat16

D_MODEL = 1024
HEAD_DIM = 64
SB_WIDTH = 512
SG_WIDTH = 512
SG_GROUPS = 8
IN_WIDTH = 3 * SB_WIDTH + 2 * SG_WIDTH
CHUNK = 128
CONV_K = 31
D_FF = 2816
FFN_K = 3
DEPTH = 4
EPS = 1e-6
N_DEV = 8
LANE = 128
HALO = 32
ATT_BLOCK = 256
FF_CHUNK = 256
CONV_ROWS = 32
CONV_LANES = 512
MIB = 2 ** 20

ADAM_LR = 0.001
ADAM_B1 = 0.9
ADAM_B2 = 0.999
ADAM_EPS = 1e-08
ADAM_WD = 0.01
ADAM_STEP = 10

LOG2E = 1.4426950408889634
LN2 = 0.6931471805599453

NT_DIMS = (((1,), (1,)), ((), ()))
TN_DIMS = (((0,), (0,)), ((), ()))


def _params(semantics, vmem_mib):
    return pltpu.CompilerParams(dimension_semantics=semantics, vmem_limit_bytes=vmem_mib * MIB)


def _full(shape):
    nd = len(shape)
    return pl.BlockSpec(shape, lambda *_: (0,) * nd)


def _sigmoid(x):
    return 1.0 / (1.0 + jnp.exp(-x))


def _gelu(x):
    return 0.5 * x * (1.0 + lax.erf(x * 0.7071067811865476))


def _gelu_grad(x):
    return 0.5 * (1.0 + lax.erf(x * 0.7071067811865476)) + x * jnp.exp(-0.5 * x * x) * 0.3989422804014327


def _silu_grad(x, s):
    return s * (1.0 + x * (1.0 - s))


def _dot(a, b):
    return jnp.dot(a, b, preferred_element_type=F32)


def _dot2(a, b):
    hi = a.astype(BF16)
    lo = (a - hi.astype(F32)).astype(BF16)
    return _dot(hi, b) + _dot(lo, b)


def _group_mean(t, bd):
    return _dot2(t, bd) * (1.0 / HEAD_DIM)


def _shift_down(v, prev8, s):
    top = pltpu.roll(jnp.concatenate([prev8, v[:8]], axis=0), s, 0)[8:16]
    return jnp.concatenate([top, pltpu.roll(v, s, 0)[8:]], axis=0)


def _shift_up(v, next8, s):
    n = v.shape[0]
    bottom = pltpu.roll(jnp.concatenate([v[n - 8:], next8], axis=0), 16 - s, 0)[0:8]
    return jnp.concatenate([pltpu.roll(v, n - s, 0)[: n - 8], bottom], axis=0)


def _mesh_pos():
    return lax.axis_index("x"), lax.axis_index("y"), lax.axis_index("c")


def _comm_scratch(n):
    return [pltpu.SemaphoreType.DMA((7 * n,)), pltpu.SemaphoreType.DMA((7 * n,)), pltpu.SemaphoreType.DMA((n,))]


class _Scatter:
    def __init__(self, src_refs, out_refs, send_sems, recv_sems, local_sems):
        x, y, cc = _mesh_pos()
        me = 4 * x + 2 * y + cc
        self.copies, self.mine = [], []
        for a, (src, out) in enumerate(zip(src_refs, out_refs)):
            self.mine.append(pltpu.make_async_copy(src.at[me], out.at[me], local_sems.at[a]))
            for k in range(1, N_DEV):
                px = 1 - x if k & 4 else x
                py = 1 - y if k & 2 else y
                pc = 1 - cc if k & 1 else cc
                self.copies.append(pltpu.make_async_remote_copy(
                    src_ref=src.at[4 * px + 2 * py + pc], dst_ref=out.at[me],
                    send_sem=send_sems.at[7 * a + k - 1], recv_sem=recv_sems.at[7 * a + k - 1],
                    device_id=(px, py, pc), device_id_type=pl.DeviceIdType.MESH))

    def start(self):
        for cp in self.mine + self.copies:
            cp.start()

    def finish(self):
        for cp in self.copies + self.mine:
            cp.wait()


class _Gather:
    def __init__(self, x_refs, out_refs, send_sems, recv_sems, local_sems):
        x, y, cc = _mesh_pos()
        self.n = len(x_refs)
        self.me, self.sibling, self.cc = (x, y, cc), (x, y, 1 - cc), cc
        self.chips = [(1 - x, y), (x, 1 - y), (1 - x, 1 - y)]
        self.x_refs, self.out_refs, self.send_sems, self.recv_sems = x_refs, out_refs, send_sems, recv_sems
        self.mine = [pltpu.make_async_copy(x_refs[a], out_refs[a].at[4 * x + 2 * y + cc], local_sems.at[a])
                     for a in range(self.n)]

    def copy(self, a, k, block, to, own=False):
        slot = self.out_refs[a].at[4 * block[0] + 2 * block[1] + block[2]]
        return pltpu.make_async_remote_copy(
            src_ref=self.x_refs[a] if own else slot, dst_ref=slot,
            send_sem=self.send_sems.at[7 * a + k], recv_sem=self.recv_sems.at[7 * a + k],
            device_id=to, device_id_type=pl.DeviceIdType.MESH)

    def first_hop(self, a):
        return [self.copy(a, 0, self.me, self.sibling, own=True)] + [
            self.copy(a, 1 + j, self.me, (*chip, self.cc), own=True) for j, chip in enumerate(self.chips)]

    def passed_on(self, a):
        return [self.copy(a, 4 + j, (*chip, self.cc), self.sibling) for j, chip in enumerate(self.chips)]

    def start(self):
        for a in range(self.n):
            self.mine[a].start()
        for a in range(self.n):
            for cp in self.first_hop(a):
                cp.start()

    def forward(self):
        for a in range(self.n):
            for j, chip in enumerate(self.chips):
                self.copy(a, 1 + j, (*chip, self.cc), self.me).wait_recv()
                self.copy(a, 4 + j, (*chip, self.cc), self.sibling).start()

    def finish(self):
        for a in range(self.n):
            self.copy(a, 0, self.sibling, self.me).wait_recv()
            for j, chip in enumerate(self.chips):
                self.copy(a, 4 + j, (*chip, 1 - self.cc), self.me).wait_recv()
        for a in range(self.n):
            for cp in self.first_hop(a) + self.passed_on(a):
                cp.wait_send()
        for cp in self.mine:
            cp.wait()


def _carried(body, n_in, n_out, arrays, kind, when):
    n = len(arrays)
    if n == 0:
        return body, [], [], [], []

    def wrapped(*refs):
        ins, srcs = refs[:n_in], refs[n_in:n_in + n]
        outs, landed = refs[n_in + n:n_in + n + n_out], refs[n_in + n + n_out:n_in + 2 * n + n_out]
        scratch = refs[n_in + 2 * n + n_out:]
        comm = (_Scatter if kind == "scatter" else _Gather)(srcs, landed, *scratch[-3:])
        pl.when(when["start"]())(comm.start)
        if kind == "gather":
            pl.when(when["forward"]())(comm.forward)
        body(*ins, *outs, *scratch[:-3])
        pl.when(when["finish"]())(comm.finish)

    any_spec = pl.BlockSpec(memory_space=pl.ANY)
    if kind == "scatter":
        shapes = [jax.ShapeDtypeStruct(a.shape, a.dtype) for a in arrays]
    else:
        shapes = [jax.ShapeDtypeStruct((N_DEV,) + a.shape, a.dtype) for a in arrays]
    return wrapped, [any_spec] * n, [any_spec] * n, shapes, _comm_scratch(n)


def _all_gather(shards, name):
    n = len(shards)

    def body(*refs):
        comm = _Gather(refs[:n], refs[n:2 * n], *refs[2 * n:])
        comm.start()
        comm.forward()
        comm.finish()

    any_spec = pl.BlockSpec(memory_space=pl.ANY)
    return pl.pallas_call(
        body, name=name, in_specs=[any_spec] * n, out_specs=[any_spec] * n,
        out_shape=[jax.ShapeDtypeStruct((N_DEV,) + s.shape, s.dtype) for s in shards],
        scratch_shapes=_comm_scratch(n),
    )(*shards)


def _all_to_all(arrays, name):
    n = len(arrays)

    def body(*refs):
        comm = _Scatter(refs[:n], refs[n:2 * n], *refs[2 * n:])
        comm.start()
        comm.finish()

    any_spec = pl.BlockSpec(memory_space=pl.ANY)
    return pl.pallas_call(
        body, name=name, in_specs=[any_spec] * n, out_specs=[any_spec] * n,
        out_shape=[jax.ShapeDtypeStruct(a.shape, a.dtype) for a in arrays],
        scratch_shapes=_comm_scratch(n),
    )(*arrays)


def _rms_matmul(x, g, w, b, name):
    s, d = x.shape
    n = w.shape[1]
    tm = min(512, s)

    def body(x_ref, g_ref, w_ref, b_ref, y_ref, h_ref):
        xv = x_ref[...]
        r = lax.rsqrt(jnp.mean(xv * xv, axis=-1, keepdims=True) + EPS)
        h = (xv * r * g_ref[...]).astype(BF16)
        h_ref[...] = h
        for c in range(0, n, 512):
            y_ref[:, c:c + 512] = _dot(h, w_ref[:, c:c + 512]) + b_ref[:, c:c + 512]

    return pl.pallas_call(
        body, name=name, grid=(s // tm,),
        in_specs=[pl.BlockSpec((tm, d), lambda i: (i, 0)), _full((1, d)), _full((d, n)), _full((1, n))],
        out_specs=[pl.BlockSpec((tm, n), lambda i: (i, 0)), pl.BlockSpec((tm, d), lambda i: (i, 0))],
        out_shape=[jax.ShapeDtypeStruct((s, n), F32), jax.ShapeDtypeStruct((s, d), BF16)],
        compiler_params=_params(("arbitrary",), 48),
    )(x, g, w, b)


def _matmul_nt(a, w, name):
    s, n = a.shape
    k = w.shape[0]
    tm = min(512, s)

    def body(a_ref, w_ref, o_ref):
        o_ref[...] = lax.dot_general(a_ref[...].astype(BF16), w_ref[...], NT_DIMS, preferred_element_type=F32)

    return pl.pallas_call(
        body, name=name, grid=(s // tm,),
        in_specs=[pl.BlockSpec((tm, n), lambda i: (i, 0)), _full((k, n))],
        out_specs=pl.BlockSpec((tm, k), lambda i: (i, 0)),
        out_shape=jax.ShapeDtypeStruct((s, k), F32),
        compiler_params=_params(("arbitrary",), 40),
    )(a, w)


def _matmul_tn(a, b, name):
    s, k = a.shape
    n = b.shape[1]
    ts = min(2048 if k <= 1024 else 1024, s)
    tn = 1024 if (n % 1024 == 0 and k <= 1024) else 512
    steps = s // ts

    def body(a_ref, b_ref, o_ref, acc):
        t = pl.program_id(1)

        @pl.when(t == 0)
        def _():
            acc[...] = jnp.zeros_like(acc)

        acc[...] += lax.dot_general(a_ref[...].astype(BF16), b_ref[...].astype(BF16), TN_DIMS,
                                    preferred_element_type=F32)

        @pl.when(t == steps - 1)
        def _():
            o_ref[...] = acc[...].astype(BF16)

    return pl.pallas_call(
        body, name=name, grid=(n // tn, steps),
        in_specs=[pl.BlockSpec((ts, k), lambda j, t: (t, 0)), pl.BlockSpec((ts, tn), lambda j, t: (t, j))],
        out_specs=pl.BlockSpec((k, tn), lambda j, t: (0, j)),
        out_shape=jax.ShapeDtypeStruct((k, n), BF16),
        scratch_shapes=[pltpu.VMEM((k, tn), F32)],
        compiler_params=_params(("arbitrary", "arbitrary"), 48),
    )(a, b)


def _nt_rms_bwd(dp, w, x, g, dres, name):
    s, n = dp.shape
    d = x.shape[1]
    tm = min(256, s)

    def body(dp_ref, w_ref, x_ref, g_ref, dres_ref, dx_ref, dg_ref):
        @pl.when(pl.program_id(0) == 0)
        def _():
            dg_ref[...] = jnp.zeros_like(dg_ref)

        dh = lax.dot_general(dp_ref[...], w_ref[...], NT_DIMS, preferred_element_type=F32)
        xv = x_ref[...]
        r = lax.rsqrt(jnp.mean(xv * xv, axis=-1, keepdims=True) + EPS)
        xh = xv * r
        dg_ref[...] += jnp.sum(dh * xh, axis=0, keepdims=True)
        dn = dh * g_ref[...]
        dx_ref[...] = dres_ref[...] + r * (dn - xh * jnp.mean(dn * xh, axis=-1, keepdims=True))

    return pl.pallas_call(
        body, name=name, grid=(s // tm,),
        in_specs=[pl.BlockSpec((tm, n), lambda i: (i, 0)), _full((d, n)), pl.BlockSpec((tm, d), lambda i: (i, 0)),
                  _full((1, d)), pl.BlockSpec((tm, d), lambda i: (i, 0))],
        out_specs=[pl.BlockSpec((tm, d), lambda i: (i, 0)), _full((1, d))],
        out_shape=[jax.ShapeDtypeStruct((s, d), F32), jax.ShapeDtypeStruct((1, d), F32)],
        compiler_params=_params(("arbitrary",), 52),
    )(dp, w, x, g, dres)


def _res_matmul(x, a, w, b, name):
    s, d = x.shape
    k = a.shape[1]
    tm = min(512, s)

    def body(x_ref, a_ref, w_ref, b_ref, o_ref):
        o_ref[...] = x_ref[...] + _dot(a_ref[...], w_ref[...]) + b_ref[...]

    return pl.pallas_call(
        body, name=name, grid=(s // tm,),
        in_specs=[pl.BlockSpec((tm, d), lambda i: (i, 0)), pl.BlockSpec((tm, k), lambda i: (i, 0)), _full((k, d)),
                  _full((1, d))],
        out_specs=pl.BlockSpec((tm, d), lambda i: (i, 0)),
        out_shape=jax.ShapeDtypeStruct((s, d), F32),
        compiler_params=_params(("arbitrary",), 32),
    )(x, a, w, b)


def _out_proj(x, o, gg, w, name):
    s, d = x.shape
    tm = min(512, s)

    def body(x_ref, o_ref, gg_ref, w_ref, y_ref, mix_ref):
        mix = jnp.concatenate([o_ref[...], gg_ref[...]], axis=1).astype(BF16)
        mix_ref[...] = mix
        y_ref[...] = x_ref[...] + _dot(mix, w_ref[...])

    return pl.pallas_call(
        body, name=name, grid=(s // tm,),
        in_specs=[pl.BlockSpec((tm, d), lambda i: (i, 0)), pl.BlockSpec((tm, SB_WIDTH), lambda i: (i, 0)),
                  pl.BlockSpec((tm, SG_WIDTH), lambda i: (i, 0)), _full((d, d))],
        out_specs=[pl.BlockSpec((tm, d), lambda i: (i, 0)), pl.BlockSpec((tm, d), lambda i: (i, 0))],
        out_shape=[jax.ShapeDtypeStruct((s, d), F32), jax.ShapeDtypeStruct((s, d), BF16)],
        compiler_params=_params(("arbitrary",), 32),
    )(x, o, gg, w)


def _qk_prep(proj, gq, gk, bd):
    s = proj.shape[0]
    tm = min(512, s)

    def body(q_ref, k_ref, v_ref, gq_ref, gk_ref, bd_ref, qn_ref, kn_ref, vb_ref):
        bdv = bd_ref[...]
        q = q_ref[...]
        k = k_ref[...]
        qn_ref[...] = (q * lax.rsqrt(_group_mean(q * q, bdv) + EPS) * gq_ref[...]).astype(BF16)
        kn_ref[...] = (k * lax.rsqrt(_group_mean(k * k, bdv) + EPS) * gk_ref[...]).astype(BF16)
        vb_ref[...] = v_ref[...].astype(BF16)

    col = lambda c: pl.BlockSpec((tm, SB_WIDTH), lambda i: (i, c))
    out = pl.BlockSpec((tm, SB_WIDTH), lambda i: (i, 0))
    return pl.pallas_call(
        body, name="qk_prep", grid=(s // tm,),
        in_specs=[col(0), col(1), col(2), _full((1, SB_WIDTH)), _full((1, SB_WIDTH)), _full((SB_WIDTH, SB_WIDTH))],
        out_specs=[out, out, out],
        out_shape=[jax.ShapeDtypeStruct((s, SB_WIDTH), BF16)] * 3,
        compiler_params=_params(("arbitrary",), 32),
    )(proj, proj, proj, gq, gk, bd)


def _qk_bwd(proj, dqn, dkn, gq, gk, bd):
    s = proj.shape[0]
    tm = min(512, s)

    def body(q_ref, k_ref, dq_ref, dk_ref, gq_ref, gk_ref, bd_ref, o_ref, dgq_ref, dgk_ref):
        @pl.when(pl.program_id(0) == 0)
        def _():
            dgq_ref[...] = jnp.zeros_like(dgq_ref)
            dgk_ref[...] = jnp.zeros_like(dgk_ref)

        bdv = bd_ref[...]

        def back(t, gain, dout, dg_ref):
            r = lax.rsqrt(_group_mean(t * t, bdv) + EPS)
            th = t * r
            dg_ref[...] += jnp.sum(dout * th, axis=0, keepdims=True)
            dn = dout * gain
            return r * (dn - th * _group_mean(dn * th, bdv))

        o_ref[:, 0:SB_WIDTH] = back(q_ref[...], gq_ref[...], dq_ref[...], dgq_ref)
        o_ref[:, SB_WIDTH:2 * SB_WIDTH] = back(k_ref[...], gk_ref[...], dk_ref[...], dgk_ref)

    col = lambda c: pl.BlockSpec((tm, SB_WIDTH), lambda i: (i, c))
    row = pl.BlockSpec((tm, SB_WIDTH), lambda i: (i, 0))
    return pl.pallas_call(
        body, name="qk_bwd", grid=(s // tm,),
        in_specs=[col(0), col(1), row, row, _full((1, SB_WIDTH)), _full((1, SB_WIDTH)), _full((SB_WIDTH, SB_WIDTH))],
        out_specs=[pl.BlockSpec((tm, 2 * SB_WIDTH), lambda i: (i, 0)), _full((1, SB_WIDTH)), _full((1, SB_WIDTH))],
        out_shape=[jax.ShapeDtypeStruct((s, 2 * SB_WIDTH), F32), jax.ShapeDtypeStruct((1, SB_WIDTH), F32),
                   jax.ShapeDtypeStruct((1, SB_WIDTH), F32)],
        compiler_params=_params(("arbitrary",), 40),
    )(proj, proj, dqn, dkn, gq, gk, bd)


def _attn_masks(tq, tk):
    lane = lax.broadcasted_iota(jnp.int32, (tk, LANE), 1)
    heads = [(lane >= hh * HEAD_DIM) & (lane < (hh + 1) * HEAD_DIM) for hh in range(2)]
    urow = lax.broadcasted_iota(jnp.int32, (tk, tk), 0)
    ucol = lax.broadcasted_iota(jnp.int32, (tk, tk), 1)
    return heads, urow, ucol


def _keep_cost(z2):
    return jnp.log(1.0 + jnp.exp2(jnp.minimum(z2, 126.0)))


def _sb_attn_fwd(q2, kn, vb, gather=()):
    s = q2.shape[0]
    tk = min(ATT_BLOCK, s)
    tq = min(2 * ATT_BLOCK, s)
    r = tq // tk
    nq = s // tq
    assert s // tk <= LANE

    def body(q_ref, k_ref, v_ref, o_ref, ls_ref):
        i = pl.program_id(1)
        heads, urow, ucol = _attn_masks(tq, tk)
        u_incl = (urow >= ucol).astype(BF16)
        row = lax.broadcasted_iota(jnp.int32, (tq, tk), 0)
        col = lax.broadcasted_iota(jnp.int32, (tq, tk), 1)
        qlane = lax.broadcasted_iota(jnp.int32, (tq, LANE), 1)
        q = q_ref[...]

        def blocks(kbs, state, masked):
            carry, acc, ls = list(state[0:2]), state[2], list(state[3:5])
            chains = [(d, hh) for d in range(len(kbs)) for hh in range(2)]
            kblk, vblk, valid = {}, {}, {}
            for d, kb in enumerate(kbs):
                off = pl.multiple_of(kb * tk, tk)
                kfull = k_ref[pl.ds(off, tk), :]
                vfull = v_ref[pl.ds(off, tk), :]
                if masked:
                    valid[d] = (kb * tk + col) < (i * tq + row)
                for hh in range(2):
                    kblk[d, hh] = jnp.where(heads[hh], kfull, jnp.zeros((), BF16))
                    vblk[d, hh] = jnp.where(heads[hh], vfull, jnp.zeros((), BF16))
            z2 = {c: lax.dot_general(q, kblk[c], NT_DIMS, preferred_element_type=F32) for c in chains}
            cost = {}
            for c in chains:
                cost[c] = _keep_cost(z2[c])
                if masked:
                    cost[c] = jnp.where(valid[c[0]], cost[c], 0.0)
            sums = {c: _dot(cost[c].astype(BF16), u_incl) for c in chains}
            a = {}
            for d, hh in chains:
                rin = carry[hh] + sums[d, hh]
                a[d, hh] = jnp.exp2(z2[d, hh] - rin * LOG2E)
                if masked:
                    a[d, hh] = jnp.where(valid[d], a[d, hh], 0.0)
                rs = jnp.sum(cost[d, hh], axis=1, keepdims=True)
                ls[hh] = ls[hh] + jnp.where(qlane == kbs[d], rs, 0.0)
                carry[hh] = carry[hh] + rs
            for c in chains:
                acc = acc + _dot(a[c].astype(BF16), vblk[c])
            return carry[0], carry[1], acc, ls[0], ls[1]

        zc = jnp.zeros((tq, 1), F32)
        zt = jnp.zeros((tq, LANE), F32)
        state = blocks([r * i + r - 1 - d for d in range(r)], (zc, zc, zt, zt, zt), True)
        state = lax.fori_loop(
            0, i, lambda n, st: blocks([r * (i - 1 - n) + r - 1 - d for d in range(r)], st, False), state)
        o_ref[...] = state[2]
        ls_ref[:, 0:LANE] = state[3]
        ls_ref[:, LANE:2 * LANE] = state[4]

    last_pair = SB_WIDTH // LANE - 1
    step_is = lambda p, i: lambda: (pl.program_id(0) == p) & (pl.program_id(1) == i)
    when = {"start": step_is(0, 0), "forward": step_is(last_pair, 0), "finish": step_is(last_pair, nq - 1)}
    kern, ins, outs, shapes, sems = _carried(body, 3, 2, gather, "gather", when)
    res = pl.pallas_call(
        kern, name="sb_attn_fwd", grid=(SB_WIDTH // LANE, nq),
        in_specs=[pl.BlockSpec((tq, LANE), lambda p, i: (i, p)), pl.BlockSpec((s, LANE), lambda p, i: (0, p)),
                  pl.BlockSpec((s, LANE), lambda p, i: (0, p))] + ins,
        out_specs=[pl.BlockSpec((tq, LANE), lambda p, i: (i, p)),
                   pl.BlockSpec((tq, 2 * LANE), lambda p, i: (i, p))] + outs,
        out_shape=[jax.ShapeDtypeStruct((s, SB_WIDTH), F32), jax.ShapeDtypeStruct((s, 2 * SB_WIDTH), F32)] + shapes,
        scratch_shapes=sems,
        compiler_params=_params(("arbitrary", "arbitrary"), 48),
    )(q2, kn, vb, *gather)
    return res[0], res[1], list(res[2:])


def _sb_attn_bwd(q2, kn, vb, lsum, dmix, send=()):
    s = q2.shape[0]
    tk = min(ATT_BLOCK, s)
    tq = min(2 * ATT_BLOCK, s)
    r = tq // tk
    nq = s // tq

    def body(q_ref, k_ref, v_ref, ls_ref, do_ref, dq_ref, dk_ref, dv_ref):
        i = pl.program_id(1)

        @pl.when(i == 0)
        def _():
            dk_ref[...] = jnp.zeros_like(dk_ref)
            dv_ref[...] = jnp.zeros_like(dv_ref)

        heads, urow, ucol = _attn_masks(tq, tk)
        u_incl = (urow >= ucol).astype(BF16)
        u_pre = (urow <= ucol).astype(BF16)
        lrow = lax.broadcasted_iota(jnp.int32, (LANE, LANE), 0)
        lcol = lax.broadcasted_iota(jnp.int32, (LANE, LANE), 1)
        u_after = (lrow > lcol).astype(BF16)
        row = lax.broadcasted_iota(jnp.int32, (tq, tk), 0)
        col = lax.broadcasted_iota(jnp.int32, (tq, tk), 1)
        qlane = lax.broadcasted_iota(jnp.int32, (tq, LANE), 1)
        qheads = [(qlane >= hh * HEAD_DIM) & (qlane < (hh + 1) * HEAD_DIM) for hh in range(2)]
        q = q_ref[...]
        dob = do_ref[...].astype(BF16)
        qm = [jnp.where(qheads[hh], q, jnp.zeros((), BF16)) for hh in range(2)]
        dom = [jnp.where(qheads[hh], dob, jnp.zeros((), BF16)) for hh in range(2)]
        after = []
        for hh in range(2):
            ls = ls_ref[:, hh * LANE:(hh + 1) * LANE]
            hi = ls.astype(BF16)
            mid = (ls - hi.astype(F32)).astype(BF16)
            lo = (ls - hi.astype(F32) - mid.astype(F32)).astype(BF16)
            after.append(_dot(hi, u_after) + _dot(mid, u_after) + _dot(lo, u_after))

        def blocks(kbs, state, masked):
            cp, dq = list(state[0:2]), state[2]
            chains = [(d, hh) for d in range(len(kbs)) for hh in range(2)]
            offs, kblk, vblk, valid = {}, {}, {}, {}
            for d, kb in enumerate(kbs):
                offs[d] = pl.multiple_of(kb * tk, tk)
                kfull = k_ref[pl.ds(offs[d], tk), :]
                vblk[d] = v_ref[pl.ds(offs[d], tk), :]
                if masked:
                    valid[d] = (kb * tk + col) < (i * tq + row)
                for hh in range(2):
                    kblk[d, hh] = jnp.where(heads[hh], kfull, jnp.zeros((), BF16))
            z2 = {c: lax.dot_general(q, kblk[c], NT_DIMS, preferred_element_type=F32) for c in chains}
            da = {(d, hh): lax.dot_general(dom[hh], vblk[d], NT_DIMS, preferred_element_type=F32) for d, hh in chains}
            cost, sig = {}, {}
            for c in chains:
                cost[c] = _keep_cost(z2[c])
                sig[c] = jnp.exp2(z2[c] - cost[c] * LOG2E)
                if masked:
                    cost[c] = jnp.where(valid[c[0]], cost[c], 0.0)
            sums = {c: _dot(cost[c].astype(BF16), u_incl) for c in chains}
            a, g = {}, {}
            for d, hh in chains:
                cr = jnp.sum(jnp.where(qlane == kbs[d], after[hh], 0.0), axis=1, keepdims=True)
                a[d, hh] = jnp.exp2(z2[d, hh] - (cr + sums[d, hh]) * LOG2E)
                if masked:
                    a[d, hh] = jnp.where(valid[d], a[d, hh], 0.0)
                g[d, hh] = da[d, hh] * a[d, hh]
            pre = {c: _dot(g[c].astype(BF16), u_pre) for c in chains}
            dzb = {}
            for d, hh in chains:
                dz = g[d, hh] - sig[d, hh] * (cp[hh] + pre[d, hh])
                if masked:
                    dz = jnp.where(valid[d], dz, 0.0)
                dzb[d, hh] = dz.astype(BF16)
                cp[hh] = cp[hh] + jnp.sum(g[d, hh], axis=1, keepdims=True)
            for d in range(len(kbs)):
                dv_ref[pl.ds(offs[d], tk), :] += sum(
                    lax.dot_general(a[d, hh].astype(BF16), dom[hh], TN_DIMS, preferred_element_type=F32) for hh in range(2))
                dk_ref[pl.ds(offs[d], tk), :] += sum(
                    lax.dot_general(dzb[d, hh], qm[hh], TN_DIMS, preferred_element_type=F32) for hh in range(2))
            for c in chains:
                dq = dq + _dot(dzb[c], kblk[c])
            return cp[0], cp[1], dq

        zc = jnp.zeros((tq, 1), F32)
        state = lax.fori_loop(0, i, lambda n, st: blocks([r * n + d for d in range(r)], st, False),
                              (zc, zc, jnp.zeros((tq, LANE), F32)))
        state = blocks([r * i + d for d in range(r)], state, True)
        dq_ref[...] = state[2]

    blk = pl.BlockSpec((tq, LANE), lambda p, i: (i, p))
    whole = pl.BlockSpec((s, LANE), lambda p, i: (0, p))
    last_pair = SB_WIDTH // LANE - 1
    step_is = lambda p, i: lambda: (pl.program_id(0) == p) & (pl.program_id(1) == i)
    kern, ins, outs, shapes, sems = _carried(body, 5, 3, send, "scatter",
                                             {"start": step_is(0, 0), "finish": step_is(last_pair, nq - 1)})
    res = pl.pallas_call(
        kern, name="sb_attn_bwd", grid=(SB_WIDTH // LANE, nq),
        in_specs=[blk, whole, whole, pl.BlockSpec((tq, 2 * LANE), lambda p, i: (i, p)), blk] + ins,
        out_specs=[blk, whole, whole] + outs,
        out_shape=[jax.ShapeDtypeStruct((s, SB_WIDTH), F32)] * 3 + shapes,
        scratch_shapes=sems,
        compiler_params=_params(("arbitrary", "arbitrary"), 56),
    )(q2, kn, vb, lsum, dmix, *send)
    return res[0], res[1], res[2], list(res[3:])


def _sgu_spatial(zn, wm_ref, lane, c):
    parts = []
    for p in range(SG_WIDTH // LANE):
        blk = zn[c * CHUNK:(c + 1) * CHUNK, p * LANE:(p + 1) * LANE].astype(BF16)
        lo = jnp.where(lane < HEAD_DIM, blk, jnp.zeros((), BF16))
        hi = jnp.where(lane >= HEAD_DIM, blk, jnp.zeros((), BF16))
        parts.append(_dot(wm_ref[2 * p], lo) + _dot(wm_ref[2 * p + 1], hi))
    return jnp.concatenate(parts, axis=1)


def _sgu_fwd(proj, gz, wm, bt, bd):
    s = proj.shape[0]
    tm = min(512, s)

    def body(u_ref, z_ref, gz_ref, wm_ref, bt_ref, bd_ref, o_ref):
        lane = lax.broadcasted_iota(jnp.int32, (CHUNK, LANE), 1)
        ug = _gelu(u_ref[...])
        zg = _gelu(z_ref[...])
        zn = zg * lax.rsqrt(_group_mean(zg * zg, bd_ref[...]) + EPS) * gz_ref[...]
        for c in range(tm // CHUNK):
            sp = _sgu_spatial(zn, wm_ref, lane, c) + bt_ref[...]
            o_ref[c * CHUNK:(c + 1) * CHUNK, :] = ug[c * CHUNK:(c + 1) * CHUNK, :] * sp

    col = lambda c: pl.BlockSpec((tm, SG_WIDTH), lambda i: (i, c))
    return pl.pallas_call(
        body, name="sgu_fwd", grid=(s // tm,),
        in_specs=[col(3), col(4), _full((1, SG_WIDTH)), _full((SG_GROUPS, CHUNK, CHUNK)), _full((CHUNK, SG_WIDTH)),
                  _full((SG_WIDTH, SG_WIDTH))],
        out_specs=pl.BlockSpec((tm, SG_WIDTH), lambda i: (i, 0)),
        out_shape=jax.ShapeDtypeStruct((s, SG_WIDTH), F32),
        compiler_params=_params(("arbitrary",), 32),
    )(proj, proj, gz, wm, bt, bd)


def _sgu_bwd(proj, dmix, gz, wm, wmt, bt, bd):
    s = proj.shape[0]
    tm = min(512, s)

    def body(u_ref, z_ref, dg_ref, gz_ref, wm_ref, wmt_ref, bt_ref, bd_ref, o_ref, dwm_ref, dbt_ref, dgz_ref):
        @pl.when(pl.program_id(0) == 0)
        def _():
            dwm_ref[...] = jnp.zeros_like(dwm_ref)
            dbt_ref[...] = jnp.zeros_like(dbt_ref)
            dgz_ref[...] = jnp.zeros_like(dgz_ref)

        lane = lax.broadcasted_iota(jnp.int32, (CHUNK, LANE), 1)
        bdv = bd_ref[...]
        u = u_ref[...]
        z = z_ref[...]
        ug = _gelu(u)
        zg = _gelu(z)
        r = lax.rsqrt(_group_mean(zg * zg, bdv) + EPS)
        zh = zg * r
        zn = zh * gz_ref[...]
        dzn_rows = []
        for c in range(tm // CHUNK):
            rows = slice(c * CHUNK, (c + 1) * CHUNK)
            sp = _sgu_spatial(zn, wm_ref, lane, c) + bt_ref[...]
            dgg = dg_ref[rows, :]
            ds = dgg * ug[rows, :]
            o_ref[rows, 0:SG_WIDTH] = dgg * sp * _gelu_grad(u[rows, :])
            dbt_ref[...] += ds
            parts = []
            for p in range(SG_WIDTH // LANE):
                dsb = ds[:, p * LANE:(p + 1) * LANE].astype(BF16)
                znb = zn[rows, p * LANE:(p + 1) * LANE].astype(BF16)
                acc = jnp.zeros((CHUNK, LANE), F32)
                for hh in range(2):
                    hm = (lane >= hh * HEAD_DIM) & (lane < (hh + 1) * HEAD_DIM)
                    dsm = jnp.where(hm, dsb, jnp.zeros((), BF16))
                    znm = jnp.where(hm, znb, jnp.zeros((), BF16))
                    acc = acc + _dot(wmt_ref[2 * p + hh], dsm)
                    dwm_ref[2 * p + hh] += lax.dot_general(dsm, znm, NT_DIMS, preferred_element_type=F32)
                parts.append(acc)
            dzn_rows.append(jnp.concatenate(parts, axis=1))
        dzn = jnp.concatenate(dzn_rows, axis=0)
        dgz_ref[...] += jnp.sum(dzn * zh, axis=0, keepdims=True)
        dn = dzn * gz_ref[...]
        o_ref[:, SG_WIDTH:2 * SG_WIDTH] = r * (dn - zh * _group_mean(dn * zh, bdv)) * _gelu_grad(z)

    col = lambda c: pl.BlockSpec((tm, SG_WIDTH), lambda i: (i, c))
    wspec = _full((SG_GROUPS, CHUNK, CHUNK))
    return pl.pallas_call(
        body, name="sgu_bwd", grid=(s // tm,),
        in_specs=[col(3), col(4), pl.BlockSpec((tm, SG_WIDTH), lambda i: (i, 1)), _full((1, SG_WIDTH)), wspec, wspec,
                  _full((CHUNK, SG_WIDTH)), _full((SG_WIDTH, SG_WIDTH))],
        out_specs=[pl.BlockSpec((tm, 2 * SG_WIDTH), lambda i: (i, 0)), wspec, _full((CHUNK, SG_WIDTH)),
                   _full((1, SG_WIDTH))],
        out_shape=[jax.ShapeDtypeStruct((s, 2 * SG_WIDTH), F32), jax.ShapeDtypeStruct((SG_GROUPS, CHUNK, CHUNK), F32),
                   jax.ShapeDtypeStruct((CHUNK, SG_WIDTH), F32), jax.ShapeDtypeStruct((1, SG_WIDTH), F32)],
        compiler_params=_params(("arbitrary",), 40),
    )(proj, proj, dmix, gz, wm, wmt, bt, bd)


def _shifted_copies(ext, sh):
    e = ext[...]
    sh[0] = e
    for b in range(1, 8):
        sh[b] = pltpu.roll(e, e.shape[0] - b, 0)


def _shifted_rows(sh, off, r0, cols):
    start = r0 + off - off % 8
    return sh[off % 8, start:start + CONV_ROWS, cols]


def _dwconv(sh, w_ref, offsets, rows, store):
    for r0 in range(0, rows, CONV_ROWS):
        for c0 in range(0, sh.shape[2], CONV_LANES):
            cols = slice(c0, c0 + CONV_LANES)
            acc = jnp.zeros((CONV_ROWS, CONV_LANES), F32)
            for j, off in enumerate(offsets):
                acc = acc + w_ref[j:j + 1, cols] * _shifted_rows(sh, off, r0, cols)
            store(slice(r0, r0 + CONV_ROWS), cols, acc)


def _conf_mid_fwd(p, wdw, bdw, lng, lnb):
    s = p.shape[0]
    c = p.shape[1] // 2
    tm = min(256, s)

    def body(a_ref, gt_ref, w_ref, b_ref, g_ref, beta_ref, yc_ref, y2_ref, ext, sh):
        i = pl.program_id(0)

        @pl.when(i == 0)
        def _():
            ext[0:HALO, :] = jnp.zeros((HALO, c), F32)

        @pl.when(i > 0)
        def _():
            ext[0:HALO, :] = ext[tm:tm + HALO, :]

        ext[HALO:HALO + tm, :] = a_ref[...] * _sigmoid(gt_ref[...])

        def store(rows, cols, block):
            yc_ref[rows, cols] = block + b_ref[:, cols]

        _shifted_copies(ext, sh)
        _dwconv(sh, w_ref, [HALO - CONV_K + 1 + j for j in range(CONV_K)], tm, store)
        acc = yc_ref[...]
        xc = acc - jnp.mean(acc, axis=-1, keepdims=True)
        ln = xc * lax.rsqrt(jnp.mean(xc * xc, axis=-1, keepdims=True) + EPS) * g_ref[...] + beta_ref[...]
        y2_ref[...] = (ln * _sigmoid(ln)).astype(BF16)

    vec = _full((1, c))
    return pl.pallas_call(
        body, name="conf_mid_fwd", grid=(s // tm,),
        in_specs=[pl.BlockSpec((tm, c), lambda i: (i, 0)), pl.BlockSpec((tm, c), lambda i: (i, 1)), _full((HALO, c)), vec,
                  vec, vec],
        out_specs=[pl.BlockSpec((tm, c), lambda i: (i, 0)), pl.BlockSpec((tm, c), lambda i: (i, 0))],
        out_shape=[jax.ShapeDtypeStruct((s, c), F32), jax.ShapeDtypeStruct((s, c), BF16)],
        scratch_shapes=[pltpu.VMEM((HALO + tm, c), F32), pltpu.VMEM((8, HALO + tm, c), F32)],
        compiler_params=_params(("arbitrary",), 40),
    )(p, p, wdw, bdw, lng, lnb)


def _conf_mid_bwd(p, yc, dy2, dout, wdw, lng, lnb, send=()):
    s = p.shape[0]
    c = p.shape[1] // 2
    tm = min(256, s)
    n = s // tm
    hb = tm // HALO

    def body(a_ref, gt_ref, ah_ref, gh_ref, yc_ref, dy2_ref, dout_ref, w_ref, g_ref, beta_ref,
             dp_ref, dw_ref, dbdw_ref, dlg_ref, dlb_ref, db1_ref, db2_ref, exty, extd, dyv, dwacc, shy, shd):
        i = pl.program_id(0)

        @pl.when(i == 0)
        def _():
            extd[tm:tm + HALO, :] = jnp.zeros((HALO, c), F32)
            for ref in (dw_ref, dbdw_ref, dlg_ref, dlb_ref, db1_ref, db2_ref, dwacc):
                ref[...] = jnp.zeros_like(ref)

        @pl.when(i > 0)
        def _():
            extd[tm:tm + HALO, :] = extd[0:HALO, :]

        a = a_ref[...]
        sg = _sigmoid(gt_ref[...])
        exty[HALO:HALO + tm, :] = a * sg
        exty[0:HALO, :] = jnp.where(i < n - 1, ah_ref[...] * _sigmoid(gh_ref[...]), 0.0)
        ycv = yc_ref[...]
        xc = ycv - jnp.mean(ycv, axis=-1, keepdims=True)
        rstd = lax.rsqrt(jnp.mean(xc * xc, axis=-1, keepdims=True) + EPS)
        xh = xc * rstd
        ln = xh * g_ref[...] + beta_ref[...]
        dln = dy2_ref[...] * _silu_grad(ln, _sigmoid(ln))
        dlg_ref[...] += jnp.sum(dln * xh, axis=0, keepdims=True)
        dlb_ref[...] += jnp.sum(dln, axis=0, keepdims=True)
        dxh = dln * g_ref[...]
        dyc = rstd * (dxh - jnp.mean(dxh, axis=-1, keepdims=True) - xh * jnp.mean(dxh * xh, axis=-1, keepdims=True))
        extd[0:tm, :] = dyc
        dbdw_ref[...] += jnp.sum(dyc, axis=0, keepdims=True)
        db2_ref[...] += jnp.sum(dout_ref[...], axis=0, keepdims=True)

        def store(rows, cols, block):
            dyv[rows, cols] = block

        _shifted_copies(extd, shd)
        _shifted_copies(exty, shy)
        _dwconv(shd, w_ref, [CONV_K - 1 - j for j in range(CONV_K)], tm, store)
        for r0 in range(0, tm, CONV_ROWS):
            for c0 in range(0, c, CONV_LANES):
                cols = slice(c0, c0 + CONV_LANES)
                dsub = extd[r0:r0 + CONV_ROWS, cols]
                for j in range(CONV_K):
                    prod = dsub * _shifted_rows(shy, HALO - CONV_K + 1 + j, r0, cols)
                    dwacc[8 * j:8 * j + 8, cols] += prod.reshape(CONV_ROWS // 8, 8, CONV_LANES).sum(axis=0)

        @pl.when(i == n - 1)
        def _():
            dw_ref[...] = dwacc[...].reshape(HALO, 8, c).sum(axis=1)

        dy = dyv[...]
        da = dy * sg
        dgt = dy * a * sg * (1.0 - sg)
        dp_ref[:, 0:c] = da.astype(BF16)
        dp_ref[:, c:2 * c] = dgt.astype(BF16)
        db1_ref[:, 0:c] += jnp.sum(da, axis=0, keepdims=True)
        db1_ref[:, c:2 * c] += jnp.sum(dgt, axis=0, keepdims=True)

    rev = lambda col: pl.BlockSpec((tm, c), lambda i: (n - 1 - i, col))
    halo = lambda col: pl.BlockSpec((HALO, c), lambda i: (jnp.maximum((n - 1 - i) * hb - 1, 0), col))
    vec = _full((1, c))
    step_is = lambda i: lambda: pl.program_id(0) == i
    kern, ins, outs, shapes, sems = _carried(body, 10, 7, send, "scatter",
                                             {"start": step_is(0), "finish": step_is(n - 1)})
    res = pl.pallas_call(
        kern, name="conf_mid_bwd", grid=(n,),
        in_specs=[rev(0), rev(1), halo(0), halo(1), rev(0), rev(0), rev(0), _full((HALO, c)), vec, vec] + ins,
        out_specs=[pl.BlockSpec((tm, 2 * c), lambda i: (n - 1 - i, 0)), _full((HALO, c)), vec, vec, vec,
                   _full((1, 2 * c)), vec] + outs,
        out_shape=[jax.ShapeDtypeStruct((s, 2 * c), BF16), jax.ShapeDtypeStruct((HALO, c), F32),
                   jax.ShapeDtypeStruct((1, c), F32), jax.ShapeDtypeStruct((1, c), F32), jax.ShapeDtypeStruct((1, c), F32),
                   jax.ShapeDtypeStruct((1, 2 * c), F32), jax.ShapeDtypeStruct((1, c), F32)] + shapes,
        scratch_shapes=[pltpu.VMEM((HALO + tm, c), F32), pltpu.VMEM((tm + HALO, c), F32), pltpu.VMEM((tm, c), F32),
                        pltpu.VMEM((8 * HALO, c), F32), pltpu.VMEM((8, HALO + tm, c), F32),
                        pltpu.VMEM((8, tm + HALO, c), F32)] + sems,
        compiler_params=_params(("arbitrary",), 56),
    )(p, p, p, p, yc, dy2, dout, wdw, lng, lnb, *send)
    return tuple(res[:7]) + (list(res[7:]),)


def _ffn_fwd(x, g, wup, wdw, bdw, wdn):
    s, d = x.shape
    ff = wdn.shape[0]
    tm = min(256, s)

    def body(x_ref, g_ref, wup_ref, wdw_ref, bdw_ref, wdn_ref, y_ref, h_ref, u_ref, carry):
        @pl.when(pl.program_id(0) == 0)
        def _():
            carry[...] = jnp.zeros_like(carry)

        xv = x_ref[...]
        r = lax.rsqrt(jnp.mean(xv * xv, axis=-1, keepdims=True) + EPS)
        h = (xv * r * g_ref[...]).astype(BF16)
        h_ref[...] = h
        acc = xv
        up = lambda c: (_dot(h, wup_ref[:, c:c + FF_CHUNK]), _dot(h, wup_ref[:, ff + c:ff + c + FF_CHUNK]))
        ahead = up(0)
        for c in range(0, ff, FF_CHUNK):
            cs = slice(c, c + FF_CHUNK)
            gp, val = ahead
            if c + FF_CHUNK < ff:
                ahead = up(c + FF_CHUNK)
            u_ref[:, cs] = gp.astype(BF16)
            u_ref[:, ff + c:ff + c + FF_CHUNK] = val.astype(BF16)
            prev = carry[:, cs]
            gate = (wdw_ref[0:1, cs] * _shift_down(gp, prev, 2) + wdw_ref[1:2, cs] * _shift_down(gp, prev, 1)
                    + wdw_ref[2:3, cs] * gp + bdw_ref[:, cs])
            act = gate * _sigmoid(gate) * val
            acc = acc + _dot(act.astype(BF16), wdn_ref[cs, :])
            carry[:, cs] = gp[tm - 8:tm, :]
        y_ref[...] = acc

    return pl.pallas_call(
        body, name="ffn_fwd", grid=(s // tm,),
        in_specs=[pl.BlockSpec((tm, d), lambda i: (i, 0)), _full((1, d)), _full((d, 2 * ff)), _full((8, ff)),
                  _full((1, ff)), _full((ff, d))],
        out_specs=[pl.BlockSpec((tm, d), lambda i: (i, 0)), pl.BlockSpec((tm, d), lambda i: (i, 0)),
                   pl.BlockSpec((tm, 2 * ff), lambda i: (i, 0))],
        out_shape=[jax.ShapeDtypeStruct((s, d), F32), jax.ShapeDtypeStruct((s, d), BF16),
                   jax.ShapeDtypeStruct((s, 2 * ff), BF16)],
        scratch_shapes=[pltpu.VMEM((8, ff), F32)],
        compiler_params=_params(("arbitrary",), 56),
    )(x, g, wup, wdw, bdw, wdn)


def _ffn_bwd(dy, u, wdw, bdw, wdn, send=()):
    s, d = dy.shape
    ff = wdn.shape[0]
    tm = min(256, s)
    n = s // tm
    hb = tm // 16

    def body(dy_ref, u_ref, uh_ref, wdw_ref, bdw_ref, wdn_ref, du_ref, act_ref, dw_ref, db_ref, carry):
        i = pl.program_id(0)

        @pl.when(i == 0)
        def _():
            carry[...] = jnp.zeros_like(carry)
            dw_ref[...] = jnp.zeros_like(dw_ref)
            db_ref[...] = jnp.zeros_like(db_ref)

        dyb = dy_ref[...].astype(BF16)
        for c in range(0, ff, FF_CHUNK):
            cs = slice(c, c + FF_CHUNK)
            vs = slice(ff + c, ff + c + FF_CHUNK)
            gp = u_ref[:, cs].astype(F32)
            val = u_ref[:, vs].astype(F32)
            prev = jnp.where(i < n - 1, uh_ref[:, cs].astype(F32)[8:16], 0.0)
            g1 = _shift_down(gp, prev, 1)
            g2 = _shift_down(gp, prev, 2)
            gate = wdw_ref[0:1, cs] * g2 + wdw_ref[1:2, cs] * g1 + wdw_ref[2:3, cs] * gp + bdw_ref[:, cs]
            sg = _sigmoid(gate)
            si = gate * sg
            act_ref[:, cs] = (si * val).astype(BF16)
            da = lax.dot_general(dyb, wdn_ref[cs, :], NT_DIMS, preferred_element_type=F32)
            dgate = da * val * _silu_grad(gate, sg)
            nxt = carry[:, cs]
            du_ref[:, cs] = (wdw_ref[2:3, cs] * dgate + wdw_ref[1:2, cs] * _shift_up(dgate, nxt, 1)
                             + wdw_ref[0:1, cs] * _shift_up(dgate, nxt, 2)).astype(BF16)
            du_ref[:, vs] = (da * si).astype(BF16)
            dw_ref[:, cs] += jnp.concatenate(
                [jnp.sum(dgate * g2, axis=0, keepdims=True), jnp.sum(dgate * g1, axis=0, keepdims=True),
                 jnp.sum(dgate * gp, axis=0, keepdims=True), jnp.zeros((5, FF_CHUNK), F32)], axis=0)
            db_ref[:, cs] += jnp.sum(dgate, axis=0, keepdims=True)
            carry[:, cs] = dgate[0:8, :]

    step_is = lambda i: lambda: pl.program_id(0) == i
    kern, ins, outs, shapes, sems = _carried(body, 6, 4, send, "scatter",
                                             {"start": step_is(0), "finish": step_is(n - 1)})
    res = pl.pallas_call(
        kern, name="ffn_bwd", grid=(n,),
        in_specs=[pl.BlockSpec((tm, d), lambda i: (n - 1 - i, 0)), pl.BlockSpec((tm, 2 * ff), lambda i: (n - 1 - i, 0)),
                  pl.BlockSpec((16, 2 * ff), lambda i: (jnp.maximum((n - 1 - i) * hb - 1, 0), 0)),
                  _full((8, ff)), _full((1, ff)), _full((ff, d))] + ins,
        out_specs=[pl.BlockSpec((tm, 2 * ff), lambda i: (n - 1 - i, 0)), pl.BlockSpec((tm, ff), lambda i: (n - 1 - i, 0)),
                   _full((8, ff)), _full((1, ff))] + outs,
        out_shape=[jax.ShapeDtypeStruct((s, 2 * ff), BF16), jax.ShapeDtypeStruct((s, ff), BF16),
                   jax.ShapeDtypeStruct((8, ff), F32), jax.ShapeDtypeStruct((1, ff), F32)] + shapes,
        scratch_shapes=[pltpu.VMEM((8, ff), F32)] + sems,
        compiler_params=_params(("arbitrary",), 48),
    )(dy, u, u, wdw, bdw, wdn, *send)
    return res[0], res[1], res[2], res[3], list(res[4:])


def _loss_head(y, target):
    s, d = y.shape
    tm = min(512, s)

    def body(y_ref, t_ref, l_ref, dy_ref):
        @pl.when(pl.program_id(0) == 0)
        def _():
            l_ref[...] = jnp.zeros_like(l_ref)

        err = y_ref[...] - t_ref[...]
        dy_ref[...] = err * (1.0 / d)
        l_ref[...] += 0.5 * jnp.sum(jnp.mean(err * err, axis=-1, keepdims=True), axis=0, keepdims=True)

    return pl.pallas_call(
        body, name="loss_head", grid=(s // tm,),
        in_specs=[pl.BlockSpec((tm, d), lambda i: (i, 0)), pl.BlockSpec((tm, d), lambda i: (i, 0))],
        out_specs=[_full((8, LANE)), pl.BlockSpec((tm, d), lambda i: (i, 0))],
        out_shape=[jax.ShapeDtypeStruct((8, LANE), F32), jax.ShapeDtypeStruct((s, d), F32)],
        compiler_params=_params(("arbitrary",), 32),
    )(y, target)


def _row_tile(rows, limit=512):
    for cand in range(min(limit, rows) // 16 * 16, 0, -16):
        if rows % cand == 0:
            return cand
    return rows


def _reduce_adamw(parts, w, m, v, name):
    _, nl, a, b = parts.shape
    ta = _row_tile(a, 256)

    def body(p_ref, w_ref, m_ref, v_ref, g_ref, d_ref, mo_ref, vo_ref):
        g = p_ref[0].astype(F32)
        for k in range(1, N_DEV):
            g = g + p_ref[k].astype(F32)
        g_ref[...] = g
        mn = ADAM_B1 * m_ref[...] + (1.0 - ADAM_B1) * g
        vn = ADAM_B2 * v_ref[...] + (1.0 - ADAM_B2) * (g * g)
        mo_ref[...] = mn
        vo_ref[...] = vn
        m_hat = mn / (1.0 - ADAM_B1 ** ADAM_STEP)
        v_hat = vn / (1.0 - ADAM_B2 ** ADAM_STEP)
        d_ref[...] = -ADAM_LR * (m_hat / (jnp.sqrt(v_hat) + ADAM_EPS) + ADAM_WD * w_ref[...])

    blk = pl.BlockSpec((1, ta, b), lambda l, i: (l, i, 0))
    return pl.pallas_call(
        body, name=name, grid=(nl, a // ta),
        in_specs=[pl.BlockSpec((N_DEV, 1, ta, b), lambda l, i: (0, l, i, 0)), blk, blk, blk],
        out_specs=[blk, blk, blk, blk],
        out_shape=[jax.ShapeDtypeStruct((nl, a, b), F32)] * 4,
        compiler_params=_params(("arbitrary", "arbitrary"), 48),
    )(parts, w, m, v)


def _unshard(w8, layer, name):
    _, _, k, n = w8.shape
    tk = _row_tile(k, 256)

    def body(w_ref, o_ref):
        for d in range(N_DEV):
            o_ref[:, d * n:(d + 1) * n] = w_ref[d, 0]

    return pl.pallas_call(
        body, name=name, grid=(k // tk,),
        in_specs=[pl.BlockSpec((N_DEV, 1, tk, n), lambda i: (0, layer, i, 0))],
        out_specs=pl.BlockSpec((tk, N_DEV * n), lambda i: (i, 0)),
        out_shape=jax.ShapeDtypeStruct((k, N_DEV * n), w8.dtype),
        compiler_params=_params(("arbitrary",), 32),
    )(w8)


def _shard_cast(g, name):
    k, n8 = g.shape
    n = n8 // N_DEV
    tk = _row_tile(k, 256)

    def body(g_ref, o_ref):
        for d in range(N_DEV):
            o_ref[d] = g_ref[:, d * n:(d + 1) * n].astype(BF16)

    return pl.pallas_call(
        body, name=name, grid=(k // tk,),
        in_specs=[pl.BlockSpec((tk, n8), lambda i: (i, 0))],
        out_specs=pl.BlockSpec((N_DEV, tk, n), lambda i: (0, i, 0)),
        out_shape=jax.ShapeDtypeStruct((N_DEV, k, n), BF16),
        compiler_params=_params(("arbitrary",), 32),
    )(g)


def _row(v):
    return v.reshape(1, -1)


def _group_ones():
    idx = jnp.arange(SB_WIDTH) // HEAD_DIM
    return (idx[:, None] == idx[None, :]).astype(BF16)


def _pad_rows(w, rows):
    return jnp.concatenate([w, jnp.zeros((rows - w.shape[0], w.shape[1]), w.dtype)], axis=0)


class _NoTraffic:
    def rest(self):
        return ()

    def install(self, wt, gathered):
        pass

    def ready(self, name, layer, grad):
        pass

    def take(self):
        return ()

    def landed(self, received):
        pass


def _local_step(x, target, wt, traffic):
    scale = HEAD_DIM ** -0.5
    bd = _group_ones()
    tril = jnp.tril(jnp.ones((CHUNK, CHUNK), dtype=bool))
    saved = []
    for i in range(DEPTH):
        j = i // 2
        lay = {"x_mix": x}
        if i % 2 == 0:
            proj, h = _rms_matmul(x, _row(wt["mix_norm_g"][i]), wt["sb_w_in"][j], jnp.zeros((1, IN_WIDTH), F32), "in_proj")
            gq = _row(jnp.tile(wt["sb_q_norm_g"][j], SB_WIDTH // HEAD_DIM)) * scale
            gk = _row(jnp.tile(wt["sb_k_norm_g"][j], SB_WIDTH // HEAD_DIM))
            qn, kn, vb = _qk_prep(proj, gq * LOG2E, gk, bd)
            o, lsum, gathered = _sb_attn_fwd(qn, kn, vb, gather=traffic.rest() if i == 0 else ())
            if i == 0:
                traffic.install(wt, gathered)
            wm = jnp.where(tril[None], wt["sg_w_spatial"][j], 0.0)
            wmb = wm.astype(BF16)
            wmt = jnp.swapaxes(wm, 1, 2).astype(BF16)
            bt = jnp.repeat(wt["sg_b_spatial"][j].T, HEAD_DIM, axis=1)
            gz = _row(wt["sg_z_norm_g"][j])
            gg = _sgu_fwd(proj, gz, wmb, bt, bd)
            x, mix = _out_proj(x, o, gg, wt["hyb_w_out"][j], "out_proj")
            lay.update(proj=proj, h=h, gq=gq, gk=gk, qn=qn, kn=kn, vb=vb, lsum=lsum, wmb=wmb, wmt=wmt, bt=bt, gz=gz, mix=mix)
        else:
            p, h = _rms_matmul(x, _row(wt["mix_norm_g"][i]), wt["cv_w_pw1"][j], _row(wt["cv_b_pw1"][j]), "conf_pw1")
            wdw = _pad_rows(wt["cv_w_dw"][j], HALO)
            yc, y2 = _conf_mid_fwd(p, wdw, _row(wt["cv_b_dw"][j]), _row(wt["cv_ln_g"][j]), _row(wt["cv_ln_b"][j]))
            x = _res_matmul(x, y2, wt["cv_w_pw2"][j], _row(wt["cv_b_pw2"][j]), "conf_pw2")
            lay.update(p=p, h=h, wdw=wdw, yc=yc, y2=y2)
        lay["x_ffn"] = x
        fdw = _pad_rows(wt["ffn_w_dw"][i], 8)
        x, hf, u = _ffn_fwd(x, _row(wt["ffn_norm_g"][i]), wt["ffn_w_up"][i], fdw, _row(wt["ffn_b_dw"][i]),
                            wt["ffn_w_down"][i])
        lay.update(hf=hf, u=u, fdw=fdw)
        saved.append(lay)

    lpart, dy = _loss_head(x, target)
    loss = lpart[0, 0]

    gr = {k: [None] * len(v) for k, v in wt.items()}

    def made(name, layer, grad):
        gr[name][layer] = grad
        traffic.ready(name, layer, grad)

    for i in reversed(range(DEPTH)):
        j = i // 2
        lay = saved[i]
        du, act, dfdw, dfb, got = _ffn_bwd(dy, lay["u"], lay["fdw"], _row(wt["ffn_b_dw"][i]), wt["ffn_w_down"][i],
                                           send=traffic.take())
        traffic.landed(got)
        made("ffn_w_down", i, _matmul_tn(act, dy, "ffn_dw_down"))
        made("ffn_w_up", i, _matmul_tn(lay["hf"], du, "ffn_dw_up"))
        gr["ffn_w_dw"][i] = dfdw[:FFN_K]
        gr["ffn_b_dw"][i] = dfb[0]
        dy, dgf = _nt_rms_bwd(du, wt["ffn_w_up"][i], lay["x_ffn"], _row(wt["ffn_norm_g"][i]), dy, "ffn_dx")
        gr["ffn_norm_g"][i] = dgf[0]
        if i % 2 == 0:
            dmix = _matmul_nt(dy, wt["hyb_w_out"][j], "out_proj_dx")
            made("hyb_w_out", j, _matmul_tn(lay["mix"], dy, "out_proj_dw"))
            dqn, dkn, dv, got = _sb_attn_bwd(lay["qn"], lay["kn"], lay["vb"], lay["lsum"], dmix, send=traffic.take())
            traffic.landed(got)
            dqk, dgq, dgk = _qk_bwd(lay["proj"], dqn, dkn, lay["gq"], lay["gk"] * LN2, bd)
            duz, dwm, dbt, dgz = _sgu_bwd(lay["proj"], dmix, lay["gz"], lay["wmb"], lay["wmt"], lay["bt"], bd)
            dproj = jnp.concatenate([dqk.astype(BF16), dv.astype(BF16), duz.astype(BF16)], axis=1)
            made("sb_w_in", j, _matmul_tn(lay["h"], dproj, "in_proj_dw"))
            gr["sb_q_norm_g"][j] = dgq.reshape(SB_WIDTH // HEAD_DIM, HEAD_DIM).sum(0) * scale
            gr["sb_k_norm_g"][j] = dgk.reshape(SB_WIDTH // HEAD_DIM, HEAD_DIM).sum(0) * LN2
            gr["sg_z_norm_g"][j] = dgz[0]
            gr["sg_w_spatial"][j] = jnp.where(tril[None], dwm, 0.0)
            gr["sg_b_spatial"][j] = dbt.reshape(CHUNK, SG_GROUPS, HEAD_DIM).sum(-1).T
            dy, dgm = _nt_rms_bwd(dproj, wt["sb_w_in"][j], lay["x_mix"], _row(wt["mix_norm_g"][i]), dy, "in_proj_dx")
        else:
            dy2 = _matmul_nt(dy, wt["cv_w_pw2"][j], "conf_pw2_dx")
            made("cv_w_pw2", j, _matmul_tn(lay["y2"], dy, "conf_pw2_dw"))
            dp, dwdw, dbdw, dlg, dlb, db1, db2, got = _conf_mid_bwd(
                lay["p"], lay["yc"], dy2, dy, lay["wdw"], _row(wt["cv_ln_g"][j]), _row(wt["cv_ln_b"][j]),
                send=traffic.take())
            traffic.landed(got)
            made("cv_w_pw1", j, _matmul_tn(lay["h"], dp, "conf_pw1_dw"))
            gr["cv_w_dw"][j] = dwdw[:CONV_K]
            gr["cv_b_dw"][j] = dbdw[0]
            gr["cv_ln_g"][j] = dlg[0]
            gr["cv_ln_b"][j] = dlb[0]
            gr["cv_b_pw1"][j] = db1[0]
            gr["cv_b_pw2"][j] = db2[0]
            dy, dgm = _nt_rms_bwd(dp, wt["cv_w_pw1"][j], lay["x_mix"], _row(wt["mix_norm_g"][i]), dy, "conf_pw1_dx")
        gr["mix_norm_g"][i] = dgm[0]
    matmul_weights = ("sb_w_in", "hyb_w_out", "cv_w_pw1", "cv_w_pw2", "ffn_w_up", "ffn_w_down")
    grads = {k: (v if k in matmul_weights else jnp.stack(v)) for k, v in gr.items()}
    return loss, dy, grads


WEIGHTS = ["mix_norm_g", "sb_w_in", "sb_q_norm_g", "sb_k_norm_g", "sg_z_norm_g", "sg_w_spatial", "sg_b_spatial",
           "hyb_w_out", "cv_w_pw1", "cv_b_pw1", "cv_w_dw", "cv_b_dw", "cv_ln_g", "cv_ln_b", "cv_w_pw2", "cv_b_pw2",
           "ffn_norm_g", "ffn_w_up", "ffn_w_dw", "ffn_b_dw", "ffn_w_down"]
BIG = [("sb_w_in", "col"), ("hyb_w_out", "row"), ("cv_w_pw1", "col"), ("cv_w_pw2", "row"), ("ffn_w_up", "col"),
       ("ffn_w_down", "row")]
SMALL = ["cv_b_pw1", "cv_w_dw", "cv_b_dw", "cv_ln_g", "cv_ln_b", "cv_b_pw2", "ffn_w_dw"]
REPLICATED = ["mix_norm_g", "sb_q_norm_g", "sb_k_norm_g", "sg_z_norm_g", "sg_w_spatial", "sg_b_spatial", "ffn_norm_g",
              "ffn_b_dw"]


def _last_dim_blocks(full):
    t = jnp.moveaxis(full.reshape(full.shape[:-1] + (N_DEV, full.shape[-1] // N_DEV)), -2, 0)
    return t.reshape(N_DEV, -1)


def _from_last_dim_blocks(blocks, shard_shape):
    t = jnp.moveaxis(blocks.reshape((N_DEV,) + tuple(shard_shape)), 0, -2)
    return t.reshape(tuple(shard_shape[:-1]) + (N_DEV * shard_shape[-1],))


def _pack(arrays):
    lead = arrays[0].shape[:-1]
    flat = jnp.concatenate([a.astype(F32) for a in arrays], axis=-1)
    rows = -(-flat.shape[-1] // (16 * LANE)) * 16
    pad = rows * LANE - flat.shape[-1]
    if pad:
        flat = jnp.concatenate([flat, jnp.zeros(lead + (pad,), F32)], axis=-1)
    return flat.reshape(lead + (rows, LANE))


def _unpack(packed, shapes):
    flat = packed.reshape(-1)
    out, off = [], 0
    for shp in shapes:
        size = 1
        for dim in shp:
            size *= dim
        out.append(flat[off:off + size].reshape(shp))
        off += size
    return out


class _ShardTraffic:
    def __init__(self, w):
        self.w = w
        self.queue, self.flying, self.received = [], [], {}

    def rest(self):
        return [self.w["sb_w_in"][1:].astype(BF16)] + [self.w[n].astype(BF16) for n, _ in BIG[1:]]

    def install(self, wt, gathered):
        wt["sb_w_in"].append(_unshard(gathered[0], 0, "unshard_sb_w_in"))
        for (n, kind), w8 in zip(BIG[1:], gathered[1:]):
            if kind == "col":
                wt[n] = [_unshard(w8, l, "unshard_" + n) for l in range(w8.shape[1])]
            else:
                wt[n] = [w8[:, l].reshape((N_DEV * w8.shape[2],) + w8.shape[3:]) for l in range(w8.shape[1])]

    def ready(self, name, layer, grad):
        if dict(BIG)[name] == "col":
            blocks = _shard_cast(grad, "shard_" + name)
        else:
            blocks = grad.reshape((N_DEV, grad.shape[0] // N_DEV) + grad.shape[1:])
        self.queue.append(((name, layer), blocks))

    def take(self):
        self.flying = [key for key, _ in self.queue]
        arrays = [blocks for _, blocks in self.queue]
        self.queue = []
        return arrays

    def landed(self, received):
        for key, blocks in zip(self.flying, received):
            self.received[key] = blocks
        self.flying = []


def kernel(x, mix_norm_g, sb_w_in, sb_q_norm_g, sb_k_norm_g, sg_z_norm_g, sg_w_spatial, sg_b_spatial, hyb_w_out, cv_w_pw1, cv_b_pw1, cv_w_dw, cv_b_dw, cv_ln_g, cv_ln_b, cv_w_pw2, cv_b_pw2, ffn_norm_g, ffn_w_up, ffn_w_dw, ffn_b_dw, ffn_w_down, loss_target, m_mix_norm_g, m_sb_w_in, m_sb_q_norm_g, m_sb_k_norm_g, m_sg_z_norm_g, m_sg_w_spatial, m_sg_b_spatial, m_hyb_w_out, m_cv_w_pw1, m_cv_b_pw1, m_cv_w_dw, m_cv_b_dw, m_cv_ln_g, m_cv_ln_b, m_cv_w_pw2, m_cv_b_pw2, m_ffn_norm_g, m_ffn_w_up, m_ffn_w_dw, m_ffn_b_dw, m_ffn_w_down, v_mix_norm_g, v_sb_w_in, v_sb_q_norm_g, v_sb_k_norm_g, v_sg_z_norm_g, v_sg_w_spatial, v_sg_b_spatial, v_hyb_w_out, v_cv_w_pw1, v_cv_b_pw1, v_cv_w_dw, v_cv_b_dw, v_cv_ln_g, v_cv_ln_b, v_cv_w_pw2, v_cv_b_pw2, v_ffn_norm_g, v_ffn_w_up, v_ffn_w_dw, v_ffn_b_dw, v_ffn_w_down):
    w = dict(zip(WEIGHTS, (mix_norm_g, sb_w_in, sb_q_norm_g, sb_k_norm_g, sg_z_norm_g, sg_w_spatial, sg_b_spatial,
                           hyb_w_out, cv_w_pw1, cv_b_pw1, cv_w_dw, cv_b_dw, cv_ln_g, cv_ln_b, cv_w_pw2, cv_b_pw2,
                           ffn_norm_g, ffn_w_up, ffn_w_dw, ffn_b_dw, ffn_w_down)))
    m = dict(zip(WEIGHTS, (m_mix_norm_g, m_sb_w_in, m_sb_q_norm_g, m_sb_k_norm_g, m_sg_z_norm_g, m_sg_w_spatial,
                           m_sg_b_spatial, m_hyb_w_out, m_cv_w_pw1, m_cv_b_pw1, m_cv_w_dw, m_cv_b_dw, m_cv_ln_g,
                           m_cv_ln_b, m_cv_w_pw2, m_cv_b_pw2, m_ffn_norm_g, m_ffn_w_up, m_ffn_w_dw, m_ffn_b_dw,
                           m_ffn_w_down)))
    v = dict(zip(WEIGHTS, (v_mix_norm_g, v_sb_w_in, v_sb_q_norm_g, v_sb_k_norm_g, v_sg_z_norm_g, v_sg_w_spatial,
                           v_sg_b_spatial, v_hyb_w_out, v_cv_w_pw1, v_cv_b_pw1, v_cv_w_dw, v_cv_b_dw, v_cv_ln_g,
                           v_cv_ln_b, v_cv_w_pw2, v_cv_b_pw2, v_ffn_norm_g, v_ffn_w_up, v_ffn_w_dw, v_ffn_b_dw,
                           v_ffn_w_down)))
    big_names = [n for n, _ in BIG]
    flat = lambda t, names: [t[n].reshape(-1) for n in names]

    first_in, small_all = _all_gather([w["sb_w_in"][0:1].astype(BF16), _pack(flat(w, SMALL))], "gather_first")
    wt = {n: w[n] for n in REPLICATED}
    wt["sb_w_in"] = [_unshard(first_in, 0, "unshard_sb_w_in")]
    small_parts = zip(*[_unpack(small_all[d], [(w[n].size,) for n in SMALL]) for d in range(N_DEV)])
    for n, parts in zip(SMALL, small_parts):
        wt[n] = _from_last_dim_blocks(jnp.stack(parts), w[n].shape)

    traffic = _ShardTraffic(w)
    loss, gx, grads = _local_step(x[0], loss_target[0], wt, traffic)
    traffic.landed(_all_to_all(traffic.take(), "exchange_last"))
    recv_big = [jnp.stack([traffic.received[n, l] for l in range(w[n].shape[0])], axis=1) for n in big_names]

    grep = _pack(flat(grads, REPLICATED))
    gsmall = _pack([_last_dim_blocks(grads[n]) for n in SMALL])
    rep_rows, small_rows = grep.shape[0], gsmall.shape[1]
    vec = jnp.concatenate([grep, gsmall.reshape(N_DEV * small_rows, LANE)], axis=0)
    vec_all = _all_gather([vec], "gather_small_grads")[0]
    me = 4 * lax.axis_index("x") + 2 * lax.axis_index("y") + lax.axis_index("c")
    recv_rep = vec_all[:, :rep_rows]
    recv_small = lax.dynamic_slice(vec_all, (0, rep_rows + small_rows * me, 0), (N_DEV, small_rows, LANE))

    out = {}
    kinds = ("grad", "delta", "new_m", "new_v")
    for n, parts in zip(big_names, recv_big):
        for kind, arr in zip(kinds, _reduce_adamw(parts, w[n], m[n], v[n], "adamw_" + n)):
            out[kind, n] = arr
    for names, recv, tag in ((SMALL, recv_small, "adamw_small"), (REPLICATED, recv_rep, "adamw_replicated")):
        res = _reduce_adamw(recv[:, None], _pack(flat(w, names))[None], _pack(flat(m, names))[None],
                            _pack(flat(v, names))[None], tag)
        shapes = [w[n].shape for n in names]
        for kind, packed in zip(kinds, res):
            for n, arr in zip(names, _unpack(packed[0], shapes)):
                out[kind, n] = arr

    loss = lax.psum(loss, ("x", "y", "c"))
    return (loss, gx[None], *[out[kind, n] for kind in kinds for n in WEIGHTS])
```

```python
import functools

import jax
import jax.numpy as jnp
from jax import lax
from jax.experimental import pallas as pl
from jax.experimental.pallas import tpu as pltpu

F32 = jnp.float32
BF16 = jnp.bfloat16

D_MODEL = 1024
HEAD_DIM = 64
SB_WIDTH = 512
SG_WIDTH = 512
SG_GROUPS = 8
IN_WIDTH = 3 * SB_WIDTH + 2 * SG_WIDTH
CHUNK = 128
CONV_K = 31
D_FF = 2816
FFN_K = 3
DEPTH = 4
EPS = 1e-6
N_DEV = 8
LANE = 128
HALO = 32
ATT_BLOCK = 256
FF_CHUNK = 256
CONV_ROWS = 32
CONV_LANES = 512
MIB = 2 ** 20

ADAM_LR = 0.001
ADAM_B1 = 0.9
ADAM_B2 = 0.999
ADAM_EPS = 1e-08
ADAM_WD = 0.01
ADAM_STEP = 10

LOG2E = 1.4426950408889634
LN2 = 0.6931471805599453

NT_DIMS = (((1,), (1,)), ((), ()))
TN_DIMS = (((0,), (0,)), ((), ()))


def _params(semantics, vmem_mib):
    return pltpu.CompilerParams(dimension_semantics=semantics, vmem_limit_bytes=vmem_mib * MIB)


def _full(shape):
    nd = len(shape)
    return pl.BlockSpec(shape, lambda *_: (0,) * nd)


def _sigmoid(x):
    return 1.0 / (1.0 + jnp.exp(-x))


def _gelu(x):
    return 0.5 * x * (1.0 + lax.erf(x * 0.7071067811865476))


def _gelu_grad(x):
    return 0.5 * (1.0 + lax.erf(x * 0.7071067811865476)) + x * jnp.exp(-0.5 * x * x) * 0.3989422804014327


def _silu_grad(x, s):
    return s * (1.0 + x * (1.0 - s))


def _dot(a, b):
    return jnp.dot(a, b, preferred_element_type=F32)


def _dot2(a, b):
    hi = a.astype(BF16)
    lo = (a - hi.astype(F32)).astype(BF16)
    return _dot(hi, b) + _dot(lo, b)


def _group_mean(t, bd):
    return _dot2(t, bd) * (1.0 / HEAD_DIM)


def _shift_down(v, prev8, s):
    top = pltpu.roll(jnp.concatenate([prev8, v[:8]], axis=0), s, 0)[8:16]
    return jnp.concatenate([top, pltpu.roll(v, s, 0)[8:]], axis=0)


def _shift_up(v, next8, s):
    n = v.shape[0]
    bottom = pltpu.roll(jnp.concatenate([v[n - 8:], next8], axis=0), 16 - s, 0)[0:8]
    return jnp.concatenate([pltpu.roll(v, n - s, 0)[: n - 8], bottom], axis=0)


def _mesh_pos():
    return lax.axis_index("x"), lax.axis_index("y"), lax.axis_index("c")


def _comm_scratch(n):
    return [pltpu.SemaphoreType.DMA((7 * n,)), pltpu.SemaphoreType.DMA((7 * n,)), pltpu.SemaphoreType.DMA((n,))]


class _Scatter:
    def __init__(self, src_refs, out_refs, send_sems, recv_sems, local_sems):
        x, y, cc = _mesh_pos()
        me = 4 * x + 2 * y + cc
        self.copies, self.mine = [], []
        for a, (src, out) in enumerate(zip(src_refs, out_refs)):
            self.mine.append(pltpu.make_async_copy(src.at[me], out.at[me], local_sems.at[a]))
            for k in range(1, N_DEV):
                px = 1 - x if k & 4 else x
                py = 1 - y if k & 2 else y
                pc = 1 - cc if k & 1 else cc
                self.copies.append(pltpu.make_async_remote_copy(
                    src_ref=src.at[4 * px + 2 * py + pc], dst_ref=out.at[me],
                    send_sem=send_sems.at[7 * a + k - 1], recv_sem=recv_sems.at[7 * a + k - 1],
                    device_id=(px, py, pc), device_id_type=pl.DeviceIdType.MESH))

    def start(self):
        for cp in self.mine + self.copies:
            cp.start()

    def finish(self):
        for cp in self.copies + self.mine:
            cp.wait()


class _Gather:
    def __init__(self, x_refs, out_refs, send_sems, recv_sems, local_sems):
        x, y, cc = _mesh_pos()
        self.n = len(x_refs)
        self.me, self.sibling, self.cc = (x, y, cc), (x, y, 1 - cc), cc
        self.chips = [(1 - x, y), (x, 1 - y), (1 - x, 1 - y)]
        self.x_refs, self.out_refs, self.send_sems, self.recv_sems = x_refs, out_refs, send_sems, recv_sems
        self.mine = [pltpu.make_async_copy(x_refs[a], out_refs[a].at[4 * x + 2 * y + cc], local_sems.at[a])
                     for a in range(self.n)]

    def copy(self, a, k, block, to, own=False):
        slot = self.out_refs[a].at[4 * block[0] + 2 * block[1] + block[2]]
        return pltpu.make_async_remote_copy(
            src_ref=self.x_refs[a] if own else slot, dst_ref=slot,
            send_sem=self.send_sems.at[7 * a + k], recv_sem=self.recv_sems.at[7 * a + k],
            device_id=to, device_id_type=pl.DeviceIdType.MESH)

    def first_hop(self, a):
        return [self.copy(a, 0, self.me, self.sibling, own=True)] + [
            self.copy(a, 1 + j, self.me, (*chip, self.cc), own=True) for j, chip in enumerate(self.chips)]

    def passed_on(self, a):
        return [self.copy(a, 4 + j, (*chip, self.cc), self.sibling) for j, chip in enumerate(self.chips)]

    def start(self):
        for a in range(self.n):
            self.mine[a].start()
        for a in range(self.n):
            for cp in self.first_hop(a):
                cp.start()

    def forward(self):
        for a in range(self.n):
            for j, chip in enumerate(self.chips):
                self.copy(a, 1 + j, (*chip, self.cc), self.me).wait_recv()
                self.copy(a, 4 + j, (*chip, self.cc), self.sibling).start()

    def finish(self):
        for a in range(self.n):
            self.copy(a, 0, self.sibling, self.me).wait_recv()
            for j, chip in enumerate(self.chips):
                self.copy(a, 4 + j, (*chip, 1 - self.cc), self.me).wait_recv()
        for a in range(self.n):
            for cp in self.first_hop(a) + self.passed_on(a):
                cp.wait_send()
        for cp in self.mine:
            cp.wait()


def _carried(body, n_in, n_out, arrays, kind, when):
    n = len(arrays)
    if n == 0:
        return body, [], [], [], []

    def wrapped(*refs):
        ins, srcs = refs[:n_in], refs[n_in:n_in + n]
        outs, landed = refs[n_in + n:n_in + n + n_out], refs[n_in + n + n_out:n_in + 2 * n + n_out]
        scratch = refs[n_in + 2 * n + n_out:]
        comm = (_Scatter if kind == "scatter" else _Gather)(srcs, landed, *scratch[-3:])
        pl.when(when["start"]())(comm.start)
        if kind == "gather":
            pl.when(when["forward"]())(comm.forward)
        body(*ins, *outs, *scratch[:-3])
        pl.when(when["finish"]())(comm.finish)

    any_spec = pl.BlockSpec(memory_space=pl.ANY)
    if kind == "scatter":
        shapes = [jax.ShapeDtypeStruct(a.shape, a.dtype) for a in arrays]
    else:
        shapes = [jax.ShapeDtypeStruct((N_DEV,) + a.shape, a.dtype) for a in arrays]
    return wrapped, [any_spec] * n, [any_spec] * n, shapes, _comm_scratch(n)


def _all_gather(shards, name):
    n = len(shards)

    def body(*refs):
        comm = _Gather(refs[:n], refs[n:2 * n], *refs[2 * n:])
        comm.start()
        comm.forward()
        comm.finish()

    any_spec = pl.BlockSpec(memory_space=pl.ANY)
    return pl.pallas_call(
        body, name=name, in_specs=[any_spec] * n, out_specs=[any_spec] * n,
        out_shape=[jax.ShapeDtypeStruct((N_DEV,) + s.shape, s.dtype) for s in shards],
        scratch_shapes=_comm_scratch(n),
    )(*shards)


def _rms_matmul(x, g, w, b, name):
    s, d = x.shape
    n = w.shape[1]
    tm = min(512, s)

    def body(x_ref, g_ref, w_ref, b_ref, y_ref, h_ref):
        xv = x_ref[...]
        r = lax.rsqrt(jnp.mean(xv * xv, axis=-1, keepdims=True) + EPS)
        h = (xv * r * g_ref[...]).astype(BF16)
        h_ref[...] = h
        for c in range(0, n, 512):
            y_ref[:, c:c + 512] = _dot(h, w_ref[:, c:c + 512]) + b_ref[:, c:c + 512]

    return pl.pallas_call(
        body, name=name, grid=(s // tm,),
        in_specs=[pl.BlockSpec((tm, d), lambda i: (i, 0)), _full((1, d)), _full((d, n)), _full((1, n))],
        out_specs=[pl.BlockSpec((tm, n), lambda i: (i, 0)), pl.BlockSpec((tm, d), lambda i: (i, 0))],
        out_shape=[jax.ShapeDtypeStruct((s, n), F32), jax.ShapeDtypeStruct((s, d), BF16)],
        compiler_params=_params(("arbitrary",), 48),
    )(x, g, w, b)


def _matmul_nt(a, w, name):
    s, n = a.shape
    k = w.shape[0]
    tm = min(512, s)

    def body(a_ref, w_ref, o_ref):
        o_ref[...] = lax.dot_general(a_ref[...].astype(BF16), w_ref[...], NT_DIMS, preferred_element_type=F32)

    return pl.pallas_call(
        body, name=name, grid=(s // tm,),
        in_specs=[pl.BlockSpec((tm, n), lambda i: (i, 0)), _full((k, n))],
        out_specs=pl.BlockSpec((tm, k), lambda i: (i, 0)),
        out_shape=jax.ShapeDtypeStruct((s, k), F32),
        compiler_params=_params(("arbitrary",), 40),
    )(a, w)


def _matmul_tn(a, b, name):
    s, k = a.shape
    n = b.shape[1]
    ts = min(2048 if k <= 1024 else 1024, s)
    tn = 1024 if (n % 1024 == 0 and k <= 1024) else 512
    steps = s // ts

    def body(a_ref, b_ref, o_ref, acc):
        t = pl.program_id(1)

        @pl.when(t == 0)
        def _():
            acc[...] = jnp.zeros_like(acc)

        acc[...] += lax.dot_general(a_ref[...].astype(BF16), b_ref[...].astype(BF16), TN_DIMS,
                                    preferred_element_type=F32)

        @pl.when(t == steps - 1)
        def _():
            o_ref[...] = acc[...].astype(BF16)

    return pl.pallas_call(
        body, name=name, grid=(n // tn, steps),
        in_specs=[pl.BlockSpec((ts, k), lambda j, t: (t, 0)), pl.BlockSpec((ts, tn), lambda j, t: (t, j))],
        out_specs=pl.BlockSpec((k, tn), lambda j, t: (0, j)),
        out_shape=jax.ShapeDtypeStruct((k, n), BF16),
        scratch_shapes=[pltpu.VMEM((k, tn), F32)],
        compiler_params=_params(("arbitrary", "arbitrary"), 48),
    )(a, b)


def _nt_rms_bwd(dp, w, x, g, dres, name, send=()):
    s, n = dp.shape
    d = x.shape[1]
    tm = min(256, s)
    steps = s // tm

    def body(dp_ref, w_ref, x_ref, g_ref, dres_ref, dx_ref, dg_ref):
        @pl.when(pl.program_id(0) == 0)
        def _():
            dg_ref[...] = jnp.zeros_like(dg_ref)

        dh = lax.dot_general(dp_ref[...], w_ref[...], NT_DIMS, preferred_element_type=F32)
        xv = x_ref[...]
        r = lax.rsqrt(jnp.mean(xv * xv, axis=-1, keepdims=True) + EPS)
        xh = xv * r
        dg_ref[...] += jnp.sum(dh * xh, axis=0, keepdims=True)
        dn = dh * g_ref[...]
        dx_ref[...] = dres_ref[...] + r * (dn - xh * jnp.mean(dn * xh, axis=-1, keepdims=True))

    step_is = lambda i: lambda: pl.program_id(0) == i
    kern, ins, outs, shapes, sems = _carried(body, 5, 2, send, "scatter",
                                             {"start": step_is(0), "finish": step_is(steps - 1)})
    res = pl.pallas_call(
        kern, name=name, grid=(steps,),
        in_specs=[pl.BlockSpec((tm, n), lambda i: (i, 0)), _full((d, n)), pl.BlockSpec((tm, d), lambda i: (i, 0)),
                  _full((1, d)), pl.BlockSpec((tm, d), lambda i: (i, 0))] + ins,
        out_specs=[pl.BlockSpec((tm, d), lambda i: (i, 0)), _full((1, d))] + outs,
        out_shape=[jax.ShapeDtypeStruct((s, d), F32), jax.ShapeDtypeStruct((1, d), F32)] + shapes,
        scratch_shapes=sems,
        compiler_params=_params(("arbitrary",), 52),
    )(dp, w, x, g, dres, *send)
    return res[0], res[1], list(res[2:])


def _res_matmul(x, a, w, b, name):
    s, d = x.shape
    k = a.shape[1]
    tm = min(512, s)

    def body(x_ref, a_ref, w_ref, b_ref, o_ref):
        o_ref[...] = x_ref[...] + _dot(a_ref[...], w_ref[...]) + b_ref[...]

    return pl.pallas_call(
        body, name=name, grid=(s // tm,),
        in_specs=[pl.BlockSpec((tm, d), lambda i: (i, 0)), pl.BlockSpec((tm, k), lambda i: (i, 0)), _full((k, d)),
                  _full((1, d))],
        out_specs=pl.BlockSpec((tm, d), lambda i: (i, 0)),
        out_shape=jax.ShapeDtypeStruct((s, d), F32),
        compiler_params=_params(("arbitrary",), 32),
    )(x, a, w, b)


def _out_proj(x, o, gg, w, name):
    s, d = x.shape
    tm = min(512, s)

    def body(x_ref, o_ref, gg_ref, w_ref, y_ref, mix_ref):
        mix = jnp.concatenate([o_ref[...], gg_ref[...]], axis=1).astype(BF16)
        mix_ref[...] = mix
        y_ref[...] = x_ref[...] + _dot(mix, w_ref[...])

    return pl.pallas_call(
        body, name=name, grid=(s // tm,),
        in_specs=[pl.BlockSpec((tm, d), lambda i: (i, 0)), pl.BlockSpec((tm, SB_WIDTH), lambda i: (i, 0)),
                  pl.BlockSpec((tm, SG_WIDTH), lambda i: (i, 0)), _full((d, d))],
        out_specs=[pl.BlockSpec((tm, d), lambda i: (i, 0)), pl.BlockSpec((tm, d), lambda i: (i, 0))],
        out_shape=[jax.ShapeDtypeStruct((s, d), F32), jax.ShapeDtypeStruct((s, d), BF16)],
        compiler_params=_params(("arbitrary",), 32),
    )(x, o, gg, w)


def _qk_prep(proj, gq, gk, bd):
    s = proj.shape[0]
    tm = min(512, s)

    def body(q_ref, k_ref, v_ref, gq_ref, gk_ref, bd_ref, qn_ref, kn_ref, vb_ref):
        bdv = bd_ref[...]
        q = q_ref[...]
        k = k_ref[...]
        qn_ref[...] = (q * lax.rsqrt(_group_mean(q * q, bdv) + EPS) * gq_ref[...]).astype(BF16)
        kn_ref[...] = (k * lax.rsqrt(_group_mean(k * k, bdv) + EPS) * gk_ref[...]).astype(BF16)
        vb_ref[...] = v_ref[...].astype(BF16)

    col = lambda c: pl.BlockSpec((tm, SB_WIDTH), lambda i: (i, c))
    out = pl.BlockSpec((tm, SB_WIDTH), lambda i: (i, 0))
    return pl.pallas_call(
        body, name="qk_prep", grid=(s // tm,),
        in_specs=[col(0), col(1), col(2), _full((1, SB_WIDTH)), _full((1, SB_WIDTH)), _full((SB_WIDTH, SB_WIDTH))],
        out_specs=[out, out, out],
        out_shape=[jax.ShapeDtypeStruct((s, SB_WIDTH), BF16)] * 3,
        compiler_params=_params(("arbitrary",), 32),
    )(proj, proj, proj, gq, gk, bd)


def _qk_bwd(proj, dqn, dkn, dv, duz, gq, gk, bd):
    s = proj.shape[0]
    tm = min(512, s)

    def body(q_ref, k_ref, dq_ref, dk_ref, dv_ref, duz_ref, gq_ref, gk_ref, bd_ref, o_ref, dgq_ref, dgk_ref):
        @pl.when(pl.program_id(0) == 0)
        def _():
            dgq_ref[...] = jnp.zeros_like(dgq_ref)
            dgk_ref[...] = jnp.zeros_like(dgk_ref)

        bdv = bd_ref[...]

        def back(t, gain, dout, dg_ref):
            r = lax.rsqrt(_group_mean(t * t, bdv) + EPS)
            th = t * r
            dg_ref[...] += jnp.sum(dout * th, axis=0, keepdims=True)
            dn = dout * gain
            return r * (dn - th * _group_mean(dn * th, bdv))

        o_ref[:, 0:SB_WIDTH] = back(q_ref[...], gq_ref[...], dq_ref[...], dgq_ref).astype(BF16)
        o_ref[:, SB_WIDTH:2 * SB_WIDTH] = back(k_ref[...], gk_ref[...], dk_ref[...], dgk_ref).astype(BF16)
        o_ref[:, 2 * SB_WIDTH:3 * SB_WIDTH] = dv_ref[...].astype(BF16)
        o_ref[:, 3 * SB_WIDTH:IN_WIDTH] = duz_ref[...].astype(BF16)

    col = lambda c: pl.BlockSpec((tm, SB_WIDTH), lambda i: (i, c))
    row = pl.BlockSpec((tm, SB_WIDTH), lambda i: (i, 0))
    return pl.pallas_call(
        body, name="qk_bwd", grid=(s // tm,),
        in_specs=[col(0), col(1), row, row, row, pl.BlockSpec((tm, 2 * SG_WIDTH), lambda i: (i, 0)),
                  _full((1, SB_WIDTH)), _full((1, SB_WIDTH)), _full((SB_WIDTH, SB_WIDTH))],
        out_specs=[pl.BlockSpec((tm, IN_WIDTH), lambda i: (i, 0)), _full((1, SB_WIDTH)), _full((1, SB_WIDTH))],
        out_shape=[jax.ShapeDtypeStruct((s, IN_WIDTH), BF16), jax.ShapeDtypeStruct((1, SB_WIDTH), F32),
                   jax.ShapeDtypeStruct((1, SB_WIDTH), F32)],
        compiler_params=_params(("arbitrary",), 40),
    )(proj, proj, dqn, dkn, dv, duz, gq, gk, bd)


def _attn_masks(tq, tk):
    lane = lax.broadcasted_iota(jnp.int32, (tk, LANE), 1)
    heads = [(lane >= hh * HEAD_DIM) & (lane < (hh + 1) * HEAD_DIM) for hh in range(2)]
    urow = lax.broadcasted_iota(jnp.int32, (tk, tk), 0)
    ucol = lax.broadcasted_iota(jnp.int32, (tk, tk), 1)
    return heads, urow, ucol


def _keep_cost(z2):
    return jnp.log(1.0 + jnp.exp2(jnp.minimum(z2, 126.0)))


def _sb_attn_fwd(q2, kn, vb, gather=()):
    s = q2.shape[0]
    tk = min(ATT_BLOCK, s)
    tq = min(2 * ATT_BLOCK, s)
    r = tq // tk
    nq = s // tq
    assert s // tk <= LANE

    def body(q_ref, k_ref, v_ref, o_ref, ls_ref):
        i = pl.program_id(1)
        heads, urow, ucol = _attn_masks(tq, tk)
        u_incl = (urow >= ucol).astype(BF16)
        row = lax.broadcasted_iota(jnp.int32, (tq, tk), 0)
        col = lax.broadcasted_iota(jnp.int32, (tq, tk), 1)
        qlane = lax.broadcasted_iota(jnp.int32, (tq, LANE), 1)
        q = q_ref[...]

        def blocks(kbs, state, masked):
            carry, acc, ls = list(state[0:2]), state[2], list(state[3:5])
            chains = [(d, hh) for d in range(len(kbs)) for hh in range(2)]
            kblk, vblk, valid = {}, {}, {}
            for d, kb in enumerate(kbs):
                off = pl.multiple_of(kb * tk, tk)
                kfull = k_ref[pl.ds(off, tk), :]
                vfull = v_ref[pl.ds(off, tk), :]
                if masked:
                    valid[d] = (kb * tk + col) < (i * tq + row)
                for hh in range(2):
                    kblk[d, hh] = jnp.where(heads[hh], kfull, jnp.zeros((), BF16))
                    vblk[d, hh] = jnp.where(heads[hh], vfull, jnp.zeros((), BF16))
            z2 = {c: lax.dot_general(q, kblk[c], NT_DIMS, preferred_element_type=F32) for c in chains}
            cost = {}
            for c in chains:
                cost[c] = _keep_cost(z2[c])
                if masked:
                    cost[c] = jnp.where(valid[c[0]], cost[c], 0.0)
            sums = {c: _dot(cost[c].astype(BF16), u_incl) for c in chains}
            a = {}
            for d, hh in chains:
                rin = carry[hh] + sums[d, hh]
                a[d, hh] = jnp.exp2(z2[d, hh] - rin * LOG2E)
                if masked:
                    a[d, hh] = jnp.where(valid[d], a[d, hh], 0.0)
                rs = jnp.sum(cost[d, hh], axis=1, keepdims=True)
                ls[hh] = ls[hh] + jnp.where(qlane == kbs[d], rs, 0.0)
                carry[hh] = carry[hh] + rs
            for c in chains:
                acc = acc + _dot(a[c].astype(BF16), vblk[c])
            return carry[0], carry[1], acc, ls[0], ls[1]

        zc = jnp.zeros((tq, 1), F32)
        zt = jnp.zeros((tq, LANE), F32)
        state = blocks([r * i + r - 1 - d for d in range(r)], (zc, zc, zt, zt, zt), True)
        pairs = lax.shift_right_logical(i, 1)
        state = lax.fori_loop(
            0, pairs, lambda n, st: blocks([r * i - 1 - 2 * r * n - d for d in range(2 * r)], st, False), state)
        state = lax.fori_loop(
            0, i - 2 * pairs, lambda n, st: blocks([r - 1 - d for d in range(r)], st, False), state)
        o_ref[...] = state[2]
        ls_ref[:, 0:LANE] = state[3]
        ls_ref[:, LANE:2 * LANE] = state[4]

    last_pair = SB_WIDTH // LANE - 1
    step_is = lambda p, i: lambda: (pl.program_id(0) == p) & (pl.program_id(1) == i)
    when = {"start": step_is(0, 0), "forward": step_is(last_pair, 0), "finish": step_is(last_pair, nq - 1)}
    kern, ins, outs, shapes, sems = _carried(body, 3, 2, gather, "gather", when)
    res = pl.pallas_call(
        kern, name="sb_attn_fwd", grid=(SB_WIDTH // LANE, nq),
        in_specs=[pl.BlockSpec((tq, LANE), lambda p, i: (i, p)), pl.BlockSpec((s, LANE), lambda p, i: (0, p)),
                  pl.BlockSpec((s, LANE), lambda p, i: (0, p))] + ins,
        out_specs=[pl.BlockSpec((tq, LANE), lambda p, i: (i, p)),
                   pl.BlockSpec((tq, 2 * LANE), lambda p, i: (i, p))] + outs,
        out_shape=[jax.ShapeDtypeStruct((s, SB_WIDTH), F32), jax.ShapeDtypeStruct((s, 2 * SB_WIDTH), F32)] + shapes,
        scratch_shapes=sems,
        compiler_params=_params(("arbitrary", "arbitrary"), 48),
    )(q2, kn, vb, *gather)
    return res[0], res[1], list(res[2:])


def _sb_attn_bwd(q2, kn, vb, lsum, dmix, send=()):
    s = q2.shape[0]
    tk = min(ATT_BLOCK, s)
    tq = min(2 * ATT_BLOCK, s)
    r = tq // tk
    nq = s // tq

    def body(q_ref, k_ref, v_ref, ls_ref, do_ref, dq_ref, dk_ref, dv_ref):
        i = pl.program_id(1)

        @pl.when(i == 0)
        def _():
            dk_ref[...] = jnp.zeros_like(dk_ref)
            dv_ref[...] = jnp.zeros_like(dv_ref)

        heads, urow, ucol = _attn_masks(tq, tk)
        u_incl = (urow >= ucol).astype(BF16)
        u_pre = (urow <= ucol).astype(BF16)
        lrow = lax.broadcasted_iota(jnp.int32, (LANE, LANE), 0)
        lcol = lax.broadcasted_iota(jnp.int32, (LANE, LANE), 1)
        u_after = (lrow > lcol).astype(BF16)
        row = lax.broadcasted_iota(jnp.int32, (tq, tk), 0)
        col = lax.broadcasted_iota(jnp.int32, (tq, tk), 1)
        qlane = lax.broadcasted_iota(jnp.int32, (tq, LANE), 1)
        qheads = [(qlane >= hh * HEAD_DIM) & (qlane < (hh + 1) * HEAD_DIM) for hh in range(2)]
        q = q_ref[...]
        dob = do_ref[...].astype(BF16)
        qm = [jnp.where(qheads[hh], q, jnp.zeros((), BF16)) for hh in range(2)]
        dom = [jnp.where(qheads[hh], dob, jnp.zeros((), BF16)) for hh in range(2)]
        after = []
        for hh in range(2):
            ls = ls_ref[:, hh * LANE:(hh + 1) * LANE]
            hi = ls.astype(BF16)
            mid = (ls - hi.astype(F32)).astype(BF16)
            lo = (ls - hi.astype(F32) - mid.astype(F32)).astype(BF16)
            after.append(_dot(hi, u_after) + _dot(mid, u_after) + _dot(lo, u_after))

        def blocks(kbs, state, masked):
            cp, dq = list(state[0:2]), state[2]
            chains = [(d, hh) for d in range(len(kbs)) for hh in range(2)]
            offs, kblk, vblk, valid = {}, {}, {}, {}
            for d, kb in enumerate(kbs):
                offs[d] = pl.multiple_of(kb * tk, tk)
                kfull = k_ref[pl.ds(offs[d], tk), :]
                vblk[d] = v_ref[pl.ds(offs[d], tk), :]
                if masked:
                    valid[d] = (kb * tk + col) < (i * tq + row)
                for hh in range(2):
                    kblk[d, hh] = jnp.where(heads[hh], kfull, jnp.zeros((), BF16))
            z2 = {c: lax.dot_general(q, kblk[c], NT_DIMS, preferred_element_type=F32) for c in chains}
            da = {(d, hh): lax.dot_general(dom[hh], vblk[d], NT_DIMS, preferred_element_type=F32) for d, hh in chains}
            cost, sig = {}, {}
            for c in chains:
                cost[c] = _keep_cost(z2[c])
                sig[c] = jnp.exp2(z2[c] - cost[c] * LOG2E)
                if masked:
                    cost[c] = jnp.where(valid[c[0]], cost[c], 0.0)
            sums = {c: _dot(cost[c].astype(BF16), u_incl) for c in chains}
            a, g = {}, {}
            for d, hh in chains:
                cr = jnp.sum(jnp.where(qlane == kbs[d], after[hh], 0.0), axis=1, keepdims=True)
                a[d, hh] = jnp.exp2(z2[d, hh] - (cr + sums[d, hh]) * LOG2E)
                if masked:
                    a[d, hh] = jnp.where(valid[d], a[d, hh], 0.0)
                g[d, hh] = da[d, hh] * a[d, hh]
            pre = {c: _dot(g[c].astype(BF16), u_pre) for c in chains}
            dzb = {}
            for d, hh in chains:
                dz = g[d, hh] - sig[d, hh] * (cp[hh] + pre[d, hh])
                if masked:
                    dz = jnp.where(valid[d], dz, 0.0)
                dzb[d, hh] = dz.astype(BF16)
                cp[hh] = cp[hh] + jnp.sum(g[d, hh], axis=1, keepdims=True)
            for d in range(len(kbs)):
                dv_ref[pl.ds(offs[d], tk), :] += sum(
                    lax.dot_general(a[d, hh].astype(BF16), dom[hh], TN_DIMS, preferred_element_type=F32) for hh in range(2))
                dk_ref[pl.ds(offs[d], tk), :] += sum(
                    lax.dot_general(dzb[d, hh], qm[hh], TN_DIMS, preferred_element_type=F32) for hh in range(2))
            for c in chains:
                dq = dq + _dot(dzb[c], kblk[c])
            return cp[0], cp[1], dq

        zc = jnp.zeros((tq, 1), F32)
        pairs = lax.shift_right_logical(i, 1)
        state = lax.fori_loop(0, pairs, lambda n, st: blocks([2 * r * n + d for d in range(2 * r)], st, False),
                              (zc, zc, jnp.zeros((tq, LANE), F32)))
        state = lax.fori_loop(0, i - 2 * pairs,
                              lambda n, st: blocks([2 * r * pairs + d for d in range(r)], st, False), state)
        state = blocks([r * i + d for d in range(r)], state, True)
        dq_ref[...] = state[2]

    blk = pl.BlockSpec((tq, LANE), lambda p, i: (i, p))
    whole = pl.BlockSpec((s, LANE), lambda p, i: (0, p))
    last_pair = SB_WIDTH // LANE - 1
    step_is = lambda p, i: lambda: (pl.program_id(0) == p) & (pl.program_id(1) == i)
    kern, ins, outs, shapes, sems = _carried(body, 5, 3, send, "scatter",
                                             {"start": step_is(0, 0), "finish": step_is(last_pair, nq - 1)})
    res = pl.pallas_call(
        kern, name="sb_attn_bwd", grid=(SB_WIDTH // LANE, nq),
        in_specs=[blk, whole, whole, pl.BlockSpec((tq, 2 * LANE), lambda p, i: (i, p)), blk] + ins,
        out_specs=[blk, whole, whole] + outs,
        out_shape=[jax.ShapeDtypeStruct((s, SB_WIDTH), F32)] * 3 + shapes,
        scratch_shapes=sems,
        compiler_params=_params(("arbitrary", "arbitrary"), 56),
    )(q2, kn, vb, lsum, dmix, *send)
    return res[0], res[1], res[2], list(res[3:])


def _sgu_spatial(zn, wm_ref, lane, c):
    parts = []
    for p in range(SG_WIDTH // LANE):
        blk = zn[c * CHUNK:(c + 1) * CHUNK, p * LANE:(p + 1) * LANE].astype(BF16)
        lo = jnp.where(lane < HEAD_DIM, blk, jnp.zeros((), BF16))
        hi = jnp.where(lane >= HEAD_DIM, blk, jnp.zeros((), BF16))
        parts.append(_dot(wm_ref[2 * p], lo) + _dot(wm_ref[2 * p + 1], hi))
    return jnp.concatenate(parts, axis=1)


def _sgu_fwd(proj, gz, wm, bt, bd):
    s = proj.shape[0]
    tm = min(512, s)

    def body(u_ref, z_ref, gz_ref, wm_ref, bt_ref, bd_ref, o_ref):
        lane = lax.broadcasted_iota(jnp.int32, (CHUNK, LANE), 1)
        ug = _gelu(u_ref[...])
        zg = _gelu(z_ref[...])
        zn = zg * lax.rsqrt(_group_mean(zg * zg, bd_ref[...]) + EPS) * gz_ref[...]
        for c in range(tm // CHUNK):
            sp = _sgu_spatial(zn, wm_ref, lane, c) + bt_ref[...]
            o_ref[c * CHUNK:(c + 1) * CHUNK, :] = ug[c * CHUNK:(c + 1) * CHUNK, :] * sp

    col = lambda c: pl.BlockSpec((tm, SG_WIDTH), lambda i: (i, c))
    return pl.pallas_call(
        body, name="sgu_fwd", grid=(s // tm,),
        in_specs=[col(3), col(4), _full((1, SG_WIDTH)), _full((SG_GROUPS, CHUNK, CHUNK)), _full((CHUNK, SG_WIDTH)),
                  _full((SG_WIDTH, SG_WIDTH))],
        out_specs=pl.BlockSpec((tm, SG_WIDTH), lambda i: (i, 0)),
        out_shape=jax.ShapeDtypeStruct((s, SG_WIDTH), F32),
        compiler_params=_params(("arbitrary",), 32),
    )(proj, proj, gz, wm, bt, bd)


def _sgu_bwd(proj, dmix, gz, wm, wmt, bt, bd):
    s = proj.shape[0]
    tm = min(512, s)

    def body(u_ref, z_ref, dg_ref, gz_ref, wm_ref, wmt_ref, bt_ref, bd_ref, o_ref, dwm_ref, dbt_ref, dgz_ref):
        @pl.when(pl.program_id(0) == 0)
        def _():
            dwm_ref[...] = jnp.zeros_like(dwm_ref)
            dbt_ref[...] = jnp.zeros_like(dbt_ref)
            dgz_ref[...] = jnp.zeros_like(dgz_ref)

        lane = lax.broadcasted_iota(jnp.int32, (CHUNK, LANE), 1)
        bdv = bd_ref[...]
        u = u_ref[...]
        z = z_ref[...]
        ug = _gelu(u)
        zg = _gelu(z)
        r = lax.rsqrt(_group_mean(zg * zg, bdv) + EPS)
        zh = zg * r
        zn = zh * gz_ref[...]
        dzn_rows = []
        for c in range(tm // CHUNK):
            rows = slice(c * CHUNK, (c + 1) * CHUNK)
            sp = _sgu_spatial(zn, wm_ref, lane, c) + bt_ref[...]
            dgg = dg_ref[rows, :]
            ds = dgg * ug[rows, :]
            o_ref[rows, 0:SG_WIDTH] = dgg * sp * _gelu_grad(u[rows, :])
            dbt_ref[...] += ds
            parts = []
            for p in range(SG_WIDTH // LANE):
                dsb = ds[:, p * LANE:(p + 1) * LANE].astype(BF16)
                znb = zn[rows, p * LANE:(p + 1) * LANE].astype(BF16)
                acc = jnp.zeros((CHUNK, LANE), F32)
                for hh in range(2):
                    hm = (lane >= hh * HEAD_DIM) & (lane < (hh + 1) * HEAD_DIM)
                    dsm = jnp.where(hm, dsb, jnp.zeros((), BF16))
                    znm = jnp.where(hm, znb, jnp.zeros((), BF16))
                    acc = acc + _dot(wmt_ref[2 * p + hh], dsm)
                    dwm_ref[2 * p + hh] += lax.dot_general(dsm, znm, NT_DIMS, preferred_element_type=F32)
                parts.append(acc)
            dzn_rows.append(jnp.concatenate(parts, axis=1))
        dzn = jnp.concatenate(dzn_rows, axis=0)
        dgz_ref[...] += jnp.sum(dzn * zh, axis=0, keepdims=True)
        dn = dzn * gz_ref[...]
        o_ref[:, SG_WIDTH:2 * SG_WIDTH] = r * (dn - zh * _group_mean(dn * zh, bdv)) * _gelu_grad(z)

    col = lambda c: pl.BlockSpec((tm, SG_WIDTH), lambda i: (i, c))
    wspec = _full((SG_GROUPS, CHUNK, CHUNK))
    return pl.pallas_call(
        body, name="sgu_bwd", grid=(s // tm,),
        in_specs=[col(3), col(4), pl.BlockSpec((tm, SG_WIDTH), lambda i: (i, 1)), _full((1, SG_WIDTH)), wspec, wspec,
                  _full((CHUNK, SG_WIDTH)), _full((SG_WIDTH, SG_WIDTH))],
        out_specs=[pl.BlockSpec((tm, 2 * SG_WIDTH), lambda i: (i, 0)), wspec, _full((CHUNK, SG_WIDTH)),
                   _full((1, SG_WIDTH))],
        out_shape=[jax.ShapeDtypeStruct((s, 2 * SG_WIDTH), F32), jax.ShapeDtypeStruct((SG_GROUPS, CHUNK, CHUNK), F32),
                   jax.ShapeDtypeStruct((CHUNK, SG_WIDTH), F32), jax.ShapeDtypeStruct((1, SG_WIDTH), F32)],
        compiler_params=_params(("arbitrary",), 40),
    )(proj, proj, dmix, gz, wm, wmt, bt, bd)


def _shifted_copies(ext, sh):
    e = ext[...]
    sh[0] = e
    for b in range(1, 8):
        sh[b] = pltpu.roll(e, e.shape[0] - b, 0)


def _shifted_rows(sh, off, r0, cols):
    start = r0 + off - off % 8
    return sh[off % 8, start:start + CONV_ROWS, cols]


def _dwconv(sh, w_ref, offsets, rows, store):
    for r0 in range(0, rows, CONV_ROWS):
        for c0 in range(0, sh.shape[2], CONV_LANES):
            cols = slice(c0, c0 + CONV_LANES)
            acc = jnp.zeros((CONV_ROWS, CONV_LANES), F32)
            for j, off in enumerate(offsets):
                acc = acc + w_ref[j:j + 1, cols] * _shifted_rows(sh, off, r0, cols)
            store(slice(r0, r0 + CONV_ROWS), cols, acc)


def _conf_mid_fwd(p, wdw, bdw, lng, lnb):
    s = p.shape[0]
    c = p.shape[1] // 2
    tm = min(256, s)

    def body(a_ref, gt_ref, w_ref, b_ref, g_ref, beta_ref, yc_ref, y2_ref, ext, sh):
        i = pl.program_id(0)

        @pl.when(i == 0)
        def _():
            ext[0:HALO, :] = jnp.zeros((HALO, c), F32)

        @pl.when(i > 0)
        def _():
            ext[0:HALO, :] = ext[tm:tm + HALO, :]

        ext[HALO:HALO + tm, :] = a_ref[...] * _sigmoid(gt_ref[...])

        def store(rows, cols, block):
            yc_ref[rows, cols] = block + b_ref[:, cols]

        _shifted_copies(ext, sh)
        _dwconv(sh, w_ref, [HALO - CONV_K + 1 + j for j in range(CONV_K)], tm, store)
        acc = yc_ref[...]
        xc = acc - jnp.mean(acc, axis=-1, keepdims=True)
        ln = xc * lax.rsqrt(jnp.mean(xc * xc, axis=-1, keepdims=True) + EPS) * g_ref[...] + beta_ref[...]
        y2_ref[...] = (ln * _sigmoid(ln)).astype(BF16)

    vec = _full((1, c))
    return pl.pallas_call(
        body, name="conf_mid_fwd", grid=(s // tm,),
        in_specs=[pl.BlockSpec((tm, c), lambda i: (i, 0)), pl.BlockSpec((tm, c), lambda i: (i, 1)), _full((HALO, c)), vec,
                  vec, vec],
        out_specs=[pl.BlockSpec((tm, c), lambda i: (i, 0)), pl.BlockSpec((tm, c), lambda i: (i, 0))],
        out_shape=[jax.ShapeDtypeStruct((s, c), F32), jax.ShapeDtypeStruct((s, c), BF16)],
        scratch_shapes=[pltpu.VMEM((HALO + tm, c), F32), pltpu.VMEM((8, HALO + tm, c), F32)],
        compiler_params=_params(("arbitrary",), 40),
    )(p, p, wdw, bdw, lng, lnb)


def _conf_mid_bwd(p, yc, dy2, dout, wdw, lng, lnb, send=()):
    s = p.shape[0]
    c = p.shape[1] // 2
    tm = min(256, s)
    n = s // tm
    hb = tm // HALO

    def body(a_ref, gt_ref, ah_ref, gh_ref, yc_ref, dy2_ref, dout_ref, w_ref, g_ref, beta_ref,
             dp_ref, dw_ref, dbdw_ref, dlg_ref, dlb_ref, db1_ref, db2_ref, exty, extd, dyv, dwacc, shy, shd):
        i = pl.program_id(0)

        @pl.when(i == 0)
        def _():
            extd[tm:tm + HALO, :] = jnp.zeros((HALO, c), F32)
            for ref in (dw_ref, dbdw_ref, dlg_ref, dlb_ref, db1_ref, db2_ref, dwacc):
                ref[...] = jnp.zeros_like(ref)

        @pl.when(i > 0)
        def _():
            extd[tm:tm + HALO, :] = extd[0:HALO, :]

        a = a_ref[...]
        sg = _sigmoid(gt_ref[...])
        exty[HALO:HALO + tm, :] = a * sg
        exty[0:HALO, :] = jnp.where(i < n - 1, ah_ref[...] * _sigmoid(gh_ref[...]), 0.0)
        ycv = yc_ref[...]
        xc = ycv - jnp.mean(ycv, axis=-1, keepdims=True)
        rstd = lax.rsqrt(jnp.mean(xc * xc, axis=-1, keepdims=True) + EPS)
        xh = xc * rstd
        ln = xh * g_ref[...] + beta_ref[...]
        dln = dy2_ref[...] * _silu_grad(ln, _sigmoid(ln))
        dlg_ref[...] += jnp.sum(dln * xh, axis=0, keepdims=True)
        dlb_ref[...] += jnp.sum(dln, axis=0, keepdims=True)
        dxh = dln * g_ref[...]
        dyc = rstd * (dxh - jnp.mean(dxh, axis=-1, keepdims=True) - xh * jnp.mean(dxh * xh, axis=-1, keepdims=True))
        extd[0:tm, :] = dyc
        dbdw_ref[...] += jnp.sum(dyc, axis=0, keepdims=True)
        db2_ref[...] += jnp.sum(dout_ref[...], axis=0, keepdims=True)

        def store(rows, cols, block):
            dyv[rows, cols] = block

        _shifted_copies(extd, shd)
        _shifted_copies(exty, shy)
        _dwconv(shd, w_ref, [CONV_K - 1 - j for j in range(CONV_K)], tm, store)
        for r0 in range(0, tm, CONV_ROWS):
            for c0 in range(0, c, CONV_LANES):
                cols = slice(c0, c0 + CONV_LANES)
                dsub = extd[r0:r0 + CONV_ROWS, cols]
                for j in range(CONV_K):
                    prod = dsub * _shifted_rows(shy, HALO - CONV_K + 1 + j, r0, cols)
                    dwacc[8 * j:8 * j + 8, cols] += prod.reshape(CONV_ROWS // 8, 8, CONV_LANES).sum(axis=0)

        @pl.when(i == n - 1)
        def _():
            dw_ref[...] = dwacc[...].reshape(HALO, 8, c).sum(axis=1)

        dy = dyv[...]
        da = dy * sg
        dgt = dy * a * sg * (1.0 - sg)
        dp_ref[:, 0:c] = da.astype(BF16)
        dp_ref[:, c:2 * c] = dgt.astype(BF16)
        db1_ref[:, 0:c] += jnp.sum(da, axis=0, keepdims=True)
        db1_ref[:, c:2 * c] += jnp.sum(dgt, axis=0, keepdims=True)

    rev = lambda col: pl.BlockSpec((tm, c), lambda i: (n - 1 - i, col))
    halo = lambda col: pl.BlockSpec((HALO, c), lambda i: (jnp.maximum((n - 1 - i) * hb - 1, 0), col))
    vec = _full((1, c))
    step_is = lambda i: lambda: pl.program_id(0) == i
    kern, ins, outs, shapes, sems = _carried(body, 10, 7, send, "scatter",
                                             {"start": step_is(0), "finish": step_is(n - 1)})
    res = pl.pallas_call(
        kern, name="conf_mid_bwd", grid=(n,),
        in_specs=[rev(0), rev(1), halo(0), halo(1), rev(0), rev(0), rev(0), _full((HALO, c)), vec, vec] + ins,
        out_specs=[pl.BlockSpec((tm, 2 * c), lambda i: (n - 1 - i, 0)), _full((HALO, c)), vec, vec, vec,
                   _full((1, 2 * c)), vec] + outs,
        out_shape=[jax.ShapeDtypeStruct((s, 2 * c), BF16), jax.ShapeDtypeStruct((HALO, c), F32),
                   jax.ShapeDtypeStruct((1, c), F32), jax.ShapeDtypeStruct((1, c), F32), jax.ShapeDtypeStruct((1, c), F32),
                   jax.ShapeDtypeStruct((1, 2 * c), F32), jax.ShapeDtypeStruct((1, c), F32)] + shapes,
        scratch_shapes=[pltpu.VMEM((HALO + tm, c), F32), pltpu.VMEM((tm + HALO, c), F32), pltpu.VMEM((tm, c), F32),
                        pltpu.VMEM((8 * HALO, c), F32), pltpu.VMEM((8, HALO + tm, c), F32),
                        pltpu.VMEM((8, tm + HALO, c), F32)] + sems,
        compiler_params=_params(("arbitrary",), 56),
    )(p, p, p, p, yc, dy2, dout, wdw, lng, lnb, *send)
    return tuple(res[:7]) + (list(res[7:]),)


def _ffn_fwd(x, g, wup, wdw, bdw, wdn):
    s, d = x.shape
    ff = wdn.shape[0]
    tm = min(256, s)

    def body(x_ref, g_ref, wup_ref, wdw_ref, bdw_ref, wdn_ref, y_ref, h_ref, u_ref, carry):
        @pl.when(pl.program_id(0) == 0)
        def _():
            carry[...] = jnp.zeros_like(carry)

        xv = x_ref[...]
        r = lax.rsqrt(jnp.mean(xv * xv, axis=-1, keepdims=True) + EPS)
        h = (xv * r * g_ref[...]).astype(BF16)
        h_ref[...] = h
        acc = xv
        up = lambda c: (_dot(h, wup_ref[:, c:c + FF_CHUNK]), _dot(h, wup_ref[:, ff + c:ff + c + FF_CHUNK]))
        ahead = up(0)
        for c in range(0, ff, FF_CHUNK):
            cs = slice(c, c + FF_CHUNK)
            gp, val = ahead
            if c + FF_CHUNK < ff:
                ahead = up(c + FF_CHUNK)
            u_ref[:, cs] = gp.astype(BF16)
            u_ref[:, ff + c:ff + c + FF_CHUNK] = val.astype(BF16)
            prev = carry[:, cs]
            gate = (wdw_ref[0:1, cs] * _shift_down(gp, prev, 2) + wdw_ref[1:2, cs] * _shift_down(gp, prev, 1)
                    + wdw_ref[2:3, cs] * gp + bdw_ref[:, cs])
            act = gate * _sigmoid(gate) * val
            acc = acc + _dot(act.astype(BF16), wdn_ref[cs, :])
            carry[:, cs] = gp[tm - 8:tm, :]
        y_ref[...] = acc

    return pl.pallas_call(
        body, name="ffn_fwd", grid=(s // tm,),
        in_specs=[pl.BlockSpec((tm, d), lambda i: (i, 0)), _full((1, d)), _full((d, 2 * ff)), _full((8, ff)),
                  _full((1, ff)), _full((ff, d))],
        out_specs=[pl.BlockSpec((tm, d), lambda i: (i, 0)), pl.BlockSpec((tm, d), lambda i: (i, 0)),
                   pl.BlockSpec((tm, 2 * ff), lambda i: (i, 0))],
        out_shape=[jax.ShapeDtypeStruct((s, d), F32), jax.ShapeDtypeStruct((s, d), BF16),
                   jax.ShapeDtypeStruct((s, 2 * ff), BF16)],
        scratch_shapes=[pltpu.VMEM((8, ff), F32)],
        compiler_params=_params(("arbitrary",), 56),
    )(x, g, wup, wdw, bdw, wdn)


def _ffn_bwd(dy, u, wdw, bdw, wdn, send=()):
    s, d = dy.shape
    ff = wdn.shape[0]
    tm = min(256, s)
    n = s // tm
    hb = tm // 16

    def body(dy_ref, u_ref, uh_ref, wdw_ref, bdw_ref, wdn_ref, du_ref, act_ref, dw_ref, db_ref, carry):
        i = pl.program_id(0)

        @pl.when(i == 0)
        def _():
            carry[...] = jnp.zeros_like(carry)
            dw_ref[...] = jnp.zeros_like(dw_ref)
            db_ref[...] = jnp.zeros_like(db_ref)

        dyb = dy_ref[...].astype(BF16)
        for c in range(0, ff, FF_CHUNK):
            cs = slice(c, c + FF_CHUNK)
            vs = slice(ff + c, ff + c + FF_CHUNK)
            gp = u_ref[:, cs].astype(F32)
            val = u_ref[:, vs].astype(F32)
            prev = jnp.where(i < n - 1, uh_ref[:, cs].astype(F32)[8:16], 0.0)
            g1 = _shift_down(gp, prev, 1)
            g2 = _shift_down(gp, prev, 2)
            gate = wdw_ref[0:1, cs] * g2 + wdw_ref[1:2, cs] * g1 + wdw_ref[2:3, cs] * gp + bdw_ref[:, cs]
            sg = _sigmoid(gate)
            si = gate * sg
            act_ref[:, cs] = (si * val).astype(BF16)
            da = lax.dot_general(dyb, wdn_ref[cs, :], NT_DIMS, preferred_element_type=F32)
            dgate = da * val * _silu_grad(gate, sg)
            nxt = carry[:, cs]
            du_ref[:, cs] = (wdw_ref[2:3, cs] * dgate + wdw_ref[1:2, cs] * _shift_up(dgate, nxt, 1)
                             + wdw_ref[0:1, cs] * _shift_up(dgate, nxt, 2)).astype(BF16)
            du_ref[:, vs] = (da * si).astype(BF16)
            dw_ref[:, cs] += jnp.concatenate(
                [jnp.sum(dgate * g2, axis=0, keepdims=True), jnp.sum(dgate * g1, axis=0, keepdims=True),
                 jnp.sum(dgate * gp, axis=0, keepdims=True), jnp.zeros((5, FF_CHUNK), F32)], axis=0)
            db_ref[:, cs] += jnp.sum(dgate, axis=0, keepdims=True)
            carry[:, cs] = dgate[0:8, :]

    step_is = lambda i: lambda: pl.program_id(0) == i
    kern, ins, outs, shapes, sems = _carried(body, 6, 4, send, "scatter",
                                             {"start": step_is(0), "finish": step_is(n - 1)})
    res = pl.pallas_call(
        kern, name="ffn_bwd", grid=(n,),
        in_specs=[pl.BlockSpec((tm, d), lambda i: (n - 1 - i, 0)), pl.BlockSpec((tm, 2 * ff), lambda i: (n - 1 - i, 0)),
                  pl.BlockSpec((16, 2 * ff), lambda i: (jnp.maximum((n - 1 - i) * hb - 1, 0), 0)),
                  _full((8, ff)), _full((1, ff)), _full((ff, d))] + ins,
        out_specs=[pl.BlockSpec((tm, 2 * ff), lambda i: (n - 1 - i, 0)), pl.BlockSpec((tm, ff), lambda i: (n - 1 - i, 0)),
                   _full((8, ff)), _full((1, ff))] + outs,
        out_shape=[jax.ShapeDtypeStruct((s, 2 * ff), BF16), jax.ShapeDtypeStruct((s, ff), BF16),
                   jax.ShapeDtypeStruct((8, ff), F32), jax.ShapeDtypeStruct((1, ff), F32)] + shapes,
        scratch_shapes=[pltpu.VMEM((8, ff), F32)] + sems,
        compiler_params=_params(("arbitrary",), 48),
    )(dy, u, u, wdw, bdw, wdn, *send)
    return res[0], res[1], res[2], res[3], list(res[4:])


def _loss_head(y, target):
    s, d = y.shape
    tm = min(512, s)

    def body(y_ref, t_ref, l_ref, dy_ref):
        @pl.when(pl.program_id(0) == 0)
        def _():
            l_ref[...] = jnp.zeros_like(l_ref)

        err = y_ref[...] - t_ref[...]
        dy_ref[...] = err * (1.0 / d)
        l_ref[...] += 0.5 * jnp.sum(jnp.mean(err * err, axis=-1, keepdims=True), axis=0, keepdims=True)

    return pl.pallas_call(
        body, name="loss_head", grid=(s // tm,),
        in_specs=[pl.BlockSpec((tm, d), lambda i: (i, 0)), pl.BlockSpec((tm, d), lambda i: (i, 0))],
        out_specs=[_full((8, LANE)), pl.BlockSpec((tm, d), lambda i: (i, 0))],
        out_shape=[jax.ShapeDtypeStruct((8, LANE), F32), jax.ShapeDtypeStruct((s, d), F32)],
        compiler_params=_params(("arbitrary",), 32),
    )(y, target)


def _row_tile(rows, limit=512):
    for cand in range(min(limit, rows) // 16 * 16, 0, -16):
        if rows % cand == 0:
            return cand
    return rows


def _reduce_adamw(parts, w, m, v, name):
    nl = len(parts)
    _, a, b = parts[0].shape
    ta = _row_tile(a, 256)

    def body(*refs):
        p_refs = refs[:nl]
        w_ref, m_ref, v_ref, g_ref, d_ref, mo_ref, vo_ref = refs[nl:]
        for layer in range(nl):
            @pl.when(pl.program_id(0) == layer)
            def _(p_ref=p_refs[layer]):
                g = p_ref[0].astype(F32)
                for k in range(1, N_DEV):
                    g = g + p_ref[k].astype(F32)
                g_ref[0] = g

        g = g_ref[0]
        mn = ADAM_B1 * m_ref[...] + (1.0 - ADAM_B1) * g
        vn = ADAM_B2 * v_ref[...] + (1.0 - ADAM_B2) * (g * g)
        mo_ref[...] = mn
        vo_ref[...] = vn
        m_hat = mn / (1.0 - ADAM_B1 ** ADAM_STEP)
        v_hat = vn / (1.0 - ADAM_B2 ** ADAM_STEP)
        d_ref[...] = -ADAM_LR * (m_hat / (jnp.sqrt(v_hat) + ADAM_EPS) + ADAM_WD * w_ref[...])

    blk = pl.BlockSpec((1, ta, b), lambda l, i: (l, i, 0))
    part = lambda layer: pl.BlockSpec((N_DEV, ta, b), lambda l, i: (0, jnp.where(l == layer, i, 0), 0))
    return pl.pallas_call(
        body, name=name, grid=(nl, a // ta),
        in_specs=[part(layer) for layer in range(nl)] + [blk, blk, blk],
        out_specs=[blk, blk, blk, blk],
        out_shape=[jax.ShapeDtypeStruct((nl, a, b), F32)] * 4,
        compiler_params=_params(("arbitrary", "arbitrary"), 56),
    )(*parts, w, m, v)


def _unshard(w8, layer, name):
    _, _, k, n = w8.shape
    tk = _row_tile(k, 256)

    def body(w_ref, o_ref):
        for d in range(N_DEV):
            o_ref[:, d * n:(d + 1) * n] = w_ref[d, 0]

    return pl.pallas_call(
        body, name=name, grid=(k // tk,),
        in_specs=[pl.BlockSpec((N_DEV, 1, tk, n), lambda i: (0, layer, i, 0))],
        out_specs=pl.BlockSpec((tk, N_DEV * n), lambda i: (i, 0)),
        out_shape=jax.ShapeDtypeStruct((k, N_DEV * n), w8.dtype),
        compiler_params=_params(("arbitrary",), 32),
    )(w8)


def _shard_cast(g, name):
    k, n8 = g.shape
    n = n8 // N_DEV
    tk = _row_tile(k, 256)

    def body(g_ref, o_ref):
        for d in range(N_DEV):
            o_ref[d] = g_ref[:, d * n:(d + 1) * n].astype(BF16)

    return pl.pallas_call(
        body, name=name, grid=(k // tk,),
        in_specs=[pl.BlockSpec((tk, n8), lambda i: (i, 0))],
        out_specs=pl.BlockSpec((N_DEV, tk, n), lambda i: (0, i, 0)),
        out_shape=jax.ShapeDtypeStruct((N_DEV, k, n), BF16),
        compiler_params=_params(("arbitrary",), 32),
    )(g)


def _row(v):
    return v.reshape(1, -1)


def _group_ones():
    idx = jnp.arange(SB_WIDTH) // HEAD_DIM
    return (idx[:, None] == idx[None, :]).astype(BF16)


def _pad_rows(w, rows):
    return jnp.concatenate([w, jnp.zeros((rows - w.shape[0], w.shape[1]), w.dtype)], axis=0)


class _NoTraffic:
    def rest(self):
        return ()

    def install(self, wt, gathered):
        pass

    def ready(self, name, layer, grad):
        pass

    def take(self):
        return ()

    def landed(self, received):
        pass


def _local_step(x, target, wt, traffic):
    scale = HEAD_DIM ** -0.5
    bd = _group_ones()
    tril = jnp.tril(jnp.ones((CHUNK, CHUNK), dtype=bool))
    saved = []
    for i in range(DEPTH):
        j = i // 2
        lay = {"x_mix": x}
        if i % 2 == 0:
            proj, h = _rms_matmul(x, _row(wt["mix_norm_g"][i]), wt["sb_w_in"][j], jnp.zeros((1, IN_WIDTH), F32), "in_proj")
            gq = _row(jnp.tile(wt["sb_q_norm_g"][j], SB_WIDTH // HEAD_DIM)) * scale
            gk = _row(jnp.tile(wt["sb_k_norm_g"][j], SB_WIDTH // HEAD_DIM))
            qn, kn, vb = _qk_prep(proj, gq * LOG2E, gk, bd)
            o, lsum, gathered = _sb_attn_fwd(qn, kn, vb, gather=traffic.rest() if i == 0 else ())
            if i == 0:
                traffic.install(wt, gathered)
            wm = jnp.where(tril[None], wt["sg_w_spatial"][j], 0.0)
            wmb = wm.astype(BF16)
            wmt = jnp.swapaxes(wm, 1, 2).astype(BF16)
            bt = jnp.repeat(wt["sg_b_spatial"][j].T, HEAD_DIM, axis=1)
            gz = _row(wt["sg_z_norm_g"][j])
            gg = _sgu_fwd(proj, gz, wmb, bt, bd)
            x, mix = _out_proj(x, o, gg, wt["hyb_w_out"][j], "out_proj")
            lay.update(proj=proj, h=h, gq=gq, gk=gk, qn=qn, kn=kn, vb=vb, lsum=lsum, wmb=wmb, wmt=wmt, bt=bt, gz=gz, mix=mix)
        else:
            p, h = _rms_matmul(x, _row(wt["mix_norm_g"][i]), wt["cv_w_pw1"][j], _row(wt["cv_b_pw1"][j]), "conf_pw1")
            wdw = _pad_rows(wt["cv_w_dw"][j], HALO)
            yc, y2 = _conf_mid_fwd(p, wdw, _row(wt["cv_b_dw"][j]), _row(wt["cv_ln_g"][j]), _row(wt["cv_ln_b"][j]))
            x = _res_matmul(x, y2, wt["cv_w_pw2"][j], _row(wt["cv_b_pw2"][j]), "conf_pw2")
            lay.update(p=p, h=h, wdw=wdw, yc=yc, y2=y2)
        lay["x_ffn"] = x
        fdw = _pad_rows(wt["ffn_w_dw"][i], 8)
        x, hf, u = _ffn_fwd(x, _row(wt["ffn_norm_g"][i]), wt["ffn_w_up"][i], fdw, _row(wt["ffn_b_dw"][i]),
                            wt["ffn_w_down"][i])
        lay.update(hf=hf, u=u, fdw=fdw)
        saved.append(lay)

    lpart, dy = _loss_head(x, target)
    loss = lpart[0, 0]

    gr = {k: [None] * len(v) for k, v in wt.items()}

    def made(name, layer, grad):
        gr[name][layer] = grad
        traffic.ready(name, layer, grad)

    for i in reversed(range(DEPTH)):
        j = i // 2
        lay = saved[i]
        du, act, dfdw, dfb, got = _ffn_bwd(dy, lay["u"], lay["fdw"], _row(wt["ffn_b_dw"][i]), wt["ffn_w_down"][i],
                                           send=traffic.take())
        traffic.landed(got)
        made("ffn_w_down", i, _matmul_tn(act, dy, "ffn_dw_down"))
        made("ffn_w_up", i, _matmul_tn(lay["hf"], du, "ffn_dw_up"))
        gr["ffn_w_dw"][i] = dfdw[:FFN_K]
        gr["ffn_b_dw"][i] = dfb[0]
        dy, dgf, _ = _nt_rms_bwd(du, wt["ffn_w_up"][i], lay["x_ffn"], _row(wt["ffn_norm_g"][i]), dy, "ffn_dx")
        gr["ffn_norm_g"][i] = dgf[0]
        if i % 2 == 0:
            dmix = _matmul_nt(dy, wt["hyb_w_out"][j], "out_proj_dx")
            made("hyb_w_out", j, _matmul_tn(lay["mix"], dy, "out_proj_dw"))
            dqn, dkn, dv, got = _sb_attn_bwd(lay["qn"], lay["kn"], lay["vb"], lay["lsum"], dmix, send=traffic.take())
            traffic.landed(got)
            duz, dwm, dbt, dgz = _sgu_bwd(lay["proj"], dmix, lay["gz"], lay["wmb"], lay["wmt"], lay["bt"], bd)
            dproj, dgq, dgk = _qk_bwd(lay["proj"], dqn, dkn, dv, duz, lay["gq"], lay["gk"] * LN2, bd)
            made("sb_w_in", j, _matmul_tn(lay["h"], dproj, "in_proj_dw"))
            gr["sb_q_norm_g"][j] = dgq.reshape(SB_WIDTH // HEAD_DIM, HEAD_DIM).sum(0) * scale
            gr["sb_k_norm_g"][j] = dgk.reshape(SB_WIDTH // HEAD_DIM, HEAD_DIM).sum(0) * LN2
            gr["sg_z_norm_g"][j] = dgz[0]
            gr["sg_w_spatial"][j] = jnp.where(tril[None], dwm, 0.0)
            gr["sg_b_spatial"][j] = dbt.reshape(CHUNK, SG_GROUPS, HEAD_DIM).sum(-1).T
            dy, dgm, got = _nt_rms_bwd(dproj, wt["sb_w_in"][j], lay["x_mix"], _row(wt["mix_norm_g"][i]), dy,
                                       "in_proj_dx", send=traffic.take() if i == 0 else ())
            if i == 0:
                traffic.landed(got)
        else:
            dy2 = _matmul_nt(dy, wt["cv_w_pw2"][j], "conf_pw2_dx")
            made("cv_w_pw2", j, _matmul_tn(lay["y2"], dy, "conf_pw2_dw"))
            dp, dwdw, dbdw, dlg, dlb, db1, db2, got = _conf_mid_bwd(
                lay["p"], lay["yc"], dy2, dy, lay["wdw"], _row(wt["cv_ln_g"][j]), _row(wt["cv_ln_b"][j]),
                send=traffic.take())
            traffic.landed(got)
            made("cv_w_pw1", j, _matmul_tn(lay["h"], dp, "conf_pw1_dw"))
            gr["cv_w_dw"][j] = dwdw[:CONV_K]
            gr["cv_b_dw"][j] = dbdw[0]
            gr["cv_ln_g"][j] = dlg[0]
            gr["cv_ln_b"][j] = dlb[0]
            gr["cv_b_pw1"][j] = db1[0]
            gr["cv_b_pw2"][j] = db2[0]
            dy, dgm, _ = _nt_rms_bwd(dp, wt["cv_w_pw1"][j], lay["x_mix"], _row(wt["mix_norm_g"][i]), dy, "conf_pw1_dx")
        gr["mix_norm_g"][i] = dgm[0]
    matmul_weights = ("sb_w_in", "hyb_w_out", "cv_w_pw1", "cv_w_pw2", "ffn_w_up", "ffn_w_down")
    grads = {k: (v if k in matmul_weights else jnp.stack(v)) for k, v in gr.items()}
    return loss, dy, grads


WEIGHTS = ["mix_norm_g", "sb_w_in", "sb_q_norm_g", "sb_k_norm_g", "sg_z_norm_g", "sg_w_spatial", "sg_b_spatial",
           "hyb_w_out", "cv_w_pw1", "cv_b_pw1", "cv_w_dw", "cv_b_dw", "cv_ln_g", "cv_ln_b", "cv_w_pw2", "cv_b_pw2",
           "ffn_norm_g", "ffn_w_up", "ffn_w_dw", "ffn_b_dw", "ffn_w_down"]
BIG = [("sb_w_in", "col"), ("hyb_w_out", "row"), ("cv_w_pw1", "col"), ("cv_w_pw2", "row"), ("ffn_w_up", "col"),
       ("ffn_w_down", "row")]
SMALL = ["cv_b_pw1", "cv_w_dw", "cv_b_dw", "cv_ln_g", "cv_ln_b", "cv_b_pw2", "ffn_w_dw"]
REPLICATED = ["mix_norm_g", "sb_q_norm_g", "sb_k_norm_g", "sg_z_norm_g", "sg_w_spatial", "sg_b_spatial", "ffn_norm_g",
              "ffn_b_dw"]


def _last_dim_blocks(full):
    t = jnp.moveaxis(full.reshape(full.shape[:-1] + (N_DEV, full.shape[-1] // N_DEV)), -2, 0)
    return t.reshape(N_DEV, -1)


def _from_last_dim_blocks(blocks, shard_shape):
    t = jnp.moveaxis(blocks.reshape((N_DEV,) + tuple(shard_shape)), 0, -2)
    return t.reshape(tuple(shard_shape[:-1]) + (N_DEV * shard_shape[-1],))


def _pack(arrays):
    lead = arrays[0].shape[:-1]
    flat = jnp.concatenate([a.astype(F32) for a in arrays], axis=-1)
    rows = -(-flat.shape[-1] // (16 * LANE)) * 16
    pad = rows * LANE - flat.shape[-1]
    if pad:
        flat = jnp.concatenate([flat, jnp.zeros(lead + (pad,), F32)], axis=-1)
    return flat.reshape(lead + (rows, LANE))


def _unpack(packed, shapes):
    flat = packed.reshape(-1)
    out, off = [], 0
    for shp in shapes:
        size = 1
        for dim in shp:
            size *= dim
        out.append(flat[off:off + size].reshape(shp))
        off += size
    return out


class _ShardTraffic:
    def __init__(self, w):
        self.w = w
        self.queue, self.flying, self.received = [], [], {}

    def rest(self):
        small = _pack([self.w[n].reshape(-1) for n in SMALL])
        return [self.w["sb_w_in"][1:].astype(BF16)] + [self.w[n].astype(BF16) for n, _ in BIG[1:]] + [small]

    def install(self, wt, gathered):
        wt["sb_w_in"].append(_unshard(gathered[0], 0, "unshard_sb_w_in"))
        for (n, kind), w8 in zip(BIG[1:], gathered[1:-1]):
            if kind == "col":
                wt[n] = [_unshard(w8, l, "unshard_" + n) for l in range(w8.shape[1])]
            else:
                wt[n] = [w8[:, l].reshape((N_DEV * w8.shape[2],) + w8.shape[3:]) for l in range(w8.shape[1])]
        sizes = [(self.w[n].size,) for n in SMALL]
        for n, parts in zip(SMALL, zip(*[_unpack(gathered[-1][d], sizes) for d in range(N_DEV)])):
            wt[n] = _from_last_dim_blocks(jnp.stack(parts), self.w[n].shape)

    def ready(self, name, layer, grad):
        if dict(BIG)[name] == "col":
            blocks = _shard_cast(grad, "shard_" + name)
        else:
            blocks = grad.reshape((N_DEV, grad.shape[0] // N_DEV) + grad.shape[1:])
        self.queue.append(((name, layer), blocks))

    def take(self):
        self.flying = [key for key, _ in self.queue]
        arrays = [blocks for _, blocks in self.queue]
        self.queue = []
        return arrays

    def landed(self, received):
        for key, blocks in zip(self.flying, received):
            self.received[key] = blocks
        self.flying = []


def kernel(x, mix_norm_g, sb_w_in, sb_q_norm_g, sb_k_norm_g, sg_z_norm_g, sg_w_spatial, sg_b_spatial, hyb_w_out, cv_w_pw1, cv_b_pw1, cv_w_dw, cv_b_dw, cv_ln_g, cv_ln_b, cv_w_pw2, cv_b_pw2, ffn_norm_g, ffn_w_up, ffn_w_dw, ffn_b_dw, ffn_w_down, loss_target, m_mix_norm_g, m_sb_w_in, m_sb_q_norm_g, m_sb_k_norm_g, m_sg_z_norm_g, m_sg_w_spatial, m_sg_b_spatial, m_hyb_w_out, m_cv_w_pw1, m_cv_b_pw1, m_cv_w_dw, m_cv_b_dw, m_cv_ln_g, m_cv_ln_b, m_cv_w_pw2, m_cv_b_pw2, m_ffn_norm_g, m_ffn_w_up, m_ffn_w_dw, m_ffn_b_dw, m_ffn_w_down, v_mix_norm_g, v_sb_w_in, v_sb_q_norm_g, v_sb_k_norm_g, v_sg_z_norm_g, v_sg_w_spatial, v_sg_b_spatial, v_hyb_w_out, v_cv_w_pw1, v_cv_b_pw1, v_cv_w_dw, v_cv_b_dw, v_cv_ln_g, v_cv_ln_b, v_cv_w_pw2, v_cv_b_pw2, v_ffn_norm_g, v_ffn_w_up, v_ffn_w_dw, v_ffn_b_dw, v_ffn_w_down):
    w = dict(zip(WEIGHTS, (mix_norm_g, sb_w_in, sb_q_norm_g, sb_k_norm_g, sg_z_norm_g, sg_w_spatial, sg_b_spatial,
                           hyb_w_out, cv_w_pw1, cv_b_pw1, cv_w_dw, cv_b_dw, cv_ln_g, cv_ln_b, cv_w_pw2, cv_b_pw2,
                           ffn_norm_g, ffn_w_up, ffn_w_dw, ffn_b_dw, ffn_w_down)))
    m = dict(zip(WEIGHTS, (m_mix_norm_g, m_sb_w_in, m_sb_q_norm_g, m_sb_k_norm_g, m_sg_z_norm_g, m_sg_w_spatial,
                           m_sg_b_spatial, m_hyb_w_out, m_cv_w_pw1, m_cv_b_pw1, m_cv_w_dw, m_cv_b_dw, m_cv_ln_g,
                           m_cv_ln_b, m_cv_w_pw2, m_cv_b_pw2, m_ffn_norm_g, m_ffn_w_up, m_ffn_w_dw, m_ffn_b_dw,
                           m_ffn_w_down)))
    v = dict(zip(WEIGHTS, (v_mix_norm_g, v_sb_w_in, v_sb_q_norm_g, v_sb_k_norm_g, v_sg_z_norm_g, v_sg_w_spatial,
                           v_sg_b_spatial, v_hyb_w_out, v_cv_w_pw1, v_cv_b_pw1, v_cv_w_dw, v_cv_b_dw, v_cv_ln_g,
                           v_cv_ln_b, v_cv_w_pw2, v_cv_b_pw2, v_ffn_norm_g, v_ffn_w_up, v_ffn_w_dw, v_ffn_b_dw,
                           v_ffn_w_down)))
    big_names = [n for n, _ in BIG]
    flat = lambda t, names: [t[n].reshape(-1) for n in names]

    first_in = _all_gather([w["sb_w_in"][0:1].astype(BF16)], "gather_first")[0]
    wt = {n: w[n] for n in REPLICATED}
    wt["sb_w_in"] = [_unshard(first_in, 0, "unshard_sb_w_in")]

    traffic = _ShardTraffic(w)
    loss, gx, grads = _local_step(x[0], loss_target[0], wt, traffic)
    assert not traffic.queue and not traffic.flying
    recv_big = [[traffic.received[n, l] for l in range(w[n].shape[0])] for n in big_names]

    grep = _pack(flat(grads, REPLICATED))
    gsmall = _pack([_last_dim_blocks(grads[n]) for n in SMALL])
    rep_rows, small_rows = grep.shape[0], gsmall.shape[1]
    vec = jnp.concatenate([grep, gsmall.reshape(N_DEV * small_rows, LANE)], axis=0)
    vec_all = _all_gather([vec], "gather_small_grads")[0]
    me = 4 * lax.axis_index("x") + 2 * lax.axis_index("y") + lax.axis_index("c")
    recv_rep = vec_all[:, :rep_rows]
    recv_small = lax.dynamic_slice(vec_all, (0, rep_rows + small_rows * me, 0), (N_DEV, small_rows, LANE))

    out = {}
    kinds = ("grad", "delta", "new_m", "new_v")
    for n, parts in zip(big_names, recv_big):
        for kind, arr in zip(kinds, _reduce_adamw(parts, w[n], m[n], v[n], "adamw_" + n)):
            out[kind, n] = arr
    for names, recv, tag in ((SMALL, recv_small, "adamw_small"), (REPLICATED, recv_rep, "adamw_replicated")):
        res = _reduce_adamw([recv], _pack(flat(w, names))[None], _pack(flat(m, names))[None],
                            _pack(flat(v, names))[None], tag)
        shapes = [w[n].shape for n in names]
        for kind, packed in zip(kinds, res):
            for n, arr in zip(names, _unpack(packed[0], shapes)):
                out[kind, n] = arr

    loss = lax.psum(loss, ("x", "y", "c"))
    return (loss, gx[None], *[out[kind, n] for kind in kinds for n in WEIGHTS])
```

```python
import functools

import jax
import jax.numpy as jnp
from jax import lax
from jax.experimental import pallas as pl
from jax.experimental.pallas import tpu as pltpu

F32 = jnp.float32
BF16 = jnp.bfloat16

D_MODEL = 1024
HEAD_DIM = 64
SB_WIDTH = 512
SG_WIDTH = 512
SG_GROUPS = 8
IN_WIDTH = 3 * SB_WIDTH + 2 * SG_WIDTH
CHUNK = 128
CONV_K = 31
D_FF = 2816
FFN_K = 3
DEPTH = 4
EPS = 1e-6
N_DEV = 8
LANE = 128
HALO = 32
ATT_BLOCK = 256
FF_CHUNK = 256
CONV_ROWS = 32
CONV_LANES = 512
MIB = 2 ** 20

ADAM_LR = 0.001
ADAM_B1 = 0.9
ADAM_B2 = 0.999
ADAM_EPS = 1e-08
ADAM_WD = 0.01
ADAM_STEP = 10

LOG2E = 1.4426950408889634
LN2 = 0.6931471805599453

NT_DIMS = (((1,), (1,)), ((), ()))
TN_DIMS = (((0,), (0,)), ((), ()))


def _params(semantics, vmem_mib):
    return pltpu.CompilerParams(dimension_semantics=semantics, vmem_limit_bytes=vmem_mib * MIB)


def _full(shape):
    nd = len(shape)
    return pl.BlockSpec(shape, lambda *_: (0,) * nd)


def _sigmoid(x):
    return 1.0 / (1.0 + jnp.exp(-x))


def _gelu(x):
    return 0.5 * x * (1.0 + lax.erf(x * 0.7071067811865476))


def _gelu_grad(x):
    return 0.5 * (1.0 + lax.erf(x * 0.7071067811865476)) + x * jnp.exp(-0.5 * x * x) * 0.3989422804014327


def _silu_grad(x, s):
    return s * (1.0 + x * (1.0 - s))


def _dot(a, b):
    return jnp.dot(a, b, preferred_element_type=F32)


def _dot2(a, b):
    hi = a.astype(BF16)
    lo = (a - hi.astype(F32)).astype(BF16)
    return _dot(hi, b) + _dot(lo, b)


def _group_mean(t, bd):
    return _dot2(t, bd) * (1.0 / HEAD_DIM)


def _shift_down(v, prev8, s):
    top = pltpu.roll(jnp.concatenate([prev8, v[:8]], axis=0), s, 0)[8:16]
    return jnp.concatenate([top, pltpu.roll(v, s, 0)[8:]], axis=0)


def _shift_up(v, next8, s):
    n = v.shape[0]
    bottom = pltpu.roll(jnp.concatenate([v[n - 8:], next8], axis=0), 16 - s, 0)[0:8]
    return jnp.concatenate([pltpu.roll(v, n - s, 0)[: n - 8], bottom], axis=0)


def _mesh_pos():
    return lax.axis_index("x"), lax.axis_index("y"), lax.axis_index("c")


def _comm_scratch(n):
    return [pltpu.SemaphoreType.DMA((7 * n,)), pltpu.SemaphoreType.DMA((7 * n,)), pltpu.SemaphoreType.DMA((n,))]


class _Scatter:
    def __init__(self, src_refs, out_refs, send_sems, recv_sems, local_sems):
        x, y, cc = _mesh_pos()
        me = 4 * x + 2 * y + cc
        self.copies, self.mine = [], []
        for a, (src, out) in enumerate(zip(src_refs, out_refs)):
            self.mine.append(pltpu.make_async_copy(src.at[me], out.at[me], local_sems.at[a]))
            for k in range(1, N_DEV):
                px = 1 - x if k & 4 else x
                py = 1 - y if k & 2 else y
                pc = 1 - cc if k & 1 else cc
                self.copies.append(pltpu.make_async_remote_copy(
                    src_ref=src.at[4 * px + 2 * py + pc], dst_ref=out.at[me],
                    send_sem=send_sems.at[7 * a + k - 1], recv_sem=recv_sems.at[7 * a + k - 1],
                    device_id=(px, py, pc), device_id_type=pl.DeviceIdType.MESH))

    def start(self):
        for cp in self.mine + self.copies:
            cp.start()

    def finish(self):
        for cp in self.copies + self.mine:
            cp.wait()


class _Gather:
    def __init__(self, x_refs, out_refs, send_sems, recv_sems, local_sems):
        x, y, cc = _mesh_pos()
        self.n = len(x_refs)
        self.me, self.sibling, self.cc = (x, y, cc), (x, y, 1 - cc), cc
        self.chips = [(1 - x, y), (x, 1 - y), (1 - x, 1 - y)]
        self.x_refs, self.out_refs, self.send_sems, self.recv_sems = x_refs, out_refs, send_sems, recv_sems
        self.mine = [pltpu.make_async_copy(x_refs[a], out_refs[a].at[4 * x + 2 * y + cc], local_sems.at[a])
                     for a in range(self.n)]

    def copy(self, a, k, block, to, own=False):
        slot = self.out_refs[a].at[4 * block[0] + 2 * block[1] + block[2]]
        return pltpu.make_async_remote_copy(
            src_ref=self.x_refs[a] if own else slot, dst_ref=slot,
            send_sem=self.send_sems.at[7 * a + k], recv_sem=self.recv_sems.at[7 * a + k],
            device_id=to, device_id_type=pl.DeviceIdType.MESH)

    def first_hop(self, a):
        return [self.copy(a, 0, self.me, self.sibling, own=True)] + [
            self.copy(a, 1 + j, self.me, (*chip, self.cc), own=True) for j, chip in enumerate(self.chips)]

    def passed_on(self, a):
        return [self.copy(a, 4 + j, (*chip, self.cc), self.sibling) for j, chip in enumerate(self.chips)]

    def start(self):
        for a in range(self.n):
            self.mine[a].start()
        for a in range(self.n):
            for cp in self.first_hop(a):
                cp.start()

    def forward(self):
        for a in range(self.n):
            for j, chip in enumerate(self.chips):
                self.copy(a, 1 + j, (*chip, self.cc), self.me).wait_recv()
                self.copy(a, 4 + j, (*chip, self.cc), self.sibling).start()

    def finish(self):
        for a in range(self.n):
            self.copy(a, 0, self.sibling, self.me).wait_recv()
            for j, chip in enumerate(self.chips):
                self.copy(a, 4 + j, (*chip, 1 - self.cc), self.me).wait_recv()
        for a in range(self.n):
            for cp in self.first_hop(a) + self.passed_on(a):
                cp.wait_send()
        for cp in self.mine:
            cp.wait()


def _carried(body, n_in, n_out, arrays, kind, when):
    n = len(arrays)
    if n == 0:
        return body, [], [], [], []

    def wrapped(*refs):
        ins, srcs = refs[:n_in], refs[n_in:n_in + n]
        outs, landed = refs[n_in + n:n_in + n + n_out], refs[n_in + n + n_out:n_in + 2 * n + n_out]
        scratch = refs[n_in + 2 * n + n_out:]
        comm = (_Scatter if kind == "scatter" else _Gather)(srcs, landed, *scratch[-3:])
        pl.when(when["start"]())(comm.start)
        if kind == "gather":
            pl.when(when["forward"]())(comm.forward)
        body(*ins, *outs, *scratch[:-3])
        pl.when(when["finish"]())(comm.finish)

    any_spec = pl.BlockSpec(memory_space=pl.ANY)
    if kind == "scatter":
        shapes = [jax.ShapeDtypeStruct(a.shape, a.dtype) for a in arrays]
    else:
        shapes = [jax.ShapeDtypeStruct((N_DEV,) + a.shape, a.dtype) for a in arrays]
    return wrapped, [any_spec] * n, [any_spec] * n, shapes, _comm_scratch(n)


def _all_gather(shards, name):
    n = len(shards)

    def body(*refs):
        comm = _Gather(refs[:n], refs[n:2 * n], *refs[2 * n:])
        comm.start()
        comm.forward()
        comm.finish()

    any_spec = pl.BlockSpec(memory_space=pl.ANY)
    return pl.pallas_call(
        body, name=name, in_specs=[any_spec] * n, out_specs=[any_spec] * n,
        out_shape=[jax.ShapeDtypeStruct((N_DEV,) + s.shape, s.dtype) for s in shards],
        scratch_shapes=_comm_scratch(n),
    )(*shards)


def _rms_matmul(x, g, w, b, name):
    s, d = x.shape
    n = w.shape[1]
    tm = min(512, s)

    def body(x_ref, g_ref, w_ref, b_ref, y_ref, h_ref):
        xv = x_ref[...]
        r = lax.rsqrt(jnp.mean(xv * xv, axis=-1, keepdims=True) + EPS)
        h = (xv * r * g_ref[...]).astype(BF16)
        h_ref[...] = h
        for c in range(0, n, 512):
            y_ref[:, c:c + 512] = _dot(h, w_ref[:, c:c + 512]) + b_ref[:, c:c + 512]

    return pl.pallas_call(
        body, name=name, grid=(s // tm,),
        in_specs=[pl.BlockSpec((tm, d), lambda i: (i, 0)), _full((1, d)), _full((d, n)), _full((1, n))],
        out_specs=[pl.BlockSpec((tm, n), lambda i: (i, 0)), pl.BlockSpec((tm, d), lambda i: (i, 0))],
        out_shape=[jax.ShapeDtypeStruct((s, n), F32), jax.ShapeDtypeStruct((s, d), BF16)],
        compiler_params=_params(("arbitrary",), 48),
    )(x, g, w, b)


def _matmul_nt(a, w, name):
    s, n = a.shape
    k = w.shape[0]
    tm = min(512, s)

    def body(a_ref, w_ref, o_ref):
        o_ref[...] = lax.dot_general(a_ref[...].astype(BF16), w_ref[...], NT_DIMS, preferred_element_type=F32)

    return pl.pallas_call(
        body, name=name, grid=(s // tm,),
        in_specs=[pl.BlockSpec((tm, n), lambda i: (i, 0)), _full((k, n))],
        out_specs=pl.BlockSpec((tm, k), lambda i: (i, 0)),
        out_shape=jax.ShapeDtypeStruct((s, k), F32),
        compiler_params=_params(("arbitrary",), 40),
    )(a, w)


def _matmul_tn(a, b, name):
    s, k = a.shape
    n = b.shape[1]
    ts = min(2048 if k <= 1024 else 1024, s)
    tn = 1024 if (n % 1024 == 0 and k <= 1024) else 512
    steps = s // ts

    def body(a_ref, b_ref, o_ref, acc):
        t = pl.program_id(1)

        @pl.when(t == 0)
        def _():
            acc[...] = jnp.zeros_like(acc)

        acc[...] += lax.dot_general(a_ref[...].astype(BF16), b_ref[...].astype(BF16), TN_DIMS,
                                    preferred_element_type=F32)

        @pl.when(t == steps - 1)
        def _():
            o_ref[...] = acc[...].astype(BF16)

    return pl.pallas_call(
        body, name=name, grid=(n // tn, steps),
        in_specs=[pl.BlockSpec((ts, k), lambda j, t: (t, 0)), pl.BlockSpec((ts, tn), lambda j, t: (t, j))],
        out_specs=pl.BlockSpec((k, tn), lambda j, t: (0, j)),
        out_shape=jax.ShapeDtypeStruct((k, n), BF16),
        scratch_shapes=[pltpu.VMEM((k, tn), F32)],
        compiler_params=_params(("arbitrary", "arbitrary"), 48),
    )(a, b)


def _nt_rms_bwd(dp, w, x, g, dres, name, send=()):
    s, n = dp.shape
    d = x.shape[1]
    tm = min(256, s)
    steps = s // tm

    def body(dp_ref, w_ref, x_ref, g_ref, dres_ref, dx_ref, dg_ref):
        @pl.when(pl.program_id(0) == 0)
        def _():
            dg_ref[...] = jnp.zeros_like(dg_ref)

        dh = lax.dot_general(dp_ref[...], w_ref[...], NT_DIMS, preferred_element_type=F32)
        xv = x_ref[...]
        r = lax.rsqrt(jnp.mean(xv * xv, axis=-1, keepdims=True) + EPS)
        xh = xv * r
        dg_ref[...] += jnp.sum(dh * xh, axis=0, keepdims=True)
        dn = dh * g_ref[...]
        dx_ref[...] = dres_ref[...] + r * (dn - xh * jnp.mean(dn * xh, axis=-1, keepdims=True))

    step_is = lambda i: lambda: pl.program_id(0) == i
    kern, ins, outs, shapes, sems = _carried(body, 5, 2, send, "scatter",
                                             {"start": step_is(0), "finish": step_is(steps - 1)})
    res = pl.pallas_call(
        kern, name=name, grid=(steps,),
        in_specs=[pl.BlockSpec((tm, n), lambda i: (i, 0)), _full((d, n)), pl.BlockSpec((tm, d), lambda i: (i, 0)),
                  _full((1, d)), pl.BlockSpec((tm, d), lambda i: (i, 0))] + ins,
        out_specs=[pl.BlockSpec((tm, d), lambda i: (i, 0)), _full((1, d))] + outs,
        out_shape=[jax.ShapeDtypeStruct((s, d), F32), jax.ShapeDtypeStruct((1, d), F32)] + shapes,
        scratch_shapes=sems,
        compiler_params=_params(("arbitrary",), 52),
    )(dp, w, x, g, dres, *send)
    return res[0], res[1], list(res[2:])


def _res_matmul(x, a, w, b, name):
    s, d = x.shape
    k = a.shape[1]
    tm = min(512, s)

    def body(x_ref, a_ref, w_ref, b_ref, o_ref):
        o_ref[...] = x_ref[...] + _dot(a_ref[...], w_ref[...]) + b_ref[...]

    return pl.pallas_call(
        body, name=name, grid=(s // tm,),
        in_specs=[pl.BlockSpec((tm, d), lambda i: (i, 0)), pl.BlockSpec((tm, k), lambda i: (i, 0)), _full((k, d)),
                  _full((1, d))],
        out_specs=pl.BlockSpec((tm, d), lambda i: (i, 0)),
        out_shape=jax.ShapeDtypeStruct((s, d), F32),
        compiler_params=_params(("arbitrary",), 32),
    )(x, a, w, b)


def _out_proj(x, o, gg, w, name):
    s, d = x.shape
    tm = min(512, s)

    def body(x_ref, o_ref, gg_ref, w_ref, y_ref, mix_ref):
        mix = jnp.concatenate([o_ref[...], gg_ref[...]], axis=1).astype(BF16)
        mix_ref[...] = mix
        y_ref[...] = x_ref[...] + _dot(mix, w_ref[...])

    return pl.pallas_call(
        body, name=name, grid=(s // tm,),
        in_specs=[pl.BlockSpec((tm, d), lambda i: (i, 0)), pl.BlockSpec((tm, SB_WIDTH), lambda i: (i, 0)),
                  pl.BlockSpec((tm, SG_WIDTH), lambda i: (i, 0)), _full((d, d))],
        out_specs=[pl.BlockSpec((tm, d), lambda i: (i, 0)), pl.BlockSpec((tm, d), lambda i: (i, 0))],
        out_shape=[jax.ShapeDtypeStruct((s, d), F32), jax.ShapeDtypeStruct((s, d), BF16)],
        compiler_params=_params(("arbitrary",), 32),
    )(x, o, gg, w)


def _qk_prep(proj, gq, gk, bd):
    s = proj.shape[0]
    tm = min(512, s)

    def body(q_ref, k_ref, v_ref, gq_ref, gk_ref, bd_ref, qn_ref, kn_ref, vb_ref):
        bdv = bd_ref[...]
        q = q_ref[...]
        k = k_ref[...]
        qn_ref[...] = (q * lax.rsqrt(_group_mean(q * q, bdv) + EPS) * gq_ref[...]).astype(BF16)
        kn_ref[...] = (k * lax.rsqrt(_group_mean(k * k, bdv) + EPS) * gk_ref[...]).astype(BF16)
        vb_ref[...] = v_ref[...].astype(BF16)

    col = lambda c: pl.BlockSpec((tm, SB_WIDTH), lambda i: (i, c))
    out = pl.BlockSpec((tm, SB_WIDTH), lambda i: (i, 0))
    return pl.pallas_call(
        body, name="qk_prep", grid=(s // tm,),
        in_specs=[col(0), col(1), col(2), _full((1, SB_WIDTH)), _full((1, SB_WIDTH)), _full((SB_WIDTH, SB_WIDTH))],
        out_specs=[out, out, out],
        out_shape=[jax.ShapeDtypeStruct((s, SB_WIDTH), BF16)] * 3,
        compiler_params=_params(("arbitrary",), 32),
    )(proj, proj, proj, gq, gk, bd)


def _qk_bwd(proj, dqn, dkn, dv, duz, gq, gk, bd):
    s = proj.shape[0]
    tm = min(512, s)

    def body(q_ref, k_ref, dq_ref, dk_ref, dv_ref, duz_ref, gq_ref, gk_ref, bd_ref, o_ref, dgq_ref, dgk_ref):
        @pl.when(pl.program_id(0) == 0)
        def _():
            dgq_ref[...] = jnp.zeros_like(dgq_ref)
            dgk_ref[...] = jnp.zeros_like(dgk_ref)

        bdv = bd_ref[...]

        def back(t, gain, dout, dg_ref):
            r = lax.rsqrt(_group_mean(t * t, bdv) + EPS)
            th = t * r
            dg_ref[...] += jnp.sum(dout * th, axis=0, keepdims=True)
            dn = dout * gain
            return r * (dn - th * _group_mean(dn * th, bdv))

        o_ref[:, 0:SB_WIDTH] = back(q_ref[...], gq_ref[...], dq_ref[...], dgq_ref).astype(BF16)
        o_ref[:, SB_WIDTH:2 * SB_WIDTH] = back(k_ref[...], gk_ref[...], dk_ref[...], dgk_ref).astype(BF16)
        o_ref[:, 2 * SB_WIDTH:3 * SB_WIDTH] = dv_ref[...].astype(BF16)
        o_ref[:, 3 * SB_WIDTH:IN_WIDTH] = duz_ref[...].astype(BF16)

    col = lambda c: pl.BlockSpec((tm, SB_WIDTH), lambda i: (i, c))
    row = pl.BlockSpec((tm, SB_WIDTH), lambda i: (i, 0))
    return pl.pallas_call(
        body, name="qk_bwd", grid=(s // tm,),
        in_specs=[col(0), col(1), row, row, row, pl.BlockSpec((tm, 2 * SG_WIDTH), lambda i: (i, 0)),
                  _full((1, SB_WIDTH)), _full((1, SB_WIDTH)), _full((SB_WIDTH, SB_WIDTH))],
        out_specs=[pl.BlockSpec((tm, IN_WIDTH), lambda i: (i, 0)), _full((1, SB_WIDTH)), _full((1, SB_WIDTH))],
        out_shape=[jax.ShapeDtypeStruct((s, IN_WIDTH), BF16), jax.ShapeDtypeStruct((1, SB_WIDTH), F32),
                   jax.ShapeDtypeStruct((1, SB_WIDTH), F32)],
        compiler_params=_params(("arbitrary",), 40),
    )(proj, proj, dqn, dkn, dv, duz, gq, gk, bd)


def _attn_masks(tq, tk):
    lane = lax.broadcasted_iota(jnp.int32, (tk, LANE), 1)
    heads = [(lane >= hh * HEAD_DIM) & (lane < (hh + 1) * HEAD_DIM) for hh in range(2)]
    urow = lax.broadcasted_iota(jnp.int32, (tk, tk), 0)
    ucol = lax.broadcasted_iota(jnp.int32, (tk, tk), 1)
    return heads, urow, ucol


def _keep_cost(z2):
    return jnp.log(1.0 + jnp.exp2(jnp.minimum(z2, 126.0)))


def _sb_attn_fwd(q2, kn, vb, gather=()):
    s = q2.shape[0]
    tk = min(ATT_BLOCK, s)
    tq = min(2 * ATT_BLOCK, s)
    r = tq // tk
    nq = s // tq
    assert s // tk <= LANE

    def body(q_ref, k_ref, v_ref, o_ref, ls_ref):
        i = pl.program_id(1)
        heads, urow, ucol = _attn_masks(tq, tk)
        u_incl = (urow >= ucol).astype(BF16)
        row = lax.broadcasted_iota(jnp.int32, (tq, tk), 0)
        col = lax.broadcasted_iota(jnp.int32, (tq, tk), 1)
        qlane = lax.broadcasted_iota(jnp.int32, (tq, LANE), 1)
        q = q_ref[...]

        def blocks(kbs, state, masked):
            carry, acc, ls = list(state[0:2]), state[2], list(state[3:5])
            chains = [(d, hh) for d in range(len(kbs)) for hh in range(2)]
            kblk, vblk, valid = {}, {}, {}
            for d, kb in enumerate(kbs):
                off = pl.multiple_of(kb * tk, tk)
                kfull = k_ref[pl.ds(off, tk), :]
                vfull = v_ref[pl.ds(off, tk), :]
                if masked:
                    valid[d] = (kb * tk + col) < (i * tq + row)
                for hh in range(2):
                    kblk[d, hh] = jnp.where(heads[hh], kfull, jnp.zeros((), BF16))
                    vblk[d, hh] = jnp.where(heads[hh], vfull, jnp.zeros((), BF16))
            z2 = {c: lax.dot_general(q, kblk[c], NT_DIMS, preferred_element_type=F32) for c in chains}
            cost = {}
            for c in chains:
                cost[c] = _keep_cost(z2[c])
                if masked:
                    cost[c] = jnp.where(valid[c[0]], cost[c], 0.0)
            sums = {c: _dot(cost[c].astype(BF16), u_incl) for c in chains}
            a = {}
            for d, hh in chains:
                rin = carry[hh] + sums[d, hh]
                a[d, hh] = jnp.exp2(z2[d, hh] - rin * LOG2E)
                if masked:
                    a[d, hh] = jnp.where(valid[d], a[d, hh], 0.0)
                rs = jnp.sum(cost[d, hh], axis=1, keepdims=True)
                ls[hh] = ls[hh] + jnp.where(qlane == kbs[d], rs, 0.0)
                carry[hh] = carry[hh] + rs
            for c in chains:
                acc = acc + _dot(a[c].astype(BF16), vblk[c])
            return carry[0], carry[1], acc, ls[0], ls[1]

        zc = jnp.zeros((tq, 1), F32)
        zt = jnp.zeros((tq, LANE), F32)
        state = blocks([r * i + r - 1 - d for d in range(r)], (zc, zc, zt, zt, zt), True)
        pairs = lax.shift_right_logical(i, 1)
        state = lax.fori_loop(
            0, pairs, lambda n, st: blocks([r * i - 1 - 2 * r * n - d for d in range(2 * r)], st, False), state)
        state = lax.fori_loop(
            0, i - 2 * pairs, lambda n, st: blocks([r - 1 - d for d in range(r)], st, False), state)
        o_ref[...] = state[2]
        ls_ref[:, 0:LANE] = state[3]
        ls_ref[:, LANE:2 * LANE] = state[4]

    last_pair = SB_WIDTH // LANE - 1
    step_is = lambda p, i: lambda: (pl.program_id(0) == p) & (pl.program_id(1) == i)
    when = {"start": step_is(0, 0), "forward": step_is(last_pair, 0), "finish": step_is(last_pair, nq - 1)}
    kern, ins, outs, shapes, sems = _carried(body, 3, 2, gather, "gather", when)
    res = pl.pallas_call(
        kern, name="sb_attn_fwd", grid=(SB_WIDTH // LANE, nq),
        in_specs=[pl.BlockSpec((tq, LANE), lambda p, i: (i, p)), pl.BlockSpec((s, LANE), lambda p, i: (0, p)),
                  pl.BlockSpec((s, LANE), lambda p, i: (0, p))] + ins,
        out_specs=[pl.BlockSpec((tq, LANE), lambda p, i: (i, p)),
                   pl.BlockSpec((tq, 2 * LANE), lambda p, i: (i, p))] + outs,
        out_shape=[jax.ShapeDtypeStruct((s, SB_WIDTH), F32), jax.ShapeDtypeStruct((s, 2 * SB_WIDTH), F32)] + shapes,
        scratch_shapes=sems,
        compiler_params=_params(("arbitrary", "arbitrary"), 48),
    )(q2, kn, vb, *gather)
    return res[0], res[1], list(res[2:])


def _sb_attn_bwd(q2, kn, vb, lsum, dmix, send=()):
    s = q2.shape[0]
    tk = min(ATT_BLOCK, s)
    tq = min(2 * ATT_BLOCK, s)
    r = tq // tk
    nq = s // tq

    def body(q_ref, k_ref, v_ref, ls_ref, do_ref, dq_ref, dk_ref, dv_ref):
        i = pl.program_id(1)

        @pl.when(i == 0)
        def _():
            dk_ref[...] = jnp.zeros_like(dk_ref)
            dv_ref[...] = jnp.zeros_like(dv_ref)

        heads, urow, ucol = _attn_masks(tq, tk)
        u_incl = (urow >= ucol).astype(BF16)
        u_pre = (urow <= ucol).astype(BF16)
        lrow = lax.broadcasted_iota(jnp.int32, (LANE, LANE), 0)
        lcol = lax.broadcasted_iota(jnp.int32, (LANE, LANE), 1)
        u_after = (lrow > lcol).astype(BF16)
        row = lax.broadcasted_iota(jnp.int32, (tq, tk), 0)
        col = lax.broadcasted_iota(jnp.int32, (tq, tk), 1)
        qlane = lax.broadcasted_iota(jnp.int32, (tq, LANE), 1)
        qheads = [(qlane >= hh * HEAD_DIM) & (qlane < (hh + 1) * HEAD_DIM) for hh in range(2)]
        q = q_ref[...]
        dob = do_ref[...].astype(BF16)
        qm = [jnp.where(qheads[hh], q, jnp.zeros((), BF16)) for hh in range(2)]
        dom = [jnp.where(qheads[hh], dob, jnp.zeros((), BF16)) for hh in range(2)]
        after = []
        for hh in range(2):
            ls = ls_ref[:, hh * LANE:(hh + 1) * LANE]
            hi = ls.astype(BF16)
            mid = (ls - hi.astype(F32)).astype(BF16)
            lo = (ls - hi.astype(F32) - mid.astype(F32)).astype(BF16)
            after.append(_dot(hi, u_after) + _dot(mid, u_after) + _dot(lo, u_after))

        def blocks(kbs, state, masked):
            cp, dq = list(state[0:2]), state[2]
            chains = [(d, hh) for d in range(len(kbs)) for hh in range(2)]
            offs, kblk, vblk, valid = {}, {}, {}, {}
            for d, kb in enumerate(kbs):
                offs[d] = pl.multiple_of(kb * tk, tk)
                kfull = k_ref[pl.ds(offs[d], tk), :]
                vblk[d] = v_ref[pl.ds(offs[d], tk), :]
                if masked:
                    valid[d] = (kb * tk + col) < (i * tq + row)
                for hh in range(2):
                    kblk[d, hh] = jnp.where(heads[hh], kfull, jnp.zeros((), BF16))
            z2 = {c: lax.dot_general(q, kblk[c], NT_DIMS, preferred_element_type=F32) for c in chains}
            da = {(d, hh): lax.dot_general(dom[hh], vblk[d], NT_DIMS, preferred_element_type=F32) for d, hh in chains}
            cost, sig = {}, {}
            for c in chains:
                cost[c] = _keep_cost(z2[c])
                sig[c] = jnp.exp2(z2[c] - cost[c] * LOG2E)
                if masked:
                    cost[c] = jnp.where(valid[c[0]], cost[c], 0.0)
            sums = {c: _dot(cost[c].astype(BF16), u_incl) for c in chains}
            a, g = {}, {}
            for d, hh in chains:
                cr = jnp.sum(jnp.where(qlane == kbs[d], after[hh], 0.0), axis=1, keepdims=True)
                a[d, hh] = jnp.exp2(z2[d, hh] - (cr + sums[d, hh]) * LOG2E)
                if masked:
                    a[d, hh] = jnp.where(valid[d], a[d, hh], 0.0)
                g[d, hh] = da[d, hh] * a[d, hh]
            pre = {c: _dot(g[c].astype(BF16), u_pre) for c in chains}
            dzb = {}
            for d, hh in chains:
                dz = g[d, hh] - sig[d, hh] * (cp[hh] + pre[d, hh])
                if masked:
                    dz = jnp.where(valid[d], dz, 0.0)
                dzb[d, hh] = dz.astype(BF16)
                cp[hh] = cp[hh] + jnp.sum(g[d, hh], axis=1, keepdims=True)
            for d in range(len(kbs)):
                dv_ref[pl.ds(offs[d], tk), :] += sum(
                    lax.dot_general(a[d, hh].astype(BF16), dom[hh], TN_DIMS, preferred_element_type=F32) for hh in range(2))
                dk_ref[pl.ds(offs[d], tk), :] += sum(
                    lax.dot_general(dzb[d, hh], qm[hh], TN_DIMS, preferred_element_type=F32) for hh in range(2))
            for c in chains:
                dq = dq + _dot(dzb[c], kblk[c])
            return cp[0], cp[1], dq

        zc = jnp.zeros((tq, 1), F32)
        pairs = lax.shift_right_logical(i, 1)
        state = lax.fori_loop(0, pairs, lambda n, st: blocks([2 * r * n + d for d in range(2 * r)], st, False),
                              (zc, zc, jnp.zeros((tq, LANE), F32)))
        state = lax.fori_loop(0, i - 2 * pairs,
                              lambda n, st: blocks([2 * r * pairs + d for d in range(r)], st, False), state)
        state = blocks([r * i + d for d in range(r)], state, True)
        dq_ref[...] = state[2]

    blk = pl.BlockSpec((tq, LANE), lambda p, i: (i, p))
    whole = pl.BlockSpec((s, LANE), lambda p, i: (0, p))
    last_pair = SB_WIDTH // LANE - 1
    step_is = lambda p, i: lambda: (pl.program_id(0) == p) & (pl.program_id(1) == i)
    kern, ins, outs, shapes, sems = _carried(body, 5, 3, send, "scatter",
                                             {"start": step_is(0, 0), "finish": step_is(last_pair, nq - 1)})
    res = pl.pallas_call(
        kern, name="sb_attn_bwd", grid=(SB_WIDTH // LANE, nq),
        in_specs=[blk, whole, whole, pl.BlockSpec((tq, 2 * LANE), lambda p, i: (i, p)), blk] + ins,
        out_specs=[blk, whole, whole] + outs,
        out_shape=[jax.ShapeDtypeStruct((s, SB_WIDTH), F32)] * 3 + shapes,
        scratch_shapes=sems,
        compiler_params=_params(("arbitrary", "arbitrary"), 56),
    )(q2, kn, vb, lsum, dmix, *send)
    return res[0], res[1], res[2], list(res[3:])


def _sgu_spatial(zn, wm_ref, lane, c):
    parts = []
    for p in range(SG_WIDTH // LANE):
        blk = zn[c * CHUNK:(c + 1) * CHUNK, p * LANE:(p + 1) * LANE].astype(BF16)
        lo = jnp.where(lane < HEAD_DIM, blk, jnp.zeros((), BF16))
        hi = jnp.where(lane >= HEAD_DIM, blk, jnp.zeros((), BF16))
        parts.append(_dot(wm_ref[2 * p], lo) + _dot(wm_ref[2 * p + 1], hi))
    return jnp.concatenate(parts, axis=1)


def _sgu_fwd(proj, gz, wm, bt, bd):
    s = proj.shape[0]
    tm = min(512, s)

    def body(u_ref, z_ref, gz_ref, wm_ref, bt_ref, bd_ref, o_ref):
        lane = lax.broadcasted_iota(jnp.int32, (CHUNK, LANE), 1)
        ug = _gelu(u_ref[...])
        zg = _gelu(z_ref[...])
        zn = zg * lax.rsqrt(_group_mean(zg * zg, bd_ref[...]) + EPS) * gz_ref[...]
        for c in range(tm // CHUNK):
            sp = _sgu_spatial(zn, wm_ref, lane, c) + bt_ref[...]
            o_ref[c * CHUNK:(c + 1) * CHUNK, :] = ug[c * CHUNK:(c + 1) * CHUNK, :] * sp

    col = lambda c: pl.BlockSpec((tm, SG_WIDTH), lambda i: (i, c))
    return pl.pallas_call(
        body, name="sgu_fwd", grid=(s // tm,),
        in_specs=[col(3), col(4), _full((1, SG_WIDTH)), _full((SG_GROUPS, CHUNK, CHUNK)), _full((CHUNK, SG_WIDTH)),
                  _full((SG_WIDTH, SG_WIDTH))],
        out_specs=pl.BlockSpec((tm, SG_WIDTH), lambda i: (i, 0)),
        out_shape=jax.ShapeDtypeStruct((s, SG_WIDTH), F32),
        compiler_params=_params(("arbitrary",), 32),
    )(proj, proj, gz, wm, bt, bd)


def _sgu_bwd(proj, dmix, gz, wm, wmt, bt, bd):
    s = proj.shape[0]
    tm = min(512, s)

    def body(u_ref, z_ref, dg_ref, gz_ref, wm_ref, wmt_ref, bt_ref, bd_ref, o_ref, dwm_ref, dbt_ref, dgz_ref):
        @pl.when(pl.program_id(0) == 0)
        def _():
            dwm_ref[...] = jnp.zeros_like(dwm_ref)
            dbt_ref[...] = jnp.zeros_like(dbt_ref)
            dgz_ref[...] = jnp.zeros_like(dgz_ref)

        lane = lax.broadcasted_iota(jnp.int32, (CHUNK, LANE), 1)
        bdv = bd_ref[...]
        u = u_ref[...]
        z = z_ref[...]
        ug = _gelu(u)
        zg = _gelu(z)
        r = lax.rsqrt(_group_mean(zg * zg, bdv) + EPS)
        zh = zg * r
        zn = zh * gz_ref[...]
        dzn_rows = []
        for c in range(tm // CHUNK):
            rows = slice(c * CHUNK, (c + 1) * CHUNK)
            sp = _sgu_spatial(zn, wm_ref, lane, c) + bt_ref[...]
            dgg = dg_ref[rows, :]
            ds = dgg * ug[rows, :]
            o_ref[rows, 0:SG_WIDTH] = dgg * sp * _gelu_grad(u[rows, :])
            dbt_ref[...] += ds
            parts = []
            for p in range(SG_WIDTH // LANE):
                dsb = ds[:, p * LANE:(p + 1) * LANE].astype(BF16)
                znb = zn[rows, p * LANE:(p + 1) * LANE].astype(BF16)
                acc = jnp.zeros((CHUNK, LANE), F32)
                for hh in range(2):
                    hm = (lane >= hh * HEAD_DIM) & (lane < (hh + 1) * HEAD_DIM)
                    dsm = jnp.where(hm, dsb, jnp.zeros((), BF16))
                    znm = jnp.where(hm, znb, jnp.zeros((), BF16))
                    acc = acc + _dot(wmt_ref[2 * p + hh], dsm)
                    dwm_ref[2 * p + hh] += lax.dot_general(dsm, znm, NT_DIMS, preferred_element_type=F32)
                parts.append(acc)
            dzn_rows.append(jnp.concatenate(parts, axis=1))
        dzn = jnp.concatenate(dzn_rows, axis=0)
        dgz_ref[...] += jnp.sum(dzn * zh, axis=0, keepdims=True)
        dn = dzn * gz_ref[...]
        o_ref[:, SG_WIDTH:2 * SG_WIDTH] = r * (dn - zh * _group_mean(dn * zh, bdv)) * _gelu_grad(z)

    col = lambda c: pl.BlockSpec((tm, SG_WIDTH), lambda i: (i, c))
    wspec = _full((SG_GROUPS, CHUNK, CHUNK))
    return pl.pallas_call(
        body, name="sgu_bwd", grid=(s // tm,),
        in_specs=[col(3), col(4), pl.BlockSpec((tm, SG_WIDTH), lambda i: (i, 1)), _full((1, SG_WIDTH)), wspec, wspec,
                  _full((CHUNK, SG_WIDTH)), _full((SG_WIDTH, SG_WIDTH))],
        out_specs=[pl.BlockSpec((tm, 2 * SG_WIDTH), lambda i: (i, 0)), wspec, _full((CHUNK, SG_WIDTH)),
                   _full((1, SG_WIDTH))],
        out_shape=[jax.ShapeDtypeStruct((s, 2 * SG_WIDTH), F32), jax.ShapeDtypeStruct((SG_GROUPS, CHUNK, CHUNK), F32),
                   jax.ShapeDtypeStruct((CHUNK, SG_WIDTH), F32), jax.ShapeDtypeStruct((1, SG_WIDTH), F32)],
        compiler_params=_params(("arbitrary",), 40),
    )(proj, proj, dmix, gz, wm, wmt, bt, bd)


def _shifted_copies(ext, sh):
    e = ext[...]
    sh[0] = e
    for b in range(1, 8):
        sh[b] = pltpu.roll(e, e.shape[0] - b, 0)


def _shifted_rows(sh, off, r0, cols):
    start = r0 + off - off % 8
    return sh[off % 8, start:start + CONV_ROWS, cols]


def _dwconv(sh, w_ref, offsets, rows, store):
    for r0 in range(0, rows, CONV_ROWS):
        for c0 in range(0, sh.shape[2], CONV_LANES):
            cols = slice(c0, c0 + CONV_LANES)
            acc = jnp.zeros((CONV_ROWS, CONV_LANES), F32)
            for j, off in enumerate(offsets):
                acc = acc + w_ref[j:j + 1, cols] * _shifted_rows(sh, off, r0, cols)
            store(slice(r0, r0 + CONV_ROWS), cols, acc)


def _conf_mid_fwd(p, wdw, bdw, lng, lnb):
    s = p.shape[0]
    c = p.shape[1] // 2
    tm = min(256, s)

    def body(a_ref, gt_ref, w_ref, b_ref, g_ref, beta_ref, yc_ref, y2_ref, ext, sh):
        i = pl.program_id(0)

        @pl.when(i == 0)
        def _():
            ext[0:HALO, :] = jnp.zeros((HALO, c), F32)

        @pl.when(i > 0)
        def _():
            ext[0:HALO, :] = ext[tm:tm + HALO, :]

        ext[HALO:HALO + tm, :] = a_ref[...] * _sigmoid(gt_ref[...])

        def store(rows, cols, block):
            yc_ref[rows, cols] = block + b_ref[:, cols]

        _shifted_copies(ext, sh)
        _dwconv(sh, w_ref, [HALO - CONV_K + 1 + j for j in range(CONV_K)], tm, store)
        acc = yc_ref[...]
        xc = acc - jnp.mean(acc, axis=-1, keepdims=True)
        ln = xc * lax.rsqrt(jnp.mean(xc * xc, axis=-1, keepdims=True) + EPS) * g_ref[...] + beta_ref[...]
        y2_ref[...] = (ln * _sigmoid(ln)).astype(BF16)

    vec = _full((1, c))
    return pl.pallas_call(
        body, name="conf_mid_fwd", grid=(s // tm,),
        in_specs=[pl.BlockSpec((tm, c), lambda i: (i, 0)), pl.BlockSpec((tm, c), lambda i: (i, 1)), _full((HALO, c)), vec,
                  vec, vec],
        out_specs=[pl.BlockSpec((tm, c), lambda i: (i, 0)), pl.BlockSpec((tm, c), lambda i: (i, 0))],
        out_shape=[jax.ShapeDtypeStruct((s, c), F32), jax.ShapeDtypeStruct((s, c), BF16)],
        scratch_shapes=[pltpu.VMEM((HALO + tm, c), F32), pltpu.VMEM((8, HALO + tm, c), F32)],
        compiler_params=_params(("arbitrary",), 40),
    )(p, p, wdw, bdw, lng, lnb)


def _conf_mid_bwd(p, yc, dout, x, gm, wdw, lng, lnb, w1, w2, send=()):
    s = p.shape[0]
    c = p.shape[1] // 2
    tm = min(256, s)
    n = s // tm
    hb = tm // HALO

    def body(a_ref, gt_ref, ah_ref, gh_ref, yc_ref, dout_ref, x_ref, gm_ref, w_ref, g_ref, beta_ref, w1_hbm, w2_hbm,
             dp_ref, dw_ref, dbdw_ref, dlg_ref, dlb_ref, db1_ref, db2_ref, dx_ref, dgm_ref,
             exty, extd, dyv, dwacc, shy, shd, w1_ref, w2_ref, wsem):
        i = pl.program_id(0)

        @pl.when(i == 0)
        def _():
            loads = [pltpu.make_async_copy(w1_hbm, w1_ref, wsem.at[0]),
                     pltpu.make_async_copy(w2_hbm, w2_ref, wsem.at[1])]
            for cp in loads:
                cp.start()
            extd[tm:tm + HALO, :] = jnp.zeros((HALO, c), F32)
            for ref in (dw_ref, dbdw_ref, dlg_ref, dlb_ref, db1_ref, db2_ref, dgm_ref, dwacc):
                ref[...] = jnp.zeros_like(ref)
            for cp in loads:
                cp.wait()

        @pl.when(i > 0)
        def _():
            extd[tm:tm + HALO, :] = extd[0:HALO, :]

        dout = dout_ref[...]
        dy2 = lax.dot_general(dout.astype(BF16), w2_ref[...], NT_DIMS, preferred_element_type=F32)
        a = a_ref[...]
        sg = _sigmoid(gt_ref[...])
        exty[HALO:HALO + tm, :] = a * sg
        exty[0:HALO, :] = jnp.where(i < n - 1, ah_ref[...] * _sigmoid(gh_ref[...]), 0.0)
        ycv = yc_ref[...]
        xc = ycv - jnp.mean(ycv, axis=-1, keepdims=True)
        rstd = lax.rsqrt(jnp.mean(xc * xc, axis=-1, keepdims=True) + EPS)
        xh = xc * rstd
        ln = xh * g_ref[...] + beta_ref[...]
        dln = dy2 * _silu_grad(ln, _sigmoid(ln))
        dlg_ref[...] += jnp.sum(dln * xh, axis=0, keepdims=True)
        dlb_ref[...] += jnp.sum(dln, axis=0, keepdims=True)
        dxh = dln * g_ref[...]
        dyc = rstd * (dxh - jnp.mean(dxh, axis=-1, keepdims=True) - xh * jnp.mean(dxh * xh, axis=-1, keepdims=True))
        extd[0:tm, :] = dyc
        dbdw_ref[...] += jnp.sum(dyc, axis=0, keepdims=True)
        db2_ref[...] += jnp.sum(dout, axis=0, keepdims=True)

        def store(rows, cols, block):
            dyv[rows, cols] = block

        _shifted_copies(extd, shd)
        _shifted_copies(exty, shy)
        _dwconv(shd, w_ref, [CONV_K - 1 - j for j in range(CONV_K)], tm, store)
        for r0 in range(0, tm, CONV_ROWS):
            for c0 in range(0, c, CONV_LANES):
                cols = slice(c0, c0 + CONV_LANES)
                dsub = extd[r0:r0 + CONV_ROWS, cols]
                for j in range(CONV_K):
                    prod = dsub * _shifted_rows(shy, HALO - CONV_K + 1 + j, r0, cols)
                    dwacc[8 * j:8 * j + 8, cols] += prod.reshape(CONV_ROWS // 8, 8, CONV_LANES).sum(axis=0)

        @pl.when(i == n - 1)
        def _():
            dw_ref[...] = dwacc[...].reshape(HALO, 8, c).sum(axis=1)

        dy = dyv[...]
        da = dy * sg
        dgt = dy * a * sg * (1.0 - sg)
        dab, dgtb = da.astype(BF16), dgt.astype(BF16)
        dp_ref[:, 0:c] = dab
        dp_ref[:, c:2 * c] = dgtb
        db1_ref[:, 0:c] += jnp.sum(da, axis=0, keepdims=True)
        db1_ref[:, c:2 * c] += jnp.sum(dgt, axis=0, keepdims=True)
        dh = (lax.dot_general(dab, w1_ref[:, 0:c], NT_DIMS, preferred_element_type=F32)
              + lax.dot_general(dgtb, w1_ref[:, c:2 * c], NT_DIMS, preferred_element_type=F32))
        xv = x_ref[...]
        r = lax.rsqrt(jnp.mean(xv * xv, axis=-1, keepdims=True) + EPS)
        xn = xv * r
        dgm_ref[...] += jnp.sum(dh * xn, axis=0, keepdims=True)
        dn = dh * gm_ref[...]
        dx_ref[...] = dout + r * (dn - xn * jnp.mean(dn * xn, axis=-1, keepdims=True))

    rev = lambda col: pl.BlockSpec((tm, c), lambda i: (n - 1 - i, col))
    halo = lambda col: pl.BlockSpec((HALO, c), lambda i: (jnp.maximum((n - 1 - i) * hb - 1, 0), col))
    vec = _full((1, c))
    step_is = lambda i: lambda: pl.program_id(0) == i
    kern, ins, outs, shapes, sems = _carried(body, 13, 9, send, "scatter",
                                             {"start": step_is(0), "finish": step_is(n - 1)})
    res = pl.pallas_call(
        kern, name="conf_mid_bwd", grid=(n,),
        in_specs=[rev(0), rev(1), halo(0), halo(1), rev(0), rev(0), rev(0), vec, _full((HALO, c)), vec, vec,
                  pl.BlockSpec(memory_space=pl.ANY), pl.BlockSpec(memory_space=pl.ANY)] + ins,
        out_specs=[pl.BlockSpec((tm, 2 * c), lambda i: (n - 1 - i, 0)), _full((HALO, c)), vec, vec, vec,
                   _full((1, 2 * c)), vec, rev(0), vec] + outs,
        out_shape=[jax.ShapeDtypeStruct((s, 2 * c), BF16), jax.ShapeDtypeStruct((HALO, c), F32),
                   jax.ShapeDtypeStruct((1, c), F32), jax.ShapeDtypeStruct((1, c), F32), jax.ShapeDtypeStruct((1, c), F32),
                   jax.ShapeDtypeStruct((1, 2 * c), F32), jax.ShapeDtypeStruct((1, c), F32),
                   jax.ShapeDtypeStruct((s, c), F32), jax.ShapeDtypeStruct((1, c), F32)] + shapes,
        scratch_shapes=[pltpu.VMEM((HALO + tm, c), F32), pltpu.VMEM((tm + HALO, c), F32), pltpu.VMEM((tm, c), F32),
                        pltpu.VMEM((8 * HALO, c), F32), pltpu.VMEM((8, HALO + tm, c), F32),
                        pltpu.VMEM((8, tm + HALO, c), F32), pltpu.VMEM((c, 2 * c), BF16), pltpu.VMEM((c, c), BF16),
                        pltpu.SemaphoreType.DMA((2,))] + sems,
        compiler_params=_params(("arbitrary",), 56),
    )(p, p, p, p, yc, dout, x, gm, wdw, lng, lnb, w1, w2, *send)
    return tuple(res[:9]) + (list(res[9:]),)


def _ffn_fwd(x, g, wup, wdw, bdw, wdn):
    s, d = x.shape
    ff = wdn.shape[0]
    tm = min(256, s)

    def body(x_ref, g_ref, wup_ref, wdw_ref, bdw_ref, wdn_ref, y_ref, h_ref, u_ref, carry):
        @pl.when(pl.program_id(0) == 0)
        def _():
            carry[...] = jnp.zeros_like(carry)

        xv = x_ref[...]
        r = lax.rsqrt(jnp.mean(xv * xv, axis=-1, keepdims=True) + EPS)
        h = (xv * r * g_ref[...]).astype(BF16)
        h_ref[...] = h
        acc = xv
        up = lambda c: (_dot(h, wup_ref[:, c:c + FF_CHUNK]), _dot(h, wup_ref[:, ff + c:ff + c + FF_CHUNK]))
        ahead = up(0)
        for c in range(0, ff, FF_CHUNK):
            cs = slice(c, c + FF_CHUNK)
            gp, val = ahead
            if c + FF_CHUNK < ff:
                ahead = up(c + FF_CHUNK)
            u_ref[:, cs] = gp.astype(BF16)
            u_ref[:, ff + c:ff + c + FF_CHUNK] = val.astype(BF16)
            prev = carry[:, cs]
            gate = (wdw_ref[0:1, cs] * _shift_down(gp, prev, 2) + wdw_ref[1:2, cs] * _shift_down(gp, prev, 1)
                    + wdw_ref[2:3, cs] * gp + bdw_ref[:, cs])
            act = gate * _sigmoid(gate) * val
            acc = acc + _dot(act.astype(BF16), wdn_ref[cs, :])
            carry[:, cs] = gp[tm - 8:tm, :]
        y_ref[...] = acc

    return pl.pallas_call(
        body, name="ffn_fwd", grid=(s // tm,),
        in_specs=[pl.BlockSpec((tm, d), lambda i: (i, 0)), _full((1, d)), _full((d, 2 * ff)), _full((8, ff)),
                  _full((1, ff)), _full((ff, d))],
        out_specs=[pl.BlockSpec((tm, d), lambda i: (i, 0)), pl.BlockSpec((tm, d), lambda i: (i, 0)),
                   pl.BlockSpec((tm, 2 * ff), lambda i: (i, 0))],
        out_shape=[jax.ShapeDtypeStruct((s, d), F32), jax.ShapeDtypeStruct((s, d), BF16),
                   jax.ShapeDtypeStruct((s, 2 * ff), BF16)],
        scratch_shapes=[pltpu.VMEM((8, ff), F32)],
        compiler_params=_params(("arbitrary",), 56),
    )(x, g, wup, wdw, bdw, wdn)


def _ffn_bwd(dy, u, x, g, wdw, bdw, wdn, wup, send=()):
    s, d = dy.shape
    ff = wdn.shape[0]
    tm = min(256, s)
    n = s // tm
    hb = tm // 16

    def body(dy_ref, u_ref, uh_ref, x_ref, g_ref, wdw_ref, bdw_ref, wdn_hbm, wup_hbm,
             du_ref, act_ref, dw_ref, db_ref, dx_ref, dg_ref, carry, wdn_ref, wup_ref, wsem):
        i = pl.program_id(0)

        @pl.when(i == 0)
        def _():
            loads = [pltpu.make_async_copy(wdn_hbm, wdn_ref, wsem.at[0]),
                     pltpu.make_async_copy(wup_hbm, wup_ref, wsem.at[1])]
            for cp in loads:
                cp.start()
            carry[...] = jnp.zeros_like(carry)
            dw_ref[...] = jnp.zeros_like(dw_ref)
            db_ref[...] = jnp.zeros_like(db_ref)
            dg_ref[...] = jnp.zeros_like(dg_ref)
            for cp in loads:
                cp.wait()

        dyv = dy_ref[...]
        dyb = dyv.astype(BF16)
        down = lambda c: lax.dot_general(dyb, wdn_ref[c:c + FF_CHUNK, :], NT_DIMS, preferred_element_type=F32)
        ahead = down(0)
        dh = jnp.zeros((tm, d), F32)
        for c in range(0, ff, FF_CHUNK):
            cs = slice(c, c + FF_CHUNK)
            vs = slice(ff + c, ff + c + FF_CHUNK)
            da = ahead
            if c + FF_CHUNK < ff:
                ahead = down(c + FF_CHUNK)
            gp = u_ref[:, cs].astype(F32)
            val = u_ref[:, vs].astype(F32)
            prev = jnp.where(i < n - 1, uh_ref[:, cs].astype(F32)[8:16], 0.0)
            g1 = _shift_down(gp, prev, 1)
            g2 = _shift_down(gp, prev, 2)
            gate = wdw_ref[0:1, cs] * g2 + wdw_ref[1:2, cs] * g1 + wdw_ref[2:3, cs] * gp + bdw_ref[:, cs]
            sg = _sigmoid(gate)
            si = gate * sg
            act_ref[:, cs] = (si * val).astype(BF16)
            dgate = da * val * _silu_grad(gate, sg)
            nxt = carry[:, cs]
            dgp = (wdw_ref[2:3, cs] * dgate + wdw_ref[1:2, cs] * _shift_up(dgate, nxt, 1)
                   + wdw_ref[0:1, cs] * _shift_up(dgate, nxt, 2)).astype(BF16)
            dval = (da * si).astype(BF16)
            du_ref[:, cs] = dgp
            du_ref[:, vs] = dval
            dh = (dh + lax.dot_general(dgp, wup_ref[:, cs], NT_DIMS, preferred_element_type=F32)
                  + lax.dot_general(dval, wup_ref[:, vs], NT_DIMS, preferred_element_type=F32))
            dw_ref[:, cs] += jnp.concatenate(
                [jnp.sum(dgate * g2, axis=0, keepdims=True), jnp.sum(dgate * g1, axis=0, keepdims=True),
                 jnp.sum(dgate * gp, axis=0, keepdims=True), jnp.zeros((5, FF_CHUNK), F32)], axis=0)
            db_ref[:, cs] += jnp.sum(dgate, axis=0, keepdims=True)
            carry[:, cs] = dgate[0:8, :]
        xv = x_ref[...]
        r = lax.rsqrt(jnp.mean(xv * xv, axis=-1, keepdims=True) + EPS)
        xh = xv * r
        dg_ref[...] += jnp.sum(dh * xh, axis=0, keepdims=True)
        dn = dh * g_ref[...]
        dx_ref[...] = dyv + r * (dn - xh * jnp.mean(dn * xh, axis=-1, keepdims=True))

    step_is = lambda i: lambda: pl.program_id(0) == i
    kern, ins, outs, shapes, sems = _carried(body, 9, 6, send, "scatter",
                                             {"start": step_is(0), "finish": step_is(n - 1)})
    rev = lambda cols: pl.BlockSpec((tm, cols), lambda i: (n - 1 - i, 0))
    any_spec = pl.BlockSpec(memory_space=pl.ANY)
    res = pl.pallas_call(
        kern, name="ffn_bwd", grid=(n,),
        in_specs=[rev(d), rev(2 * ff),
                  pl.BlockSpec((16, 2 * ff), lambda i: (jnp.maximum((n - 1 - i) * hb - 1, 0), 0)),
                  rev(d), _full((1, d)), _full((8, ff)), _full((1, ff)), any_spec, any_spec] + ins,
        out_specs=[rev(2 * ff), rev(ff), _full((8, ff)), _full((1, ff)), rev(d), _full((1, d))] + outs,
        out_shape=[jax.ShapeDtypeStruct((s, 2 * ff), BF16), jax.ShapeDtypeStruct((s, ff), BF16),
                   jax.ShapeDtypeStruct((8, ff), F32), jax.ShapeDtypeStruct((1, ff), F32),
                   jax.ShapeDtypeStruct((s, d), F32), jax.ShapeDtypeStruct((1, d), F32)] + shapes,
        scratch_shapes=[pltpu.VMEM((8, ff), F32), pltpu.VMEM((ff, d), BF16), pltpu.VMEM((d, 2 * ff), BF16),
                        pltpu.SemaphoreType.DMA((2,))] + sems,
        compiler_params=_params(("arbitrary",), 56),
    )(dy, u, u, x, g, wdw, bdw, wdn, wup, *send)
    return tuple(res[:6]) + (list(res[6:]),)


def _loss_head(y, target):
    s, d = y.shape
    tm = min(512, s)

    def body(y_ref, t_ref, l_ref, dy_ref):
        @pl.when(pl.program_id(0) == 0)
        def _():
            l_ref[...] = jnp.zeros_like(l_ref)

        err = y_ref[...] - t_ref[...]
        dy_ref[...] = err * (1.0 / d)
        l_ref[...] += 0.5 * jnp.sum(jnp.mean(err * err, axis=-1, keepdims=True), axis=0, keepdims=True)

    return pl.pallas_call(
        body, name="loss_head", grid=(s // tm,),
        in_specs=[pl.BlockSpec((tm, d), lambda i: (i, 0)), pl.BlockSpec((tm, d), lambda i: (i, 0))],
        out_specs=[_full((8, LANE)), pl.BlockSpec((tm, d), lambda i: (i, 0))],
        out_shape=[jax.ShapeDtypeStruct((8, LANE), F32), jax.ShapeDtypeStruct((s, d), F32)],
        compiler_params=_params(("arbitrary",), 32),
    )(y, target)


def _row_tile(rows, limit=512):
    for cand in range(min(limit, rows) // 16 * 16, 0, -16):
        if rows % cand == 0:
            return cand
    return rows


def _reduce_adamw(parts, w, m, v, name):
    nl = len(parts)
    _, a, b = parts[0].shape
    ta = _row_tile(a, 256)

    def body(*refs):
        p_refs = refs[:nl]
        w_ref, m_ref, v_ref, g_ref, d_ref, mo_ref, vo_ref = refs[nl:]
        for layer in range(nl):
            @pl.when(pl.program_id(0) == layer)
            def _(p_ref=p_refs[layer]):
                g = p_ref[0].astype(F32)
                for k in range(1, N_DEV):
                    g = g + p_ref[k].astype(F32)
                g_ref[0] = g

        g = g_ref[0]
        mn = ADAM_B1 * m_ref[...] + (1.0 - ADAM_B1) * g
        vn = ADAM_B2 * v_ref[...] + (1.0 - ADAM_B2) * (g * g)
        mo_ref[...] = mn
        vo_ref[...] = vn
        m_hat = mn / (1.0 - ADAM_B1 ** ADAM_STEP)
        v_hat = vn / (1.0 - ADAM_B2 ** ADAM_STEP)
        d_ref[...] = -ADAM_LR * (m_hat / (jnp.sqrt(v_hat) + ADAM_EPS) + ADAM_WD * w_ref[...])

    blk = pl.BlockSpec((1, ta, b), lambda l, i: (l, i, 0))
    part = lambda layer: pl.BlockSpec((N_DEV, ta, b), lambda l, i: (0, jnp.where(l == layer, i, 0), 0))
    return pl.pallas_call(
        body, name=name, grid=(nl, a // ta),
        in_specs=[part(layer) for layer in range(nl)] + [blk, blk, blk],
        out_specs=[blk, blk, blk, blk],
        out_shape=[jax.ShapeDtypeStruct((nl, a, b), F32)] * 4,
        compiler_params=_params(("arbitrary", "arbitrary"), 56),
    )(*parts, w, m, v)


def _unshard(w8, layer, name):
    _, _, k, n = w8.shape
    tk = _row_tile(k, 256)

    def body(w_ref, o_ref):
        for d in range(N_DEV):
            o_ref[:, d * n:(d + 1) * n] = w_ref[d, 0]

    return pl.pallas_call(
        body, name=name, grid=(k // tk,),
        in_specs=[pl.BlockSpec((N_DEV, 1, tk, n), lambda i: (0, layer, i, 0))],
        out_specs=pl.BlockSpec((tk, N_DEV * n), lambda i: (i, 0)),
        out_shape=jax.ShapeDtypeStruct((k, N_DEV * n), w8.dtype),
        compiler_params=_params(("arbitrary",), 32),
    )(w8)


def _shard_cast(g, name):
    k, n8 = g.shape
    n = n8 // N_DEV
    tk = _row_tile(k, 256)

    def body(g_ref, o_ref):
        for d in range(N_DEV):
            o_ref[d] = g_ref[:, d * n:(d + 1) * n].astype(BF16)

    return pl.pallas_call(
        body, name=name, grid=(k // tk,),
        in_specs=[pl.BlockSpec((tk, n8), lambda i: (i, 0))],
        out_specs=pl.BlockSpec((N_DEV, tk, n), lambda i: (0, i, 0)),
        out_shape=jax.ShapeDtypeStruct((N_DEV, k, n), BF16),
        compiler_params=_params(("arbitrary",), 32),
    )(g)


def _row(v):
    return v.reshape(1, -1)


def _group_ones():
    idx = jnp.arange(SB_WIDTH) // HEAD_DIM
    return (idx[:, None] == idx[None, :]).astype(BF16)


def _pad_rows(w, rows):
    return jnp.concatenate([w, jnp.zeros((rows - w.shape[0], w.shape[1]), w.dtype)], axis=0)


class _NoTraffic:
    def rest(self):
        return ()

    def install(self, wt, gathered):
        pass

    def ready(self, name, layer, grad):
        pass

    def take(self):
        return ()

    def landed(self, received):
        pass


def _local_step(x, target, wt, traffic):
    scale = HEAD_DIM ** -0.5
    bd = _group_ones()
    tril = jnp.tril(jnp.ones((CHUNK, CHUNK), dtype=bool))
    saved = []
    for i in range(DEPTH):
        j = i // 2
        lay = {"x_mix": x}
        if i % 2 == 0:
            proj, h = _rms_matmul(x, _row(wt["mix_norm_g"][i]), wt["sb_w_in"][j], jnp.zeros((1, IN_WIDTH), F32), "in_proj")
            gq = _row(jnp.tile(wt["sb_q_norm_g"][j], SB_WIDTH // HEAD_DIM)) * scale
            gk = _row(jnp.tile(wt["sb_k_norm_g"][j], SB_WIDTH // HEAD_DIM))
            qn, kn, vb = _qk_prep(proj, gq * LOG2E, gk, bd)
            o, lsum, gathered = _sb_attn_fwd(qn, kn, vb, gather=traffic.rest() if i == 0 else ())
            if i == 0:
                traffic.install(wt, gathered)
            wm = jnp.where(tril[None], wt["sg_w_spatial"][j], 0.0)
            wmb = wm.astype(BF16)
            wmt = jnp.swapaxes(wm, 1, 2).astype(BF16)
            bt = jnp.repeat(wt["sg_b_spatial"][j].T, HEAD_DIM, axis=1)
            gz = _row(wt["sg_z_norm_g"][j])
            gg = _sgu_fwd(proj, gz, wmb, bt, bd)
            x, mix = _out_proj(x, o, gg, wt["hyb_w_out"][j], "out_proj")
            lay.update(proj=proj, h=h, gq=gq, gk=gk, qn=qn, kn=kn, vb=vb, lsum=lsum, wmb=wmb, wmt=wmt, bt=bt, gz=gz, mix=mix)
        else:
            p, h = _rms_matmul(x, _row(wt["mix_norm_g"][i]), wt["cv_w_pw1"][j], _row(wt["cv_b_pw1"][j]), "conf_pw1")
            wdw = _pad_rows(wt["cv_w_dw"][j], HALO)
            yc, y2 = _conf_mid_fwd(p, wdw, _row(wt["cv_b_dw"][j]), _row(wt["cv_ln_g"][j]), _row(wt["cv_ln_b"][j]))
            x = _res_matmul(x, y2, wt["cv_w_pw2"][j], _row(wt["cv_b_pw2"][j]), "conf_pw2")
            lay.update(p=p, h=h, wdw=wdw, yc=yc, y2=y2)
        lay["x_ffn"] = x
        fdw = _pad_rows(wt["ffn_w_dw"][i], 8)
        x, hf, u = _ffn_fwd(x, _row(wt["ffn_norm_g"][i]), wt["ffn_w_up"][i], fdw, _row(wt["ffn_b_dw"][i]),
                            wt["ffn_w_down"][i])
        lay.update(hf=hf, u=u, fdw=fdw)
        saved.append(lay)

    lpart, dy = _loss_head(x, target)
    loss = lpart[0, 0]

    gr = {k: [None] * len(v) for k, v in wt.items()}

    def made(name, layer, grad):
        gr[name][layer] = grad
        traffic.ready(name, layer, grad)

    for i in reversed(range(DEPTH)):
        j = i // 2
        lay = saved[i]
        du, act, dfdw, dfb, dx, dgf, got = _ffn_bwd(
            dy, lay["u"], lay["x_ffn"], _row(wt["ffn_norm_g"][i]), lay["fdw"], _row(wt["ffn_b_dw"][i]),
            wt["ffn_w_down"][i], wt["ffn_w_up"][i], send=traffic.take())
        traffic.landed(got)
        made("ffn_w_down", i, _matmul_tn(act, dy, "ffn_dw_down"))
        made("ffn_w_up", i, _matmul_tn(lay["hf"], du, "ffn_dw_up"))
        gr["ffn_w_dw"][i] = dfdw[:FFN_K]
        gr["ffn_b_dw"][i] = dfb[0]
        gr["ffn_norm_g"][i] = dgf[0]
        dy = dx
        if i % 2 == 0:
            dmix = _matmul_nt(dy, wt["hyb_w_out"][j], "out_proj_dx")
            made("hyb_w_out", j, _matmul_tn(lay["mix"], dy, "out_proj_dw"))
            dqn, dkn, dv, got = _sb_attn_bwd(lay["qn"], lay["kn"], lay["vb"], lay["lsum"], dmix, send=traffic.take())
            traffic.landed(got)
            duz, dwm, dbt, dgz = _sgu_bwd(lay["proj"], dmix, lay["gz"], lay["wmb"], lay["wmt"], lay["bt"], bd)
            dproj, dgq, dgk = _qk_bwd(lay["proj"], dqn, dkn, dv, duz, lay["gq"], lay["gk"] * LN2, bd)
            made("sb_w_in", j, _matmul_tn(lay["h"], dproj, "in_proj_dw"))
            gr["sb_q_norm_g"][j] = dgq.reshape(SB_WIDTH // HEAD_DIM, HEAD_DIM).sum(0) * scale
            gr["sb_k_norm_g"][j] = dgk.reshape(SB_WIDTH // HEAD_DIM, HEAD_DIM).sum(0) * LN2
            gr["sg_z_norm_g"][j] = dgz[0]
            gr["sg_w_spatial"][j] = jnp.where(tril[None], dwm, 0.0)
            gr["sg_b_spatial"][j] = dbt.reshape(CHUNK, SG_GROUPS, HEAD_DIM).sum(-1).T
            dy, dgm, got = _nt_rms_bwd(dproj, wt["sb_w_in"][j], lay["x_mix"], _row(wt["mix_norm_g"][i]), dy,
                                       "in_proj_dx", send=traffic.take() if i == 0 else ())
            if i == 0:
                traffic.landed(got)
        else:
            made("cv_w_pw2", j, _matmul_tn(lay["y2"], dy, "conf_pw2_dw"))
            dp, dwdw, dbdw, dlg, dlb, db1, db2, dx, dgm, got = _conf_mid_bwd(
                lay["p"], lay["yc"], dy, lay["x_mix"], _row(wt["mix_norm_g"][i]), lay["wdw"],
                _row(wt["cv_ln_g"][j]), _row(wt["cv_ln_b"][j]), wt["cv_w_pw1"][j], wt["cv_w_pw2"][j],
                send=traffic.take())
            traffic.landed(got)
            dy = dx
            made("cv_w_pw1", j, _matmul_tn(lay["h"], dp, "conf_pw1_dw"))
            gr["cv_w_dw"][j] = dwdw[:CONV_K]
            gr["cv_b_dw"][j] = dbdw[0]
            gr["cv_ln_g"][j] = dlg[0]
            gr["cv_ln_b"][j] = dlb[0]
            gr["cv_b_pw1"][j] = db1[0]
            gr["cv_b_pw2"][j] = db2[0]
        gr["mix_norm_g"][i] = dgm[0]
    matmul_weights = ("sb_w_in", "hyb_w_out", "cv_w_pw1", "cv_w_pw2", "ffn_w_up", "ffn_w_down")
    grads = {k: (v if k in matmul_weights else jnp.stack(v)) for k, v in gr.items()}
    return loss, dy, grads


WEIGHTS = ["mix_norm_g", "sb_w_in", "sb_q_norm_g", "sb_k_norm_g", "sg_z_norm_g", "sg_w_spatial", "sg_b_spatial",
           "hyb_w_out", "cv_w_pw1", "cv_b_pw1", "cv_w_dw", "cv_b_dw", "cv_ln_g", "cv_ln_b", "cv_w_pw2", "cv_b_pw2",
           "ffn_norm_g", "ffn_w_up", "ffn_w_dw", "ffn_b_dw", "ffn_w_down"]
BIG = [("sb_w_in", "col"), ("hyb_w_out", "row"), ("cv_w_pw1", "col"), ("cv_w_pw2", "row"), ("ffn_w_up", "col"),
       ("ffn_w_down", "row")]
SMALL = ["cv_b_pw1", "cv_w_dw", "cv_b_dw", "cv_ln_g", "cv_ln_b", "cv_b_pw2", "ffn_w_dw"]
REPLICATED = ["mix_norm_g", "sb_q_norm_g", "sb_k_norm_g", "sg_z_norm_g", "sg_w_spatial", "sg_b_spatial", "ffn_norm_g",
              "ffn_b_dw"]


def _last_dim_blocks(full):
    t = jnp.moveaxis(full.reshape(full.shape[:-1] + (N_DEV, full.shape[-1] // N_DEV)), -2, 0)
    return t.reshape(N_DEV, -1)


def _from_last_dim_blocks(blocks, shard_shape):
    t = jnp.moveaxis(blocks.reshape((N_DEV,) + tuple(shard_shape)), 0, -2)
    return t.reshape(tuple(shard_shape[:-1]) + (N_DEV * shard_shape[-1],))


def _pack(arrays):
    lead = arrays[0].shape[:-1]
    flat = jnp.concatenate([a.astype(F32) for a in arrays], axis=-1)
    rows = -(-flat.shape[-1] // (16 * LANE)) * 16
    pad = rows * LANE - flat.shape[-1]
    if pad:
        flat = jnp.concatenate([flat, jnp.zeros(lead + (pad,), F32)], axis=-1)
    return flat.reshape(lead + (rows, LANE))


def _unpack(packed, shapes):
    flat = packed.reshape(-1)
    out, off = [], 0
    for shp in shapes:
        size = 1
        for dim in shp:
            size *= dim
        out.append(flat[off:off + size].reshape(shp))
        off += size
    return out


class _ShardTraffic:
    def __init__(self, w):
        self.w = w
        self.queue, self.flying, self.received = [], [], {}

    def rest(self):
        small = _pack([self.w[n].reshape(-1) for n in SMALL])
        return [self.w["sb_w_in"][1:].astype(BF16)] + [self.w[n].astype(BF16) for n, _ in BIG[1:]] + [small]

    def install(self, wt, gathered):
        wt["sb_w_in"].append(_unshard(gathered[0], 0, "unshard_sb_w_in"))
        for (n, kind), w8 in zip(BIG[1:], gathered[1:-1]):
            if kind == "col":
                wt[n] = [_unshard(w8, l, "unshard_" + n) for l in range(w8.shape[1])]
            else:
                wt[n] = [w8[:, l].reshape((N_DEV * w8.shape[2],) + w8.shape[3:]) for l in range(w8.shape[1])]
        sizes = [(self.w[n].size,) for n in SMALL]
        for n, parts in zip(SMALL, zip(*[_unpack(gathered[-1][d], sizes) for d in range(N_DEV)])):
            wt[n] = _from_last_dim_blocks(jnp.stack(parts), self.w[n].shape)

    def ready(self, name, layer, grad):
        if dict(BIG)[name] == "col":
            blocks = _shard_cast(grad, "shard_" + name)
        else:
            blocks = grad.reshape((N_DEV, grad.shape[0] // N_DEV) + grad.shape[1:])
        self.queue.append(((name, layer), blocks))

    def take(self):
        self.flying = [key for key, _ in self.queue]
        arrays = [blocks for _, blocks in self.queue]
        self.queue = []
        return arrays

    def landed(self, received):
        for key, blocks in zip(self.flying, received):
            self.received[key] = blocks
        self.flying = []


def kernel(x, mix_norm_g, sb_w_in, sb_q_norm_g, sb_k_norm_g, sg_z_norm_g, sg_w_spatial, sg_b_spatial, hyb_w_out, cv_w_pw1, cv_b_pw1, cv_w_dw, cv_b_dw, cv_ln_g, cv_ln_b, cv_w_pw2, cv_b_pw2, ffn_norm_g, ffn_w_up, ffn_w_dw, ffn_b_dw, ffn_w_down, loss_target, m_mix_norm_g, m_sb_w_in, m_sb_q_norm_g, m_sb_k_norm_g, m_sg_z_norm_g, m_sg_w_spatial, m_sg_b_spatial, m_hyb_w_out, m_cv_w_pw1, m_cv_b_pw1, m_cv_w_dw, m_cv_b_dw, m_cv_ln_g, m_cv_ln_b, m_cv_w_pw2, m_cv_b_pw2, m_ffn_norm_g, m_ffn_w_up, m_ffn_w_dw, m_ffn_b_dw, m_ffn_w_down, v_mix_norm_g, v_sb_w_in, v_sb_q_norm_g, v_sb_k_norm_g, v_sg_z_norm_g, v_sg_w_spatial, v_sg_b_spatial, v_hyb_w_out, v_cv_w_pw1, v_cv_b_pw1, v_cv_w_dw, v_cv_b_dw, v_cv_ln_g, v_cv_ln_b, v_cv_w_pw2, v_cv_b_pw2, v_ffn_norm_g, v_ffn_w_up, v_ffn_w_dw, v_ffn_b_dw, v_ffn_w_down):
    w = dict(zip(WEIGHTS, (mix_norm_g, sb_w_in, sb_q_norm_g, sb_k_norm_g, sg_z_norm_g, sg_w_spatial, sg_b_spatial,
                           hyb_w_out, cv_w_pw1, cv_b_pw1, cv_w_dw, cv_b_dw, cv_ln_g, cv_ln_b, cv_w_pw2, cv_b_pw2,
                           ffn_norm_g, ffn_w_up, ffn_w_dw, ffn_b_dw, ffn_w_down)))
    m = dict(zip(WEIGHTS, (m_mix_norm_g, m_sb_w_in, m_sb_q_norm_g, m_sb_k_norm_g, m_sg_z_norm_g, m_sg_w_spatial,
                           m_sg_b_spatial, m_hyb_w_out, m_cv_w_pw1, m_cv_b_pw1, m_cv_w_dw, m_cv_b_dw, m_cv_ln_g,
                           m_cv_ln_b, m_cv_w_pw2, m_cv_b_pw2, m_ffn_norm_g, m_ffn_w_up, m_ffn_w_dw, m_ffn_b_dw,
                           m_ffn_w_down)))
    v = dict(zip(WEIGHTS, (v_mix_norm_g, v_sb_w_in, v_sb_q_norm_g, v_sb_k_norm_g, v_sg_z_norm_g, v_sg_w_spatial,
                           v_sg_b_spatial, v_hyb_w_out, v_cv_w_pw1, v_cv_b_pw1, v_cv_w_dw, v_cv_b_dw, v_cv_ln_g,
                           v_cv_ln_b, v_cv_w_pw2, v_cv_b_pw2, v_ffn_norm_g, v_ffn_w_up, v_ffn_w_dw, v_ffn_b_dw,
                           v_ffn_w_down)))
    big_names = [n for n, _ in BIG]
    flat = lambda t, names: [t[n].reshape(-1) for n in names]

    first_in = _all_gather([w["sb_w_in"][0:1].astype(BF16)], "gather_first")[0]
    wt = {n: w[n] for n in REPLICATED}
    wt["sb_w_in"] = [_unshard(first_in, 0, "unshard_sb_w_in")]

    traffic = _ShardTraffic(w)
    loss, gx, grads = _local_step(x[0], loss_target[0], wt, traffic)
    assert not traffic.queue and not traffic.flying
    recv_big = [[traffic.received[n, l] for l in range(w[n].shape[0])] for n in big_names]

    grep = _pack(flat(grads, REPLICATED))
    gsmall = _pack([_last_dim_blocks(grads[n]) for n in SMALL])
    rep_rows, small_rows = grep.shape[0], gsmall.shape[1]
    vec = jnp.concatenate([grep, gsmall.reshape(N_DEV * small_rows, LANE)], axis=0)
    vec_all = _all_gather([vec], "gather_small_grads")[0]
    me = 4 * lax.axis_index("x") + 2 * lax.axis_index("y") + lax.axis_index("c")
    recv_rep = vec_all[:, :rep_rows]
    recv_small = lax.dynamic_slice(vec_all, (0, rep_rows + small_rows * me, 0), (N_DEV, small_rows, LANE))

    out = {}
    kinds = ("grad", "delta", "new_m", "new_v")
    for n, parts in zip(big_names, recv_big):
        for kind, arr in zip(kinds, _reduce_adamw(parts, w[n], m[n], v[n], "adamw_" + n)):
            out[kind, n] = arr
    for names, recv, tag in ((SMALL, recv_small, "adamw_small"), (REPLICATED, recv_rep, "adamw_replicated")):
        res = _reduce_adamw([recv], _pack(flat(w, names))[None], _pack(flat(m, names))[None],
                            _pack(flat(v, names))[None], tag)
        shapes = [w[n].shape for n in names]
        for kind, packed in zip(kinds, res):
            for n, arr in zip(names, _unpack(packed[0], shapes)):
                out[kind, n] = arr

    loss = lax.psum(loss, ("x", "y", "c"))
    return (loss, gx[None], *[out[kind, n] for kind in kinds for n in WEIGHTS])
```

```python
import functools

import jax
import jax.numpy as jnp
from jax import lax
from jax.experimental import pallas as pl
from jax.experimental.pallas import tpu as pltpu

F32 = jnp.float32
BF16 = jnp.bfloat16

D_MODEL = 1024
HEAD_DIM = 64
SB_WIDTH = 512
SG_WIDTH = 512
SG_GROUPS = 8
IN_WIDTH = 3 * SB_WIDTH + 2 * SG_WIDTH
CHUNK = 128
CONV_K = 31
D_FF = 2816
FFN_K = 3
DEPTH = 4
EPS = 1e-6
N_DEV = 8
LANE = 128
HALO = 32
ATT_BLOCK = 256
FF_CHUNK = 256
CONV_ROWS = 32
CONV_LANES = 512
MIB = 2 ** 20

ADAM_LR = 0.001
ADAM_B1 = 0.9
ADAM_B2 = 0.999
ADAM_EPS = 1e-08
ADAM_WD = 0.01
ADAM_STEP = 10

LOG2E = 1.4426950408889634
LN2 = 0.6931471805599453

NT_DIMS = (((1,), (1,)), ((), ()))
TN_DIMS = (((0,), (0,)), ((), ()))


def _params(semantics, vmem_mib):
    return pltpu.CompilerParams(dimension_semantics=semantics, vmem_limit_bytes=vmem_mib * MIB)


def _full(shape):
    nd = len(shape)
    return pl.BlockSpec(shape, lambda *_: (0,) * nd)


def _sigmoid(x):
    return 1.0 / (1.0 + jnp.exp(-x))


def _gelu(x):
    return 0.5 * x * (1.0 + lax.erf(x * 0.7071067811865476))


def _gelu_grad(x):
    return 0.5 * (1.0 + lax.erf(x * 0.7071067811865476)) + x * jnp.exp(-0.5 * x * x) * 0.3989422804014327


def _silu_grad(x, s):
    return s * (1.0 + x * (1.0 - s))


def _dot(a, b):
    return jnp.dot(a, b, preferred_element_type=F32)


def _dot2(a, b):
    hi = a.astype(BF16)
    lo = (a - hi.astype(F32)).astype(BF16)
    return _dot(hi, b) + _dot(lo, b)


def _group_mean(t, bd):
    return _dot2(t, bd) * (1.0 / HEAD_DIM)


def _shift_down(v, prev8, s):
    top = pltpu.roll(jnp.concatenate([prev8, v[:8]], axis=0), s, 0)[8:16]
    return jnp.concatenate([top, pltpu.roll(v, s, 0)[8:]], axis=0)


def _shift_up(v, next8, s):
    n = v.shape[0]
    bottom = pltpu.roll(jnp.concatenate([v[n - 8:], next8], axis=0), 16 - s, 0)[0:8]
    return jnp.concatenate([pltpu.roll(v, n - s, 0)[: n - 8], bottom], axis=0)


def _mesh_pos():
    return lax.axis_index("x"), lax.axis_index("y"), lax.axis_index("c")


def _comm_scratch(n):
    return [pltpu.SemaphoreType.DMA((7 * n,)), pltpu.SemaphoreType.DMA((7 * n,)), pltpu.SemaphoreType.DMA((n,))]


class _Scatter:
    def __init__(self, src_refs, out_refs, send_sems, recv_sems, local_sems):
        x, y, cc = _mesh_pos()
        me = 4 * x + 2 * y + cc
        self.copies, self.mine = [], []
        for a, (src, out) in enumerate(zip(src_refs, out_refs)):
            self.mine.append(pltpu.make_async_copy(src.at[me], out.at[me], local_sems.at[a]))
            for k in range(1, N_DEV):
                px = 1 - x if k & 4 else x
                py = 1 - y if k & 2 else y
                pc = 1 - cc if k & 1 else cc
                self.copies.append(pltpu.make_async_remote_copy(
                    src_ref=src.at[4 * px + 2 * py + pc], dst_ref=out.at[me],
                    send_sem=send_sems.at[7 * a + k - 1], recv_sem=recv_sems.at[7 * a + k - 1],
                    device_id=(px, py, pc), device_id_type=pl.DeviceIdType.MESH))

    def start(self):
        for cp in self.mine + self.copies:
            cp.start()

    def finish(self):
        for cp in self.copies + self.mine:
            cp.wait()


class _Gather:
    def __init__(self, x_refs, out_refs, send_sems, recv_sems, local_sems):
        x, y, cc = _mesh_pos()
        self.n = len(x_refs)
        self.me, self.sibling, self.cc = (x, y, cc), (x, y, 1 - cc), cc
        self.chips = [(1 - x, y), (x, 1 - y), (1 - x, 1 - y)]
        self.x_refs, self.out_refs, self.send_sems, self.recv_sems = x_refs, out_refs, send_sems, recv_sems
        self.mine = [pltpu.make_async_copy(x_refs[a], out_refs[a].at[4 * x + 2 * y + cc], local_sems.at[a])
                     for a in range(self.n)]

    def copy(self, a, k, block, to, own=False):
        slot = self.out_refs[a].at[4 * block[0] + 2 * block[1] + block[2]]
        return pltpu.make_async_remote_copy(
            src_ref=self.x_refs[a] if own else slot, dst_ref=slot,
            send_sem=self.send_sems.at[7 * a + k], recv_sem=self.recv_sems.at[7 * a + k],
            device_id=to, device_id_type=pl.DeviceIdType.MESH)

    def first_hop(self, a):
        return [self.copy(a, 0, self.me, self.sibling, own=True)] + [
            self.copy(a, 1 + j, self.me, (*chip, self.cc), own=True) for j, chip in enumerate(self.chips)]

    def passed_on(self, a):
        return [self.copy(a, 4 + j, (*chip, self.cc), self.sibling) for j, chip in enumerate(self.chips)]

    def start(self):
        for a in range(self.n):
            self.mine[a].start()
        for a in range(self.n):
            for cp in self.first_hop(a):
                cp.start()

    def forward(self):
        for a in range(self.n):
            for j, chip in enumerate(self.chips):
                self.copy(a, 1 + j, (*chip, self.cc), self.me).wait_recv()
                self.copy(a, 4 + j, (*chip, self.cc), self.sibling).start()

    def finish(self):
        for a in range(self.n):
            self.copy(a, 0, self.sibling, self.me).wait_recv()
            for j, chip in enumerate(self.chips):
                self.copy(a, 4 + j, (*chip, 1 - self.cc), self.me).wait_recv()
        for a in range(self.n):
            for cp in self.first_hop(a) + self.passed_on(a):
                cp.wait_send()
        for cp in self.mine:
            cp.wait()


def _carried(body, n_in, n_out, arrays, kind, when):
    n = len(arrays)
    if n == 0:
        return body, [], [], [], []

    def wrapped(*refs):
        ins, srcs = refs[:n_in], refs[n_in:n_in + n]
        outs, landed = refs[n_in + n:n_in + n + n_out], refs[n_in + n + n_out:n_in + 2 * n + n_out]
        scratch = refs[n_in + 2 * n + n_out:]
        comm = (_Scatter if kind == "scatter" else _Gather)(srcs, landed, *scratch[-3:])
        pl.when(when["start"]())(comm.start)
        if kind == "gather":
            pl.when(when["forward"]())(comm.forward)
        body(*ins, *outs, *scratch[:-3])
        pl.when(when["finish"]())(comm.finish)

    any_spec = pl.BlockSpec(memory_space=pl.ANY)
    if kind == "scatter":
        shapes = [jax.ShapeDtypeStruct(a.shape, a.dtype) for a in arrays]
    else:
        shapes = [jax.ShapeDtypeStruct((N_DEV,) + a.shape, a.dtype) for a in arrays]
    return wrapped, [any_spec] * n, [any_spec] * n, shapes, _comm_scratch(n)


def _all_gather(shards, name):
    n = len(shards)

    def body(*refs):
        comm = _Gather(refs[:n], refs[n:2 * n], *refs[2 * n:])
        comm.start()
        comm.forward()
        comm.finish()

    any_spec = pl.BlockSpec(memory_space=pl.ANY)
    return pl.pallas_call(
        body, name=name, in_specs=[any_spec] * n, out_specs=[any_spec] * n,
        out_shape=[jax.ShapeDtypeStruct((N_DEV,) + s.shape, s.dtype) for s in shards],
        scratch_shapes=_comm_scratch(n),
    )(*shards)


def _rms_matmul(x, g, w, b, name):
    s, d = x.shape
    n = w.shape[1]
    tm = min(512, s)

    def body(x_ref, g_ref, w_ref, b_ref, y_ref, h_ref):
        xv = x_ref[...]
        r = lax.rsqrt(jnp.mean(xv * xv, axis=-1, keepdims=True) + EPS)
        h = (xv * r * g_ref[...]).astype(BF16)
        h_ref[...] = h
        for c in range(0, n, 512):
            y_ref[:, c:c + 512] = _dot(h, w_ref[:, c:c + 512]) + b_ref[:, c:c + 512]

    return pl.pallas_call(
        body, name=name, grid=(s // tm,),
        in_specs=[pl.BlockSpec((tm, d), lambda i: (i, 0)), _full((1, d)), _full((d, n)), _full((1, n))],
        out_specs=[pl.BlockSpec((tm, n), lambda i: (i, 0)), pl.BlockSpec((tm, d), lambda i: (i, 0))],
        out_shape=[jax.ShapeDtypeStruct((s, n), F32), jax.ShapeDtypeStruct((s, d), BF16)],
        compiler_params=_params(("arbitrary",), 48),
    )(x, g, w, b)


def _matmul_nt(a, w, name):
    s, n = a.shape
    k = w.shape[0]
    tm = min(512, s)

    def body(a_ref, w_ref, o_ref):
        o_ref[...] = lax.dot_general(a_ref[...].astype(BF16), w_ref[...], NT_DIMS, preferred_element_type=F32)

    return pl.pallas_call(
        body, name=name, grid=(s // tm,),
        in_specs=[pl.BlockSpec((tm, n), lambda i: (i, 0)), _full((k, n))],
        out_specs=pl.BlockSpec((tm, k), lambda i: (i, 0)),
        out_shape=jax.ShapeDtypeStruct((s, k), F32),
        compiler_params=_params(("arbitrary",), 40),
    )(a, w)


def _matmul_tn(a, b, name):
    s, k = a.shape
    n = b.shape[1]
    ts = min(2048 if k <= 1024 else 1024, s)
    tn = 1024 if (n % 1024 == 0 and k <= 1024) else 512
    steps = s // ts

    def body(a_ref, b_ref, o_ref, acc):
        t = pl.program_id(1)

        @pl.when(t == 0)
        def _():
            acc[...] = jnp.zeros_like(acc)

        acc[...] += lax.dot_general(a_ref[...].astype(BF16), b_ref[...].astype(BF16), TN_DIMS,
                                    preferred_element_type=F32)

        @pl.when(t == steps - 1)
        def _():
            o_ref[...] = acc[...].astype(BF16)

    return pl.pallas_call(
        body, name=name, grid=(n // tn, steps),
        in_specs=[pl.BlockSpec((ts, k), lambda j, t: (t, 0)), pl.BlockSpec((ts, tn), lambda j, t: (t, j))],
        out_specs=pl.BlockSpec((k, tn), lambda j, t: (0, j)),
        out_shape=jax.ShapeDtypeStruct((k, n), BF16),
        scratch_shapes=[pltpu.VMEM((k, tn), F32)],
        compiler_params=_params(("arbitrary", "arbitrary"), 48),
    )(a, b)


def _nt_rms_bwd(dp, w, x, g, dres, name, send=()):
    s, n = dp.shape
    d = x.shape[1]
    tm = min(256, s)
    steps = s // tm

    def body(dp_ref, w_ref, x_ref, g_ref, dres_ref, dx_ref, dg_ref):
        @pl.when(pl.program_id(0) == 0)
        def _():
            dg_ref[...] = jnp.zeros_like(dg_ref)

        dh = lax.dot_general(dp_ref[...], w_ref[...], NT_DIMS, preferred_element_type=F32)
        xv = x_ref[...]
        r = lax.rsqrt(jnp.mean(xv * xv, axis=-1, keepdims=True) + EPS)
        xh = xv * r
        dg_ref[...] += jnp.sum(dh * xh, axis=0, keepdims=True)
        dn = dh * g_ref[...]
        dx_ref[...] = dres_ref[...] + r * (dn - xh * jnp.mean(dn * xh, axis=-1, keepdims=True))

    step_is = lambda i: lambda: pl.program_id(0) == i
    kern, ins, outs, shapes, sems = _carried(body, 5, 2, send, "scatter",
                                             {"start": step_is(0), "finish": step_is(steps - 1)})
    res = pl.pallas_call(
        kern, name=name, grid=(steps,),
        in_specs=[pl.BlockSpec((tm, n), lambda i: (i, 0)), _full((d, n)), pl.BlockSpec((tm, d), lambda i: (i, 0)),
                  _full((1, d)), pl.BlockSpec((tm, d), lambda i: (i, 0))] + ins,
        out_specs=[pl.BlockSpec((tm, d), lambda i: (i, 0)), _full((1, d))] + outs,
        out_shape=[jax.ShapeDtypeStruct((s, d), F32), jax.ShapeDtypeStruct((1, d), F32)] + shapes,
        scratch_shapes=sems,
        compiler_params=_params(("arbitrary",), 52),
    )(dp, w, x, g, dres, *send)
    return res[0], res[1], list(res[2:])


def _res_matmul(x, a, w, b, name):
    s, d = x.shape
    k = a.shape[1]
    tm = min(512, s)

    def body(x_ref, a_ref, w_ref, b_ref, o_ref):
        o_ref[...] = x_ref[...] + _dot(a_ref[...], w_ref[...]) + b_ref[...]

    return pl.pallas_call(
        body, name=name, grid=(s // tm,),
        in_specs=[pl.BlockSpec((tm, d), lambda i: (i, 0)), pl.BlockSpec((tm, k), lambda i: (i, 0)), _full((k, d)),
                  _full((1, d))],
        out_specs=pl.BlockSpec((tm, d), lambda i: (i, 0)),
        out_shape=jax.ShapeDtypeStruct((s, d), F32),
        compiler_params=_params(("arbitrary",), 32),
    )(x, a, w, b)


def _out_proj(x, o, gg, w, name):
    s, d = x.shape
    tm = min(512, s)

    def body(x_ref, o_ref, gg_ref, w_ref, y_ref, mix_ref):
        mix = jnp.concatenate([o_ref[...], gg_ref[...]], axis=1).astype(BF16)
        mix_ref[...] = mix
        y_ref[...] = x_ref[...] + _dot(mix, w_ref[...])

    return pl.pallas_call(
        body, name=name, grid=(s // tm,),
        in_specs=[pl.BlockSpec((tm, d), lambda i: (i, 0)), pl.BlockSpec((tm, SB_WIDTH), lambda i: (i, 0)),
                  pl.BlockSpec((tm, SG_WIDTH), lambda i: (i, 0)), _full((d, d))],
        out_specs=[pl.BlockSpec((tm, d), lambda i: (i, 0)), pl.BlockSpec((tm, d), lambda i: (i, 0))],
        out_shape=[jax.ShapeDtypeStruct((s, d), F32), jax.ShapeDtypeStruct((s, d), BF16)],
        compiler_params=_params(("arbitrary",), 32),
    )(x, o, gg, w)


def _qk_prep(proj, gq, gk, bd):
    s = proj.shape[0]
    tm = min(512, s)

    def body(q_ref, k_ref, v_ref, gq_ref, gk_ref, bd_ref, qn_ref, kn_ref, vb_ref):
        bdv = bd_ref[...]
        q = q_ref[...]
        k = k_ref[...]
        qn_ref[...] = (q * lax.rsqrt(_group_mean(q * q, bdv) + EPS) * gq_ref[...]).astype(BF16)
        kn_ref[...] = (k * lax.rsqrt(_group_mean(k * k, bdv) + EPS) * gk_ref[...]).astype(BF16)
        vb_ref[...] = v_ref[...].astype(BF16)

    col = lambda c: pl.BlockSpec((tm, SB_WIDTH), lambda i: (i, c))
    out = pl.BlockSpec((tm, SB_WIDTH), lambda i: (i, 0))
    return pl.pallas_call(
        body, name="qk_prep", grid=(s // tm,),
        in_specs=[col(0), col(1), col(2), _full((1, SB_WIDTH)), _full((1, SB_WIDTH)), _full((SB_WIDTH, SB_WIDTH))],
        out_specs=[out, out, out],
        out_shape=[jax.ShapeDtypeStruct((s, SB_WIDTH), BF16)] * 3,
        compiler_params=_params(("arbitrary",), 32),
    )(proj, proj, proj, gq, gk, bd)


def _qk_bwd(proj, dqn, dkn, dv, duz, gq, gk, bd):
    s = proj.shape[0]
    tm = min(512, s)

    def body(q_ref, k_ref, dq_ref, dk_ref, dv_ref, duz_ref, gq_ref, gk_ref, bd_ref, o_ref, dgq_ref, dgk_ref):
        @pl.when(pl.program_id(0) == 0)
        def _():
            dgq_ref[...] = jnp.zeros_like(dgq_ref)
            dgk_ref[...] = jnp.zeros_like(dgk_ref)

        bdv = bd_ref[...]

        def back(t, gain, dout, dg_ref):
            r = lax.rsqrt(_group_mean(t * t, bdv) + EPS)
            th = t * r
            dg_ref[...] += jnp.sum(dout * th, axis=0, keepdims=True)
            dn = dout * gain
            return r * (dn - th * _group_mean(dn * th, bdv))

        o_ref[:, 0:SB_WIDTH] = back(q_ref[...], gq_ref[...], dq_ref[...], dgq_ref).astype(BF16)
        o_ref[:, SB_WIDTH:2 * SB_WIDTH] = back(k_ref[...], gk_ref[...], dk_ref[...], dgk_ref).astype(BF16)
        o_ref[:, 2 * SB_WIDTH:3 * SB_WIDTH] = dv_ref[...].astype(BF16)
        o_ref[:, 3 * SB_WIDTH:IN_WIDTH] = duz_ref[...].astype(BF16)

    col = lambda c: pl.BlockSpec((tm, SB_WIDTH), lambda i: (i, c))
    row = pl.BlockSpec((tm, SB_WIDTH), lambda i: (i, 0))
    return pl.pallas_call(
        body, name="qk_bwd", grid=(s // tm,),
        in_specs=[col(0), col(1), row, row, row, pl.BlockSpec((tm, 2 * SG_WIDTH), lambda i: (i, 0)),
                  _full((1, SB_WIDTH)), _full((1, SB_WIDTH)), _full((SB_WIDTH, SB_WIDTH))],
        out_specs=[pl.BlockSpec((tm, IN_WIDTH), lambda i: (i, 0)), _full((1, SB_WIDTH)), _full((1, SB_WIDTH))],
        out_shape=[jax.ShapeDtypeStruct((s, IN_WIDTH), BF16), jax.ShapeDtypeStruct((1, SB_WIDTH), F32),
                   jax.ShapeDtypeStruct((1, SB_WIDTH), F32)],
        compiler_params=_params(("arbitrary",), 40),
    )(proj, proj, dqn, dkn, dv, duz, gq, gk, bd)


def _attn_masks(tq, tk):
    lane = lax.broadcasted_iota(jnp.int32, (tk, LANE), 1)
    heads = [(lane >= hh * HEAD_DIM) & (lane < (hh + 1) * HEAD_DIM) for hh in range(2)]
    urow = lax.broadcasted_iota(jnp.int32, (tk, tk), 0)
    ucol = lax.broadcasted_iota(jnp.int32, (tk, tk), 1)
    return heads, urow, ucol


def _keep_cost(z2):
    return jnp.log(1.0 + jnp.exp2(jnp.minimum(z2, 126.0)))


def _at_rows(part, row0, total):
    rows, width = part.shape
    pieces = [jnp.zeros((row0, width), part.dtype)] if row0 else []
    pieces.append(part)
    if total - row0 - rows:
        pieces.append(jnp.zeros((total - row0 - rows, width), part.dtype))
    return pieces[0] if len(pieces) == 1 else jnp.concatenate(pieces, axis=0)


def _diagonal_specs(i, r, tq, tk, left_to_right):
    if r == 2:
        specs = [(2 * i + 1, tk, tk, True), (2 * i, tk, tk, False), (2 * i, 0, tk, True)]
    else:
        specs = [(r * i + r - 1 - d, 0, tq, True) for d in range(r)]
    return specs[::-1] if left_to_right else specs


def _sb_attn_fwd(q2, kn, vb, gather=()):
    s = q2.shape[0]
    tk = min(ATT_BLOCK, s)
    tq = min(2 * ATT_BLOCK, s)
    r = tq // tk
    nq = s // tq
    assert s // tk <= LANE

    def body(q_ref, k_ref, v_ref, o_ref, ls_ref):
        i = pl.program_id(1)
        heads, urow, ucol = _attn_masks(tq, tk)
        u_incl = (urow >= ucol).astype(BF16)
        row = lax.broadcasted_iota(jnp.int32, (tq, tk), 0)
        col = lax.broadcasted_iota(jnp.int32, (tq, tk), 1)
        qlane = lax.broadcasted_iota(jnp.int32, (tq, LANE), 1)
        q = q_ref[...]

        def blocks(specs, state):
            carry, acc, ls = list(state[0:2]), state[2], list(state[3:5])
            chains = [(d, hh) for d in range(len(specs)) for hh in range(2)]
            kblk, vblk, valid = {}, {}, {}
            for d, (kb, row0, rows, masked) in enumerate(specs):
                off = pl.multiple_of(kb * tk, tk)
                kfull = k_ref[pl.ds(off, tk), :]
                vfull = v_ref[pl.ds(off, tk), :]
                if masked:
                    valid[d] = (kb * tk + col[:rows]) < (i * tq + row0 + row[:rows])
                for hh in range(2):
                    kblk[d, hh] = jnp.where(heads[hh], kfull, jnp.zeros((), BF16))
                    vblk[d, hh] = jnp.where(heads[hh], vfull, jnp.zeros((), BF16))
            z2 = {(d, hh): lax.dot_general(q[specs[d][1]:specs[d][1] + specs[d][2]], kblk[d, hh], NT_DIMS,
                                           preferred_element_type=F32) for d, hh in chains}
            cost = {}
            for d, hh in chains:
                cost[d, hh] = _keep_cost(z2[d, hh])
                if specs[d][3]:
                    cost[d, hh] = jnp.where(valid[d], cost[d, hh], 0.0)
            sums = {c: _dot(cost[c].astype(BF16), u_incl) for c in chains}
            a = {}
            for d, hh in chains:
                kb, row0, rows, masked = specs[d]
                rin = carry[hh][row0:row0 + rows] + sums[d, hh]
                a[d, hh] = jnp.exp2(z2[d, hh] - rin * LOG2E)
                if masked:
                    a[d, hh] = jnp.where(valid[d], a[d, hh], 0.0)
                rs = jnp.sum(_at_rows(cost[d, hh], row0, tq), axis=1, keepdims=True)
                ls[hh] = ls[hh] + jnp.where(qlane == kb, rs, 0.0)
                carry[hh] = carry[hh] + rs
            for d, hh in chains:
                acc = acc + _at_rows(_dot(a[d, hh].astype(BF16), vblk[d, hh]), specs[d][1], tq)
            return carry[0], carry[1], acc, ls[0], ls[1]

        zc = jnp.zeros((tq, 1), F32)
        zt = jnp.zeros((tq, LANE), F32)
        state = blocks(_diagonal_specs(i, r, tq, tk, False), (zc, zc, zt, zt, zt))
        pairs = lax.shift_right_logical(i, 1)
        state = lax.fori_loop(
            0, pairs,
            lambda n, st: blocks([(r * i - 1 - 2 * r * n - d, 0, tq, False) for d in range(2 * r)], st), state)
        state = lax.fori_loop(
            0, i - 2 * pairs, lambda n, st: blocks([(r - 1 - d, 0, tq, False) for d in range(r)], st), state)
        o_ref[...] = state[2]
        ls_ref[:, 0:LANE] = state[3]
        ls_ref[:, LANE:2 * LANE] = state[4]

    last_pair = SB_WIDTH // LANE - 1
    step_is = lambda p, i: lambda: (pl.program_id(0) == p) & (pl.program_id(1) == i)
    when = {"start": step_is(0, 0), "forward": step_is(last_pair, 0), "finish": step_is(last_pair, nq - 1)}
    kern, ins, outs, shapes, sems = _carried(body, 3, 2, gather, "gather", when)
    res = pl.pallas_call(
        kern, name="sb_attn_fwd", grid=(SB_WIDTH // LANE, nq),
        in_specs=[pl.BlockSpec((tq, LANE), lambda p, i: (i, p)), pl.BlockSpec((s, LANE), lambda p, i: (0, p)),
                  pl.BlockSpec((s, LANE), lambda p, i: (0, p))] + ins,
        out_specs=[pl.BlockSpec((tq, LANE), lambda p, i: (i, p)),
                   pl.BlockSpec((tq, 2 * LANE), lambda p, i: (i, p))] + outs,
        out_shape=[jax.ShapeDtypeStruct((s, SB_WIDTH), F32), jax.ShapeDtypeStruct((s, 2 * SB_WIDTH), F32)] + shapes,
        scratch_shapes=sems,
        compiler_params=_params(("arbitrary", "arbitrary"), 48),
    )(q2, kn, vb, *gather)
    return res[0], res[1], list(res[2:])


def _sb_attn_bwd(q2, kn, vb, lsum, dmix, send=()):
    s = q2.shape[0]
    tk = min(ATT_BLOCK, s)
    tq = min(2 * ATT_BLOCK, s)
    r = tq // tk
    nq = s // tq

    def body(q_ref, k_ref, v_ref, ls_ref, do_ref, dq_ref, dk_ref, dv_ref):
        i = pl.program_id(1)

        @pl.when(i == 0)
        def _():
            dk_ref[...] = jnp.zeros_like(dk_ref)
            dv_ref[...] = jnp.zeros_like(dv_ref)

        heads, urow, ucol = _attn_masks(tq, tk)
        u_incl = (urow >= ucol).astype(BF16)
        u_pre = (urow <= ucol).astype(BF16)
        lrow = lax.broadcasted_iota(jnp.int32, (LANE, LANE), 0)
        lcol = lax.broadcasted_iota(jnp.int32, (LANE, LANE), 1)
        u_after = (lrow > lcol).astype(BF16)
        row = lax.broadcasted_iota(jnp.int32, (tq, tk), 0)
        col = lax.broadcasted_iota(jnp.int32, (tq, tk), 1)
        qlane = lax.broadcasted_iota(jnp.int32, (tq, LANE), 1)
        qheads = [(qlane >= hh * HEAD_DIM) & (qlane < (hh + 1) * HEAD_DIM) for hh in range(2)]
        q = q_ref[...]
        dob = do_ref[...].astype(BF16)
        qm = [jnp.where(qheads[hh], q, jnp.zeros((), BF16)) for hh in range(2)]
        dom = [jnp.where(qheads[hh], dob, jnp.zeros((), BF16)) for hh in range(2)]
        after = []
        for hh in range(2):
            ls = ls_ref[:, hh * LANE:(hh + 1) * LANE]
            hi = ls.astype(BF16)
            mid = (ls - hi.astype(F32)).astype(BF16)
            lo = (ls - hi.astype(F32) - mid.astype(F32)).astype(BF16)
            after.append(_dot(hi, u_after) + _dot(mid, u_after) + _dot(lo, u_after))

        def blocks(specs, state):
            cp, dq = list(state[0:2]), state[2]
            chains = [(d, hh) for d in range(len(specs)) for hh in range(2)]
            rows_of = lambda d: slice(specs[d][1], specs[d][1] + specs[d][2])
            offs, kblk, vblk, valid = {}, {}, {}, {}
            for d, (kb, row0, rows, masked) in enumerate(specs):
                offs[d] = pl.multiple_of(kb * tk, tk)
                kfull = k_ref[pl.ds(offs[d], tk), :]
                vblk[d] = v_ref[pl.ds(offs[d], tk), :]
                if masked:
                    valid[d] = (kb * tk + col[:rows]) < (i * tq + row0 + row[:rows])
                for hh in range(2):
                    kblk[d, hh] = jnp.where(heads[hh], kfull, jnp.zeros((), BF16))
            z2 = {(d, hh): lax.dot_general(q[rows_of(d)], kblk[d, hh], NT_DIMS, preferred_element_type=F32)
                  for d, hh in chains}
            da = {(d, hh): lax.dot_general(dom[hh][rows_of(d)], vblk[d], NT_DIMS, preferred_element_type=F32)
                  for d, hh in chains}
            cost, sig = {}, {}
            for d, hh in chains:
                cost[d, hh] = _keep_cost(z2[d, hh])
                sig[d, hh] = jnp.exp2(z2[d, hh] - cost[d, hh] * LOG2E)
                if specs[d][3]:
                    cost[d, hh] = jnp.where(valid[d], cost[d, hh], 0.0)
            sums = {c: _dot(cost[c].astype(BF16), u_incl) for c in chains}
            a, g = {}, {}
            for d, hh in chains:
                kb, row0, rows, masked = specs[d]
                cr = jnp.sum(jnp.where(qlane[:rows] == kb, after[hh][rows_of(d)], 0.0), axis=1, keepdims=True)
                a[d, hh] = jnp.exp2(z2[d, hh] - (cr + sums[d, hh]) * LOG2E)
                if masked:
                    a[d, hh] = jnp.where(valid[d], a[d, hh], 0.0)
                g[d, hh] = da[d, hh] * a[d, hh]
            pre = {c: _dot(g[c].astype(BF16), u_pre) for c in chains}
            dzb = {}
            for d, hh in chains:
                dz = g[d, hh] - sig[d, hh] * (cp[hh][rows_of(d)] + pre[d, hh])
                if specs[d][3]:
                    dz = jnp.where(valid[d], dz, 0.0)
                dzb[d, hh] = dz.astype(BF16)
                cp[hh] = cp[hh] + _at_rows(jnp.sum(g[d, hh], axis=1, keepdims=True), specs[d][1], tq)
            for d in range(len(specs)):
                dv_ref[pl.ds(offs[d], tk), :] += sum(
                    lax.dot_general(a[d, hh].astype(BF16), dom[hh][rows_of(d)], TN_DIMS, preferred_element_type=F32)
                    for hh in range(2))
                dk_ref[pl.ds(offs[d], tk), :] += sum(
                    lax.dot_general(dzb[d, hh], qm[hh][rows_of(d)], TN_DIMS, preferred_element_type=F32)
                    for hh in range(2))
            for d, hh in chains:
                dq = dq + _at_rows(_dot(dzb[d, hh], kblk[d, hh]), specs[d][1], tq)
            return cp[0], cp[1], dq

        zc = jnp.zeros((tq, 1), F32)
        pairs = lax.shift_right_logical(i, 1)
        state = lax.fori_loop(
            0, pairs, lambda n, st: blocks([(2 * r * n + d, 0, tq, False) for d in range(2 * r)], st),
            (zc, zc, jnp.zeros((tq, LANE), F32)))
        state = lax.fori_loop(
            0, i - 2 * pairs, lambda n, st: blocks([(2 * r * pairs + d, 0, tq, False) for d in range(r)], st), state)
        state = blocks(_diagonal_specs(i, r, tq, tk, True), state)
        dq_ref[...] = state[2]

    blk = pl.BlockSpec((tq, LANE), lambda p, i: (i, p))
    whole = pl.BlockSpec((s, LANE), lambda p, i: (0, p))
    last_pair = SB_WIDTH // LANE - 1
    step_is = lambda p, i: lambda: (pl.program_id(0) == p) & (pl.program_id(1) == i)
    kern, ins, outs, shapes, sems = _carried(body, 5, 3, send, "scatter",
                                             {"start": step_is(0, 0), "finish": step_is(last_pair, nq - 1)})
    res = pl.pallas_call(
        kern, name="sb_attn_bwd", grid=(SB_WIDTH // LANE, nq),
        in_specs=[blk, whole, whole, pl.BlockSpec((tq, 2 * LANE), lambda p, i: (i, p)), blk] + ins,
        out_specs=[blk, whole, whole] + outs,
        out_shape=[jax.ShapeDtypeStruct((s, SB_WIDTH), F32)] * 3 + shapes,
        scratch_shapes=sems,
        compiler_params=_params(("arbitrary", "arbitrary"), 56),
    )(q2, kn, vb, lsum, dmix, *send)
    return res[0], res[1], res[2], list(res[3:])


def _sgu_spatial(zn, wm_ref, lane, c):
    parts = []
    for p in range(SG_WIDTH // LANE):
        blk = zn[c * CHUNK:(c + 1) * CHUNK, p * LANE:(p + 1) * LANE].astype(BF16)
        lo = jnp.where(lane < HEAD_DIM, blk, jnp.zeros((), BF16))
        hi = jnp.where(lane >= HEAD_DIM, blk, jnp.zeros((), BF16))
        parts.append(_dot(wm_ref[2 * p], lo) + _dot(wm_ref[2 * p + 1], hi))
    return jnp.concatenate(parts, axis=1)


def _sgu_fwd(proj, gz, wm, bt, bd):
    s = proj.shape[0]
    tm = min(512, s)

    def body(u_ref, z_ref, gz_ref, wm_ref, bt_ref, bd_ref, o_ref):
        lane = lax.broadcasted_iota(jnp.int32, (CHUNK, LANE), 1)
        ug = _gelu(u_ref[...])
        zg = _gelu(z_ref[...])
        zn = zg * lax.rsqrt(_group_mean(zg * zg, bd_ref[...]) + EPS) * gz_ref[...]
        for c in range(tm // CHUNK):
            sp = _sgu_spatial(zn, wm_ref, lane, c) + bt_ref[...]
            o_ref[c * CHUNK:(c + 1) * CHUNK, :] = ug[c * CHUNK:(c + 1) * CHUNK, :] * sp

    col = lambda c: pl.BlockSpec((tm, SG_WIDTH), lambda i: (i, c))
    return pl.pallas_call(
        body, name="sgu_fwd", grid=(s // tm,),
        in_specs=[col(3), col(4), _full((1, SG_WIDTH)), _full((SG_GROUPS, CHUNK, CHUNK)), _full((CHUNK, SG_WIDTH)),
                  _full((SG_WIDTH, SG_WIDTH))],
        out_specs=pl.BlockSpec((tm, SG_WIDTH), lambda i: (i, 0)),
        out_shape=jax.ShapeDtypeStruct((s, SG_WIDTH), F32),
        compiler_params=_params(("arbitrary",), 32),
    )(proj, proj, gz, wm, bt, bd)


def _sgu_bwd(proj, dmix, gz, wm, wmt, bt, bd):
    s = proj.shape[0]
    tm = min(512, s)

    def body(u_ref, z_ref, dg_ref, gz_ref, wm_ref, wmt_ref, bt_ref, bd_ref, o_ref, dwm_ref, dbt_ref, dgz_ref):
        @pl.when(pl.program_id(0) == 0)
        def _():
            dwm_ref[...] = jnp.zeros_like(dwm_ref)
            dbt_ref[...] = jnp.zeros_like(dbt_ref)
            dgz_ref[...] = jnp.zeros_like(dgz_ref)

        lane = lax.broadcasted_iota(jnp.int32, (CHUNK, LANE), 1)
        bdv = bd_ref[...]
        u = u_ref[...]
        z = z_ref[...]
        ug = _gelu(u)
        zg = _gelu(z)
        r = lax.rsqrt(_group_mean(zg * zg, bdv) + EPS)
        zh = zg * r
        zn = zh * gz_ref[...]
        dzn_rows = []
        for c in range(tm // CHUNK):
            rows = slice(c * CHUNK, (c + 1) * CHUNK)
            sp = _sgu_spatial(zn, wm_ref, lane, c) + bt_ref[...]
            dgg = dg_ref[rows, :]
            ds = dgg * ug[rows, :]
            o_ref[rows, 0:SG_WIDTH] = dgg * sp * _gelu_grad(u[rows, :])
            dbt_ref[...] += ds
            parts = []
            for p in range(SG_WIDTH // LANE):
                dsb = ds[:, p * LANE:(p + 1) * LANE].astype(BF16)
                znb = zn[rows, p * LANE:(p + 1) * LANE].astype(BF16)
                acc = jnp.zeros((CHUNK, LANE), F32)
                for hh in range(2):
                    hm = (lane >= hh * HEAD_DIM) & (lane < (hh + 1) * HEAD_DIM)
                    dsm = jnp.where(hm, dsb, jnp.zeros((), BF16))
                    znm = jnp.where(hm, znb, jnp.zeros((), BF16))
                    acc = acc + _dot(wmt_ref[2 * p + hh], dsm)
                    dwm_ref[2 * p + hh] += lax.dot_general(dsm, znm, NT_DIMS, preferred_element_type=F32)
                parts.append(acc)
            dzn_rows.append(jnp.concatenate(parts, axis=1))
        dzn = jnp.concatenate(dzn_rows, axis=0)
        dgz_ref[...] += jnp.sum(dzn * zh, axis=0, keepdims=True)
        dn = dzn * gz_ref[...]
        o_ref[:, SG_WIDTH:2 * SG_WIDTH] = r * (dn - zh * _group_mean(dn * zh, bdv)) * _gelu_grad(z)

    col = lambda c: pl.BlockSpec((tm, SG_WIDTH), lambda i: (i, c))
    wspec = _full((SG_GROUPS, CHUNK, CHUNK))
    return pl.pallas_call(
        body, name="sgu_bwd", grid=(s // tm,),
        in_specs=[col(3), col(4), pl.BlockSpec((tm, SG_WIDTH), lambda i: (i, 1)), _full((1, SG_WIDTH)), wspec, wspec,
                  _full((CHUNK, SG_WIDTH)), _full((SG_WIDTH, SG_WIDTH))],
        out_specs=[pl.BlockSpec((tm, 2 * SG_WIDTH), lambda i: (i, 0)), wspec, _full((CHUNK, SG_WIDTH)),
                   _full((1, SG_WIDTH))],
        out_shape=[jax.ShapeDtypeStruct((s, 2 * SG_WIDTH), F32), jax.ShapeDtypeStruct((SG_GROUPS, CHUNK, CHUNK), F32),
                   jax.ShapeDtypeStruct((CHUNK, SG_WIDTH), F32), jax.ShapeDtypeStruct((1, SG_WIDTH), F32)],
        compiler_params=_params(("arbitrary",), 40),
    )(proj, proj, dmix, gz, wm, wmt, bt, bd)


def _shifted_copies(ext, sh):
    e = ext[...]
    sh[0] = e
    for b in range(1, 8):
        sh[b] = pltpu.roll(e, e.shape[0] - b, 0)


def _shifted_rows(sh, off, r0, cols):
    start = r0 + off - off % 8
    return sh[off % 8, start:start + CONV_ROWS, cols]


def _dwconv(sh, w_ref, offsets, rows, store):
    for r0 in range(0, rows, CONV_ROWS):
        for c0 in range(0, sh.shape[2], CONV_LANES):
            cols = slice(c0, c0 + CONV_LANES)
            acc = jnp.zeros((CONV_ROWS, CONV_LANES), F32)
            for j, off in enumerate(offsets):
                acc = acc + w_ref[j:j + 1, cols] * _shifted_rows(sh, off, r0, cols)
            store(slice(r0, r0 + CONV_ROWS), cols, acc)


def _conf_mid_fwd(p, wdw, bdw, lng, lnb):
    s = p.shape[0]
    c = p.shape[1] // 2
    tm = min(256, s)

    def body(a_ref, gt_ref, w_ref, b_ref, g_ref, beta_ref, yc_ref, y2_ref, ext, sh):
        i = pl.program_id(0)

        @pl.when(i == 0)
        def _():
            ext[0:HALO, :] = jnp.zeros((HALO, c), F32)

        @pl.when(i > 0)
        def _():
            ext[0:HALO, :] = ext[tm:tm + HALO, :]

        ext[HALO:HALO + tm, :] = a_ref[...] * _sigmoid(gt_ref[...])

        def store(rows, cols, block):
            yc_ref[rows, cols] = block + b_ref[:, cols]

        _shifted_copies(ext, sh)
        _dwconv(sh, w_ref, [HALO - CONV_K + 1 + j for j in range(CONV_K)], tm, store)
        acc = yc_ref[...]
        xc = acc - jnp.mean(acc, axis=-1, keepdims=True)
        ln = xc * lax.rsqrt(jnp.mean(xc * xc, axis=-1, keepdims=True) + EPS) * g_ref[...] + beta_ref[...]
        y2_ref[...] = (ln * _sigmoid(ln)).astype(BF16)

    vec = _full((1, c))
    return pl.pallas_call(
        body, name="conf_mid_fwd", grid=(s // tm,),
        in_specs=[pl.BlockSpec((tm, c), lambda i: (i, 0)), pl.BlockSpec((tm, c), lambda i: (i, 1)), _full((HALO, c)), vec,
                  vec, vec],
        out_specs=[pl.BlockSpec((tm, c), lambda i: (i, 0)), pl.BlockSpec((tm, c), lambda i: (i, 0))],
        out_shape=[jax.ShapeDtypeStruct((s, c), F32), jax.ShapeDtypeStruct((s, c), BF16)],
        scratch_shapes=[pltpu.VMEM((HALO + tm, c), F32), pltpu.VMEM((8, HALO + tm, c), F32)],
        compiler_params=_params(("arbitrary",), 40),
    )(p, p, wdw, bdw, lng, lnb)


def _conf_mid_bwd(p, yc, dout, x, gm, wdw, lng, lnb, w1, w2, send=()):
    s = p.shape[0]
    c = p.shape[1] // 2
    tm = min(256, s)
    n = s // tm
    hb = tm // HALO

    def body(a_ref, gt_ref, ah_ref, gh_ref, yc_ref, dout_ref, x_ref, gm_ref, w_ref, g_ref, beta_ref, w1_hbm, w2_hbm,
             dp_ref, dw_ref, dbdw_ref, dlg_ref, dlb_ref, db1_ref, db2_ref, dx_ref, dgm_ref,
             exty, extd, dyv, dwacc, shy, shd, w1_ref, w2_ref, wsem):
        i = pl.program_id(0)

        @pl.when(i == 0)
        def _():
            loads = [pltpu.make_async_copy(w1_hbm, w1_ref, wsem.at[0]),
                     pltpu.make_async_copy(w2_hbm, w2_ref, wsem.at[1])]
            for cp in loads:
                cp.start()
            extd[tm:tm + HALO, :] = jnp.zeros((HALO, c), F32)
            for ref in (dw_ref, dbdw_ref, dlg_ref, dlb_ref, db1_ref, db2_ref, dgm_ref, dwacc):
                ref[...] = jnp.zeros_like(ref)
            for cp in loads:
                cp.wait()

        @pl.when(i > 0)
        def _():
            extd[tm:tm + HALO, :] = extd[0:HALO, :]

        dout = dout_ref[...]
        dy2 = lax.dot_general(dout.astype(BF16), w2_ref[...], NT_DIMS, preferred_element_type=F32)
        a = a_ref[...]
        sg = _sigmoid(gt_ref[...])
        exty[HALO:HALO + tm, :] = a * sg
        exty[0:HALO, :] = jnp.where(i < n - 1, ah_ref[...] * _sigmoid(gh_ref[...]), 0.0)
        ycv = yc_ref[...]
        xc = ycv - jnp.mean(ycv, axis=-1, keepdims=True)
        rstd = lax.rsqrt(jnp.mean(xc * xc, axis=-1, keepdims=True) + EPS)
        xh = xc * rstd
        ln = xh * g_ref[...] + beta_ref[...]
        dln = dy2 * _silu_grad(ln, _sigmoid(ln))
        dlg_ref[...] += jnp.sum(dln * xh, axis=0, keepdims=True)
        dlb_ref[...] += jnp.sum(dln, axis=0, keepdims=True)
        dxh = dln * g_ref[...]
        dyc = rstd * (dxh - jnp.mean(dxh, axis=-1, keepdims=True) - xh * jnp.mean(dxh * xh, axis=-1, keepdims=True))
        extd[0:tm, :] = dyc
        dbdw_ref[...] += jnp.sum(dyc, axis=0, keepdims=True)
        db2_ref[...] += jnp.sum(dout, axis=0, keepdims=True)

        def store(rows, cols, block):
            dyv[rows, cols] = block

        _shifted_copies(extd, shd)
        _shifted_copies(exty, shy)
        _dwconv(shd, w_ref, [CONV_K - 1 - j for j in range(CONV_K)], tm, store)
        for r0 in range(0, tm, CONV_ROWS):
            for c0 in range(0, c, CONV_LANES):
                cols = slice(c0, c0 + CONV_LANES)
                dsub = extd[r0:r0 + CONV_ROWS, cols]
                for j in range(CONV_K):
                    prod = dsub * _shifted_rows(shy, HALO - CONV_K + 1 + j, r0, cols)
                    dwacc[8 * j:8 * j + 8, cols] += prod.reshape(CONV_ROWS // 8, 8, CONV_LANES).sum(axis=0)

        @pl.when(i == n - 1)
        def _():
            dw_ref[...] = dwacc[...].reshape(HALO, 8, c).sum(axis=1)

        dy = dyv[...]
        da = dy * sg
        dgt = dy * a * sg * (1.0 - sg)
        dab, dgtb = da.astype(BF16), dgt.astype(BF16)
        dp_ref[:, 0:c] = dab
        dp_ref[:, c:2 * c] = dgtb
        db1_ref[:, 0:c] += jnp.sum(da, axis=0, keepdims=True)
        db1_ref[:, c:2 * c] += jnp.sum(dgt, axis=0, keepdims=True)
        dh = (lax.dot_general(dab, w1_ref[:, 0:c], NT_DIMS, preferred_element_type=F32)
              + lax.dot_general(dgtb, w1_ref[:, c:2 * c], NT_DIMS, preferred_element_type=F32))
        xv = x_ref[...]
        r = lax.rsqrt(jnp.mean(xv * xv, axis=-1, keepdims=True) + EPS)
        xn = xv * r
        dgm_ref[...] += jnp.sum(dh * xn, axis=0, keepdims=True)
        dn = dh * gm_ref[...]
        dx_ref[...] = dout + r * (dn - xn * jnp.mean(dn * xn, axis=-1, keepdims=True))

    rev = lambda col: pl.BlockSpec((tm, c), lambda i: (n - 1 - i, col))
    halo = lambda col: pl.BlockSpec((HALO, c), lambda i: (jnp.maximum((n - 1 - i) * hb - 1, 0), col))
    vec = _full((1, c))
    step_is = lambda i: lambda: pl.program_id(0) == i
    kern, ins, outs, shapes, sems = _carried(body, 13, 9, send, "scatter",
                                             {"start": step_is(0), "finish": step_is(n - 1)})
    res = pl.pallas_call(
        kern, name="conf_mid_bwd", grid=(n,),
        in_specs=[rev(0), rev(1), halo(0), halo(1), rev(0), rev(0), rev(0), vec, _full((HALO, c)), vec, vec,
                  pl.BlockSpec(memory_space=pl.ANY), pl.BlockSpec(memory_space=pl.ANY)] + ins,
        out_specs=[pl.BlockSpec((tm, 2 * c), lambda i: (n - 1 - i, 0)), _full((HALO, c)), vec, vec, vec,
                   _full((1, 2 * c)), vec, rev(0), vec] + outs,
        out_shape=[jax.ShapeDtypeStruct((s, 2 * c), BF16), jax.ShapeDtypeStruct((HALO, c), F32),
                   jax.ShapeDtypeStruct((1, c), F32), jax.ShapeDtypeStruct((1, c), F32), jax.ShapeDtypeStruct((1, c), F32),
                   jax.ShapeDtypeStruct((1, 2 * c), F32), jax.ShapeDtypeStruct((1, c), F32),
                   jax.ShapeDtypeStruct((s, c), F32), jax.ShapeDtypeStruct((1, c), F32)] + shapes,
        scratch_shapes=[pltpu.VMEM((HALO + tm, c), F32), pltpu.VMEM((tm + HALO, c), F32), pltpu.VMEM((tm, c), F32),
                        pltpu.VMEM((8 * HALO, c), F32), pltpu.VMEM((8, HALO + tm, c), F32),
                        pltpu.VMEM((8, tm + HALO, c), F32), pltpu.VMEM((c, 2 * c), BF16), pltpu.VMEM((c, c), BF16),
                        pltpu.SemaphoreType.DMA((2,))] + sems,
        compiler_params=_params(("arbitrary",), 56),
    )(p, p, p, p, yc, dout, x, gm, wdw, lng, lnb, w1, w2, *send)
    return tuple(res[:9]) + (list(res[9:]),)


def _ffn_fwd(x, g, wup, wdw, bdw, wdn):
    s, d = x.shape
    ff = wdn.shape[0]
    tm = min(256, s)

    def body(x_ref, g_ref, wup_ref, wdw_ref, bdw_ref, wdn_ref, y_ref, h_ref, u_ref, carry):
        @pl.when(pl.program_id(0) == 0)
        def _():
            carry[...] = jnp.zeros_like(carry)

        xv = x_ref[...]
        r = lax.rsqrt(jnp.mean(xv * xv, axis=-1, keepdims=True) + EPS)
        h = (xv * r * g_ref[...]).astype(BF16)
        h_ref[...] = h
        acc = xv
        up = lambda c: (_dot(h, wup_ref[:, c:c + FF_CHUNK]), _dot(h, wup_ref[:, ff + c:ff + c + FF_CHUNK]))
        ahead = up(0)
        for c in range(0, ff, FF_CHUNK):
            cs = slice(c, c + FF_CHUNK)
            gp, val = ahead
            if c + FF_CHUNK < ff:
                ahead = up(c + FF_CHUNK)
            u_ref[:, cs] = gp.astype(BF16)
            u_ref[:, ff + c:ff + c + FF_CHUNK] = val.astype(BF16)
            prev = carry[:, cs]
            gate = (wdw_ref[0:1, cs] * _shift_down(gp, prev, 2) + wdw_ref[1:2, cs] * _shift_down(gp, prev, 1)
                    + wdw_ref[2:3, cs] * gp + bdw_ref[:, cs])
            act = gate * _sigmoid(gate) * val
            acc = acc + _dot(act.astype(BF16), wdn_ref[cs, :])
            carry[:, cs] = gp[tm - 8:tm, :]
        y_ref[...] = acc

    return pl.pallas_call(
        body, name="ffn_fwd", grid=(s // tm,),
        in_specs=[pl.BlockSpec((tm, d), lambda i: (i, 0)), _full((1, d)), _full((d, 2 * ff)), _full((8, ff)),
                  _full((1, ff)), _full((ff, d))],
        out_specs=[pl.BlockSpec((tm, d), lambda i: (i, 0)), pl.BlockSpec((tm, d), lambda i: (i, 0)),
                   pl.BlockSpec((tm, 2 * ff), lambda i: (i, 0))],
        out_shape=[jax.ShapeDtypeStruct((s, d), F32), jax.ShapeDtypeStruct((s, d), BF16),
                   jax.ShapeDtypeStruct((s, 2 * ff), BF16)],
        scratch_shapes=[pltpu.VMEM((8, ff), F32)],
        compiler_params=_params(("arbitrary",), 56),
    )(x, g, wup, wdw, bdw, wdn)


def _ffn_bwd(dy, u, x, g, wdw, bdw, wdn, wup, send=()):
    s, d = dy.shape
    ff = wdn.shape[0]
    tm = min(256, s)
    n = s // tm
    hb = tm // 16

    def body(dy_ref, u_ref, uh_ref, x_ref, g_ref, wdw_ref, bdw_ref, wdn_hbm, wup_hbm,
             du_ref, act_ref, dw_ref, db_ref, dx_ref, dg_ref, carry, wdn_ref, wup_ref, wsem):
        i = pl.program_id(0)

        @pl.when(i == 0)
        def _():
            loads = [pltpu.make_async_copy(wdn_hbm, wdn_ref, wsem.at[0]),
                     pltpu.make_async_copy(wup_hbm, wup_ref, wsem.at[1])]
            for cp in loads:
                cp.start()
            carry[...] = jnp.zeros_like(carry)
            dw_ref[...] = jnp.zeros_like(dw_ref)
            db_ref[...] = jnp.zeros_like(db_ref)
            dg_ref[...] = jnp.zeros_like(dg_ref)
            for cp in loads:
                cp.wait()

        dyv = dy_ref[...]
        dyb = dyv.astype(BF16)
        down = lambda c: lax.dot_general(dyb, wdn_ref[c:c + FF_CHUNK, :], NT_DIMS, preferred_element_type=F32)
        ahead = down(0)
        dh = jnp.zeros((tm, d), F32)
        for c in range(0, ff, FF_CHUNK):
            cs = slice(c, c + FF_CHUNK)
            vs = slice(ff + c, ff + c + FF_CHUNK)
            da = ahead
            if c + FF_CHUNK < ff:
                ahead = down(c + FF_CHUNK)
            gp = u_ref[:, cs].astype(F32)
            val = u_ref[:, vs].astype(F32)
            prev = jnp.where(i < n - 1, uh_ref[:, cs].astype(F32)[8:16], 0.0)
            g1 = _shift_down(gp, prev, 1)
            g2 = _shift_down(gp, prev, 2)
            gate = wdw_ref[0:1, cs] * g2 + wdw_ref[1:2, cs] * g1 + wdw_ref[2:3, cs] * gp + bdw_ref[:, cs]
            sg = _sigmoid(gate)
            si = gate * sg
            act_ref[:, cs] = (si * val).astype(BF16)
            dgate = da * val * _silu_grad(gate, sg)
            nxt = carry[:, cs]
            dgp = (wdw_ref[2:3, cs] * dgate + wdw_ref[1:2, cs] * _shift_up(dgate, nxt, 1)
                   + wdw_ref[0:1, cs] * _shift_up(dgate, nxt, 2)).astype(BF16)
            dval = (da * si).astype(BF16)
            du_ref[:, cs] = dgp
            du_ref[:, vs] = dval
            dh = (dh + lax.dot_general(dgp, wup_ref[:, cs], NT_DIMS, preferred_element_type=F32)
                  + lax.dot_general(dval, wup_ref[:, vs], NT_DIMS, preferred_element_type=F32))
            dw_ref[:, cs] += jnp.concatenate(
                [jnp.sum(dgate * g2, axis=0, keepdims=True), jnp.sum(dgate * g1, axis=0, keepdims=True),
                 jnp.sum(dgate * gp, axis=0, keepdims=True), jnp.zeros((5, FF_CHUNK), F32)], axis=0)
            db_ref[:, cs] += jnp.sum(dgate, axis=0, keepdims=True)
            carry[:, cs] = dgate[0:8, :]
        xv = x_ref[...]
        r = lax.rsqrt(jnp.mean(xv * xv, axis=-1, keepdims=True) + EPS)
        xh = xv * r
        dg_ref[...] += jnp.sum(dh * xh, axis=0, keepdims=True)
        dn = dh * g_ref[...]
        dx_ref[...] = dyv + r * (dn - xh * jnp.mean(dn * xh, axis=-1, keepdims=True))

    step_is = lambda i: lambda: pl.program_id(0) == i
    kern, ins, outs, shapes, sems = _carried(body, 9, 6, send, "scatter",
                                             {"start": step_is(0), "finish": step_is(n - 1)})
    rev = lambda cols: pl.BlockSpec((tm, cols), lambda i: (n - 1 - i, 0))
    any_spec = pl.BlockSpec(memory_space=pl.ANY)
    res = pl.pallas_call(
        kern, name="ffn_bwd", grid=(n,),
        in_specs=[rev(d), rev(2 * ff),
                  pl.BlockSpec((16, 2 * ff), lambda i: (jnp.maximum((n - 1 - i) * hb - 1, 0), 0)),
                  rev(d), _full((1, d)), _full((8, ff)), _full((1, ff)), any_spec, any_spec] + ins,
        out_specs=[rev(2 * ff), rev(ff), _full((8, ff)), _full((1, ff)), rev(d), _full((1, d))] + outs,
        out_shape=[jax.ShapeDtypeStruct((s, 2 * ff), BF16), jax.ShapeDtypeStruct((s, ff), BF16),
                   jax.ShapeDtypeStruct((8, ff), F32), jax.ShapeDtypeStruct((1, ff), F32),
                   jax.ShapeDtypeStruct((s, d), F32), jax.ShapeDtypeStruct((1, d), F32)] + shapes,
        scratch_shapes=[pltpu.VMEM((8, ff), F32), pltpu.VMEM((ff, d), BF16), pltpu.VMEM((d, 2 * ff), BF16),
                        pltpu.SemaphoreType.DMA((2,))] + sems,
        compiler_params=_params(("arbitrary",), 56),
    )(dy, u, u, x, g, wdw, bdw, wdn, wup, *send)
    return tuple(res[:6]) + (list(res[6:]),)


def _loss_head(y, target):
    s, d = y.shape
    tm = min(512, s)

    def body(y_ref, t_ref, l_ref, dy_ref):
        @pl.when(pl.program_id(0) == 0)
        def _():
            l_ref[...] = jnp.zeros_like(l_ref)

        err = y_ref[...] - t_ref[...]
        dy_ref[...] = err * (1.0 / d)
        l_ref[...] += 0.5 * jnp.sum(jnp.mean(err * err, axis=-1, keepdims=True), axis=0, keepdims=True)

    return pl.pallas_call(
        body, name="loss_head", grid=(s // tm,),
        in_specs=[pl.BlockSpec((tm, d), lambda i: (i, 0)), pl.BlockSpec((tm, d), lambda i: (i, 0))],
        out_specs=[_full((8, LANE)), pl.BlockSpec((tm, d), lambda i: (i, 0))],
        out_shape=[jax.ShapeDtypeStruct((8, LANE), F32), jax.ShapeDtypeStruct((s, d), F32)],
        compiler_params=_params(("arbitrary",), 32),
    )(y, target)


def _row_tile(rows, limit=512):
    for cand in range(min(limit, rows) // 16 * 16, 0, -16):
        if rows % cand == 0:
            return cand
    return rows


def _reduce_adamw(parts, w, m, v, name):
    nl = len(parts)
    _, a, b = parts[0].shape
    ta = _row_tile(a, 256)

    def body(*refs):
        p_refs = refs[:nl]
        w_ref, m_ref, v_ref, g_ref, d_ref, mo_ref, vo_ref = refs[nl:]
        for layer in range(nl):
            @pl.when(pl.program_id(0) == layer)
            def _(p_ref=p_refs[layer]):
                g = p_ref[0].astype(F32)
                for k in range(1, N_DEV):
                    g = g + p_ref[k].astype(F32)
                g_ref[0] = g

        g = g_ref[0]
        mn = ADAM_B1 * m_ref[...] + (1.0 - ADAM_B1) * g
        vn = ADAM_B2 * v_ref[...] + (1.0 - ADAM_B2) * (g * g)
        mo_ref[...] = mn
        vo_ref[...] = vn
        m_hat = mn / (1.0 - ADAM_B1 ** ADAM_STEP)
        v_hat = vn / (1.0 - ADAM_B2 ** ADAM_STEP)
        d_ref[...] = -ADAM_LR * (m_hat / (jnp.sqrt(v_hat) + ADAM_EPS) + ADAM_WD * w_ref[...])

    blk = pl.BlockSpec((1, ta, b), lambda l, i: (l, i, 0))
    part = lambda layer: pl.BlockSpec((N_DEV, ta, b), lambda l, i: (0, jnp.where(l == layer, i, 0), 0))
    return pl.pallas_call(
        body, name=name, grid=(nl, a // ta),
        in_specs=[part(layer) for layer in range(nl)] + [blk, blk, blk],
        out_specs=[blk, blk, blk, blk],
        out_shape=[jax.ShapeDtypeStruct((nl, a, b), F32)] * 4,
        compiler_params=_params(("arbitrary", "arbitrary"), 56),
    )(*parts, w, m, v)


def _unshard(w8, layer, name):
    _, _, k, n = w8.shape
    tk = _row_tile(k, 256)

    def body(w_ref, o_ref):
        for d in range(N_DEV):
            o_ref[:, d * n:(d + 1) * n] = w_ref[d, 0]

    return pl.pallas_call(
        body, name=name, grid=(k // tk,),
        in_specs=[pl.BlockSpec((N_DEV, 1, tk, n), lambda i: (0, layer, i, 0))],
        out_specs=pl.BlockSpec((tk, N_DEV * n), lambda i: (i, 0)),
        out_shape=jax.ShapeDtypeStruct((k, N_DEV * n), w8.dtype),
        compiler_params=_params(("arbitrary",), 32),
    )(w8)


def _shard_cast(g, name):
    k, n8 = g.shape
    n = n8 // N_DEV
    tk = _row_tile(k, 256)

    def body(g_ref, o_ref):
        for d in range(N_DEV):
            o_ref[d] = g_ref[:, d * n:(d + 1) * n].astype(BF16)

    return pl.pallas_call(
        body, name=name, grid=(k // tk,),
        in_specs=[pl.BlockSpec((tk, n8), lambda i: (i, 0))],
        out_specs=pl.BlockSpec((N_DEV, tk, n), lambda i: (0, i, 0)),
        out_shape=jax.ShapeDtypeStruct((N_DEV, k, n), BF16),
        compiler_params=_params(("arbitrary",), 32),
    )(g)


def _row(v):
    return v.reshape(1, -1)


def _group_ones():
    idx = jnp.arange(SB_WIDTH) // HEAD_DIM
    return (idx[:, None] == idx[None, :]).astype(BF16)


def _pad_rows(w, rows):
    return jnp.concatenate([w, jnp.zeros((rows - w.shape[0], w.shape[1]), w.dtype)], axis=0)


class _NoTraffic:
    def rest(self):
        return ()

    def install(self, wt, gathered):
        pass

    def ready(self, name, layer, grad):
        pass

    def take(self):
        return ()

    def landed(self, received):
        pass


def _local_step(x, target, wt, traffic):
    scale = HEAD_DIM ** -0.5
    bd = _group_ones()
    tril = jnp.tril(jnp.ones((CHUNK, CHUNK), dtype=bool))
    saved = []
    for i in range(DEPTH):
        j = i // 2
        lay = {"x_mix": x}
        if i % 2 == 0:
            proj, h = _rms_matmul(x, _row(wt["mix_norm_g"][i]), wt["sb_w_in"][j], jnp.zeros((1, IN_WIDTH), F32), "in_proj")
            gq = _row(jnp.tile(wt["sb_q_norm_g"][j], SB_WIDTH // HEAD_DIM)) * scale
            gk = _row(jnp.tile(wt["sb_k_norm_g"][j], SB_WIDTH // HEAD_DIM))
            qn, kn, vb = _qk_prep(proj, gq * LOG2E, gk, bd)
            o, lsum, gathered = _sb_attn_fwd(qn, kn, vb, gather=traffic.rest() if i == 0 else ())
            if i == 0:
                traffic.install(wt, gathered)
            wm = jnp.where(tril[None], wt["sg_w_spatial"][j], 0.0)
            wmb = wm.astype(BF16)
            wmt = jnp.swapaxes(wm, 1, 2).astype(BF16)
            bt = jnp.repeat(wt["sg_b_spatial"][j].T, HEAD_DIM, axis=1)
            gz = _row(wt["sg_z_norm_g"][j])
            gg = _sgu_fwd(proj, gz, wmb, bt, bd)
            x, mix = _out_proj(x, o, gg, wt["hyb_w_out"][j], "out_proj")
            lay.update(proj=proj, h=h, gq=gq, gk=gk, qn=qn, kn=kn, vb=vb, lsum=lsum, wmb=wmb, wmt=wmt, bt=bt, gz=gz, mix=mix)
        else:
            p, h = _rms_matmul(x, _row(wt["mix_norm_g"][i]), wt["cv_w_pw1"][j], _row(wt["cv_b_pw1"][j]), "conf_pw1")
            wdw = _pad_rows(wt["cv_w_dw"][j], HALO)
            yc, y2 = _conf_mid_fwd(p, wdw, _row(wt["cv_b_dw"][j]), _row(wt["cv_ln_g"][j]), _row(wt["cv_ln_b"][j]))
            x = _res_matmul(x, y2, wt["cv_w_pw2"][j], _row(wt["cv_b_pw2"][j]), "conf_pw2")
            lay.update(p=p, h=h, wdw=wdw, yc=yc, y2=y2)
        lay["x_ffn"] = x
        fdw = _pad_rows(wt["ffn_w_dw"][i], 8)
        x, hf, u = _ffn_fwd(x, _row(wt["ffn_norm_g"][i]), wt["ffn_w_up"][i], fdw, _row(wt["ffn_b_dw"][i]),
                            wt["ffn_w_down"][i])
        lay.update(hf=hf, u=u, fdw=fdw)
        saved.append(lay)

    lpart, dy = _loss_head(x, target)
    loss = lpart[0, 0]

    gr = {k: [None] * len(v) for k, v in wt.items()}

    def made(name, layer, grad):
        gr[name][layer] = grad
        traffic.ready(name, layer, grad)

    for i in reversed(range(DEPTH)):
        j = i // 2
        lay = saved[i]
        du, act, dfdw, dfb, dx, dgf, got = _ffn_bwd(
            dy, lay["u"], lay["x_ffn"], _row(wt["ffn_norm_g"][i]), lay["fdw"], _row(wt["ffn_b_dw"][i]),
            wt["ffn_w_down"][i], wt["ffn_w_up"][i], send=traffic.take())
        traffic.landed(got)
        made("ffn_w_down", i, _matmul_tn(act, dy, "ffn_dw_down"))
        made("ffn_w_up", i, _matmul_tn(lay["hf"], du, "ffn_dw_up"))
        gr["ffn_w_dw"][i] = dfdw[:FFN_K]
        gr["ffn_b_dw"][i] = dfb[0]
        gr["ffn_norm_g"][i] = dgf[0]
        dy = dx
        if i % 2 == 0:
            dmix = _matmul_nt(dy, wt["hyb_w_out"][j], "out_proj_dx")
            made("hyb_w_out", j, _matmul_tn(lay["mix"], dy, "out_proj_dw"))
            dqn, dkn, dv, got = _sb_attn_bwd(lay["qn"], lay["kn"], lay["vb"], lay["lsum"], dmix, send=traffic.take())
            traffic.landed(got)
            duz, dwm, dbt, dgz = _sgu_bwd(lay["proj"], dmix, lay["gz"], lay["wmb"], lay["wmt"], lay["bt"], bd)
            dproj, dgq, dgk = _qk_bwd(lay["proj"], dqn, dkn, dv, duz, lay["gq"], lay["gk"] * LN2, bd)
            made("sb_w_in", j, _matmul_tn(lay["h"], dproj, "in_proj_dw"))
            gr["sb_q_norm_g"][j] = dgq.reshape(SB_WIDTH // HEAD_DIM, HEAD_DIM).sum(0) * scale
            gr["sb_k_norm_g"][j] = dgk.reshape(SB_WIDTH // HEAD_DIM, HEAD_DIM).sum(0) * LN2
            gr["sg_z_norm_g"][j] = dgz[0]
            gr["sg_w_spatial"][j] = jnp.where(tril[None], dwm, 0.0)
            gr["sg_b_spatial"][j] = dbt.reshape(CHUNK, SG_GROUPS, HEAD_DIM).sum(-1).T
            dy, dgm, got = _nt_rms_bwd(dproj, wt["sb_w_in"][j], lay["x_mix"], _row(wt["mix_norm_g"][i]), dy,
                                       "in_proj_dx", send=traffic.take() if i == 0 else ())
            if i == 0:
                traffic.landed(got)
        else:
            made("cv_w_pw2", j, _matmul_tn(lay["y2"], dy, "conf_pw2_dw"))
            dp, dwdw, dbdw, dlg, dlb, db1, db2, dx, dgm, got = _conf_mid_bwd(
                lay["p"], lay["yc"], dy, lay["x_mix"], _row(wt["mix_norm_g"][i]), lay["wdw"],
                _row(wt["cv_ln_g"][j]), _row(wt["cv_ln_b"][j]), wt["cv_w_pw1"][j], wt["cv_w_pw2"][j],
                send=traffic.take())
            traffic.landed(got)
            dy = dx
            made("cv_w_pw1", j, _matmul_tn(lay["h"], dp, "conf_pw1_dw"))
            gr["cv_w_dw"][j] = dwdw[:CONV_K]
            gr["cv_b_dw"][j] = dbdw[0]
            gr["cv_ln_g"][j] = dlg[0]
            gr["cv_ln_b"][j] = dlb[0]
            gr["cv_b_pw1"][j] = db1[0]
            gr["cv_b_pw2"][j] = db2[0]
        gr["mix_norm_g"][i] = dgm[0]
    matmul_weights = ("sb_w_in", "hyb_w_out", "cv_w_pw1", "cv_w_pw2", "ffn_w_up", "ffn_w_down")
    grads = {k: (v if k in matmul_weights else jnp.stack(v)) for k, v in gr.items()}
    return loss, dy, grads


WEIGHTS = ["mix_norm_g", "sb_w_in", "sb_q_norm_g", "sb_k_norm_g", "sg_z_norm_g", "sg_w_spatial", "sg_b_spatial",
           "hyb_w_out", "cv_w_pw1", "cv_b_pw1", "cv_w_dw", "cv_b_dw", "cv_ln_g", "cv_ln_b", "cv_w_pw2", "cv_b_pw2",
           "ffn_norm_g", "ffn_w_up", "ffn_w_dw", "ffn_b_dw", "ffn_w_down"]
BIG = [("sb_w_in", "col"), ("hyb_w_out", "row"), ("cv_w_pw1", "col"), ("cv_w_pw2", "row"), ("ffn_w_up", "col"),
       ("ffn_w_down", "row")]
SMALL = ["cv_b_pw1", "cv_w_dw", "cv_b_dw", "cv_ln_g", "cv_ln_b", "cv_b_pw2", "ffn_w_dw"]
REPLICATED = ["mix_norm_g", "sb_q_norm_g", "sb_k_norm_g", "sg_z_norm_g", "sg_w_spatial", "sg_b_spatial", "ffn_norm_g",
              "ffn_b_dw"]


def _last_dim_blocks(full):
    t = jnp.moveaxis(full.reshape(full.shape[:-1] + (N_DEV, full.shape[-1] // N_DEV)), -2, 0)
    return t.reshape(N_DEV, -1)


def _from_last_dim_blocks(blocks, shard_shape):
    t = jnp.moveaxis(blocks.reshape((N_DEV,) + tuple(shard_shape)), 0, -2)
    return t.reshape(tuple(shard_shape[:-1]) + (N_DEV * shard_shape[-1],))


def _pack(arrays):
    lead = arrays[0].shape[:-1]
    flat = jnp.concatenate([a.astype(F32) for a in arrays], axis=-1)
    rows = -(-flat.shape[-1] // (16 * LANE)) * 16
    pad = rows * LANE - flat.shape[-1]
    if pad:
        flat = jnp.concatenate([flat, jnp.zeros(lead + (pad,), F32)], axis=-1)
    return flat.reshape(lead + (rows, LANE))


def _unpack(packed, shapes):
    flat = packed.reshape(-1)
    out, off = [], 0
    for shp in shapes:
        size = 1
        for dim in shp:
            size *= dim
        out.append(flat[off:off + size].reshape(shp))
        off += size
    return out


class _ShardTraffic:
    def __init__(self, w):
        self.w = w
        self.queue, self.flying, self.received = [], [], {}

    def rest(self):
        small = _pack([self.w[n].reshape(-1) for n in SMALL])
        return [self.w["sb_w_in"][1:].astype(BF16)] + [self.w[n].astype(BF16) for n, _ in BIG[1:]] + [small]

    def install(self, wt, gathered):
        wt["sb_w_in"].append(_unshard(gathered[0], 0, "unshard_sb_w_in"))
        for (n, kind), w8 in zip(BIG[1:], gathered[1:-1]):
            if kind == "col":
                wt[n] = [_unshard(w8, l, "unshard_" + n) for l in range(w8.shape[1])]
            else:
                wt[n] = [w8[:, l].reshape((N_DEV * w8.shape[2],) + w8.shape[3:]) for l in range(w8.shape[1])]
        sizes = [(self.w[n].size,) for n in SMALL]
        for n, parts in zip(SMALL, zip(*[_unpack(gathered[-1][d], sizes) for d in range(N_DEV)])):
            wt[n] = _from_last_dim_blocks(jnp.stack(parts), self.w[n].shape)

    def ready(self, name, layer, grad):
        if dict(BIG)[name] == "col":
            blocks = _shard_cast(grad, "shard_" + name)
        else:
            blocks = grad.reshape((N_DEV, grad.shape[0] // N_DEV) + grad.shape[1:])
        self.queue.append(((name, layer), blocks))

    def take(self):
        self.flying = [key for key, _ in self.queue]
        arrays = [blocks for _, blocks in self.queue]
        self.queue = []
        return arrays

    def landed(self, received):
        for key, blocks in zip(self.flying, received):
            self.received[key] = blocks
        self.flying = []


def kernel(x, mix_norm_g, sb_w_in, sb_q_norm_g, sb_k_norm_g, sg_z_norm_g, sg_w_spatial, sg_b_spatial, hyb_w_out, cv_w_pw1, cv_b_pw1, cv_w_dw, cv_b_dw, cv_ln_g, cv_ln_b, cv_w_pw2, cv_b_pw2, ffn_norm_g, ffn_w_up, ffn_w_dw, ffn_b_dw, ffn_w_down, loss_target, m_mix_norm_g, m_sb_w_in, m_sb_q_norm_g, m_sb_k_norm_g, m_sg_z_norm_g, m_sg_w_spatial, m_sg_b_spatial, m_hyb_w_out, m_cv_w_pw1, m_cv_b_pw1, m_cv_w_dw, m_cv_b_dw, m_cv_ln_g, m_cv_ln_b, m_cv_w_pw2, m_cv_b_pw2, m_ffn_norm_g, m_ffn_w_up, m_ffn_w_dw, m_ffn_b_dw, m_ffn_w_down, v_mix_norm_g, v_sb_w_in, v_sb_q_norm_g, v_sb_k_norm_g, v_sg_z_norm_g, v_sg_w_spatial, v_sg_b_spatial, v_hyb_w_out, v_cv_w_pw1, v_cv_b_pw1, v_cv_w_dw, v_cv_b_dw, v_cv_ln_g, v_cv_ln_b, v_cv_w_pw2, v_cv_b_pw2, v_ffn_norm_g, v_ffn_w_up, v_ffn_w_dw, v_ffn_b_dw, v_ffn_w_down):
    w = dict(zip(WEIGHTS, (mix_norm_g, sb_w_in, sb_q_norm_g, sb_k_norm_g, sg_z_norm_g, sg_w_spatial, sg_b_spatial,
                           hyb_w_out, cv_w_pw1, cv_b_pw1, cv_w_dw, cv_b_dw, cv_ln_g, cv_ln_b, cv_w_pw2, cv_b_pw2,
                           ffn_norm_g, ffn_w_up, ffn_w_dw, ffn_b_dw, ffn_w_down)))
    m = dict(zip(WEIGHTS, (m_mix_norm_g, m_sb_w_in, m_sb_q_norm_g, m_sb_k_norm_g, m_sg_z_norm_g, m_sg_w_spatial,
                           m_sg_b_spatial, m_hyb_w_out, m_cv_w_pw1, m_cv_b_pw1, m_cv_w_dw, m_cv_b_dw, m_cv_ln_g,
                           m_cv_ln_b, m_cv_w_pw2, m_cv_b_pw2, m_ffn_norm_g, m_ffn_w_up, m_ffn_w_dw, m_ffn_b_dw,
                           m_ffn_w_down)))
    v = dict(zip(WEIGHTS, (v_mix_norm_g, v_sb_w_in, v_sb_q_norm_g, v_sb_k_norm_g, v_sg_z_norm_g, v_sg_w_spatial,
                           v_sg_b_spatial, v_hyb_w_out, v_cv_w_pw1, v_cv_b_pw1, v_cv_w_dw, v_cv_b_dw, v_cv_ln_g,
                           v_cv_ln_b, v_cv_w_pw2, v_cv_b_pw2, v_ffn_norm_g, v_ffn_w_up, v_ffn_w_dw, v_ffn_b_dw,
                           v_ffn_w_down)))
    big_names = [n for n, _ in BIG]
    flat = lambda t, names: [t[n].reshape(-1) for n in names]

    first_in = _all_gather([w["sb_w_in"][0:1].astype(BF16)], "gather_first")[0]
    wt = {n: w[n] for n in REPLICATED}
    wt["sb_w_in"] = [_unshard(first_in, 0, "unshard_sb_w_in")]

    traffic = _ShardTraffic(w)
    loss, gx, grads = _local_step(x[0], loss_target[0], wt, traffic)
    assert not traffic.queue and not traffic.flying
    recv_big = [[traffic.received[n, l] for l in range(w[n].shape[0])] for n in big_names]

    grep = _pack(flat(grads, REPLICATED))
    gsmall = _pack([_last_dim_blocks(grads[n]) for n in SMALL])
    rep_rows, small_rows = grep.shape[0], gsmall.shape[1]
    vec = jnp.concatenate([grep, gsmall.reshape(N_DEV * small_rows, LANE)], axis=0).astype(BF16)
    vec_all = _all_gather([vec], "gather_small_grads")[0]
    me = 4 * lax.axis_index("x") + 2 * lax.axis_index("y") + lax.axis_index("c")
    recv_rep = vec_all[:, :rep_rows]
    recv_small = lax.dynamic_slice(vec_all, (0, rep_rows + small_rows * me, 0), (N_DEV, small_rows, LANE))

    out = {}
    kinds = ("grad", "delta", "new_m", "new_v")
    for n, parts in zip(big_names, recv_big):
        for kind, arr in zip(kinds, _reduce_adamw(parts, w[n], m[n], v[n], "adamw_" + n)):
            out[kind, n] = arr
    for names, recv, tag in ((SMALL, recv_small, "adamw_small"), (REPLICATED, recv_rep, "adamw_replicated")):
        res = _reduce_adamw([recv], _pack(flat(w, names))[None], _pack(flat(m, names))[None],
                            _pack(flat(v, names))[None], tag)
        shapes = [w[n].shape for n in names]
        for kind, packed in zip(kinds, res):
            for n, arr in zip(names, _unpack(packed[0], shapes)):
                out[kind, n] = arr

    loss = lax.psum(loss, ("x", "y", "c"))
    return (loss, gx[None], *[out[kind, n] for kind in kinds for n in WEIGHTS])
```

```python
import functools

import jax
import jax.numpy as jnp
from jax import lax
from jax.experimental import pallas as pl
from jax.experimental.pallas import tpu as pltpu

F32 = jnp.float32
BF16 = jnp.bfloat16

D_MODEL = 1024
HEAD_DIM = 64
SB_WIDTH = 512
SG_WIDTH = 512
SG_GROUPS = 8
IN_WIDTH = 3 * SB_WIDTH + 2 * SG_WIDTH
CHUNK = 128
CONV_K = 31
D_FF = 2816
FFN_K = 3
DEPTH = 4
EPS = 1e-6
N_DEV = 8
LANE = 128
HALO = 32
ATT_BLOCK = 256
FF_CHUNK = 256
CONV_ROWS = 32
CONV_LANES = 512
MIB = 2 ** 20

ADAM_LR = 0.001
ADAM_B1 = 0.9
ADAM_B2 = 0.999
ADAM_EPS = 1e-08
ADAM_WD = 0.01
ADAM_STEP = 10

LOG2E = 1.4426950408889634
LN2 = 0.6931471805599453

NT_DIMS = (((1,), (1,)), ((), ()))
TN_DIMS = (((0,), (0,)), ((), ()))


def _params(semantics, vmem_mib):
    return pltpu.CompilerParams(dimension_semantics=semantics, vmem_limit_bytes=vmem_mib * MIB)


def _full(shape):
    nd = len(shape)
    return pl.BlockSpec(shape, lambda *_: (0,) * nd)


def _sigmoid(x):
    return 1.0 / (1.0 + jnp.exp(-x))


def _gelu(x):
    return 0.5 * x * (1.0 + lax.erf(x * 0.7071067811865476))


def _gelu_grad(x):
    return 0.5 * (1.0 + lax.erf(x * 0.7071067811865476)) + x * jnp.exp(-0.5 * x * x) * 0.3989422804014327


def _silu_grad(x, s):
    return s * (1.0 + x * (1.0 - s))


def _dot(a, b):
    return jnp.dot(a, b, preferred_element_type=F32)


def _dot2(a, b):
    hi = a.astype(BF16)
    lo = (a - hi.astype(F32)).astype(BF16)
    return _dot(hi, b) + _dot(lo, b)


def _group_mean(t, bd):
    return _dot2(t, bd) * (1.0 / HEAD_DIM)


def _shift_down(v, prev8, s):
    top = pltpu.roll(jnp.concatenate([prev8, v[:8]], axis=0), s, 0)[8:16]
    return jnp.concatenate([top, pltpu.roll(v, s, 0)[8:]], axis=0)


def _shift_up(v, next8, s):
    n = v.shape[0]
    bottom = pltpu.roll(jnp.concatenate([v[n - 8:], next8], axis=0), 16 - s, 0)[0:8]
    return jnp.concatenate([pltpu.roll(v, n - s, 0)[: n - 8], bottom], axis=0)


def _mesh_pos():
    return lax.axis_index("x"), lax.axis_index("y"), lax.axis_index("c")


def _comm_scratch(n):
    return [pltpu.SemaphoreType.DMA((7 * n,)), pltpu.SemaphoreType.DMA((7 * n,)), pltpu.SemaphoreType.DMA((n,))]


class _Scatter:
    def __init__(self, src_refs, out_refs, send_sems, recv_sems, local_sems):
        x, y, cc = _mesh_pos()
        me = 4 * x + 2 * y + cc
        self.copies, self.mine = [], []
        for a, (src, out) in enumerate(zip(src_refs, out_refs)):
            self.mine.append(pltpu.make_async_copy(src.at[me], out.at[me], local_sems.at[a]))
            for k in range(1, N_DEV):
                px = 1 - x if k & 4 else x
                py = 1 - y if k & 2 else y
                pc = 1 - cc if k & 1 else cc
                self.copies.append(pltpu.make_async_remote_copy(
                    src_ref=src.at[4 * px + 2 * py + pc], dst_ref=out.at[me],
                    send_sem=send_sems.at[7 * a + k - 1], recv_sem=recv_sems.at[7 * a + k - 1],
                    device_id=(px, py, pc), device_id_type=pl.DeviceIdType.MESH))

    def start(self):
        for cp in self.mine + self.copies:
            cp.start()

    def finish(self):
        for cp in self.copies + self.mine:
            cp.wait()


class _Gather:
    def __init__(self, x_refs, out_refs, send_sems, recv_sems, local_sems):
        x, y, cc = _mesh_pos()
        self.n = len(x_refs)
        self.me, self.sibling, self.cc = (x, y, cc), (x, y, 1 - cc), cc
        self.chips = [(1 - x, y), (x, 1 - y), (1 - x, 1 - y)]
        self.x_refs, self.out_refs, self.send_sems, self.recv_sems = x_refs, out_refs, send_sems, recv_sems
        self.mine = [pltpu.make_async_copy(x_refs[a], out_refs[a].at[4 * x + 2 * y + cc], local_sems.at[a])
                     for a in range(self.n)]

    def copy(self, a, k, block, to, own=False):
        slot = self.out_refs[a].at[4 * block[0] + 2 * block[1] + block[2]]
        return pltpu.make_async_remote_copy(
            src_ref=self.x_refs[a] if own else slot, dst_ref=slot,
            send_sem=self.send_sems.at[7 * a + k], recv_sem=self.recv_sems.at[7 * a + k],
            device_id=to, device_id_type=pl.DeviceIdType.MESH)

    def first_hop(self, a):
        return [self.copy(a, 0, self.me, self.sibling, own=True)] + [
            self.copy(a, 1 + j, self.me, (*chip, self.cc), own=True) for j, chip in enumerate(self.chips)]

    def passed_on(self, a):
        return [self.copy(a, 4 + j, (*chip, self.cc), self.sibling) for j, chip in enumerate(self.chips)]

    def start(self):
        for a in range(self.n):
            self.mine[a].start()
        for a in range(self.n):
            for cp in self.first_hop(a):
                cp.start()

    def forward(self):
        for a in range(self.n):
            for j, chip in enumerate(self.chips):
                self.copy(a, 1 + j, (*chip, self.cc), self.me).wait_recv()
                self.copy(a, 4 + j, (*chip, self.cc), self.sibling).start()

    def finish(self):
        for a in range(self.n):
            self.copy(a, 0, self.sibling, self.me).wait_recv()
            for j, chip in enumerate(self.chips):
                self.copy(a, 4 + j, (*chip, 1 - self.cc), self.me).wait_recv()
        for a in range(self.n):
            for cp in self.first_hop(a) + self.passed_on(a):
                cp.wait_send()
        for cp in self.mine:
            cp.wait()


def _carried(body, n_in, n_out, arrays, kind, when):
    n = len(arrays)
    if n == 0:
        return body, [], [], [], []

    def wrapped(*refs):
        ins, srcs = refs[:n_in], refs[n_in:n_in + n]
        outs, landed = refs[n_in + n:n_in + n + n_out], refs[n_in + n + n_out:n_in + 2 * n + n_out]
        scratch = refs[n_in + 2 * n + n_out:]
        comm = (_Scatter if kind == "scatter" else _Gather)(srcs, landed, *scratch[-3:])
        pl.when(when["start"]())(comm.start)
        if kind == "gather":
            pl.when(when["forward"]())(comm.forward)
        body(*ins, *outs, *scratch[:-3])
        pl.when(when["finish"]())(comm.finish)

    any_spec = pl.BlockSpec(memory_space=pl.ANY)
    if kind == "scatter":
        shapes = [jax.ShapeDtypeStruct(a.shape, a.dtype) for a in arrays]
    else:
        shapes = [jax.ShapeDtypeStruct((N_DEV,) + a.shape, a.dtype) for a in arrays]
    return wrapped, [any_spec] * n, [any_spec] * n, shapes, _comm_scratch(n)


def _all_gather(shards, name):
    n = len(shards)

    def body(*refs):
        comm = _Gather(refs[:n], refs[n:2 * n], *refs[2 * n:])
        comm.start()
        comm.forward()
        comm.finish()

    any_spec = pl.BlockSpec(memory_space=pl.ANY)
    return pl.pallas_call(
        body, name=name, in_specs=[any_spec] * n, out_specs=[any_spec] * n,
        out_shape=[jax.ShapeDtypeStruct((N_DEV,) + s.shape, s.dtype) for s in shards],
        scratch_shapes=_comm_scratch(n),
    )(*shards)


def _rms_matmul(x, g, w, b, name):
    s, d = x.shape
    n = w.shape[1]
    tm = min(512, s)

    def body(x_ref, g_ref, w_ref, b_ref, y_ref, h_ref):
        xv = x_ref[...]
        r = lax.rsqrt(jnp.mean(xv * xv, axis=-1, keepdims=True) + EPS)
        h = (xv * r * g_ref[...]).astype(BF16)
        h_ref[...] = h
        for c in range(0, n, 512):
            y_ref[:, c:c + 512] = _dot(h, w_ref[:, c:c + 512]) + b_ref[:, c:c + 512]

    return pl.pallas_call(
        body, name=name, grid=(s // tm,),
        in_specs=[pl.BlockSpec((tm, d), lambda i: (i, 0)), _full((1, d)), _full((d, n)), _full((1, n))],
        out_specs=[pl.BlockSpec((tm, n), lambda i: (i, 0)), pl.BlockSpec((tm, d), lambda i: (i, 0))],
        out_shape=[jax.ShapeDtypeStruct((s, n), F32), jax.ShapeDtypeStruct((s, d), BF16)],
        compiler_params=_params(("arbitrary",), 48),
    )(x, g, w, b)


def _matmul_nt(a, w, name):
    s, n = a.shape
    k = w.shape[0]
    tm = min(512, s)

    def body(a_ref, w_ref, o_ref):
        o_ref[...] = lax.dot_general(a_ref[...].astype(BF16), w_ref[...], NT_DIMS, preferred_element_type=F32)

    return pl.pallas_call(
        body, name=name, grid=(s // tm,),
        in_specs=[pl.BlockSpec((tm, n), lambda i: (i, 0)), _full((k, n))],
        out_specs=pl.BlockSpec((tm, k), lambda i: (i, 0)),
        out_shape=jax.ShapeDtypeStruct((s, k), F32),
        compiler_params=_params(("arbitrary",), 40),
    )(a, w)


def _matmul_tn(a, b, name):
    s, k = a.shape
    n = b.shape[1]
    ts = min(2048 if k <= 1024 else 1024, s)
    tn = 1024 if (n % 1024 == 0 and k <= 1024) else 512
    steps = s // ts

    def body(a_ref, b_ref, o_ref, acc):
        t = pl.program_id(1)

        @pl.when(t == 0)
        def _():
            acc[...] = jnp.zeros_like(acc)

        acc[...] += lax.dot_general(a_ref[...].astype(BF16), b_ref[...].astype(BF16), TN_DIMS,
                                    preferred_element_type=F32)

        @pl.when(t == steps - 1)
        def _():
            o_ref[...] = acc[...].astype(BF16)

    return pl.pallas_call(
        body, name=name, grid=(n // tn, steps),
        in_specs=[pl.BlockSpec((ts, k), lambda j, t: (t, 0)), pl.BlockSpec((ts, tn), lambda j, t: (t, j))],
        out_specs=pl.BlockSpec((k, tn), lambda j, t: (0, j)),
        out_shape=jax.ShapeDtypeStruct((k, n), BF16),
        scratch_shapes=[pltpu.VMEM((k, tn), F32)],
        compiler_params=_params(("arbitrary", "arbitrary"), 48),
    )(a, b)


def _nt_rms_bwd(dp, w, x, g, dres, name, send=()):
    s, n = dp.shape
    d = x.shape[1]
    tm = min(256, s)
    steps = s // tm

    def body(dp_ref, w_ref, x_ref, g_ref, dres_ref, dx_ref, dg_ref):
        @pl.when(pl.program_id(0) == 0)
        def _():
            dg_ref[...] = jnp.zeros_like(dg_ref)

        dh = lax.dot_general(dp_ref[...], w_ref[...], NT_DIMS, preferred_element_type=F32)
        xv = x_ref[...]
        r = lax.rsqrt(jnp.mean(xv * xv, axis=-1, keepdims=True) + EPS)
        xh = xv * r
        dg_ref[...] += jnp.sum(dh * xh, axis=0, keepdims=True)
        dn = dh * g_ref[...]
        dx_ref[...] = dres_ref[...] + r * (dn - xh * jnp.mean(dn * xh, axis=-1, keepdims=True))

    step_is = lambda i: lambda: pl.program_id(0) == i
    kern, ins, outs, shapes, sems = _carried(body, 5, 2, send, "scatter",
                                             {"start": step_is(0), "finish": step_is(steps - 1)})
    res = pl.pallas_call(
        kern, name=name, grid=(steps,),
        in_specs=[pl.BlockSpec((tm, n), lambda i: (i, 0)), _full((d, n)), pl.BlockSpec((tm, d), lambda i: (i, 0)),
                  _full((1, d)), pl.BlockSpec((tm, d), lambda i: (i, 0))] + ins,
        out_specs=[pl.BlockSpec((tm, d), lambda i: (i, 0)), _full((1, d))] + outs,
        out_shape=[jax.ShapeDtypeStruct((s, d), F32), jax.ShapeDtypeStruct((1, d), F32)] + shapes,
        scratch_shapes=sems,
        compiler_params=_params(("arbitrary",), 52),
    )(dp, w, x, g, dres, *send)
    return res[0], res[1], list(res[2:])


def _res_matmul(x, a, w, b, name):
    s, d = x.shape
    k = a.shape[1]
    tm = min(512, s)

    def body(x_ref, a_ref, w_ref, b_ref, o_ref):
        o_ref[...] = x_ref[...] + _dot(a_ref[...], w_ref[...]) + b_ref[...]

    return pl.pallas_call(
        body, name=name, grid=(s // tm,),
        in_specs=[pl.BlockSpec((tm, d), lambda i: (i, 0)), pl.BlockSpec((tm, k), lambda i: (i, 0)), _full((k, d)),
                  _full((1, d))],
        out_specs=pl.BlockSpec((tm, d), lambda i: (i, 0)),
        out_shape=jax.ShapeDtypeStruct((s, d), F32),
        compiler_params=_params(("arbitrary",), 32),
    )(x, a, w, b)


def _out_proj(x, o, gg, w, name):
    s, d = x.shape
    tm = min(512, s)

    def body(x_ref, o_ref, gg_ref, w_ref, y_ref, mix_ref):
        mix = jnp.concatenate([o_ref[...], gg_ref[...]], axis=1).astype(BF16)
        mix_ref[...] = mix
        y_ref[...] = x_ref[...] + _dot(mix, w_ref[...])

    return pl.pallas_call(
        body, name=name, grid=(s // tm,),
        in_specs=[pl.BlockSpec((tm, d), lambda i: (i, 0)), pl.BlockSpec((tm, SB_WIDTH), lambda i: (i, 0)),
                  pl.BlockSpec((tm, SG_WIDTH), lambda i: (i, 0)), _full((d, d))],
        out_specs=[pl.BlockSpec((tm, d), lambda i: (i, 0)), pl.BlockSpec((tm, d), lambda i: (i, 0))],
        out_shape=[jax.ShapeDtypeStruct((s, d), F32), jax.ShapeDtypeStruct((s, d), BF16)],
        compiler_params=_params(("arbitrary",), 32),
    )(x, o, gg, w)


def _qk_prep(proj, gq, gk, bd):
    s = proj.shape[0]
    tm = min(512, s)

    def body(q_ref, k_ref, v_ref, gq_ref, gk_ref, bd_ref, qn_ref, kn_ref, vb_ref):
        bdv = bd_ref[...]
        q = q_ref[...]
        k = k_ref[...]
        qn_ref[...] = (q * lax.rsqrt(_group_mean(q * q, bdv) + EPS) * gq_ref[...]).astype(BF16)
        kn_ref[...] = (k * lax.rsqrt(_group_mean(k * k, bdv) + EPS) * gk_ref[...]).astype(BF16)
        vb_ref[...] = v_ref[...].astype(BF16)

    col = lambda c: pl.BlockSpec((tm, SB_WIDTH), lambda i: (i, c))
    out = pl.BlockSpec((tm, SB_WIDTH), lambda i: (i, 0))
    return pl.pallas_call(
        body, name="qk_prep", grid=(s // tm,),
        in_specs=[col(0), col(1), col(2), _full((1, SB_WIDTH)), _full((1, SB_WIDTH)), _full((SB_WIDTH, SB_WIDTH))],
        out_specs=[out, out, out],
        out_shape=[jax.ShapeDtypeStruct((s, SB_WIDTH), BF16)] * 3,
        compiler_params=_params(("arbitrary",), 32),
    )(proj, proj, proj, gq, gk, bd)


def _qk_bwd(proj, dqn, dkn, dv, duz, gq, gk, bd):
    s = proj.shape[0]
    tm = min(512, s)

    def body(q_ref, k_ref, dq_ref, dk_ref, dv_ref, duz_ref, gq_ref, gk_ref, bd_ref, o_ref, dgq_ref, dgk_ref):
        @pl.when(pl.program_id(0) == 0)
        def _():
            dgq_ref[...] = jnp.zeros_like(dgq_ref)
            dgk_ref[...] = jnp.zeros_like(dgk_ref)

        bdv = bd_ref[...]

        def back(t, gain, dout, dg_ref):
            r = lax.rsqrt(_group_mean(t * t, bdv) + EPS)
            th = t * r
            dg_ref[...] += jnp.sum(dout * th, axis=0, keepdims=True)
            dn = dout * gain
            return r * (dn - th * _group_mean(dn * th, bdv))

        o_ref[:, 0:SB_WIDTH] = back(q_ref[...], gq_ref[...], dq_ref[...], dgq_ref).astype(BF16)
        o_ref[:, SB_WIDTH:2 * SB_WIDTH] = back(k_ref[...], gk_ref[...], dk_ref[...], dgk_ref).astype(BF16)
        o_ref[:, 2 * SB_WIDTH:3 * SB_WIDTH] = dv_ref[...].astype(BF16)
        o_ref[:, 3 * SB_WIDTH:IN_WIDTH] = duz_ref[...].astype(BF16)

    col = lambda c: pl.BlockSpec((tm, SB_WIDTH), lambda i: (i, c))
    row = pl.BlockSpec((tm, SB_WIDTH), lambda i: (i, 0))
    return pl.pallas_call(
        body, name="qk_bwd", grid=(s // tm,),
        in_specs=[col(0), col(1), row, row, row, pl.BlockSpec((tm, 2 * SG_WIDTH), lambda i: (i, 0)),
                  _full((1, SB_WIDTH)), _full((1, SB_WIDTH)), _full((SB_WIDTH, SB_WIDTH))],
        out_specs=[pl.BlockSpec((tm, IN_WIDTH), lambda i: (i, 0)), _full((1, SB_WIDTH)), _full((1, SB_WIDTH))],
        out_shape=[jax.ShapeDtypeStruct((s, IN_WIDTH), BF16), jax.ShapeDtypeStruct((1, SB_WIDTH), F32),
                   jax.ShapeDtypeStruct((1, SB_WIDTH), F32)],
        compiler_params=_params(("arbitrary",), 40),
    )(proj, proj, dqn, dkn, dv, duz, gq, gk, bd)


def _attn_masks(tq, tk):
    lane = lax.broadcasted_iota(jnp.int32, (tk, LANE), 1)
    heads = [(lane >= hh * HEAD_DIM) & (lane < (hh + 1) * HEAD_DIM) for hh in range(2)]
    urow = lax.broadcasted_iota(jnp.int32, (tk, tk), 0)
    ucol = lax.broadcasted_iota(jnp.int32, (tk, tk), 1)
    return heads, urow, ucol


def _keep_cost(z2):
    return jnp.log(1.0 + jnp.exp2(jnp.minimum(z2, 126.0)))


def _at_rows(part, row0, total):
    rows, width = part.shape
    pieces = [jnp.zeros((row0, width), part.dtype)] if row0 else []
    pieces.append(part)
    if total - row0 - rows:
        pieces.append(jnp.zeros((total - row0 - rows, width), part.dtype))
    return pieces[0] if len(pieces) == 1 else jnp.concatenate(pieces, axis=0)


def _diagonal_specs(i, r, tq, tk, left_to_right):
    if r == 2:
        specs = [(2 * i + 1, tk, tk, True), (2 * i, tk, tk, False), (2 * i, 0, tk, True)]
    else:
        specs = [(r * i + r - 1 - d, 0, tq, True) for d in range(r)]
    return specs[::-1] if left_to_right else specs


def _sb_attn_fwd(q2, kn, vb, gather=()):
    s = q2.shape[0]
    tk = min(ATT_BLOCK, s)
    tq = min(2 * ATT_BLOCK, s)
    r = tq // tk
    nq = s // tq
    assert s // tk <= LANE

    def body(q_ref, k_ref, v_ref, o_ref, ls_ref):
        i = pl.program_id(1)
        heads, urow, ucol = _attn_masks(tq, tk)
        u_incl = (urow >= ucol).astype(BF16)
        row = lax.broadcasted_iota(jnp.int32, (tq, tk), 0)
        col = lax.broadcasted_iota(jnp.int32, (tq, tk), 1)
        qlane = lax.broadcasted_iota(jnp.int32, (tq, LANE), 1)
        q = q_ref[...]

        def blocks(specs, state):
            carry, acc, ls = list(state[0:2]), state[2], list(state[3:5])
            chains = [(d, hh) for d in range(len(specs)) for hh in range(2)]
            kblk, vblk, valid = {}, {}, {}
            for d, (kb, row0, rows, masked) in enumerate(specs):
                off = pl.multiple_of(kb * tk, tk)
                kfull = k_ref[pl.ds(off, tk), :]
                vfull = v_ref[pl.ds(off, tk), :]
                if masked:
                    valid[d] = (kb * tk + col[:rows]) < (i * tq + row0 + row[:rows])
                for hh in range(2):
                    kblk[d, hh] = jnp.where(heads[hh], kfull, jnp.zeros((), BF16))
                    vblk[d, hh] = jnp.where(heads[hh], vfull, jnp.zeros((), BF16))
            z2 = {(d, hh): lax.dot_general(q[specs[d][1]:specs[d][1] + specs[d][2]], kblk[d, hh], NT_DIMS,
                                           preferred_element_type=F32) for d, hh in chains}
            cost = {}
            for d, hh in chains:
                cost[d, hh] = _keep_cost(z2[d, hh])
                if specs[d][3]:
                    cost[d, hh] = jnp.where(valid[d], cost[d, hh], 0.0)
            sums = {c: _dot(cost[c].astype(BF16), u_incl) for c in chains}
            a = {}
            for d, hh in chains:
                kb, row0, rows, masked = specs[d]
                rin = carry[hh][row0:row0 + rows] + sums[d, hh]
                a[d, hh] = jnp.exp2(z2[d, hh] - rin * LOG2E)
                if masked:
                    a[d, hh] = jnp.where(valid[d], a[d, hh], 0.0)
                rs = jnp.sum(_at_rows(cost[d, hh], row0, tq), axis=1, keepdims=True)
                ls[hh] = ls[hh] + jnp.where(qlane == kb, rs, 0.0)
                carry[hh] = carry[hh] + rs
            for d, hh in chains:
                acc = acc + _at_rows(_dot(a[d, hh].astype(BF16), vblk[d, hh]), specs[d][1], tq)
            return carry[0], carry[1], acc, ls[0], ls[1]

        zc = jnp.zeros((tq, 1), F32)
        zt = jnp.zeros((tq, LANE), F32)
        state = blocks(_diagonal_specs(i, r, tq, tk, False), (zc, zc, zt, zt, zt))
        pairs = lax.shift_right_logical(i, 1)
        state = lax.fori_loop(
            0, pairs,
            lambda n, st: blocks([(r * i - 1 - 2 * r * n - d, 0, tq, False) for d in range(2 * r)], st), state)
        state = lax.fori_loop(
            0, i - 2 * pairs, lambda n, st: blocks([(r - 1 - d, 0, tq, False) for d in range(r)], st), state)
        o_ref[...] = state[2]
        ls_ref[:, 0:LANE] = state[3]
        ls_ref[:, LANE:2 * LANE] = state[4]

    last_pair = SB_WIDTH // LANE - 1
    step_is = lambda p, i: lambda: (pl.program_id(0) == p) & (pl.program_id(1) == i)
    when = {"start": step_is(0, 0), "forward": step_is(last_pair, 0), "finish": step_is(last_pair, nq - 1)}
    kern, ins, outs, shapes, sems = _carried(body, 3, 2, gather, "gather", when)
    res = pl.pallas_call(
        kern, name="sb_attn_fwd", grid=(SB_WIDTH // LANE, nq),
        in_specs=[pl.BlockSpec((tq, LANE), lambda p, i: (i, p)), pl.BlockSpec((s, LANE), lambda p, i: (0, p)),
                  pl.BlockSpec((s, LANE), lambda p, i: (0, p))] + ins,
        out_specs=[pl.BlockSpec((tq, LANE), lambda p, i: (i, p)),
                   pl.BlockSpec((tq, 2 * LANE), lambda p, i: (i, p))] + outs,
        out_shape=[jax.ShapeDtypeStruct((s, SB_WIDTH), F32), jax.ShapeDtypeStruct((s, 2 * SB_WIDTH), F32)] + shapes,
        scratch_shapes=sems,
        compiler_params=_params(("arbitrary", "arbitrary"), 48),
    )(q2, kn, vb, *gather)
    return res[0], res[1], list(res[2:])


def _sb_attn_bwd(q2, kn, vb, lsum, dmix, send=()):
    s = q2.shape[0]
    tk = min(ATT_BLOCK, s)
    tq = min(2 * ATT_BLOCK, s)
    r = tq // tk
    nq = s // tq

    def body(q_ref, k_ref, v_ref, ls_ref, do_ref, dq_ref, dk_ref, dv_ref):
        i = pl.program_id(1)

        @pl.when(i == 0)
        def _():
            dk_ref[...] = jnp.zeros_like(dk_ref)
            dv_ref[...] = jnp.zeros_like(dv_ref)

        heads, urow, ucol = _attn_masks(tq, tk)
        u_incl = (urow >= ucol).astype(BF16)
        u_pre = (urow <= ucol).astype(BF16)
        lrow = lax.broadcasted_iota(jnp.int32, (LANE, LANE), 0)
        lcol = lax.broadcasted_iota(jnp.int32, (LANE, LANE), 1)
        u_after = (lrow > lcol).astype(BF16)
        row = lax.broadcasted_iota(jnp.int32, (tq, tk), 0)
        col = lax.broadcasted_iota(jnp.int32, (tq, tk), 1)
        qlane = lax.broadcasted_iota(jnp.int32, (tq, LANE), 1)
        qheads = [(qlane >= hh * HEAD_DIM) & (qlane < (hh + 1) * HEAD_DIM) for hh in range(2)]
        q = q_ref[...]
        dob = do_ref[...].astype(BF16)
        qm = [jnp.where(qheads[hh], q, jnp.zeros((), BF16)) for hh in range(2)]
        dom = [jnp.where(qheads[hh], dob, jnp.zeros((), BF16)) for hh in range(2)]
        after = []
        for hh in range(2):
            ls = ls_ref[:, hh * LANE:(hh + 1) * LANE]
            hi = ls.astype(BF16)
            mid = (ls - hi.astype(F32)).astype(BF16)
            lo = (ls - hi.astype(F32) - mid.astype(F32)).astype(BF16)
            after.append(_dot(hi, u_after) + _dot(mid, u_after) + _dot(lo, u_after))

        def blocks(specs, state):
            cp, dq = list(state[0:2]), state[2]
            chains = [(d, hh) for d in range(len(specs)) for hh in range(2)]
            rows_of = lambda d: slice(specs[d][1], specs[d][1] + specs[d][2])
            offs, kblk, vblk, valid = {}, {}, {}, {}
            for d, (kb, row0, rows, masked) in enumerate(specs):
                offs[d] = pl.multiple_of(kb * tk, tk)
                kfull = k_ref[pl.ds(offs[d], tk), :]
                vblk[d] = v_ref[pl.ds(offs[d], tk), :]
                if masked:
                    valid[d] = (kb * tk + col[:rows]) < (i * tq + row0 + row[:rows])
                for hh in range(2):
                    kblk[d, hh] = jnp.where(heads[hh], kfull, jnp.zeros((), BF16))
            z2 = {(d, hh): lax.dot_general(q[rows_of(d)], kblk[d, hh], NT_DIMS, preferred_element_type=F32)
                  for d, hh in chains}
            da = {(d, hh): lax.dot_general(dom[hh][rows_of(d)], vblk[d], NT_DIMS, preferred_element_type=F32)
                  for d, hh in chains}
            cost, sig = {}, {}
            for d, hh in chains:
                cost[d, hh] = _keep_cost(z2[d, hh])
                sig[d, hh] = jnp.exp2(z2[d, hh] - cost[d, hh] * LOG2E)
                if specs[d][3]:
                    cost[d, hh] = jnp.where(valid[d], cost[d, hh], 0.0)
            sums = {c: _dot(cost[c].astype(BF16), u_incl) for c in chains}
            a, g = {}, {}
            for d, hh in chains:
                kb, row0, rows, masked = specs[d]
                cr = jnp.sum(jnp.where(qlane[:rows] == kb, after[hh][rows_of(d)], 0.0), axis=1, keepdims=True)
                a[d, hh] = jnp.exp2(z2[d, hh] - (cr + sums[d, hh]) * LOG2E)
                if masked:
                    a[d, hh] = jnp.where(valid[d], a[d, hh], 0.0)
                g[d, hh] = da[d, hh] * a[d, hh]
            pre = {c: _dot(g[c].astype(BF16), u_pre) for c in chains}
            dzb = {}
            for d, hh in chains:
                dz = g[d, hh] - sig[d, hh] * (cp[hh][rows_of(d)] + pre[d, hh])
                if specs[d][3]:
                    dz = jnp.where(valid[d], dz, 0.0)
                dzb[d, hh] = dz.astype(BF16)
                cp[hh] = cp[hh] + _at_rows(jnp.sum(g[d, hh], axis=1, keepdims=True), specs[d][1], tq)
            for d in range(len(specs)):
                dv_ref[pl.ds(offs[d], tk), :] += sum(
                    lax.dot_general(a[d, hh].astype(BF16), dom[hh][rows_of(d)], TN_DIMS, preferred_element_type=F32)
                    for hh in range(2))
                dk_ref[pl.ds(offs[d], tk), :] += sum(
                    lax.dot_general(dzb[d, hh], qm[hh][rows_of(d)], TN_DIMS, preferred_element_type=F32)
                    for hh in range(2))
            for d, hh in chains:
                dq = dq + _at_rows(_dot(dzb[d, hh], kblk[d, hh]), specs[d][1], tq)
            return cp[0], cp[1], dq

        zc = jnp.zeros((tq, 1), F32)
        pairs = lax.shift_right_logical(i, 1)
        state = lax.fori_loop(
            0, pairs, lambda n, st: blocks([(2 * r * n + d, 0, tq, False) for d in range(2 * r)], st),
            (zc, zc, jnp.zeros((tq, LANE), F32)))
        state = lax.fori_loop(
            0, i - 2 * pairs, lambda n, st: blocks([(2 * r * pairs + d, 0, tq, False) for d in range(r)], st), state)
        state = blocks(_diagonal_specs(i, r, tq, tk, True), state)
        dq_ref[...] = state[2]

    blk = pl.BlockSpec((tq, LANE), lambda p, i: (i, p))
    whole = pl.BlockSpec((s, LANE), lambda p, i: (0, p))
    last_pair = SB_WIDTH // LANE - 1
    step_is = lambda p, i: lambda: (pl.program_id(0) == p) & (pl.program_id(1) == i)
    kern, ins, outs, shapes, sems = _carried(body, 5, 3, send, "scatter",
                                             {"start": step_is(0, 0), "finish": step_is(last_pair, nq - 1)})
    res = pl.pallas_call(
        kern, name="sb_attn_bwd", grid=(SB_WIDTH // LANE, nq),
        in_specs=[blk, whole, whole, pl.BlockSpec((tq, 2 * LANE), lambda p, i: (i, p)), blk] + ins,
        out_specs=[blk, whole, whole] + outs,
        out_shape=[jax.ShapeDtypeStruct((s, SB_WIDTH), F32)] * 3 + shapes,
        scratch_shapes=sems,
        compiler_params=_params(("arbitrary", "arbitrary"), 56),
    )(q2, kn, vb, lsum, dmix, *send)
    return res[0], res[1], res[2], list(res[3:])


def _sgu_spatial(zn, wm_ref, lane, c):
    parts = []
    for p in range(SG_WIDTH // LANE):
        blk = zn[c * CHUNK:(c + 1) * CHUNK, p * LANE:(p + 1) * LANE].astype(BF16)
        lo = jnp.where(lane < HEAD_DIM, blk, jnp.zeros((), BF16))
        hi = jnp.where(lane >= HEAD_DIM, blk, jnp.zeros((), BF16))
        parts.append(_dot(wm_ref[2 * p], lo) + _dot(wm_ref[2 * p + 1], hi))
    return jnp.concatenate(parts, axis=1)


def _sgu_fwd(proj, gz, wm, bt, bd):
    s = proj.shape[0]
    tm = min(512, s)

    def body(u_ref, z_ref, gz_ref, wm_ref, bt_ref, bd_ref, o_ref):
        lane = lax.broadcasted_iota(jnp.int32, (CHUNK, LANE), 1)
        ug = _gelu(u_ref[...])
        zg = _gelu(z_ref[...])
        zn = zg * lax.rsqrt(_group_mean(zg * zg, bd_ref[...]) + EPS) * gz_ref[...]
        for c in range(tm // CHUNK):
            sp = _sgu_spatial(zn, wm_ref, lane, c) + bt_ref[...]
            o_ref[c * CHUNK:(c + 1) * CHUNK, :] = ug[c * CHUNK:(c + 1) * CHUNK, :] * sp

    col = lambda c: pl.BlockSpec((tm, SG_WIDTH), lambda i: (i, c))
    return pl.pallas_call(
        body, name="sgu_fwd", grid=(s // tm,),
        in_specs=[col(3), col(4), _full((1, SG_WIDTH)), _full((SG_GROUPS, CHUNK, CHUNK)), _full((CHUNK, SG_WIDTH)),
                  _full((SG_WIDTH, SG_WIDTH))],
        out_specs=pl.BlockSpec((tm, SG_WIDTH), lambda i: (i, 0)),
        out_shape=jax.ShapeDtypeStruct((s, SG_WIDTH), F32),
        compiler_params=_params(("arbitrary",), 32),
    )(proj, proj, gz, wm, bt, bd)


def _sgu_bwd(proj, dmix, gz, wm, wmt, bt, bd):
    s = proj.shape[0]
    tm = min(512, s)

    def body(u_ref, z_ref, dg_ref, gz_ref, wm_ref, wmt_ref, bt_ref, bd_ref, o_ref, dwm_ref, dbt_ref, dgz_ref):
        @pl.when(pl.program_id(0) == 0)
        def _():
            dwm_ref[...] = jnp.zeros_like(dwm_ref)
            dbt_ref[...] = jnp.zeros_like(dbt_ref)
            dgz_ref[...] = jnp.zeros_like(dgz_ref)

        lane = lax.broadcasted_iota(jnp.int32, (CHUNK, LANE), 1)
        bdv = bd_ref[...]
        u = u_ref[...]
        z = z_ref[...]
        ug = _gelu(u)
        zg = _gelu(z)
        r = lax.rsqrt(_group_mean(zg * zg, bdv) + EPS)
        zh = zg * r
        zn = zh * gz_ref[...]
        dzn_rows = []
        for c in range(tm // CHUNK):
            rows = slice(c * CHUNK, (c + 1) * CHUNK)
            sp = _sgu_spatial(zn, wm_ref, lane, c) + bt_ref[...]
            dgg = dg_ref[rows, :]
            ds = dgg * ug[rows, :]
            o_ref[rows, 0:SG_WIDTH] = dgg * sp * _gelu_grad(u[rows, :])
            dbt_ref[...] += ds
            parts = []
            for p in range(SG_WIDTH // LANE):
                dsb = ds[:, p * LANE:(p + 1) * LANE].astype(BF16)
                znb = zn[rows, p * LANE:(p + 1) * LANE].astype(BF16)
                acc = jnp.zeros((CHUNK, LANE), F32)
                for hh in range(2):
                    hm = (lane >= hh * HEAD_DIM) & (lane < (hh + 1) * HEAD_DIM)
                    dsm = jnp.where(hm, dsb, jnp.zeros((), BF16))
                    znm = jnp.where(hm, znb, jnp.zeros((), BF16))
                    acc = acc + _dot(wmt_ref[2 * p + hh], dsm)
                    dwm_ref[2 * p + hh] += lax.dot_general(dsm, znm, NT_DIMS, preferred_element_type=F32)
                parts.append(acc)
            dzn_rows.append(jnp.concatenate(parts, axis=1))
        dzn = jnp.concatenate(dzn_rows, axis=0)
        dgz_ref[...] += jnp.sum(dzn * zh, axis=0, keepdims=True)
        dn = dzn * gz_ref[...]
        o_ref[:, SG_WIDTH:2 * SG_WIDTH] = r * (dn - zh * _group_mean(dn * zh, bdv)) * _gelu_grad(z)

    col = lambda c: pl.BlockSpec((tm, SG_WIDTH), lambda i: (i, c))
    wspec = _full((SG_GROUPS, CHUNK, CHUNK))
    return pl.pallas_call(
        body, name="sgu_bwd", grid=(s // tm,),
        in_specs=[col(3), col(4), pl.BlockSpec((tm, SG_WIDTH), lambda i: (i, 1)), _full((1, SG_WIDTH)), wspec, wspec,
                  _full((CHUNK, SG_WIDTH)), _full((SG_WIDTH, SG_WIDTH))],
        out_specs=[pl.BlockSpec((tm, 2 * SG_WIDTH), lambda i: (i, 0)), wspec, _full((CHUNK, SG_WIDTH)),
                   _full((1, SG_WIDTH))],
        out_shape=[jax.ShapeDtypeStruct((s, 2 * SG_WIDTH), F32), jax.ShapeDtypeStruct((SG_GROUPS, CHUNK, CHUNK), F32),
                   jax.ShapeDtypeStruct((CHUNK, SG_WIDTH), F32), jax.ShapeDtypeStruct((1, SG_WIDTH), F32)],
        compiler_params=_params(("arbitrary",), 40),
    )(proj, proj, dmix, gz, wm, wmt, bt, bd)


def _shifted_copies(ext, sh):
    e = ext[...]
    sh[0] = e
    for b in range(1, 8):
        sh[b] = pltpu.roll(e, e.shape[0] - b, 0)


def _shifted_rows(sh, off, r0, cols):
    return sh[off % 8, pl.ds(r0 + (off - off % 8), CONV_ROWS), cols]


def _for_row_blocks(rows, block, looped):
    if not looped:
        for r0 in range(0, rows, CONV_ROWS):
            block(r0)
        return

    def step(rb, carry):
        block(pl.multiple_of(rb * CONV_ROWS, CONV_ROWS))
        return carry

    lax.fori_loop(0, rows // CONV_ROWS, step, 0)


def _dwconv(sh, w_ref, offsets, rows, store, looped):
    def block(r0):
        for c0 in range(0, sh.shape[2], CONV_LANES):
            cols = slice(c0, c0 + CONV_LANES)
            acc = jnp.zeros((CONV_ROWS, CONV_LANES), F32)
            for j, off in enumerate(offsets):
                acc = acc + w_ref[j:j + 1, cols] * _shifted_rows(sh, off, r0, cols)
            store(r0, cols, acc)

    _for_row_blocks(rows, block, looped)


def _conf_mid_fwd(p, wdw, bdw, lng, lnb):
    s = p.shape[0]
    c = p.shape[1] // 2
    tm = min(256, s)

    def body(a_ref, gt_ref, w_ref, b_ref, g_ref, beta_ref, yc_ref, y2_ref, ext, sh):
        i = pl.program_id(0)

        @pl.when(i == 0)
        def _():
            ext[0:HALO, :] = jnp.zeros((HALO, c), F32)

        @pl.when(i > 0)
        def _():
            ext[0:HALO, :] = ext[tm:tm + HALO, :]

        ext[HALO:HALO + tm, :] = a_ref[...] * _sigmoid(gt_ref[...])

        def store(r0, cols, block):
            yc_ref[pl.ds(r0, CONV_ROWS), cols] = block + b_ref[:, cols]

        _shifted_copies(ext, sh)
        _dwconv(sh, w_ref, [HALO - CONV_K + 1 + j for j in range(CONV_K)], tm, store, looped=False)
        acc = yc_ref[...]
        xc = acc - jnp.mean(acc, axis=-1, keepdims=True)
        ln = xc * lax.rsqrt(jnp.mean(xc * xc, axis=-1, keepdims=True) + EPS) * g_ref[...] + beta_ref[...]
        y2_ref[...] = (ln * _sigmoid(ln)).astype(BF16)

    vec = _full((1, c))
    return pl.pallas_call(
        body, name="conf_mid_fwd", grid=(s // tm,),
        in_specs=[pl.BlockSpec((tm, c), lambda i: (i, 0)), pl.BlockSpec((tm, c), lambda i: (i, 1)), _full((HALO, c)), vec,
                  vec, vec],
        out_specs=[pl.BlockSpec((tm, c), lambda i: (i, 0)), pl.BlockSpec((tm, c), lambda i: (i, 0))],
        out_shape=[jax.ShapeDtypeStruct((s, c), F32), jax.ShapeDtypeStruct((s, c), BF16)],
        scratch_shapes=[pltpu.VMEM((HALO + tm, c), F32), pltpu.VMEM((8, HALO + tm, c), F32)],
        compiler_params=_params(("arbitrary",), 40),
    )(p, p, wdw, bdw, lng, lnb)


def _conf_mid_bwd(p, yc, dout, x, gm, wdw, lng, lnb, w1, w2, send=()):
    s = p.shape[0]
    c = p.shape[1] // 2
    tm = min(256, s)
    n = s // tm

    def body(a_ref, gt_ref, yc_ref, dout_ref, x_ref, gm_ref, w_ref, g_ref, beta_ref, w1_hbm, w2_hbm,
             dp_ref, dw_ref, dbdw_ref, dlg_ref, dlb_ref, db1_ref, db2_ref, dx_ref, dgm_ref,
             yv, extd, dyv, dwacc, shd, w1_ref, w2_ref, wsem):
        i = pl.program_id(0)

        @pl.when(i == 0)
        def _():
            loads = [pltpu.make_async_copy(w1_hbm, w1_ref, wsem.at[0]),
                     pltpu.make_async_copy(w2_hbm, w2_ref, wsem.at[1])]
            for cp in loads:
                cp.start()
            extd[tm:tm + HALO, :] = jnp.zeros((HALO, c), F32)
            for ref in (dw_ref, dbdw_ref, dlg_ref, dlb_ref, db1_ref, db2_ref, dgm_ref, dwacc):
                ref[...] = jnp.zeros_like(ref)
            for cp in loads:
                cp.wait()

        @pl.when(i > 0)
        def _():
            extd[tm:tm + HALO, :] = extd[0:HALO, :]

        dout = dout_ref[...]
        dy2 = lax.dot_general(dout.astype(BF16), w2_ref[...], NT_DIMS, preferred_element_type=F32)
        a = a_ref[...]
        sg = _sigmoid(gt_ref[...])
        yv[...] = a * sg
        ycv = yc_ref[...]
        xc = ycv - jnp.mean(ycv, axis=-1, keepdims=True)
        rstd = lax.rsqrt(jnp.mean(xc * xc, axis=-1, keepdims=True) + EPS)
        xh = xc * rstd
        ln = xh * g_ref[...] + beta_ref[...]
        dln = dy2 * _silu_grad(ln, _sigmoid(ln))
        dlg_ref[...] += jnp.sum(dln * xh, axis=0, keepdims=True)
        dlb_ref[...] += jnp.sum(dln, axis=0, keepdims=True)
        dxh = dln * g_ref[...]
        dyc = rstd * (dxh - jnp.mean(dxh, axis=-1, keepdims=True) - xh * jnp.mean(dxh * xh, axis=-1, keepdims=True))
        extd[0:tm, :] = dyc
        dbdw_ref[...] += jnp.sum(dyc, axis=0, keepdims=True)
        db2_ref[...] += jnp.sum(dout, axis=0, keepdims=True)

        _shifted_copies(extd, shd)

        def conv_block(r0):
            for c0 in range(0, c, CONV_LANES):
                cols = slice(c0, c0 + CONV_LANES)
                ysub = yv[pl.ds(r0, CONV_ROWS), cols]
                acc = jnp.zeros((CONV_ROWS, CONV_LANES), F32)
                for j in range(CONV_K):
                    later = _shifted_rows(shd, CONV_K - 1 - j, r0, cols)
                    acc = acc + w_ref[j:j + 1, cols] * later
                    dwacc[8 * j:8 * j + 8, cols] += (ysub * later).reshape(CONV_ROWS // 8, 8, CONV_LANES).sum(axis=0)
                dyv[pl.ds(r0, CONV_ROWS), cols] = acc

        _for_row_blocks(tm, conv_block, looped=True)

        @pl.when(i == n - 1)
        def _():
            dw_ref[...] = dwacc[...].reshape(HALO, 8, c).sum(axis=1)

        dy = dyv[...]
        da = dy * sg
        dgt = dy * a * sg * (1.0 - sg)
        dab, dgtb = da.astype(BF16), dgt.astype(BF16)
        dp_ref[:, 0:c] = dab
        dp_ref[:, c:2 * c] = dgtb
        db1_ref[:, 0:c] += jnp.sum(da, axis=0, keepdims=True)
        db1_ref[:, c:2 * c] += jnp.sum(dgt, axis=0, keepdims=True)
        dh = (lax.dot_general(dab, w1_ref[:, 0:c], NT_DIMS, preferred_element_type=F32)
              + lax.dot_general(dgtb, w1_ref[:, c:2 * c], NT_DIMS, preferred_element_type=F32))
        xv = x_ref[...]
        r = lax.rsqrt(jnp.mean(xv * xv, axis=-1, keepdims=True) + EPS)
        xn = xv * r
        dgm_ref[...] += jnp.sum(dh * xn, axis=0, keepdims=True)
        dn = dh * gm_ref[...]
        dx_ref[...] = dout + r * (dn - xn * jnp.mean(dn * xn, axis=-1, keepdims=True))

    rev = lambda col: pl.BlockSpec((tm, c), lambda i: (n - 1 - i, col))
    vec = _full((1, c))
    step_is = lambda i: lambda: pl.program_id(0) == i
    kern, ins, outs, shapes, sems = _carried(body, 11, 9, send, "scatter",
                                             {"start": step_is(0), "finish": step_is(n - 1)})
    res = pl.pallas_call(
        kern, name="conf_mid_bwd", grid=(n,),
        in_specs=[rev(0), rev(1), rev(0), rev(0), rev(0), vec, _full((HALO, c)), vec, vec,
                  pl.BlockSpec(memory_space=pl.ANY), pl.BlockSpec(memory_space=pl.ANY)] + ins,
        out_specs=[pl.BlockSpec((tm, 2 * c), lambda i: (n - 1 - i, 0)), _full((HALO, c)), vec, vec, vec,
                   _full((1, 2 * c)), vec, rev(0), vec] + outs,
        out_shape=[jax.ShapeDtypeStruct((s, 2 * c), BF16), jax.ShapeDtypeStruct((HALO, c), F32),
                   jax.ShapeDtypeStruct((1, c), F32), jax.ShapeDtypeStruct((1, c), F32), jax.ShapeDtypeStruct((1, c), F32),
                   jax.ShapeDtypeStruct((1, 2 * c), F32), jax.ShapeDtypeStruct((1, c), F32),
                   jax.ShapeDtypeStruct((s, c), F32), jax.ShapeDtypeStruct((1, c), F32)] + shapes,
        scratch_shapes=[pltpu.VMEM((tm, c), F32), pltpu.VMEM((tm + HALO, c), F32), pltpu.VMEM((tm, c), F32),
                        pltpu.VMEM((8 * HALO, c), F32), pltpu.VMEM((8, tm + HALO, c), F32),
                        pltpu.VMEM((c, 2 * c), BF16), pltpu.VMEM((c, c), BF16),
                        pltpu.SemaphoreType.DMA((2,))] + sems,
        compiler_params=_params(("arbitrary",), 56),
    )(p, p, yc, dout, x, gm, wdw, lng, lnb, w1, w2, *send)
    return tuple(res[:9]) + (list(res[9:]),)


def _ffn_fwd(x, g, wup, wdw, bdw, wdn):
    s, d = x.shape
    ff = wdn.shape[0]
    tm = min(256, s)

    def body(x_ref, g_ref, wup_ref, wdw_ref, bdw_ref, wdn_ref, y_ref, h_ref, u_ref, carry):
        @pl.when(pl.program_id(0) == 0)
        def _():
            carry[...] = jnp.zeros_like(carry)

        xv = x_ref[...]
        r = lax.rsqrt(jnp.mean(xv * xv, axis=-1, keepdims=True) + EPS)
        h = (xv * r * g_ref[...]).astype(BF16)
        h_ref[...] = h
        acc = xv
        up = lambda c: (_dot(h, wup_ref[:, c:c + FF_CHUNK]), _dot(h, wup_ref[:, ff + c:ff + c + FF_CHUNK]))
        ahead = up(0)
        for c in range(0, ff, FF_CHUNK):
            cs = slice(c, c + FF_CHUNK)
            gp, val = ahead
            if c + FF_CHUNK < ff:
                ahead = up(c + FF_CHUNK)
            u_ref[:, cs] = gp.astype(BF16)
            u_ref[:, ff + c:ff + c + FF_CHUNK] = val.astype(BF16)
            prev = carry[:, cs]
            gate = (wdw_ref[0:1, cs] * _shift_down(gp, prev, 2) + wdw_ref[1:2, cs] * _shift_down(gp, prev, 1)
                    + wdw_ref[2:3, cs] * gp + bdw_ref[:, cs])
            act = gate * _sigmoid(gate) * val
            acc = acc + _dot(act.astype(BF16), wdn_ref[cs, :])
            carry[:, cs] = gp[tm - 8:tm, :]
        y_ref[...] = acc

    return pl.pallas_call(
        body, name="ffn_fwd", grid=(s // tm,),
        in_specs=[pl.BlockSpec((tm, d), lambda i: (i, 0)), _full((1, d)), _full((d, 2 * ff)), _full((8, ff)),
                  _full((1, ff)), _full((ff, d))],
        out_specs=[pl.BlockSpec((tm, d), lambda i: (i, 0)), pl.BlockSpec((tm, d), lambda i: (i, 0)),
                   pl.BlockSpec((tm, 2 * ff), lambda i: (i, 0))],
        out_shape=[jax.ShapeDtypeStruct((s, d), F32), jax.ShapeDtypeStruct((s, d), BF16),
                   jax.ShapeDtypeStruct((s, 2 * ff), BF16)],
        scratch_shapes=[pltpu.VMEM((8, ff), F32)],
        compiler_params=_params(("arbitrary",), 56),
    )(x, g, wup, wdw, bdw, wdn)


def _ffn_bwd(dy, u, x, g, wdw, bdw, wdn, wup, send=()):
    s, d = dy.shape
    ff = wdn.shape[0]
    tm = min(256, s)
    n = s // tm
    hb = tm // 16

    def body(dy_ref, u_ref, uh_ref, x_ref, g_ref, wdw_ref, bdw_ref, wdn_hbm, wup_hbm,
             du_ref, act_ref, dw_ref, db_ref, dx_ref, dg_ref, carry, wdn_ref, wup_ref, wsem):
        i = pl.program_id(0)

        @pl.when(i == 0)
        def _():
            loads = [pltpu.make_async_copy(wdn_hbm, wdn_ref, wsem.at[0]),
                     pltpu.make_async_copy(wup_hbm, wup_ref, wsem.at[1])]
            for cp in loads:
                cp.start()
            carry[...] = jnp.zeros_like(carry)
            dw_ref[...] = jnp.zeros_like(dw_ref)
            db_ref[...] = jnp.zeros_like(db_ref)
            dg_ref[...] = jnp.zeros_like(dg_ref)
            for cp in loads:
                cp.wait()

        dyv = dy_ref[...]
        dyb = dyv.astype(BF16)
        down = lambda c: lax.dot_general(dyb, wdn_ref[c:c + FF_CHUNK, :], NT_DIMS, preferred_element_type=F32)
        ahead = down(0)
        dh = jnp.zeros((tm, d), F32)
        for c in range(0, ff, FF_CHUNK):
            cs = slice(c, c + FF_CHUNK)
            vs = slice(ff + c, ff + c + FF_CHUNK)
            da = ahead
            if c + FF_CHUNK < ff:
                ahead = down(c + FF_CHUNK)
            gp = u_ref[:, cs].astype(F32)
            val = u_ref[:, vs].astype(F32)
            prev = jnp.where(i < n - 1, uh_ref[:, cs].astype(F32)[8:16], 0.0)
            g1 = _shift_down(gp, prev, 1)
            g2 = _shift_down(gp, prev, 2)
            gate = wdw_ref[0:1, cs] * g2 + wdw_ref[1:2, cs] * g1 + wdw_ref[2:3, cs] * gp + bdw_ref[:, cs]
            sg = _sigmoid(gate)
            si = gate * sg
            act_ref[:, cs] = (si * val).astype(BF16)
            dgate = da * val * _silu_grad(gate, sg)
            nxt = carry[:, cs]
            dgp = (wdw_ref[2:3, cs] * dgate + wdw_ref[1:2, cs] * _shift_up(dgate, nxt, 1)
                   + wdw_ref[0:1, cs] * _shift_up(dgate, nxt, 2)).astype(BF16)
            dval = (da * si).astype(BF16)
            du_ref[:, cs] = dgp
            du_ref[:, vs] = dval
            dh = (dh + lax.dot_general(dgp, wup_ref[:, cs], NT_DIMS, preferred_element_type=F32)
                  + lax.dot_general(dval, wup_ref[:, vs], NT_DIMS, preferred_element_type=F32))
            dw_ref[:, cs] += jnp.concatenate(
                [jnp.sum(dgate * g2, axis=0, keepdims=True), jnp.sum(dgate * g1, axis=0, keepdims=True),
                 jnp.sum(dgate * gp, axis=0, keepdims=True), jnp.zeros((5, FF_CHUNK), F32)], axis=0)
            db_ref[:, cs] += jnp.sum(dgate, axis=0, keepdims=True)
            carry[:, cs] = dgate[0:8, :]
        xv = x_ref[...]
        r = lax.rsqrt(jnp.mean(xv * xv, axis=-1, keepdims=True) + EPS)
        xh = xv * r
        dg_ref[...] += jnp.sum(dh * xh, axis=0, keepdims=True)
        dn = dh * g_ref[...]
        dx_ref[...] = dyv + r * (dn - xh * jnp.mean(dn * xh, axis=-1, keepdims=True))

    step_is = lambda i: lambda: pl.program_id(0) == i
    kern, ins, outs, shapes, sems = _carried(body, 9, 6, send, "scatter",
                                             {"start": step_is(0), "finish": step_is(n - 1)})
    rev = lambda cols: pl.BlockSpec((tm, cols), lambda i: (n - 1 - i, 0))
    any_spec = pl.BlockSpec(memory_space=pl.ANY)
    res = pl.pallas_call(
        kern, name="ffn_bwd", grid=(n,),
        in_specs=[rev(d), rev(2 * ff),
                  pl.BlockSpec((16, 2 * ff), lambda i: (jnp.maximum((n - 1 - i) * hb - 1, 0), 0)),
                  rev(d), _full((1, d)), _full((8, ff)), _full((1, ff)), any_spec, any_spec] + ins,
        out_specs=[rev(2 * ff), rev(ff), _full((8, ff)), _full((1, ff)), rev(d), _full((1, d))] + outs,
        out_shape=[jax.ShapeDtypeStruct((s, 2 * ff), BF16), jax.ShapeDtypeStruct((s, ff), BF16),
                   jax.ShapeDtypeStruct((8, ff), F32), jax.ShapeDtypeStruct((1, ff), F32),
                   jax.ShapeDtypeStruct((s, d), F32), jax.ShapeDtypeStruct((1, d), F32)] + shapes,
        scratch_shapes=[pltpu.VMEM((8, ff), F32), pltpu.VMEM((ff, d), BF16), pltpu.VMEM((d, 2 * ff), BF16),
                        pltpu.SemaphoreType.DMA((2,))] + sems,
        compiler_params=_params(("arbitrary",), 56),
    )(dy, u, u, x, g, wdw, bdw, wdn, wup, *send)
    return tuple(res[:6]) + (list(res[6:]),)


def _loss_head(y, target):
    s, d = y.shape
    tm = min(512, s)

    def body(y_ref, t_ref, l_ref, dy_ref):
        @pl.when(pl.program_id(0) == 0)
        def _():
            l_ref[...] = jnp.zeros_like(l_ref)

        err = y_ref[...] - t_ref[...]
        dy_ref[...] = err * (1.0 / d)
        l_ref[...] += 0.5 * jnp.sum(jnp.mean(err * err, axis=-1, keepdims=True), axis=0, keepdims=True)

    return pl.pallas_call(
        body, name="loss_head", grid=(s // tm,),
        in_specs=[pl.BlockSpec((tm, d), lambda i: (i, 0)), pl.BlockSpec((tm, d), lambda i: (i, 0))],
        out_specs=[_full((8, LANE)), pl.BlockSpec((tm, d), lambda i: (i, 0))],
        out_shape=[jax.ShapeDtypeStruct((8, LANE), F32), jax.ShapeDtypeStruct((s, d), F32)],
        compiler_params=_params(("arbitrary",), 32),
    )(y, target)


def _row_tile(rows, limit=512):
    for cand in range(min(limit, rows) // 16 * 16, 0, -16):
        if rows % cand == 0:
            return cand
    return rows


def _reduce_adamw(parts, w, m, v, name):
    nl = len(parts)
    _, a, b = parts[0].shape
    ta = _row_tile(a, 256)

    def body(*refs):
        p_refs = refs[:nl]
        w_ref, m_ref, v_ref, g_ref, d_ref, mo_ref, vo_ref = refs[nl:]
        for layer in range(nl):
            @pl.when(pl.program_id(0) == layer)
            def _(p_ref=p_refs[layer]):
                g = p_ref[0].astype(F32)
                for k in range(1, N_DEV):
                    g = g + p_ref[k].astype(F32)
                g_ref[0] = g

        g = g_ref[0]
        mn = ADAM_B1 * m_ref[...] + (1.0 - ADAM_B1) * g
        vn = ADAM_B2 * v_ref[...] + (1.0 - ADAM_B2) * (g * g)
        mo_ref[...] = mn
        vo_ref[...] = vn
        m_hat = mn / (1.0 - ADAM_B1 ** ADAM_STEP)
        v_hat = vn / (1.0 - ADAM_B2 ** ADAM_STEP)
        d_ref[...] = -ADAM_LR * (m_hat / (jnp.sqrt(v_hat) + ADAM_EPS) + ADAM_WD * w_ref[...])

    blk = pl.BlockSpec((1, ta, b), lambda l, i: (l, i, 0))
    part = lambda layer: pl.BlockSpec((N_DEV, ta, b), lambda l, i: (0, jnp.where(l == layer, i, 0), 0))
    return pl.pallas_call(
        body, name=name, grid=(nl, a // ta),
        in_specs=[part(layer) for layer in range(nl)] + [blk, blk, blk],
        out_specs=[blk, blk, blk, blk],
        out_shape=[jax.ShapeDtypeStruct((nl, a, b), F32)] * 4,
        compiler_params=_params(("arbitrary", "arbitrary"), 56),
    )(*parts, w, m, v)


def _unshard(w8, layer, name):
    _, _, k, n = w8.shape
    tk = _row_tile(k, 256)

    def body(w_ref, o_ref):
        for d in range(N_DEV):
            o_ref[:, d * n:(d + 1) * n] = w_ref[d, 0]

    return pl.pallas_call(
        body, name=name, grid=(k // tk,),
        in_specs=[pl.BlockSpec((N_DEV, 1, tk, n), lambda i: (0, layer, i, 0))],
        out_specs=pl.BlockSpec((tk, N_DEV * n), lambda i: (i, 0)),
        out_shape=jax.ShapeDtypeStruct((k, N_DEV * n), w8.dtype),
        compiler_params=_params(("arbitrary",), 32),
    )(w8)


def _shard_cast(g, name):
    k, n8 = g.shape
    n = n8 // N_DEV
    tk = _row_tile(k, 256)

    def body(g_ref, o_ref):
        for d in range(N_DEV):
            o_ref[d] = g_ref[:, d * n:(d + 1) * n].astype(BF16)

    return pl.pallas_call(
        body, name=name, grid=(k // tk,),
        in_specs=[pl.BlockSpec((tk, n8), lambda i: (i, 0))],
        out_specs=pl.BlockSpec((N_DEV, tk, n), lambda i: (0, i, 0)),
        out_shape=jax.ShapeDtypeStruct((N_DEV, k, n), BF16),
        compiler_params=_params(("arbitrary",), 32),
    )(g)


def _row(v):
    return v.reshape(1, -1)


def _group_ones():
    idx = jnp.arange(SB_WIDTH) // HEAD_DIM
    return (idx[:, None] == idx[None, :]).astype(BF16)


def _pad_rows(w, rows):
    return jnp.concatenate([w, jnp.zeros((rows - w.shape[0], w.shape[1]), w.dtype)], axis=0)


class _NoTraffic:
    def rest(self):
        return ()

    def install(self, wt, gathered):
        pass

    def ready(self, name, layer, grad):
        pass

    def take(self):
        return ()

    def landed(self, received):
        pass


def _local_step(x, target, wt, traffic):
    scale = HEAD_DIM ** -0.5
    bd = _group_ones()
    tril = jnp.tril(jnp.ones((CHUNK, CHUNK), dtype=bool))
    saved = []
    for i in range(DEPTH):
        j = i // 2
        lay = {"x_mix": x}
        if i % 2 == 0:
            proj, h = _rms_matmul(x, _row(wt["mix_norm_g"][i]), wt["sb_w_in"][j], jnp.zeros((1, IN_WIDTH), F32), "in_proj")
            gq = _row(jnp.tile(wt["sb_q_norm_g"][j], SB_WIDTH // HEAD_DIM)) * scale
            gk = _row(jnp.tile(wt["sb_k_norm_g"][j], SB_WIDTH // HEAD_DIM))
            qn, kn, vb = _qk_prep(proj, gq * LOG2E, gk, bd)
            o, lsum, gathered = _sb_attn_fwd(qn, kn, vb, gather=traffic.rest() if i == 0 else ())
            if i == 0:
                traffic.install(wt, gathered)
            wm = jnp.where(tril[None], wt["sg_w_spatial"][j], 0.0)
            wmb = wm.astype(BF16)
            wmt = jnp.swapaxes(wm, 1, 2).astype(BF16)
            bt = jnp.repeat(wt["sg_b_spatial"][j].T, HEAD_DIM, axis=1)
            gz = _row(wt["sg_z_norm_g"][j])
            gg = _sgu_fwd(proj, gz, wmb, bt, bd)
            x, mix = _out_proj(x, o, gg, wt["hyb_w_out"][j], "out_proj")
            lay.update(proj=proj, h=h, gq=gq, gk=gk, qn=qn, kn=kn, vb=vb, lsum=lsum, wmb=wmb, wmt=wmt, bt=bt, gz=gz, mix=mix)
        else:
            p, h = _rms_matmul(x, _row(wt["mix_norm_g"][i]), wt["cv_w_pw1"][j], _row(wt["cv_b_pw1"][j]), "conf_pw1")
            wdw = _pad_rows(wt["cv_w_dw"][j], HALO)
            yc, y2 = _conf_mid_fwd(p, wdw, _row(wt["cv_b_dw"][j]), _row(wt["cv_ln_g"][j]), _row(wt["cv_ln_b"][j]))
            x = _res_matmul(x, y2, wt["cv_w_pw2"][j], _row(wt["cv_b_pw2"][j]), "conf_pw2")
            lay.update(p=p, h=h, wdw=wdw, yc=yc, y2=y2)
        lay["x_ffn"] = x
        fdw = _pad_rows(wt["ffn_w_dw"][i], 8)
        x, hf, u = _ffn_fwd(x, _row(wt["ffn_norm_g"][i]), wt["ffn_w_up"][i], fdw, _row(wt["ffn_b_dw"][i]),
                            wt["ffn_w_down"][i])
        lay.update(hf=hf, u=u, fdw=fdw)
        saved.append(lay)

    lpart, dy = _loss_head(x, target)
    loss = lpart[0, 0]

    gr = {k: [None] * len(v) for k, v in wt.items()}

    def made(name, layer, grad):
        gr[name][layer] = grad
        traffic.ready(name, layer, grad)

    for i in reversed(range(DEPTH)):
        j = i // 2
        lay = saved[i]
        du, act, dfdw, dfb, dx, dgf, got = _ffn_bwd(
            dy, lay["u"], lay["x_ffn"], _row(wt["ffn_norm_g"][i]), lay["fdw"], _row(wt["ffn_b_dw"][i]),
            wt["ffn_w_down"][i], wt["ffn_w_up"][i], send=traffic.take())
        traffic.landed(got)
        made("ffn_w_down", i, _matmul_tn(act, dy, "ffn_dw_down"))
        made("ffn_w_up", i, _matmul_tn(lay["hf"], du, "ffn_dw_up"))
        gr["ffn_w_dw"][i] = dfdw[:FFN_K]
        gr["ffn_b_dw"][i] = dfb[0]
        gr["ffn_norm_g"][i] = dgf[0]
        dy = dx
        if i % 2 == 0:
            dmix = _matmul_nt(dy, wt["hyb_w_out"][j], "out_proj_dx")
            made("hyb_w_out", j, _matmul_tn(lay["mix"], dy, "out_proj_dw"))
            dqn, dkn, dv, got = _sb_attn_bwd(lay["qn"], lay["kn"], lay["vb"], lay["lsum"], dmix, send=traffic.take())
            traffic.landed(got)
            duz, dwm, dbt, dgz = _sgu_bwd(lay["proj"], dmix, lay["gz"], lay["wmb"], lay["wmt"], lay["bt"], bd)
            dproj, dgq, dgk = _qk_bwd(lay["proj"], dqn, dkn, dv, duz, lay["gq"], lay["gk"] * LN2, bd)
            made("sb_w_in", j, _matmul_tn(lay["h"], dproj, "in_proj_dw"))
            gr["sb_q_norm_g"][j] = dgq.reshape(SB_WIDTH // HEAD_DIM, HEAD_DIM).sum(0) * scale
            gr["sb_k_norm_g"][j] = dgk.reshape(SB_WIDTH // HEAD_DIM, HEAD_DIM).sum(0) * LN2
            gr["sg_z_norm_g"][j] = dgz[0]
            gr["sg_w_spatial"][j] = jnp.where(tril[None], dwm, 0.0)
            gr["sg_b_spatial"][j] = dbt.reshape(CHUNK, SG_GROUPS, HEAD_DIM).sum(-1).T
            dy, dgm, got = _nt_rms_bwd(dproj, wt["sb_w_in"][j], lay["x_mix"], _row(wt["mix_norm_g"][i]), dy,
                                       "in_proj_dx", send=traffic.take() if i == 0 else ())
            if i == 0:
                traffic.landed(got)
        else:
            made("cv_w_pw2", j, _matmul_tn(lay["y2"], dy, "conf_pw2_dw"))
            dp, dwdw, dbdw, dlg, dlb, db1, db2, dx, dgm, got = _conf_mid_bwd(
                lay["p"], lay["yc"], dy, lay["x_mix"], _row(wt["mix_norm_g"][i]), lay["wdw"],
                _row(wt["cv_ln_g"][j]), _row(wt["cv_ln_b"][j]), wt["cv_w_pw1"][j], wt["cv_w_pw2"][j],
                send=traffic.take())
            traffic.landed(got)
            dy = dx
            made("cv_w_pw1", j, _matmul_tn(lay["h"], dp, "conf_pw1_dw"))
            gr["cv_w_dw"][j] = dwdw[:CONV_K]
            gr["cv_b_dw"][j] = dbdw[0]
            gr["cv_ln_g"][j] = dlg[0]
            gr["cv_ln_b"][j] = dlb[0]
            gr["cv_b_pw1"][j] = db1[0]
            gr["cv_b_pw2"][j] = db2[0]
        gr["mix_norm_g"][i] = dgm[0]
    matmul_weights = ("sb_w_in", "hyb_w_out", "cv_w_pw1", "cv_w_pw2", "ffn_w_up", "ffn_w_down")
    grads = {k: (v if k in matmul_weights else jnp.stack(v)) for k, v in gr.items()}
    return loss, dy, grads


WEIGHTS = ["mix_norm_g", "sb_w_in", "sb_q_norm_g", "sb_k_norm_g", "sg_z_norm_g", "sg_w_spatial", "sg_b_spatial",
           "hyb_w_out", "cv_w_pw1", "cv_b_pw1", "cv_w_dw", "cv_b_dw", "cv_ln_g", "cv_ln_b", "cv_w_pw2", "cv_b_pw2",
           "ffn_norm_g", "ffn_w_up", "ffn_w_dw", "ffn_b_dw", "ffn_w_down"]
BIG = [("sb_w_in", "col"), ("hyb_w_out", "row"), ("cv_w_pw1", "col"), ("cv_w_pw2", "row"), ("ffn_w_up", "col"),
       ("ffn_w_down", "row")]
SMALL = ["cv_b_pw1", "cv_w_dw", "cv_b_dw", "cv_ln_g", "cv_ln_b", "cv_b_pw2", "ffn_w_dw"]
REPLICATED = ["mix_norm_g", "sb_q_norm_g", "sb_k_norm_g", "sg_z_norm_g", "sg_w_spatial", "sg_b_spatial", "ffn_norm_g",
              "ffn_b_dw"]


def _last_dim_blocks(full):
    t = jnp.moveaxis(full.reshape(full.shape[:-1] + (N_DEV, full.shape[-1] // N_DEV)), -2, 0)
    return t.reshape(N_DEV, -1)


def _from_last_dim_blocks(blocks, shard_shape):
    t = jnp.moveaxis(blocks.reshape((N_DEV,) + tuple(shard_shape)), 0, -2)
    return t.reshape(tuple(shard_shape[:-1]) + (N_DEV * shard_shape[-1],))


def _pack(arrays):
    lead = arrays[0].shape[:-1]
    flat = jnp.concatenate([a.astype(F32) for a in arrays], axis=-1)
    rows = -(-flat.shape[-1] // (16 * LANE)) * 16
    pad = rows * LANE - flat.shape[-1]
    if pad:
        flat = jnp.concatenate([flat, jnp.zeros(lead + (pad,), F32)], axis=-1)
    return flat.reshape(lead + (rows, LANE))


def _unpack(packed, shapes):
    flat = packed.reshape(-1)
    out, off = [], 0
    for shp in shapes:
        size = 1
        for dim in shp:
            size *= dim
        out.append(flat[off:off + size].reshape(shp))
        off += size
    return out


class _ShardTraffic:
    def __init__(self, w):
        self.w = w
        self.queue, self.flying, self.received = [], [], {}

    def rest(self):
        small = _pack([self.w[n].reshape(-1) for n in SMALL])
        return [self.w["sb_w_in"][1:].astype(BF16)] + [self.w[n].astype(BF16) for n, _ in BIG[1:]] + [small]

    def install(self, wt, gathered):
        wt["sb_w_in"].append(_unshard(gathered[0], 0, "unshard_sb_w_in"))
        for (n, kind), w8 in zip(BIG[1:], gathered[1:-1]):
            if kind == "col":
                wt[n] = [_unshard(w8, l, "unshard_" + n) for l in range(w8.shape[1])]
            else:
                wt[n] = [w8[:, l].reshape((N_DEV * w8.shape[2],) + w8.shape[3:]) for l in range(w8.shape[1])]
        sizes = [(self.w[n].size,) for n in SMALL]
        for n, parts in zip(SMALL, zip(*[_unpack(gathered[-1][d], sizes) for d in range(N_DEV)])):
            wt[n] = _from_last_dim_blocks(jnp.stack(parts), self.w[n].shape)

    def ready(self, name, layer, grad):
        if dict(BIG)[name] == "col":
            blocks = _shard_cast(grad, "shard_" + name)
        else:
            blocks = grad.reshape((N_DEV, grad.shape[0] // N_DEV) + grad.shape[1:])
        self.queue.append(((name, layer), blocks))

    def take(self):
        self.flying = [key for key, _ in self.queue]
        arrays = [blocks for _, blocks in self.queue]
        self.queue = []
        return arrays

    def landed(self, received):
        for key, blocks in zip(self.flying, received):
            self.received[key] = blocks
        self.flying = []


def kernel(x, mix_norm_g, sb_w_in, sb_q_norm_g, sb_k_norm_g, sg_z_norm_g, sg_w_spatial, sg_b_spatial, hyb_w_out, cv_w_pw1, cv_b_pw1, cv_w_dw, cv_b_dw, cv_ln_g, cv_ln_b, cv_w_pw2, cv_b_pw2, ffn_norm_g, ffn_w_up, ffn_w_dw, ffn_b_dw, ffn_w_down, loss_target, m_mix_norm_g, m_sb_w_in, m_sb_q_norm_g, m_sb_k_norm_g, m_sg_z_norm_g, m_sg_w_spatial, m_sg_b_spatial, m_hyb_w_out, m_cv_w_pw1, m_cv_b_pw1, m_cv_w_dw, m_cv_b_dw, m_cv_ln_g, m_cv_ln_b, m_cv_w_pw2, m_cv_b_pw2, m_ffn_norm_g, m_ffn_w_up, m_ffn_w_dw, m_ffn_b_dw, m_ffn_w_down, v_mix_norm_g, v_sb_w_in, v_sb_q_norm_g, v_sb_k_norm_g, v_sg_z_norm_g, v_sg_w_spatial, v_sg_b_spatial, v_hyb_w_out, v_cv_w_pw1, v_cv_b_pw1, v_cv_w_dw, v_cv_b_dw, v_cv_ln_g, v_cv_ln_b, v_cv_w_pw2, v_cv_b_pw2, v_ffn_norm_g, v_ffn_w_up, v_ffn_w_dw, v_ffn_b_dw, v_ffn_w_down):
    w = dict(zip(WEIGHTS, (mix_norm_g, sb_w_in, sb_q_norm_g, sb_k_norm_g, sg_z_norm_g, sg_w_spatial, sg_b_spatial,
                           hyb_w_out, cv_w_pw1, cv_b_pw1, cv_w_dw, cv_b_dw, cv_ln_g, cv_ln_b, cv_w_pw2, cv_b_pw2,
                           ffn_norm_g, ffn_w_up, ffn_w_dw, ffn_b_dw, ffn_w_down)))
    m = dict(zip(WEIGHTS, (m_mix_norm_g, m_sb_w_in, m_sb_q_norm_g, m_sb_k_norm_g, m_sg_z_norm_g, m_sg_w_spatial,
                           m_sg_b_spatial, m_hyb_w_out, m_cv_w_pw1, m_cv_b_pw1, m_cv_w_dw, m_cv_b_dw, m_cv_ln_g,
                           m_cv_ln_b, m_cv_w_pw2, m_cv_b_pw2, m_ffn_norm_g, m_ffn_w_up, m_ffn_w_dw, m_ffn_b_dw,
                           m_ffn_w_down)))
    v = dict(zip(WEIGHTS, (v_mix_norm_g, v_sb_w_in, v_sb_q_norm_g, v_sb_k_norm_g, v_sg_z_norm_g, v_sg_w_spatial,
                           v_sg_b_spatial, v_hyb_w_out, v_cv_w_pw1, v_cv_b_pw1, v_cv_w_dw, v_cv_b_dw, v_cv_ln_g,
                           v_cv_ln_b, v_cv_w_pw2, v_cv_b_pw2, v_ffn_norm_g, v_ffn_w_up, v_ffn_w_dw, v_ffn_b_dw,
                           v_ffn_w_down)))
    big_names = [n for n, _ in BIG]
    flat = lambda t, names: [t[n].reshape(-1) for n in names]

    first_in = _all_gather([w["sb_w_in"][0:1].astype(BF16)], "gather_first")[0]
    wt = {n: w[n] for n in REPLICATED}
    wt["sb_w_in"] = [_unshard(first_in, 0, "unshard_sb_w_in")]

    traffic = _ShardTraffic(w)
    loss, gx, grads = _local_step(x[0], loss_target[0], wt, traffic)
    assert not traffic.queue and not traffic.flying
    recv_big = [[traffic.received[n, l] for l in range(w[n].shape[0])] for n in big_names]

    grep = _pack(flat(grads, REPLICATED))
    gsmall = _pack([_last_dim_blocks(grads[n]) for n in SMALL])
    rep_rows, small_rows = grep.shape[0], gsmall.shape[1]
    vec = jnp.concatenate([grep, gsmall.reshape(N_DEV * small_rows, LANE)], axis=0).astype(BF16)
    vec_all = _all_gather([vec], "gather_small_grads")[0]
    me = 4 * lax.axis_index("x") + 2 * lax.axis_index("y") + lax.axis_index("c")
    recv_rep = vec_all[:, :rep_rows]
    recv_small = lax.dynamic_slice(vec_all, (0, rep_rows + small_rows * me, 0), (N_DEV, small_rows, LANE))

    out = {}
    kinds = ("grad", "delta", "new_m", "new_v")
    for n, parts in zip(big_names, recv_big):
        for kind, arr in zip(kinds, _reduce_adamw(parts, w[n], m[n], v[n], "adamw_" + n)):
            out[kind, n] = arr
    for names, recv, tag in ((SMALL, recv_small, "adamw_small"), (REPLICATED, recv_rep, "adamw_replicated")):
        res = _reduce_adamw([recv], _pack(flat(w, names))[None], _pack(flat(m, names))[None],
                            _pack(flat(v, names))[None], tag)
        shapes = [w[n].shape for n in names]
        for kind, packed in zip(kinds, res):
            for n, arr in zip(names, _unpack(packed[0], shapes)):
                out[kind, n] = arr

    loss = lax.psum(loss, ("x", "y", "c"))
    return (loss, gx[None], *[out[kind, n] for kind in kinds for n in WEIGHTS])
```

```python
import functools

import jax
import jax.numpy as jnp
from jax import lax
from jax.experimental import pallas as pl
from jax.experimental.pallas import tpu as pltpu

F32 = jnp.float32
BF16 = jnp.bfloat16

D_MODEL = 1024
HEAD_DIM = 64
SB_WIDTH = 512
SG_WIDTH = 512
SG_GROUPS = 8
IN_WIDTH = 3 * SB_WIDTH + 2 * SG_WIDTH
CHUNK = 128
CONV_K = 31
D_FF = 2816
FFN_K = 3
DEPTH = 4
EPS = 1e-6
N_DEV = 8
LANE = 128
HALO = 32
ATT_BLOCK = 256
FF_CHUNK = 256
CONV_ROWS = 32
CONV_LANES = 512
MIB = 2 ** 20

ADAM_LR = 0.001
ADAM_B1 = 0.9
ADAM_B2 = 0.999
ADAM_EPS = 1e-08
ADAM_WD = 0.01
ADAM_STEP = 10

LOG2E = 1.4426950408889634
LN2 = 0.6931471805599453

NT_DIMS = (((1,), (1,)), ((), ()))
TN_DIMS = (((0,), (0,)), ((), ()))


def _params(semantics, vmem_mib):
    return pltpu.CompilerParams(dimension_semantics=semantics, vmem_limit_bytes=vmem_mib * MIB)


def _full(shape):
    nd = len(shape)
    return pl.BlockSpec(shape, lambda *_: (0,) * nd)


def _sigmoid(x):
    return 1.0 / (1.0 + jnp.exp(-x))


def _gelu(x):
    return 0.5 * x * (1.0 + lax.erf(x * 0.7071067811865476))


def _gelu_grad(x):
    return 0.5 * (1.0 + lax.erf(x * 0.7071067811865476)) + x * jnp.exp(-0.5 * x * x) * 0.3989422804014327


def _silu_grad(x, s):
    return s * (1.0 + x * (1.0 - s))


def _dot(a, b):
    return jnp.dot(a, b, preferred_element_type=F32)


def _dot2(a, b):
    hi = a.astype(BF16)
    lo = (a - hi.astype(F32)).astype(BF16)
    return _dot(hi, b) + _dot(lo, b)


def _group_mean(t, bd):
    return _dot2(t, bd) * (1.0 / HEAD_DIM)


def _shift_down(v, prev8, s):
    top = pltpu.roll(jnp.concatenate([prev8, v[:8]], axis=0), s, 0)[8:16]
    return jnp.concatenate([top, pltpu.roll(v, s, 0)[8:]], axis=0)


def _shift_up(v, next8, s):
    n = v.shape[0]
    bottom = pltpu.roll(jnp.concatenate([v[n - 8:], next8], axis=0), 16 - s, 0)[0:8]
    return jnp.concatenate([pltpu.roll(v, n - s, 0)[: n - 8], bottom], axis=0)


def _mesh_pos():
    return lax.axis_index("x"), lax.axis_index("y"), lax.axis_index("c")


def _comm_scratch(n):
    return [pltpu.SemaphoreType.DMA((7 * n,)), pltpu.SemaphoreType.DMA((7 * n,)), pltpu.SemaphoreType.DMA((n,))]


class _Scatter:
    def __init__(self, src_refs, out_refs, send_sems, recv_sems, local_sems):
        x, y, cc = _mesh_pos()
        me = 4 * x + 2 * y + cc
        self.copies, self.mine = [], []
        for a, (src, out) in enumerate(zip(src_refs, out_refs)):
            self.mine.append(pltpu.make_async_copy(src.at[me], out.at[me], local_sems.at[a]))
            for k in range(1, N_DEV):
                px = 1 - x if k & 4 else x
                py = 1 - y if k & 2 else y
                pc = 1 - cc if k & 1 else cc
                self.copies.append(pltpu.make_async_remote_copy(
                    src_ref=src.at[4 * px + 2 * py + pc], dst_ref=out.at[me],
                    send_sem=send_sems.at[7 * a + k - 1], recv_sem=recv_sems.at[7 * a + k - 1],
                    device_id=(px, py, pc), device_id_type=pl.DeviceIdType.MESH))

    def start(self):
        for cp in self.mine + self.copies:
            cp.start()

    def finish(self):
        for cp in self.copies + self.mine:
            cp.wait()


class _Gather:
    def __init__(self, x_refs, out_refs, send_sems, recv_sems, local_sems):
        x, y, cc = _mesh_pos()
        self.n = len(x_refs)
        self.me, self.sibling, self.cc = (x, y, cc), (x, y, 1 - cc), cc
        self.chips = [(1 - x, y), (x, 1 - y), (1 - x, 1 - y)]
        self.x_refs, self.out_refs, self.send_sems, self.recv_sems = x_refs, out_refs, send_sems, recv_sems
        self.mine = [pltpu.make_async_copy(x_refs[a], out_refs[a].at[4 * x + 2 * y + cc], local_sems.at[a])
                     for a in range(self.n)]

    def copy(self, a, k, block, to, own=False):
        slot = self.out_refs[a].at[4 * block[0] + 2 * block[1] + block[2]]
        return pltpu.make_async_remote_copy(
            src_ref=self.x_refs[a] if own else slot, dst_ref=slot,
            send_sem=self.send_sems.at[7 * a + k], recv_sem=self.recv_sems.at[7 * a + k],
            device_id=to, device_id_type=pl.DeviceIdType.MESH)

    def first_hop(self, a):
        return [self.copy(a, 0, self.me, self.sibling, own=True)] + [
            self.copy(a, 1 + j, self.me, (*chip, self.cc), own=True) for j, chip in enumerate(self.chips)]

    def passed_on(self, a):
        return [self.copy(a, 4 + j, (*chip, self.cc), self.sibling) for j, chip in enumerate(self.chips)]

    def start(self):
        for a in range(self.n):
            self.mine[a].start()
        for a in range(self.n):
            for cp in self.first_hop(a):
                cp.start()

    def forward(self):
        for a in range(self.n):
            for j, chip in enumerate(self.chips):
                self.copy(a, 1 + j, (*chip, self.cc), self.me).wait_recv()
                self.copy(a, 4 + j, (*chip, self.cc), self.sibling).start()

    def finish(self):
        for a in range(self.n):
            self.copy(a, 0, self.sibling, self.me).wait_recv()
            for j, chip in enumerate(self.chips):
                self.copy(a, 4 + j, (*chip, 1 - self.cc), self.me).wait_recv()
        for a in range(self.n):
            for cp in self.first_hop(a) + self.passed_on(a):
                cp.wait_send()
        for cp in self.mine:
            cp.wait()


def _carried(body, n_in, n_out, arrays, kind, when):
    n = len(arrays)
    if n == 0:
        return body, [], [], [], []

    def wrapped(*refs):
        ins, srcs = refs[:n_in], refs[n_in:n_in + n]
        outs, landed = refs[n_in + n:n_in + n + n_out], refs[n_in + n + n_out:n_in + 2 * n + n_out]
        scratch = refs[n_in + 2 * n + n_out:]
        comm = (_Scatter if kind == "scatter" else _Gather)(srcs, landed, *scratch[-3:])
        pl.when(when["start"]())(comm.start)
        if kind == "gather":
            pl.when(when["forward"]())(comm.forward)
        body(*ins, *outs, *scratch[:-3])
        pl.when(when["finish"]())(comm.finish)

    any_spec = pl.BlockSpec(memory_space=pl.ANY)
    if kind == "scatter":
        shapes = [jax.ShapeDtypeStruct(a.shape, a.dtype) for a in arrays]
    else:
        shapes = [jax.ShapeDtypeStruct((N_DEV,) + a.shape, a.dtype) for a in arrays]
    return wrapped, [any_spec] * n, [any_spec] * n, shapes, _comm_scratch(n)


def _all_gather(shards, name):
    n = len(shards)

    def body(*refs):
        comm = _Gather(refs[:n], refs[n:2 * n], *refs[2 * n:])
        comm.start()
        comm.forward()
        comm.finish()

    any_spec = pl.BlockSpec(memory_space=pl.ANY)
    return pl.pallas_call(
        body, name=name, in_specs=[any_spec] * n, out_specs=[any_spec] * n,
        out_shape=[jax.ShapeDtypeStruct((N_DEV,) + s.shape, s.dtype) for s in shards],
        scratch_shapes=_comm_scratch(n),
    )(*shards)


def _rms_matmul(x, g, w, b, name):
    s, d = x.shape
    n = w.shape[1]
    tm = min(512, s)

    def body(x_ref, g_ref, w_ref, b_ref, y_ref, h_ref):
        xv = x_ref[...]
        r = lax.rsqrt(jnp.mean(xv * xv, axis=-1, keepdims=True) + EPS)
        h = (xv * r * g_ref[...]).astype(BF16)
        h_ref[...] = h
        for c in range(0, n, 512):
            y_ref[:, c:c + 512] = _dot(h, w_ref[:, c:c + 512]) + b_ref[:, c:c + 512]

    return pl.pallas_call(
        body, name=name, grid=(s // tm,),
        in_specs=[pl.BlockSpec((tm, d), lambda i: (i, 0)), _full((1, d)), _full((d, n)), _full((1, n))],
        out_specs=[pl.BlockSpec((tm, n), lambda i: (i, 0)), pl.BlockSpec((tm, d), lambda i: (i, 0))],
        out_shape=[jax.ShapeDtypeStruct((s, n), F32), jax.ShapeDtypeStruct((s, d), BF16)],
        compiler_params=_params(("arbitrary",), 48),
    )(x, g, w, b)


def _matmul_nt(a, w, name):
    s, n = a.shape
    k = w.shape[0]
    tm = min(512, s)

    def body(a_ref, w_ref, o_ref):
        o_ref[...] = lax.dot_general(a_ref[...].astype(BF16), w_ref[...], NT_DIMS, preferred_element_type=F32)

    return pl.pallas_call(
        body, name=name, grid=(s // tm,),
        in_specs=[pl.BlockSpec((tm, n), lambda i: (i, 0)), _full((k, n))],
        out_specs=pl.BlockSpec((tm, k), lambda i: (i, 0)),
        out_shape=jax.ShapeDtypeStruct((s, k), F32),
        compiler_params=_params(("arbitrary",), 40),
    )(a, w)


def _matmul_tn(a, b, name):
    s, k = a.shape
    n = b.shape[1]
    ts = min(2048 if k <= 1024 else 1024, s)
    tn = 1024 if (n % 1024 == 0 and k <= 1024) else 512
    steps = s // ts

    def body(a_ref, b_ref, o_ref, acc):
        t = pl.program_id(1)

        @pl.when(t == 0)
        def _():
            acc[...] = jnp.zeros_like(acc)

        acc[...] += lax.dot_general(a_ref[...].astype(BF16), b_ref[...].astype(BF16), TN_DIMS,
                                    preferred_element_type=F32)

        @pl.when(t == steps - 1)
        def _():
            o_ref[...] = acc[...].astype(BF16)

    return pl.pallas_call(
        body, name=name, grid=(n // tn, steps),
        in_specs=[pl.BlockSpec((ts, k), lambda j, t: (t, 0)), pl.BlockSpec((ts, tn), lambda j, t: (t, j))],
        out_specs=pl.BlockSpec((k, tn), lambda j, t: (0, j)),
        out_shape=jax.ShapeDtypeStruct((k, n), BF16),
        scratch_shapes=[pltpu.VMEM((k, tn), F32)],
        compiler_params=_params(("arbitrary", "arbitrary"), 48),
    )(a, b)


def _nt_rms_bwd(dp, w, x, g, dres, name, send=()):
    s, n = dp.shape
    d = x.shape[1]
    tm = min(256, s)
    steps = s // tm

    def body(dp_ref, w_ref, x_ref, g_ref, dres_ref, dx_ref, dg_ref):
        @pl.when(pl.program_id(0) == 0)
        def _():
            dg_ref[...] = jnp.zeros_like(dg_ref)

        dh = lax.dot_general(dp_ref[...], w_ref[...], NT_DIMS, preferred_element_type=F32)
        xv = x_ref[...]
        r = lax.rsqrt(jnp.mean(xv * xv, axis=-1, keepdims=True) + EPS)
        xh = xv * r
        dg_ref[...] += jnp.sum(dh * xh, axis=0, keepdims=True)
        dn = dh * g_ref[...]
        dx_ref[...] = dres_ref[...] + r * (dn - xh * jnp.mean(dn * xh, axis=-1, keepdims=True))

    step_is = lambda i: lambda: pl.program_id(0) == i
    kern, ins, outs, shapes, sems = _carried(body, 5, 2, send, "scatter",
                                             {"start": step_is(0), "finish": step_is(steps - 1)})
    res = pl.pallas_call(
        kern, name=name, grid=(steps,),
        in_specs=[pl.BlockSpec((tm, n), lambda i: (i, 0)), _full((d, n)), pl.BlockSpec((tm, d), lambda i: (i, 0)),
                  _full((1, d)), pl.BlockSpec((tm, d), lambda i: (i, 0))] + ins,
        out_specs=[pl.BlockSpec((tm, d), lambda i: (i, 0)), _full((1, d))] + outs,
        out_shape=[jax.ShapeDtypeStruct((s, d), F32), jax.ShapeDtypeStruct((1, d), F32)] + shapes,
        scratch_shapes=sems,
        compiler_params=_params(("arbitrary",), 52),
    )(dp, w, x, g, dres, *send)
    return res[0], res[1], list(res[2:])


def _res_matmul(x, a, w, b, name):
    s, d = x.shape
    k = a.shape[1]
    tm = min(512, s)

    def body(x_ref, a_ref, w_ref, b_ref, o_ref):
        o_ref[...] = x_ref[...] + _dot(a_ref[...], w_ref[...]) + b_ref[...]

    return pl.pallas_call(
        body, name=name, grid=(s // tm,),
        in_specs=[pl.BlockSpec((tm, d), lambda i: (i, 0)), pl.BlockSpec((tm, k), lambda i: (i, 0)), _full((k, d)),
                  _full((1, d))],
        out_specs=pl.BlockSpec((tm, d), lambda i: (i, 0)),
        out_shape=jax.ShapeDtypeStruct((s, d), F32),
        compiler_params=_params(("arbitrary",), 32),
    )(x, a, w, b)


def _out_proj(x, o, gg, w, name):
    s, d = x.shape
    tm = min(512, s)

    def body(x_ref, o_ref, gg_ref, w_ref, y_ref, mix_ref):
        mix = jnp.concatenate([o_ref[...], gg_ref[...]], axis=1).astype(BF16)
        mix_ref[...] = mix
        y_ref[...] = x_ref[...] + _dot(mix, w_ref[...])

    return pl.pallas_call(
        body, name=name, grid=(s // tm,),
        in_specs=[pl.BlockSpec((tm, d), lambda i: (i, 0)), pl.BlockSpec((tm, SB_WIDTH), lambda i: (i, 0)),
                  pl.BlockSpec((tm, SG_WIDTH), lambda i: (i, 0)), _full((d, d))],
        out_specs=[pl.BlockSpec((tm, d), lambda i: (i, 0)), pl.BlockSpec((tm, d), lambda i: (i, 0))],
        out_shape=[jax.ShapeDtypeStruct((s, d), F32), jax.ShapeDtypeStruct((s, d), BF16)],
        compiler_params=_params(("arbitrary",), 32),
    )(x, o, gg, w)


def _qk_prep(proj, gq, gk, bd):
    s = proj.shape[0]
    tm = min(512, s)

    def body(q_ref, k_ref, v_ref, gq_ref, gk_ref, bd_ref, qn_ref, kn_ref, vb_ref):
        bdv = bd_ref[...]
        q = q_ref[...]
        k = k_ref[...]
        qn_ref[...] = (q * lax.rsqrt(_group_mean(q * q, bdv) + EPS) * gq_ref[...]).astype(BF16)
        kn_ref[...] = (k * lax.rsqrt(_group_mean(k * k, bdv) + EPS) * gk_ref[...]).astype(BF16)
        vb_ref[...] = v_ref[...].astype(BF16)

    col = lambda c: pl.BlockSpec((tm, SB_WIDTH), lambda i: (i, c))
    out = pl.BlockSpec((tm, SB_WIDTH), lambda i: (i, 0))
    return pl.pallas_call(
        body, name="qk_prep", grid=(s // tm,),
        in_specs=[col(0), col(1), col(2), _full((1, SB_WIDTH)), _full((1, SB_WIDTH)), _full((SB_WIDTH, SB_WIDTH))],
        out_specs=[out, out, out],
        out_shape=[jax.ShapeDtypeStruct((s, SB_WIDTH), BF16)] * 3,
        compiler_params=_params(("arbitrary",), 32),
    )(proj, proj, proj, gq, gk, bd)


def _qk_bwd(proj, dqn, dkn, dv, duz, gq, gk, bd):
    s = proj.shape[0]
    tm = min(512, s)

    def body(q_ref, k_ref, dq_ref, dk_ref, dv_ref, duz_ref, gq_ref, gk_ref, bd_ref, o_ref, dgq_ref, dgk_ref):
        @pl.when(pl.program_id(0) == 0)
        def _():
            dgq_ref[...] = jnp.zeros_like(dgq_ref)
            dgk_ref[...] = jnp.zeros_like(dgk_ref)

        bdv = bd_ref[...]

        def back(t, gain, dout, dg_ref):
            r = lax.rsqrt(_group_mean(t * t, bdv) + EPS)
            th = t * r
            dg_ref[...] += jnp.sum(dout * th, axis=0, keepdims=True)
            dn = dout * gain
            return r * (dn - th * _group_mean(dn * th, bdv))

        o_ref[:, 0:SB_WIDTH] = back(q_ref[...], gq_ref[...], dq_ref[...], dgq_ref).astype(BF16)
        o_ref[:, SB_WIDTH:2 * SB_WIDTH] = back(k_ref[...], gk_ref[...], dk_ref[...], dgk_ref).astype(BF16)
        o_ref[:, 2 * SB_WIDTH:3 * SB_WIDTH] = dv_ref[...].astype(BF16)
        o_ref[:, 3 * SB_WIDTH:IN_WIDTH] = duz_ref[...].astype(BF16)

    col = lambda c: pl.BlockSpec((tm, SB_WIDTH), lambda i: (i, c))
    row = pl.BlockSpec((tm, SB_WIDTH), lambda i: (i, 0))
    return pl.pallas_call(
        body, name="qk_bwd", grid=(s // tm,),
        in_specs=[col(0), col(1), row, row, row, pl.BlockSpec((tm, 2 * SG_WIDTH), lambda i: (i, 0)),
                  _full((1, SB_WIDTH)), _full((1, SB_WIDTH)), _full((SB_WIDTH, SB_WIDTH))],
        out_specs=[pl.BlockSpec((tm, IN_WIDTH), lambda i: (i, 0)), _full((1, SB_WIDTH)), _full((1, SB_WIDTH))],
        out_shape=[jax.ShapeDtypeStruct((s, IN_WIDTH), BF16), jax.ShapeDtypeStruct((1, SB_WIDTH), F32),
                   jax.ShapeDtypeStruct((1, SB_WIDTH), F32)],
        compiler_params=_params(("arbitrary",), 40),
    )(proj, proj, dqn, dkn, dv, duz, gq, gk, bd)


def _attn_masks(tq, tk):
    lane = lax.broadcasted_iota(jnp.int32, (tk, LANE), 1)
    heads = [(lane >= hh * HEAD_DIM) & (lane < (hh + 1) * HEAD_DIM) for hh in range(2)]
    urow = lax.broadcasted_iota(jnp.int32, (tk, tk), 0)
    ucol = lax.broadcasted_iota(jnp.int32, (tk, tk), 1)
    return heads, urow, ucol


Z2_MAX = 126.0


def _keep_cost(z2):
    return jnp.log(1.0 + jnp.exp2(z2))


def _at_rows(part, row0, total):
    rows, width = part.shape
    pieces = [jnp.zeros((row0, width), part.dtype)] if row0 else []
    pieces.append(part)
    if total - row0 - rows:
        pieces.append(jnp.zeros((total - row0 - rows, width), part.dtype))
    return pieces[0] if len(pieces) == 1 else jnp.concatenate(pieces, axis=0)


def _diagonal_specs(i, r, tq, tk, left_to_right):
    if r == 2:
        specs = [(2 * i + 1, tk, tk, True), (2 * i, tk, tk, False), (2 * i, 0, tk, True)]
    else:
        specs = [(r * i + r - 1 - d, 0, tq, True) for d in range(r)]
    return specs[::-1] if left_to_right else specs


def _sb_attn_fwd(q2, kn, vb, gather=()):
    s = q2.shape[0]
    tk = min(ATT_BLOCK, s)
    tq = min(2 * ATT_BLOCK, s)
    r = tq // tk
    nq = s // tq
    assert s // tk <= LANE

    def body(q_ref, k_ref, v_ref, o_ref, ls_ref):
        i = pl.program_id(1)
        heads, urow, ucol = _attn_masks(tq, tk)
        u_incl = (urow >= ucol).astype(BF16)
        row = lax.broadcasted_iota(jnp.int32, (tq, tk), 0)
        col = lax.broadcasted_iota(jnp.int32, (tq, tk), 1)
        qlane = lax.broadcasted_iota(jnp.int32, (tq, LANE), 1)
        q = q_ref[...]

        def blocks(specs, state):
            carry, acc, ls = list(state[0:2]), state[2], list(state[3:5])
            chains = [(d, hh) for d in range(len(specs)) for hh in range(2)]
            kblk, vblk, valid = {}, {}, {}
            for d, (kb, row0, rows, masked) in enumerate(specs):
                off = pl.multiple_of(kb * tk, tk)
                kfull = k_ref[pl.ds(off, tk), :]
                vfull = v_ref[pl.ds(off, tk), :]
                if masked:
                    valid[d] = (kb * tk + col[:rows]) < (i * tq + row0 + row[:rows])
                for hh in range(2):
                    kblk[d, hh] = jnp.where(heads[hh], kfull, jnp.zeros((), BF16))
                    vblk[d, hh] = jnp.where(heads[hh], vfull, jnp.zeros((), BF16))
            z2 = {(d, hh): jnp.minimum(lax.dot_general(q[specs[d][1]:specs[d][1] + specs[d][2]], kblk[d, hh], NT_DIMS,
                                                       preferred_element_type=F32), Z2_MAX) for d, hh in chains}
            cost = {}
            for d, hh in chains:
                cost[d, hh] = _keep_cost(z2[d, hh])
                if specs[d][3]:
                    cost[d, hh] = jnp.where(valid[d], cost[d, hh], 0.0)
            sums = {c: _dot(cost[c].astype(BF16), u_incl) for c in chains}
            a = {}
            for d, hh in chains:
                kb, row0, rows, masked = specs[d]
                rin = carry[hh][row0:row0 + rows] + sums[d, hh]
                a[d, hh] = jnp.exp2(z2[d, hh] - rin * LOG2E)
                if masked:
                    a[d, hh] = jnp.where(valid[d], a[d, hh], 0.0)
                rs = jnp.sum(_at_rows(cost[d, hh], row0, tq), axis=1, keepdims=True)
                ls[hh] = ls[hh] + jnp.where(qlane == kb, rs, 0.0)
                carry[hh] = carry[hh] + rs
            for d, hh in chains:
                acc = acc + _at_rows(_dot(a[d, hh].astype(BF16), vblk[d, hh]), specs[d][1], tq)
            return carry[0], carry[1], acc, ls[0], ls[1]

        zc = jnp.zeros((tq, 1), F32)
        zt = jnp.zeros((tq, LANE), F32)
        state = blocks(_diagonal_specs(i, r, tq, tk, False), (zc, zc, zt, zt, zt))
        pairs = lax.shift_right_logical(i, 1)
        state = lax.fori_loop(
            0, pairs,
            lambda n, st: blocks([(r * i - 1 - 2 * r * n - d, 0, tq, False) for d in range(2 * r)], st), state)
        state = lax.fori_loop(
            0, i - 2 * pairs, lambda n, st: blocks([(r - 1 - d, 0, tq, False) for d in range(r)], st), state)
        o_ref[...] = state[2]
        ls_ref[:, 0:LANE] = state[3]
        ls_ref[:, LANE:2 * LANE] = state[4]

    last_pair = SB_WIDTH // LANE - 1
    step_is = lambda p, i: lambda: (pl.program_id(0) == p) & (pl.program_id(1) == i)
    when = {"start": step_is(0, 0), "forward": step_is(last_pair, 0), "finish": step_is(last_pair, nq - 1)}
    kern, ins, outs, shapes, sems = _carried(body, 3, 2, gather, "gather", when)
    res = pl.pallas_call(
        kern, name="sb_attn_fwd", grid=(SB_WIDTH // LANE, nq),
        in_specs=[pl.BlockSpec((tq, LANE), lambda p, i: (i, p)), pl.BlockSpec((s, LANE), lambda p, i: (0, p)),
                  pl.BlockSpec((s, LANE), lambda p, i: (0, p))] + ins,
        out_specs=[pl.BlockSpec((tq, LANE), lambda p, i: (i, p)),
                   pl.BlockSpec((tq, 2 * LANE), lambda p, i: (i, p))] + outs,
        out_shape=[jax.ShapeDtypeStruct((s, SB_WIDTH), F32), jax.ShapeDtypeStruct((s, 2 * SB_WIDTH), F32)] + shapes,
        scratch_shapes=sems,
        compiler_params=_params(("arbitrary", "arbitrary"), 48),
    )(q2, kn, vb, *gather)
    return res[0], res[1], list(res[2:])


def _sb_attn_bwd(q2, kn, vb, lsum, dmix, send=()):
    s = q2.shape[0]
    tk = min(ATT_BLOCK, s)
    tq = min(2 * ATT_BLOCK, s)
    r = tq // tk
    nq = s // tq

    def body(q_ref, k_ref, v_ref, ls_ref, do_ref, dq_ref, dk_ref, dv_ref):
        i = pl.program_id(1)

        @pl.when(i == 0)
        def _():
            dk_ref[...] = jnp.zeros_like(dk_ref)
            dv_ref[...] = jnp.zeros_like(dv_ref)

        heads, urow, ucol = _attn_masks(tq, tk)
        u_incl = (urow >= ucol).astype(BF16)
        u_pre = (urow <= ucol).astype(BF16)
        lrow = lax.broadcasted_iota(jnp.int32, (LANE, LANE), 0)
        lcol = lax.broadcasted_iota(jnp.int32, (LANE, LANE), 1)
        u_after = (lrow > lcol).astype(BF16)
        row = lax.broadcasted_iota(jnp.int32, (tq, tk), 0)
        col = lax.broadcasted_iota(jnp.int32, (tq, tk), 1)
        qlane = lax.broadcasted_iota(jnp.int32, (tq, LANE), 1)
        qheads = [(qlane >= hh * HEAD_DIM) & (qlane < (hh + 1) * HEAD_DIM) for hh in range(2)]
        q = q_ref[...]
        dob = do_ref[...].astype(BF16)
        qm = [jnp.where(qheads[hh], q, jnp.zeros((), BF16)) for hh in range(2)]
        dom = [jnp.where(qheads[hh], dob, jnp.zeros((), BF16)) for hh in range(2)]
        after = []
        for hh in range(2):
            ls = ls_ref[:, hh * LANE:(hh + 1) * LANE]
            hi = ls.astype(BF16)
            mid = (ls - hi.astype(F32)).astype(BF16)
            lo = (ls - hi.astype(F32) - mid.astype(F32)).astype(BF16)
            after.append(_dot(hi, u_after) + _dot(mid, u_after) + _dot(lo, u_after))

        def blocks(specs, state):
            cp, dq = list(state[0:2]), state[2]
            chains = [(d, hh) for d in range(len(specs)) for hh in range(2)]
            rows_of = lambda d: slice(specs[d][1], specs[d][1] + specs[d][2])
            offs, kblk, vblk, valid = {}, {}, {}, {}
            for d, (kb, row0, rows, masked) in enumerate(specs):
                offs[d] = pl.multiple_of(kb * tk, tk)
                kfull = k_ref[pl.ds(offs[d], tk), :]
                vblk[d] = v_ref[pl.ds(offs[d], tk), :]
                if masked:
                    valid[d] = (kb * tk + col[:rows]) < (i * tq + row0 + row[:rows])
                for hh in range(2):
                    kblk[d, hh] = jnp.where(heads[hh], kfull, jnp.zeros((), BF16))
            z2 = {(d, hh): jnp.minimum(lax.dot_general(q[rows_of(d)], kblk[d, hh], NT_DIMS, preferred_element_type=F32),
                                       Z2_MAX) for d, hh in chains}
            da = {(d, hh): lax.dot_general(dom[hh][rows_of(d)], vblk[d], NT_DIMS, preferred_element_type=F32)
                  for d, hh in chains}
            cost, sig = {}, {}
            for d, hh in chains:
                cost[d, hh] = _keep_cost(z2[d, hh])
                sig[d, hh] = jnp.exp2(z2[d, hh] - cost[d, hh] * LOG2E)
                if specs[d][3]:
                    cost[d, hh] = jnp.where(valid[d], cost[d, hh], 0.0)
            sums = {c: _dot(cost[c].astype(BF16), u_incl) for c in chains}
            a, g = {}, {}
            for d, hh in chains:
                kb, row0, rows, masked = specs[d]
                cr = jnp.sum(jnp.where(qlane[:rows] == kb, after[hh][rows_of(d)], 0.0), axis=1, keepdims=True)
                a[d, hh] = jnp.exp2(z2[d, hh] - (cr + sums[d, hh]) * LOG2E)
                if masked:
                    a[d, hh] = jnp.where(valid[d], a[d, hh], 0.0)
                g[d, hh] = da[d, hh] * a[d, hh]
            pre = {c: _dot(g[c].astype(BF16), u_pre) for c in chains}
            dzb = {}
            for d, hh in chains:
                dz = g[d, hh] - sig[d, hh] * (cp[hh][rows_of(d)] + pre[d, hh])
                if specs[d][3]:
                    dz = jnp.where(valid[d], dz, 0.0)
                dzb[d, hh] = dz.astype(BF16)
                cp[hh] = cp[hh] + _at_rows(jnp.sum(g[d, hh], axis=1, keepdims=True), specs[d][1], tq)
            for d in range(len(specs)):
                dv_ref[pl.ds(offs[d], tk), :] += sum(
                    lax.dot_general(a[d, hh].astype(BF16), dom[hh][rows_of(d)], TN_DIMS, preferred_element_type=F32)
                    for hh in range(2))
                dk_ref[pl.ds(offs[d], tk), :] += sum(
                    lax.dot_general(dzb[d, hh], qm[hh][rows_of(d)], TN_DIMS, preferred_element_type=F32)
                    for hh in range(2))
            for d, hh in chains:
                dq = dq + _at_rows(_dot(dzb[d, hh], kblk[d, hh]), specs[d][1], tq)
            return cp[0], cp[1], dq

        zc = jnp.zeros((tq, 1), F32)
        pairs = lax.shift_right_logical(i, 1)
        state = lax.fori_loop(
            0, pairs, lambda n, st: blocks([(2 * r * n + d, 0, tq, False) for d in range(2 * r)], st),
            (zc, zc, jnp.zeros((tq, LANE), F32)))
        state = lax.fori_loop(
            0, i - 2 * pairs, lambda n, st: blocks([(2 * r * pairs + d, 0, tq, False) for d in range(r)], st), state)
        state = blocks(_diagonal_specs(i, r, tq, tk, True), state)
        dq_ref[...] = state[2]

    blk = pl.BlockSpec((tq, LANE), lambda p, i: (i, p))
    whole = pl.BlockSpec((s, LANE), lambda p, i: (0, p))
    last_pair = SB_WIDTH // LANE - 1
    step_is = lambda p, i: lambda: (pl.program_id(0) == p) & (pl.program_id(1) == i)
    kern, ins, outs, shapes, sems = _carried(body, 5, 3, send, "scatter",
                                             {"start": step_is(0, 0), "finish": step_is(last_pair, nq - 1)})
    res = pl.pallas_call(
        kern, name="sb_attn_bwd", grid=(SB_WIDTH // LANE, nq),
        in_specs=[blk, whole, whole, pl.BlockSpec((tq, 2 * LANE), lambda p, i: (i, p)), blk] + ins,
        out_specs=[blk, whole, whole] + outs,
        out_shape=[jax.ShapeDtypeStruct((s, SB_WIDTH), F32)] * 3 + shapes,
        scratch_shapes=sems,
        compiler_params=_params(("arbitrary", "arbitrary"), 56),
    )(q2, kn, vb, lsum, dmix, *send)
    return res[0], res[1], res[2], list(res[3:])


def _sgu_spatial(zn, wm_ref, lane, c):
    parts = []
    for p in range(SG_WIDTH // LANE):
        blk = zn[c * CHUNK:(c + 1) * CHUNK, p * LANE:(p + 1) * LANE].astype(BF16)
        lo = jnp.where(lane < HEAD_DIM, blk, jnp.zeros((), BF16))
        hi = jnp.where(lane >= HEAD_DIM, blk, jnp.zeros((), BF16))
        parts.append(_dot(wm_ref[2 * p], lo) + _dot(wm_ref[2 * p + 1], hi))
    return jnp.concatenate(parts, axis=1)


def _sgu_fwd(proj, gz, wm, bt, bd):
    s = proj.shape[0]
    tm = min(512, s)

    def body(u_ref, z_ref, gz_ref, wm_ref, bt_ref, bd_ref, o_ref):
        lane = lax.broadcasted_iota(jnp.int32, (CHUNK, LANE), 1)
        ug = _gelu(u_ref[...])
        zg = _gelu(z_ref[...])
        zn = zg * lax.rsqrt(_group_mean(zg * zg, bd_ref[...]) + EPS) * gz_ref[...]
        for c in range(tm // CHUNK):
            sp = _sgu_spatial(zn, wm_ref, lane, c) + bt_ref[...]
            o_ref[c * CHUNK:(c + 1) * CHUNK, :] = ug[c * CHUNK:(c + 1) * CHUNK, :] * sp

    col = lambda c: pl.BlockSpec((tm, SG_WIDTH), lambda i: (i, c))
    return pl.pallas_call(
        body, name="sgu_fwd", grid=(s // tm,),
        in_specs=[col(3), col(4), _full((1, SG_WIDTH)), _full((SG_GROUPS, CHUNK, CHUNK)), _full((CHUNK, SG_WIDTH)),
                  _full((SG_WIDTH, SG_WIDTH))],
        out_specs=pl.BlockSpec((tm, SG_WIDTH), lambda i: (i, 0)),
        out_shape=jax.ShapeDtypeStruct((s, SG_WIDTH), F32),
        compiler_params=_params(("arbitrary",), 32),
    )(proj, proj, gz, wm, bt, bd)


def _sgu_bwd(proj, dmix, gz, wm, wmt, bt, bd):
    s = proj.shape[0]
    tm = min(512, s)

    def body(u_ref, z_ref, dg_ref, gz_ref, wm_ref, wmt_ref, bt_ref, bd_ref, o_ref, dwm_ref, dbt_ref, dgz_ref):
        @pl.when(pl.program_id(0) == 0)
        def _():
            dwm_ref[...] = jnp.zeros_like(dwm_ref)
            dbt_ref[...] = jnp.zeros_like(dbt_ref)
            dgz_ref[...] = jnp.zeros_like(dgz_ref)

        lane = lax.broadcasted_iota(jnp.int32, (CHUNK, LANE), 1)
        bdv = bd_ref[...]
        u = u_ref[...]
        z = z_ref[...]
        ug = _gelu(u)
        zg = _gelu(z)
        r = lax.rsqrt(_group_mean(zg * zg, bdv) + EPS)
        zh = zg * r
        zn = zh * gz_ref[...]
        dzn_rows = []
        for c in range(tm // CHUNK):
            rows = slice(c * CHUNK, (c + 1) * CHUNK)
            sp = _sgu_spatial(zn, wm_ref, lane, c) + bt_ref[...]
            dgg = dg_ref[rows, :]
            ds = dgg * ug[rows, :]
            o_ref[rows, 0:SG_WIDTH] = dgg * sp * _gelu_grad(u[rows, :])
            dbt_ref[...] += ds
            parts = []
            for p in range(SG_WIDTH // LANE):
                dsb = ds[:, p * LANE:(p + 1) * LANE].astype(BF16)
                znb = zn[rows, p * LANE:(p + 1) * LANE].astype(BF16)
                acc = jnp.zeros((CHUNK, LANE), F32)
                for hh in range(2):
                    hm = (lane >= hh * HEAD_DIM) & (lane < (hh + 1) * HEAD_DIM)
                    dsm = jnp.where(hm, dsb, jnp.zeros((), BF16))
                    znm = jnp.where(hm, znb, jnp.zeros((), BF16))
                    acc = acc + _dot(wmt_ref[2 * p + hh], dsm)
                    dwm_ref[2 * p + hh] += lax.dot_general(dsm, znm, NT_DIMS, preferred_element_type=F32)
                parts.append(acc)
            dzn_rows.append(jnp.concatenate(parts, axis=1))
        dzn = jnp.concatenate(dzn_rows, axis=0)
        dgz_ref[...] += jnp.sum(dzn * zh, axis=0, keepdims=True)
        dn = dzn * gz_ref[...]
        o_ref[:, SG_WIDTH:2 * SG_WIDTH] = r * (dn - zh * _group_mean(dn * zh, bdv)) * _gelu_grad(z)

    col = lambda c: pl.BlockSpec((tm, SG_WIDTH), lambda i: (i, c))
    wspec = _full((SG_GROUPS, CHUNK, CHUNK))
    return pl.pallas_call(
        body, name="sgu_bwd", grid=(s // tm,),
        in_specs=[col(3), col(4), pl.BlockSpec((tm, SG_WIDTH), lambda i: (i, 1)), _full((1, SG_WIDTH)), wspec, wspec,
                  _full((CHUNK, SG_WIDTH)), _full((SG_WIDTH, SG_WIDTH))],
        out_specs=[pl.BlockSpec((tm, 2 * SG_WIDTH), lambda i: (i, 0)), wspec, _full((CHUNK, SG_WIDTH)),
                   _full((1, SG_WIDTH))],
        out_shape=[jax.ShapeDtypeStruct((s, 2 * SG_WIDTH), F32), jax.ShapeDtypeStruct((SG_GROUPS, CHUNK, CHUNK), F32),
                   jax.ShapeDtypeStruct((CHUNK, SG_WIDTH), F32), jax.ShapeDtypeStruct((1, SG_WIDTH), F32)],
        compiler_params=_params(("arbitrary",), 40),
    )(proj, proj, dmix, gz, wm, wmt, bt, bd)


def _shifted_copies(ext, sh):
    e = ext[...]
    sh[0] = e
    for b in range(1, 8):
        sh[b] = pltpu.roll(e, e.shape[0] - b, 0)


def _shifted_rows(sh, off, r0, cols):
    return sh[off % 8, pl.ds(r0 + (off - off % 8), CONV_ROWS), cols]


def _for_row_blocks(rows, block, looped):
    if not looped:
        for r0 in range(0, rows, CONV_ROWS):
            block(r0)
        return

    def step(rb, carry):
        block(pl.multiple_of(rb * CONV_ROWS, CONV_ROWS))
        return carry

    lax.fori_loop(0, rows // CONV_ROWS, step, 0)


def _dwconv(sh, w_ref, offsets, rows, store, looped):
    def block(r0):
        for c0 in range(0, sh.shape[2], CONV_LANES):
            cols = slice(c0, c0 + CONV_LANES)
            acc = jnp.zeros((CONV_ROWS, CONV_LANES), F32)
            for j, off in enumerate(offsets):
                acc = acc + w_ref[j:j + 1, cols] * _shifted_rows(sh, off, r0, cols)
            store(r0, cols, acc)

    _for_row_blocks(rows, block, looped)


def _conf_mid_fwd(p, wdw, bdw, lng, lnb):
    s = p.shape[0]
    c = p.shape[1] // 2
    tm = min(256, s)

    def body(a_ref, gt_ref, w_ref, b_ref, g_ref, beta_ref, yc_ref, y2_ref, ext, sh):
        i = pl.program_id(0)

        @pl.when(i == 0)
        def _():
            ext[0:HALO, :] = jnp.zeros((HALO, c), F32)

        @pl.when(i > 0)
        def _():
            ext[0:HALO, :] = ext[tm:tm + HALO, :]

        ext[HALO:HALO + tm, :] = a_ref[...] * _sigmoid(gt_ref[...])

        def store(r0, cols, block):
            yc_ref[pl.ds(r0, CONV_ROWS), cols] = block + b_ref[:, cols]

        _shifted_copies(ext, sh)
        _dwconv(sh, w_ref, [HALO - CONV_K + 1 + j for j in range(CONV_K)], tm, store, looped=False)
        acc = yc_ref[...]
        xc = acc - jnp.mean(acc, axis=-1, keepdims=True)
        ln = xc * lax.rsqrt(jnp.mean(xc * xc, axis=-1, keepdims=True) + EPS) * g_ref[...] + beta_ref[...]
        y2_ref[...] = (ln * _sigmoid(ln)).astype(BF16)

    vec = _full((1, c))
    return pl.pallas_call(
        body, name="conf_mid_fwd", grid=(s // tm,),
        in_specs=[pl.BlockSpec((tm, c), lambda i: (i, 0)), pl.BlockSpec((tm, c), lambda i: (i, 1)), _full((HALO, c)), vec,
                  vec, vec],
        out_specs=[pl.BlockSpec((tm, c), lambda i: (i, 0)), pl.BlockSpec((tm, c), lambda i: (i, 0))],
        out_shape=[jax.ShapeDtypeStruct((s, c), F32), jax.ShapeDtypeStruct((s, c), BF16)],
        scratch_shapes=[pltpu.VMEM((HALO + tm, c), F32), pltpu.VMEM((8, HALO + tm, c), F32)],
        compiler_params=_params(("arbitrary",), 40),
    )(p, p, wdw, bdw, lng, lnb)


def _conf_mid_bwd(p, yc, dout, x, gm, wdw, lng, lnb, w1, w2, send=()):
    s = p.shape[0]
    c = p.shape[1] // 2
    tm = min(256, s)
    n = s // tm

    def body(a_ref, gt_ref, yc_ref, dout_ref, x_ref, gm_ref, w_ref, g_ref, beta_ref, w1_hbm, w2_hbm,
             dp_ref, dw_ref, dbdw_ref, dlg_ref, dlb_ref, db1_ref, db2_ref, dx_ref, dgm_ref,
             yv, extd, dyv, dwacc, shd, w1_ref, w2_ref, wsem):
        i = pl.program_id(0)

        @pl.when(i == 0)
        def _():
            loads = [pltpu.make_async_copy(w1_hbm, w1_ref, wsem.at[0]),
                     pltpu.make_async_copy(w2_hbm, w2_ref, wsem.at[1])]
            for cp in loads:
                cp.start()
            extd[tm:tm + HALO, :] = jnp.zeros((HALO, c), F32)
            for ref in (dw_ref, dbdw_ref, dlg_ref, dlb_ref, db1_ref, db2_ref, dgm_ref, dwacc):
                ref[...] = jnp.zeros_like(ref)
            for cp in loads:
                cp.wait()

        @pl.when(i > 0)
        def _():
            extd[tm:tm + HALO, :] = extd[0:HALO, :]

        dout = dout_ref[...]
        dy2 = lax.dot_general(dout.astype(BF16), w2_ref[...], NT_DIMS, preferred_element_type=F32)
        a = a_ref[...]
        sg = _sigmoid(gt_ref[...])
        yv[...] = a * sg
        ycv = yc_ref[...]
        xc = ycv - jnp.mean(ycv, axis=-1, keepdims=True)
        rstd = lax.rsqrt(jnp.mean(xc * xc, axis=-1, keepdims=True) + EPS)
        xh = xc * rstd
        ln = xh * g_ref[...] + beta_ref[...]
        dln = dy2 * _silu_grad(ln, _sigmoid(ln))
        dlg_ref[...] += jnp.sum(dln * xh, axis=0, keepdims=True)
        dlb_ref[...] += jnp.sum(dln, axis=0, keepdims=True)
        dxh = dln * g_ref[...]
        dyc = rstd * (dxh - jnp.mean(dxh, axis=-1, keepdims=True) - xh * jnp.mean(dxh * xh, axis=-1, keepdims=True))
        extd[0:tm, :] = dyc
        dbdw_ref[...] += jnp.sum(dyc, axis=0, keepdims=True)
        db2_ref[...] += jnp.sum(dout, axis=0, keepdims=True)

        _shifted_copies(extd, shd)

        def conv_block(r0):
            for c0 in range(0, c, CONV_LANES):
                cols = slice(c0, c0 + CONV_LANES)
                ysub = yv[pl.ds(r0, CONV_ROWS), cols]
                acc = jnp.zeros((CONV_ROWS, CONV_LANES), F32)
                for j in range(CONV_K):
                    later = _shifted_rows(shd, CONV_K - 1 - j, r0, cols)
                    acc = acc + w_ref[j:j + 1, cols] * later
                    dwacc[8 * j:8 * j + 8, cols] += (ysub * later).reshape(CONV_ROWS // 8, 8, CONV_LANES).sum(axis=0)
                dyv[pl.ds(r0, CONV_ROWS), cols] = acc

        _for_row_blocks(tm, conv_block, looped=True)

        @pl.when(i == n - 1)
        def _():
            dw_ref[...] = dwacc[...].reshape(HALO, 8, c).sum(axis=1)

        dy = dyv[...]
        da = dy * sg
        dgt = dy * a * sg * (1.0 - sg)
        dab, dgtb = da.astype(BF16), dgt.astype(BF16)
        dp_ref[:, 0:c] = dab
        dp_ref[:, c:2 * c] = dgtb
        db1_ref[:, 0:c] += jnp.sum(da, axis=0, keepdims=True)
        db1_ref[:, c:2 * c] += jnp.sum(dgt, axis=0, keepdims=True)
        dh = (lax.dot_general(dab, w1_ref[:, 0:c], NT_DIMS, preferred_element_type=F32)
              + lax.dot_general(dgtb, w1_ref[:, c:2 * c], NT_DIMS, preferred_element_type=F32))
        xv = x_ref[...]
        r = lax.rsqrt(jnp.mean(xv * xv, axis=-1, keepdims=True) + EPS)
        xn = xv * r
        dgm_ref[...] += jnp.sum(dh * xn, axis=0, keepdims=True)
        dn = dh * gm_ref[...]
        dx_ref[...] = dout + r * (dn - xn * jnp.mean(dn * xn, axis=-1, keepdims=True))

    rev = lambda col: pl.BlockSpec((tm, c), lambda i: (n - 1 - i, col))
    vec = _full((1, c))
    step_is = lambda i: lambda: pl.program_id(0) == i
    kern, ins, outs, shapes, sems = _carried(body, 11, 9, send, "scatter",
                                             {"start": step_is(0), "finish": step_is(n - 1)})
    res = pl.pallas_call(
        kern, name="conf_mid_bwd", grid=(n,),
        in_specs=[rev(0), rev(1), rev(0), rev(0), rev(0), vec, _full((HALO, c)), vec, vec,
                  pl.BlockSpec(memory_space=pl.ANY), pl.BlockSpec(memory_space=pl.ANY)] + ins,
        out_specs=[pl.BlockSpec((tm, 2 * c), lambda i: (n - 1 - i, 0)), _full((HALO, c)), vec, vec, vec,
                   _full((1, 2 * c)), vec, rev(0), vec] + outs,
        out_shape=[jax.ShapeDtypeStruct((s, 2 * c), BF16), jax.ShapeDtypeStruct((HALO, c), F32),
                   jax.ShapeDtypeStruct((1, c), F32), jax.ShapeDtypeStruct((1, c), F32), jax.ShapeDtypeStruct((1, c), F32),
                   jax.ShapeDtypeStruct((1, 2 * c), F32), jax.ShapeDtypeStruct((1, c), F32),
                   jax.ShapeDtypeStruct((s, c), F32), jax.ShapeDtypeStruct((1, c), F32)] + shapes,
        scratch_shapes=[pltpu.VMEM((tm, c), F32), pltpu.VMEM((tm + HALO, c), F32), pltpu.VMEM((tm, c), F32),
                        pltpu.VMEM((8 * HALO, c), F32), pltpu.VMEM((8, tm + HALO, c), F32),
                        pltpu.VMEM((c, 2 * c), BF16), pltpu.VMEM((c, c), BF16),
                        pltpu.SemaphoreType.DMA((2,))] + sems,
        compiler_params=_params(("arbitrary",), 56),
    )(p, p, yc, dout, x, gm, wdw, lng, lnb, w1, w2, *send)
    return tuple(res[:9]) + (list(res[9:]),)


def _ffn_fwd(x, g, wup, wdw, bdw, wdn):
    s, d = x.shape
    ff = wdn.shape[0]
    tm = min(256, s)

    def body(x_ref, g_ref, wup_ref, wdw_ref, bdw_ref, wdn_ref, y_ref, h_ref, u_ref, carry):
        @pl.when(pl.program_id(0) == 0)
        def _():
            carry[...] = jnp.zeros_like(carry)

        xv = x_ref[...]
        r = lax.rsqrt(jnp.mean(xv * xv, axis=-1, keepdims=True) + EPS)
        h = (xv * r * g_ref[...]).astype(BF16)
        h_ref[...] = h
        acc = xv
        up = lambda c: (_dot(h, wup_ref[:, c:c + FF_CHUNK]), _dot(h, wup_ref[:, ff + c:ff + c + FF_CHUNK]))
        ahead = up(0)
        for c in range(0, ff, FF_CHUNK):
            cs = slice(c, c + FF_CHUNK)
            gp, val = ahead
            if c + FF_CHUNK < ff:
                ahead = up(c + FF_CHUNK)
            u_ref[:, cs] = gp.astype(BF16)
            u_ref[:, ff + c:ff + c + FF_CHUNK] = val.astype(BF16)
            prev = carry[:, cs]
            gate = (wdw_ref[0:1, cs] * _shift_down(gp, prev, 2) + wdw_ref[1:2, cs] * _shift_down(gp, prev, 1)
                    + wdw_ref[2:3, cs] * gp + bdw_ref[:, cs])
            act = gate * _sigmoid(gate) * val
            acc = acc + _dot(act.astype(BF16), wdn_ref[cs, :])
            carry[:, cs] = gp[tm - 8:tm, :]
        y_ref[...] = acc

    return pl.pallas_call(
        body, name="ffn_fwd", grid=(s // tm,),
        in_specs=[pl.BlockSpec((tm, d), lambda i: (i, 0)), _full((1, d)), _full((d, 2 * ff)), _full((8, ff)),
                  _full((1, ff)), _full((ff, d))],
        out_specs=[pl.BlockSpec((tm, d), lambda i: (i, 0)), pl.BlockSpec((tm, d), lambda i: (i, 0)),
                   pl.BlockSpec((tm, 2 * ff), lambda i: (i, 0))],
        out_shape=[jax.ShapeDtypeStruct((s, d), F32), jax.ShapeDtypeStruct((s, d), BF16),
                   jax.ShapeDtypeStruct((s, 2 * ff), BF16)],
        scratch_shapes=[pltpu.VMEM((8, ff), F32)],
        compiler_params=_params(("arbitrary",), 56),
    )(x, g, wup, wdw, bdw, wdn)


def _ffn_bwd(dy, u, x, g, wdw, bdw, wdn, wup, send=()):
    s, d = dy.shape
    ff = wdn.shape[0]
    tm = min(256, s)
    n = s // tm
    hb = tm // 16

    def body(dy_ref, u_ref, uh_ref, x_ref, g_ref, wdw_ref, bdw_ref, wdn_hbm, wup_hbm,
             du_ref, act_ref, dw_ref, db_ref, dx_ref, dg_ref, carry, wdn_ref, wup_ref, wsem):
        i = pl.program_id(0)

        @pl.when(i == 0)
        def _():
            loads = [pltpu.make_async_copy(wdn_hbm, wdn_ref, wsem.at[0]),
                     pltpu.make_async_copy(wup_hbm, wup_ref, wsem.at[1])]
            for cp in loads:
                cp.start()
            carry[...] = jnp.zeros_like(carry)
            dw_ref[...] = jnp.zeros_like(dw_ref)
            db_ref[...] = jnp.zeros_like(db_ref)
            dg_ref[...] = jnp.zeros_like(dg_ref)
            for cp in loads:
                cp.wait()

        dyv = dy_ref[...]
        dyb = dyv.astype(BF16)
        down = lambda c: lax.dot_general(dyb, wdn_ref[c:c + FF_CHUNK, :], NT_DIMS, preferred_element_type=F32)
        ahead = down(0)
        dh = jnp.zeros((tm, d), F32)
        for c in range(0, ff, FF_CHUNK):
            cs = slice(c, c + FF_CHUNK)
            vs = slice(ff + c, ff + c + FF_CHUNK)
            da = ahead
            if c + FF_CHUNK < ff:
                ahead = down(c + FF_CHUNK)
            gp = u_ref[:, cs].astype(F32)
            val = u_ref[:, vs].astype(F32)
            prev = jnp.where(i < n - 1, uh_ref[:, cs].astype(F32)[8:16], 0.0)
            g1 = _shift_down(gp, prev, 1)
            g2 = _shift_down(gp, prev, 2)
            gate = wdw_ref[0:1, cs] * g2 + wdw_ref[1:2, cs] * g1 + wdw_ref[2:3, cs] * gp + bdw_ref[:, cs]
            sg = _sigmoid(gate)
            si = gate * sg
            act_ref[:, cs] = (si * val).astype(BF16)
            dgate = da * val * _silu_grad(gate, sg)
            nxt = carry[:, cs]
            dgp = (wdw_ref[2:3, cs] * dgate + wdw_ref[1:2, cs] * _shift_up(dgate, nxt, 1)
                   + wdw_ref[0:1, cs] * _shift_up(dgate, nxt, 2)).astype(BF16)
            dval = (da * si).astype(BF16)
            du_ref[:, cs] = dgp
            du_ref[:, vs] = dval
            dh = (dh + lax.dot_general(dgp, wup_ref[:, cs], NT_DIMS, preferred_element_type=F32)
                  + lax.dot_general(dval, wup_ref[:, vs], NT_DIMS, preferred_element_type=F32))
            dw_ref[:, cs] += jnp.concatenate(
                [jnp.sum(dgate * g2, axis=0, keepdims=True), jnp.sum(dgate * g1, axis=0, keepdims=True),
                 jnp.sum(dgate * gp, axis=0, keepdims=True), jnp.zeros((5, FF_CHUNK), F32)], axis=0)
            db_ref[:, cs] += jnp.sum(dgate, axis=0, keepdims=True)
            carry[:, cs] = dgate[0:8, :]
        xv = x_ref[...]
        r = lax.rsqrt(jnp.mean(xv * xv, axis=-1, keepdims=True) + EPS)
        xh = xv * r
        dg_ref[...] += jnp.sum(dh * xh, axis=0, keepdims=True)
        dn = dh * g_ref[...]
        dx_ref[...] = dyv + r * (dn - xh * jnp.mean(dn * xh, axis=-1, keepdims=True))

    step_is = lambda i: lambda: pl.program_id(0) == i
    kern, ins, outs, shapes, sems = _carried(body, 9, 6, send, "scatter",
                                             {"start": step_is(0), "finish": step_is(n - 1)})
    rev = lambda cols: pl.BlockSpec((tm, cols), lambda i: (n - 1 - i, 0))
    any_spec = pl.BlockSpec(memory_space=pl.ANY)
    res = pl.pallas_call(
        kern, name="ffn_bwd", grid=(n,),
        in_specs=[rev(d), rev(2 * ff),
                  pl.BlockSpec((16, 2 * ff), lambda i: (jnp.maximum((n - 1 - i) * hb - 1, 0), 0)),
                  rev(d), _full((1, d)), _full((8, ff)), _full((1, ff)), any_spec, any_spec] + ins,
        out_specs=[rev(2 * ff), rev(ff), _full((8, ff)), _full((1, ff)), rev(d), _full((1, d))] + outs,
        out_shape=[jax.ShapeDtypeStruct((s, 2 * ff), BF16), jax.ShapeDtypeStruct((s, ff), BF16),
                   jax.ShapeDtypeStruct((8, ff), F32), jax.ShapeDtypeStruct((1, ff), F32),
                   jax.ShapeDtypeStruct((s, d), F32), jax.ShapeDtypeStruct((1, d), F32)] + shapes,
        scratch_shapes=[pltpu.VMEM((8, ff), F32), pltpu.VMEM((ff, d), BF16), pltpu.VMEM((d, 2 * ff), BF16),
                        pltpu.SemaphoreType.DMA((2,))] + sems,
        compiler_params=_params(("arbitrary",), 56),
    )(dy, u, u, x, g, wdw, bdw, wdn, wup, *send)
    return tuple(res[:6]) + (list(res[6:]),)


def _loss_head(y, target):
    s, d = y.shape
    tm = min(512, s)

    def body(y_ref, t_ref, l_ref, dy_ref):
        @pl.when(pl.program_id(0) == 0)
        def _():
            l_ref[...] = jnp.zeros_like(l_ref)

        err = y_ref[...] - t_ref[...]
        dy_ref[...] = err * (1.0 / d)
        l_ref[...] += 0.5 * jnp.sum(jnp.mean(err * err, axis=-1, keepdims=True), axis=0, keepdims=True)

    return pl.pallas_call(
        body, name="loss_head", grid=(s // tm,),
        in_specs=[pl.BlockSpec((tm, d), lambda i: (i, 0)), pl.BlockSpec((tm, d), lambda i: (i, 0))],
        out_specs=[_full((8, LANE)), pl.BlockSpec((tm, d), lambda i: (i, 0))],
        out_shape=[jax.ShapeDtypeStruct((8, LANE), F32), jax.ShapeDtypeStruct((s, d), F32)],
        compiler_params=_params(("arbitrary",), 32),
    )(y, target)


def _row_tile(rows, limit=512):
    for cand in range(min(limit, rows) // 16 * 16, 0, -16):
        if rows % cand == 0:
            return cand
    return rows


def _reduce_adamw(parts, w, m, v, name):
    nl = len(parts)
    _, a, b = parts[0].shape
    ta = _row_tile(a, 256)

    def body(*refs):
        p_refs = refs[:nl]
        w_ref, m_ref, v_ref, g_ref, d_ref, mo_ref, vo_ref = refs[nl:]
        for layer in range(nl):
            @pl.when(pl.program_id(0) == layer)
            def _(p_ref=p_refs[layer]):
                g = p_ref[0].astype(F32)
                for k in range(1, N_DEV):
                    g = g + p_ref[k].astype(F32)
                g_ref[0] = g

        g = g_ref[0]
        mn = ADAM_B1 * m_ref[...] + (1.0 - ADAM_B1) * g
        vn = ADAM_B2 * v_ref[...] + (1.0 - ADAM_B2) * (g * g)
        mo_ref[...] = mn
        vo_ref[...] = vn
        m_hat = mn / (1.0 - ADAM_B1 ** ADAM_STEP)
        v_hat = vn / (1.0 - ADAM_B2 ** ADAM_STEP)
        d_ref[...] = -ADAM_LR * (m_hat / (jnp.sqrt(v_hat) + ADAM_EPS) + ADAM_WD * w_ref[...])

    blk = pl.BlockSpec((1, ta, b), lambda l, i: (l, i, 0))
    part = lambda layer: pl.BlockSpec((N_DEV, ta, b), lambda l, i: (0, jnp.where(l == layer, i, 0), 0))
    return pl.pallas_call(
        body, name=name, grid=(nl, a // ta),
        in_specs=[part(layer) for layer in range(nl)] + [blk, blk, blk],
        out_specs=[blk, blk, blk, blk],
        out_shape=[jax.ShapeDtypeStruct((nl, a, b), F32)] * 4,
        compiler_params=_params(("arbitrary", "arbitrary"), 56),
    )(*parts, w, m, v)


def _unshard(w8, layer, name):
    _, _, k, n = w8.shape
    tk = _row_tile(k, 256)

    def body(w_ref, o_ref):
        for d in range(N_DEV):
            o_ref[:, d * n:(d + 1) * n] = w_ref[d, 0]

    return pl.pallas_call(
        body, name=name, grid=(k // tk,),
        in_specs=[pl.BlockSpec((N_DEV, 1, tk, n), lambda i: (0, layer, i, 0))],
        out_specs=pl.BlockSpec((tk, N_DEV * n), lambda i: (i, 0)),
        out_shape=jax.ShapeDtypeStruct((k, N_DEV * n), w8.dtype),
        compiler_params=_params(("arbitrary",), 32),
    )(w8)


def _shard_cast(g, name):
    k, n8 = g.shape
    n = n8 // N_DEV
    tk = _row_tile(k, 256)

    def body(g_ref, o_ref):
        for d in range(N_DEV):
            o_ref[d] = g_ref[:, d * n:(d + 1) * n].astype(BF16)

    return pl.pallas_call(
        body, name=name, grid=(k // tk,),
        in_specs=[pl.BlockSpec((tk, n8), lambda i: (i, 0))],
        out_specs=pl.BlockSpec((N_DEV, tk, n), lambda i: (0, i, 0)),
        out_shape=jax.ShapeDtypeStruct((N_DEV, k, n), BF16),
        compiler_params=_params(("arbitrary",), 32),
    )(g)


def _row(v):
    return v.reshape(1, -1)


def _group_ones():
    idx = jnp.arange(SB_WIDTH) // HEAD_DIM
    return (idx[:, None] == idx[None, :]).astype(BF16)


def _pad_rows(w, rows):
    return jnp.concatenate([w, jnp.zeros((rows - w.shape[0], w.shape[1]), w.dtype)], axis=0)


class _NoTraffic:
    def rest(self):
        return ()

    def install(self, wt, gathered):
        pass

    def ready(self, name, layer, grad):
        pass

    def take(self):
        return ()

    def landed(self, received):
        pass


def _local_step(x, target, wt, traffic):
    scale = HEAD_DIM ** -0.5
    bd = _group_ones()
    tril = jnp.tril(jnp.ones((CHUNK, CHUNK), dtype=bool))
    saved = []
    for i in range(DEPTH):
        j = i // 2
        lay = {"x_mix": x}
        if i % 2 == 0:
            proj, h = _rms_matmul(x, _row(wt["mix_norm_g"][i]), wt["sb_w_in"][j], jnp.zeros((1, IN_WIDTH), F32), "in_proj")
            gq = _row(jnp.tile(wt["sb_q_norm_g"][j], SB_WIDTH // HEAD_DIM)) * scale
            gk = _row(jnp.tile(wt["sb_k_norm_g"][j], SB_WIDTH // HEAD_DIM))
            qn, kn, vb = _qk_prep(proj, gq * LOG2E, gk, bd)
            o, lsum, gathered = _sb_attn_fwd(qn, kn, vb, gather=traffic.rest() if i == 0 else ())
            if i == 0:
                traffic.install(wt, gathered)
            wm = jnp.where(tril[None], wt["sg_w_spatial"][j], 0.0)
            wmb = wm.astype(BF16)
            wmt = jnp.swapaxes(wm, 1, 2).astype(BF16)
            bt = jnp.repeat(wt["sg_b_spatial"][j].T, HEAD_DIM, axis=1)
            gz = _row(wt["sg_z_norm_g"][j])
            gg = _sgu_fwd(proj, gz, wmb, bt, bd)
            x, mix = _out_proj(x, o, gg, wt["hyb_w_out"][j], "out_proj")
            lay.update(proj=proj, h=h, gq=gq, gk=gk, qn=qn, kn=kn, vb=vb, lsum=lsum, wmb=wmb, wmt=wmt, bt=bt, gz=gz, mix=mix)
        else:
            p, h = _rms_matmul(x, _row(wt["mix_norm_g"][i]), wt["cv_w_pw1"][j], _row(wt["cv_b_pw1"][j]), "conf_pw1")
            wdw = _pad_rows(wt["cv_w_dw"][j], HALO)
            yc, y2 = _conf_mid_fwd(p, wdw, _row(wt["cv_b_dw"][j]), _row(wt["cv_ln_g"][j]), _row(wt["cv_ln_b"][j]))
            x = _res_matmul(x, y2, wt["cv_w_pw2"][j], _row(wt["cv_b_pw2"][j]), "conf_pw2")
            lay.update(p=p, h=h, wdw=wdw, yc=yc, y2=y2)
        lay["x_ffn"] = x
        fdw = _pad_rows(wt["ffn_w_dw"][i], 8)
        x, hf, u = _ffn_fwd(x, _row(wt["ffn_norm_g"][i]), wt["ffn_w_up"][i], fdw, _row(wt["ffn_b_dw"][i]),
                            wt["ffn_w_down"][i])
        lay.update(hf=hf, u=u, fdw=fdw)
        saved.append(lay)

    lpart, dy = _loss_head(x, target)
    loss = lpart[0, 0]

    gr = {k: [None] * len(v) for k, v in wt.items()}

    def made(name, layer, grad):
        gr[name][layer] = grad
        traffic.ready(name, layer, grad)

    for i in reversed(range(DEPTH)):
        j = i // 2
        lay = saved[i]
        du, act, dfdw, dfb, dx, dgf, got = _ffn_bwd(
            dy, lay["u"], lay["x_ffn"], _row(wt["ffn_norm_g"][i]), lay["fdw"], _row(wt["ffn_b_dw"][i]),
            wt["ffn_w_down"][i], wt["ffn_w_up"][i], send=traffic.take())
        traffic.landed(got)
        made("ffn_w_down", i, _matmul_tn(act, dy, "ffn_dw_down"))
        made("ffn_w_up", i, _matmul_tn(lay["hf"], du, "ffn_dw_up"))
        gr["ffn_w_dw"][i] = dfdw[:FFN_K]
        gr["ffn_b_dw"][i] = dfb[0]
        gr["ffn_norm_g"][i] = dgf[0]
        dy = dx
        if i % 2 == 0:
            dmix = _matmul_nt(dy, wt["hyb_w_out"][j], "out_proj_dx")
            made("hyb_w_out", j, _matmul_tn(lay["mix"], dy, "out_proj_dw"))
            dqn, dkn, dv, got = _sb_attn_bwd(lay["qn"], lay["kn"], lay["vb"], lay["lsum"], dmix, send=traffic.take())
            traffic.landed(got)
            duz, dwm, dbt, dgz = _sgu_bwd(lay["proj"], dmix, lay["gz"], lay["wmb"], lay["wmt"], lay["bt"], bd)
            dproj, dgq, dgk = _qk_bwd(lay["proj"], dqn, dkn, dv, duz, lay["gq"], lay["gk"] * LN2, bd)
            made("sb_w_in", j, _matmul_tn(lay["h"], dproj, "in_proj_dw"))
            gr["sb_q_norm_g"][j] = dgq.reshape(SB_WIDTH // HEAD_DIM, HEAD_DIM).sum(0) * scale
            gr["sb_k_norm_g"][j] = dgk.reshape(SB_WIDTH // HEAD_DIM, HEAD_DIM).sum(0) * LN2
            gr["sg_z_norm_g"][j] = dgz[0]
            gr["sg_w_spatial"][j] = jnp.where(tril[None], dwm, 0.0)
            gr["sg_b_spatial"][j] = dbt.reshape(CHUNK, SG_GROUPS, HEAD_DIM).sum(-1).T
            dy, dgm, got = _nt_rms_bwd(dproj, wt["sb_w_in"][j], lay["x_mix"], _row(wt["mix_norm_g"][i]), dy,
                                       "in_proj_dx", send=traffic.take() if i == 0 else ())
            if i == 0:
                traffic.landed(got)
        else:
            made("cv_w_pw2", j, _matmul_tn(lay["y2"], dy, "conf_pw2_dw"))
            dp, dwdw, dbdw, dlg, dlb, db1, db2, dx, dgm, got = _conf_mid_bwd(
                lay["p"], lay["yc"], dy, lay["x_mix"], _row(wt["mix_norm_g"][i]), lay["wdw"],
                _row(wt["cv_ln_g"][j]), _row(wt["cv_ln_b"][j]), wt["cv_w_pw1"][j], wt["cv_w_pw2"][j],
                send=traffic.take())
            traffic.landed(got)
            dy = dx
            made("cv_w_pw1", j, _matmul_tn(lay["h"], dp, "conf_pw1_dw"))
            gr["cv_w_dw"][j] = dwdw[:CONV_K]
            gr["cv_b_dw"][j] = dbdw[0]
            gr["cv_ln_g"][j] = dlg[0]
            gr["cv_ln_b"][j] = dlb[0]
            gr["cv_b_pw1"][j] = db1[0]
            gr["cv_b_pw2"][j] = db2[0]
        gr["mix_norm_g"][i] = dgm[0]
    matmul_weights = ("sb_w_in", "hyb_w_out", "cv_w_pw1", "cv_w_pw2", "ffn_w_up", "ffn_w_down")
    grads = {k: (v if k in matmul_weights else jnp.stack(v)) for k, v in gr.items()}
    return loss, dy, grads


WEIGHTS = ["mix_norm_g", "sb_w_in", "sb_q_norm_g", "sb_k_norm_g", "sg_z_norm_g", "sg_w_spatial", "sg_b_spatial",
           "hyb_w_out", "cv_w_pw1", "cv_b_pw1", "cv_w_dw", "cv_b_dw", "cv_ln_g", "cv_ln_b", "cv_w_pw2", "cv_b_pw2",
           "ffn_norm_g", "ffn_w_up", "ffn_w_dw", "ffn_b_dw", "ffn_w_down"]
BIG = [("sb_w_in", "col"), ("hyb_w_out", "row"), ("cv_w_pw1", "col"), ("cv_w_pw2", "row"), ("ffn_w_up", "col"),
       ("ffn_w_down", "row")]
SMALL = ["cv_b_pw1", "cv_w_dw", "cv_b_dw", "cv_ln_g", "cv_ln_b", "cv_b_pw2", "ffn_w_dw"]
REPLICATED = ["mix_norm_g", "sb_q_norm_g", "sb_k_norm_g", "sg_z_norm_g", "sg_w_spatial", "sg_b_spatial", "ffn_norm_g",
              "ffn_b_dw"]


def _last_dim_blocks(full):
    t = jnp.moveaxis(full.reshape(full.shape[:-1] + (N_DEV, full.shape[-1] // N_DEV)), -2, 0)
    return t.reshape(N_DEV, -1)


def _from_last_dim_blocks(blocks, shard_shape):
    t = jnp.moveaxis(blocks.reshape((N_DEV,) + tuple(shard_shape)), 0, -2)
    return t.reshape(tuple(shard_shape[:-1]) + (N_DEV * shard_shape[-1],))


def _pack(arrays):
    lead = arrays[0].shape[:-1]
    flat = jnp.concatenate([a.astype(F32) for a in arrays], axis=-1)
    rows = -(-flat.shape[-1] // (16 * LANE)) * 16
    pad = rows * LANE - flat.shape[-1]
    if pad:
        flat = jnp.concatenate([flat, jnp.zeros(lead + (pad,), F32)], axis=-1)
    return flat.reshape(lead + (rows, LANE))


def _unpack(packed, shapes):
    flat = packed.reshape(-1)
    out, off = [], 0
    for shp in shapes:
        size = 1
        for dim in shp:
            size *= dim
        out.append(flat[off:off + size].reshape(shp))
        off += size
    return out


class _ShardTraffic:
    def __init__(self, w):
        self.w = w
        self.queue, self.flying, self.received = [], [], {}

    def rest(self):
        small = _pack([self.w[n].reshape(-1) for n in SMALL])
        return [self.w["sb_w_in"][1:].astype(BF16)] + [self.w[n].astype(BF16) for n, _ in BIG[1:]] + [small]

    def install(self, wt, gathered):
        wt["sb_w_in"].append(_unshard(gathered[0], 0, "unshard_sb_w_in"))
        for (n, kind), w8 in zip(BIG[1:], gathered[1:-1]):
            if kind == "col":
                wt[n] = [_unshard(w8, l, "unshard_" + n) for l in range(w8.shape[1])]
            else:
                wt[n] = [w8[:, l].reshape((N_DEV * w8.shape[2],) + w8.shape[3:]) for l in range(w8.shape[1])]
        sizes = [(self.w[n].size,) for n in SMALL]
        for n, parts in zip(SMALL, zip(*[_unpack(gathered[-1][d], sizes) for d in range(N_DEV)])):
            wt[n] = _from_last_dim_blocks(jnp.stack(parts), self.w[n].shape)

    def ready(self, name, layer, grad):
        if dict(BIG)[name] == "col":
            blocks = _shard_cast(grad, "shard_" + name)
        else:
            blocks = grad.reshape((N_DEV, grad.shape[0] // N_DEV) + grad.shape[1:])
        self.queue.append(((name, layer), blocks))

    def take(self):
        self.flying = [key for key, _ in self.queue]
        arrays = [blocks for _, blocks in self.queue]
        self.queue = []
        return arrays

    def landed(self, received):
        for key, blocks in zip(self.flying, received):
            self.received[key] = blocks
        self.flying = []


def kernel(x, mix_norm_g, sb_w_in, sb_q_norm_g, sb_k_norm_g, sg_z_norm_g, sg_w_spatial, sg_b_spatial, hyb_w_out, cv_w_pw1, cv_b_pw1, cv_w_dw, cv_b_dw, cv_ln_g, cv_ln_b, cv_w_pw2, cv_b_pw2, ffn_norm_g, ffn_w_up, ffn_w_dw, ffn_b_dw, ffn_w_down, loss_target, m_mix_norm_g, m_sb_w_in, m_sb_q_norm_g, m_sb_k_norm_g, m_sg_z_norm_g, m_sg_w_spatial, m_sg_b_spatial, m_hyb_w_out, m_cv_w_pw1, m_cv_b_pw1, m_cv_w_dw, m_cv_b_dw, m_cv_ln_g, m_cv_ln_b, m_cv_w_pw2, m_cv_b_pw2, m_ffn_norm_g, m_ffn_w_up, m_ffn_w_dw, m_ffn_b_dw, m_ffn_w_down, v_mix_norm_g, v_sb_w_in, v_sb_q_norm_g, v_sb_k_norm_g, v_sg_z_norm_g, v_sg_w_spatial, v_sg_b_spatial, v_hyb_w_out, v_cv_w_pw1, v_cv_b_pw1, v_cv_w_dw, v_cv_b_dw, v_cv_ln_g, v_cv_ln_b, v_cv_w_pw2, v_cv_b_pw2, v_ffn_norm_g, v_ffn_w_up, v_ffn_w_dw, v_ffn_b_dw, v_ffn_w_down):
    w = dict(zip(WEIGHTS, (mix_norm_g, sb_w_in, sb_q_norm_g, sb_k_norm_g, sg_z_norm_g, sg_w_spatial, sg_b_spatial,
                           hyb_w_out, cv_w_pw1, cv_b_pw1, cv_w_dw, cv_b_dw, cv_ln_g, cv_ln_b, cv_w_pw2, cv_b_pw2,
                           ffn_norm_g, ffn_w_up, ffn_w_dw, ffn_b_dw, ffn_w_down)))
    m = dict(zip(WEIGHTS, (m_mix_norm_g, m_sb_w_in, m_sb_q_norm_g, m_sb_k_norm_g, m_sg_z_norm_g, m_sg_w_spatial,
                           m_sg_b_spatial, m_hyb_w_out, m_cv_w_pw1, m_cv_b_pw1, m_cv_w_dw, m_cv_b_dw, m_cv_ln_g,
                           m_cv_ln_b, m_cv_w_pw2, m_cv_b_pw2, m_ffn_norm_g, m_ffn_w_up, m_ffn_w_dw, m_ffn_b_dw,
                           m_ffn_w_down)))
    v = dict(zip(WEIGHTS, (v_mix_norm_g, v_sb_w_in, v_sb_q_norm_g, v_sb_k_norm_g, v_sg_z_norm_g, v_sg_w_spatial,
                           v_sg_b_spatial, v_hyb_w_out, v_cv_w_pw1, v_cv_b_pw1, v_cv_w_dw, v_cv_b_dw, v_cv_ln_g,
                           v_cv_ln_b, v_cv_w_pw2, v_cv_b_pw2, v_ffn_norm_g, v_ffn_w_up, v_ffn_w_dw, v_ffn_b_dw,
                           v_ffn_w_down)))
    big_names = [n for n, _ in BIG]
    flat = lambda t, names: [t[n].reshape(-1) for n in names]

    first_in = _all_gather([w["sb_w_in"][0:1].astype(BF16)], "gather_first")[0]
    wt = {n: w[n] for n in REPLICATED}
    wt["sb_w_in"] = [_unshard(first_in, 0, "unshard_sb_w_in")]

    traffic = _ShardTraffic(w)
    loss, gx, grads = _local_step(x[0], loss_target[0], wt, traffic)
    assert not traffic.queue and not traffic.flying
    recv_big = [[traffic.received[n, l] for l in range(w[n].shape[0])] for n in big_names]

    grep = _pack(flat(grads, REPLICATED))
    gsmall = _pack([_last_dim_blocks(grads[n]) for n in SMALL])
    rep_rows, small_rows = grep.shape[0], gsmall.shape[1]
    vec = jnp.concatenate([grep, gsmall.reshape(N_DEV * small_rows, LANE)], axis=0).astype(BF16)
    vec_all = _all_gather([vec], "gather_small_grads")[0]
    me = 4 * lax.axis_index("x") + 2 * lax.axis_index("y") + lax.axis_index("c")
    recv_rep = vec_all[:, :rep_rows]
    recv_small = lax.dynamic_slice(vec_all, (0, rep_rows + small_rows * me, 0), (N_DEV, small_rows, LANE))

    out = {}
    kinds = ("grad", "delta", "new_m", "new_v")
    for n, parts in zip(big_names, recv_big):
        for kind, arr in zip(kinds, _reduce_adamw(parts, w[n], m[n], v[n], "adamw_" + n)):
            out[kind, n] = arr
    for names, recv, tag in ((SMALL, recv_small, "adamw_small"), (REPLICATED, recv_rep, "adamw_replicated")):
        res = _reduce_adamw([recv], _pack(flat(w, names))[None], _pack(flat(m, names))[None],
                            _pack(flat(v, names))[None], tag)
        shapes = [w[n].shape for n in names]
        for kind, packed in zip(kinds, res):
            for n, arr in zip(names, _unpack(packed[0], shapes)):
                out[kind, n] = arr

    loss = lax.psum(loss, ("x", "y", "c"))
    return (loss, gx[None], *[out[kind, n] for kind in kinds for n in WEIGHTS])
```

```python
import jax
import jax.numpy as jnp
from jax import lax
from jax.experimental import pallas as pl
from jax.experimental.pallas import tpu as pltpu

F32 = jnp.float32
BF16 = jnp.bfloat16

D_MODEL = 1024
HEAD_DIM = 64
SB_WIDTH = 512
SG_WIDTH = 512
SG_GROUPS = 8
IN_WIDTH = 3 * SB_WIDTH + 2 * SG_WIDTH
CHUNK = 128
CONV_K = 31
D_FF = 2816
FFN_K = 3
DEPTH = 4
EPS = 1e-6
N_DEV = 8
LANE = 128
HALO = 32
ATT_BLOCK = 256
FF_CHUNK = 256
CONV_ROWS = 32
CONV_LANES = 512
MIB = 2 ** 20

ADAM_LR = 0.001
ADAM_B1 = 0.9
ADAM_B2 = 0.999
ADAM_EPS = 1e-08
ADAM_WD = 0.01
ADAM_STEP = 10

LOG2E = 1.4426950408889634
LN2 = 0.6931471805599453

NT_DIMS = (((1,), (1,)), ((), ()))
TN_DIMS = (((0,), (0,)), ((), ()))


def _params(semantics, vmem_mib):
    return pltpu.CompilerParams(dimension_semantics=semantics, vmem_limit_bytes=vmem_mib * MIB)


def _full(shape):
    nd = len(shape)
    return pl.BlockSpec(shape, lambda *_: (0,) * nd)


def _sigmoid(x):
    return 1.0 / (1.0 + jnp.exp(-x))


def _gelu(x):
    return 0.5 * x * (1.0 + lax.erf(x * 0.7071067811865476))


def _gelu_grad(x):
    return 0.5 * (1.0 + lax.erf(x * 0.7071067811865476)) + x * jnp.exp(-0.5 * x * x) * 0.3989422804014327


def _silu_grad(x, s):
    return s * (1.0 + x * (1.0 - s))


def _dot(a, b):
    return jnp.dot(a, b, preferred_element_type=F32)


def _dot2(a, b):
    hi = a.astype(BF16)
    lo = (a - hi.astype(F32)).astype(BF16)
    return _dot(hi, b) + _dot(lo, b)


def _group_mean(t, bd):
    return _dot2(t, bd) * (1.0 / HEAD_DIM)


def _shift_down(v, prev8, s):
    top = pltpu.roll(jnp.concatenate([prev8, v[:8]], axis=0), s, 0)[8:16]
    return jnp.concatenate([top, pltpu.roll(v, s, 0)[8:]], axis=0)


def _shift_up(v, next8, s):
    n = v.shape[0]
    bottom = pltpu.roll(jnp.concatenate([v[n - 8:], next8], axis=0), 16 - s, 0)[0:8]
    return jnp.concatenate([pltpu.roll(v, n - s, 0)[: n - 8], bottom], axis=0)


def _mesh_pos():
    return lax.axis_index("x"), lax.axis_index("y"), lax.axis_index("c")


def _comm_scratch(n):
    return [pltpu.SemaphoreType.DMA((7 * n,)), pltpu.SemaphoreType.DMA((7 * n,)), pltpu.SemaphoreType.DMA((n,))]


class _Scatter:
    def __init__(self, src_refs, out_refs, send_sems, recv_sems, local_sems):
        x, y, cc = _mesh_pos()
        me = 4 * x + 2 * y + cc
        self.copies, self.mine = [], []
        for a, (src, out) in enumerate(zip(src_refs, out_refs)):
            self.mine.append(pltpu.make_async_copy(src.at[me], out.at[me], local_sems.at[a]))
            for k in range(1, N_DEV):
                px = 1 - x if k & 4 else x
                py = 1 - y if k & 2 else y
                pc = 1 - cc if k & 1 else cc
                self.copies.append(pltpu.make_async_remote_copy(
                    src_ref=src.at[4 * px + 2 * py + pc], dst_ref=out.at[me],
                    send_sem=send_sems.at[7 * a + k - 1], recv_sem=recv_sems.at[7 * a + k - 1],
                    device_id=(px, py, pc), device_id_type=pl.DeviceIdType.MESH))

    def start(self):
        for cp in self.mine + self.copies:
            cp.start()

    def finish(self):
        for cp in self.copies + self.mine:
            cp.wait()


class _Gather:
    def __init__(self, x_refs, out_refs, send_sems, recv_sems, local_sems):
        x, y, cc = _mesh_pos()
        self.n = len(x_refs)
        self.me, self.sibling, self.cc = (x, y, cc), (x, y, 1 - cc), cc
        self.chips = [(1 - x, y), (x, 1 - y), (1 - x, 1 - y)]
        self.x_refs, self.out_refs, self.send_sems, self.recv_sems = x_refs, out_refs, send_sems, recv_sems
        self.mine = [pltpu.make_async_copy(x_refs[a], out_refs[a].at[4 * x + 2 * y + cc], local_sems.at[a])
                     for a in range(self.n)]

    def copy(self, a, k, block, to, own=False):
        slot = self.out_refs[a].at[4 * block[0] + 2 * block[1] + block[2]]
        return pltpu.make_async_remote_copy(
            src_ref=self.x_refs[a] if own else slot, dst_ref=slot,
            send_sem=self.send_sems.at[7 * a + k], recv_sem=self.recv_sems.at[7 * a + k],
            device_id=to, device_id_type=pl.DeviceIdType.MESH)

    def first_hop(self, a):
        return [self.copy(a, 0, self.me, self.sibling, own=True)] + [
            self.copy(a, 1 + j, self.me, (*chip, self.cc), own=True) for j, chip in enumerate(self.chips)]

    def passed_on(self, a):
        return [self.copy(a, 4 + j, (*chip, self.cc), self.sibling) for j, chip in enumerate(self.chips)]

    def start(self):
        for a in range(self.n):
            self.mine[a].start()
        for a in range(self.n):
            for cp in self.first_hop(a):
                cp.start()

    def forward(self):
        for a in range(self.n):
            for j, chip in enumerate(self.chips):
                self.copy(a, 1 + j, (*chip, self.cc), self.me).wait_recv()
                self.copy(a, 4 + j, (*chip, self.cc), self.sibling).start()

    def finish(self):
        for a in range(self.n):
            self.copy(a, 0, self.sibling, self.me).wait_recv()
            for j, chip in enumerate(self.chips):
                self.copy(a, 4 + j, (*chip, 1 - self.cc), self.me).wait_recv()
        for a in range(self.n):
            for cp in self.first_hop(a) + self.passed_on(a):
                cp.wait_send()
        for cp in self.mine:
            cp.wait()


def _carried(body, n_in, n_out, arrays, kind, when):
    n = len(arrays)
    if n == 0:
        return body, [], [], [], []

    def wrapped(*refs):
        ins, srcs = refs[:n_in], refs[n_in:n_in + n]
        outs, landed = refs[n_in + n:n_in + n + n_out], refs[n_in + n + n_out:n_in + 2 * n + n_out]
        scratch = refs[n_in + 2 * n + n_out:]
        comm = (_Scatter if kind == "scatter" else _Gather)(srcs, landed, *scratch[-3:])
        pl.when(when["start"]())(comm.start)
        if kind == "gather":
            pl.when(when["forward"]())(comm.forward)
        body(*ins, *outs, *scratch[:-3])
        pl.when(when["finish"]())(comm.finish)

    any_spec = pl.BlockSpec(memory_space=pl.ANY)
    if kind == "scatter":
        shapes = [jax.ShapeDtypeStruct(a.shape, a.dtype) for a in arrays]
    else:
        shapes = [jax.ShapeDtypeStruct((N_DEV,) + a.shape, a.dtype) for a in arrays]
    return wrapped, [any_spec] * n, [any_spec] * n, shapes, _comm_scratch(n)


def _all_gather(shards, name):
    n = len(shards)

    def body(*refs):
        comm = _Gather(refs[:n], refs[n:2 * n], *refs[2 * n:])
        comm.start()
        comm.forward()
        comm.finish()

    any_spec = pl.BlockSpec(memory_space=pl.ANY)
    return pl.pallas_call(
        body, name=name, in_specs=[any_spec] * n, out_specs=[any_spec] * n,
        out_shape=[jax.ShapeDtypeStruct((N_DEV,) + s.shape, s.dtype) for s in shards],
        scratch_shapes=_comm_scratch(n),
    )(*shards)


def _rms_matmul(x, g, w, b, name):
    s, d = x.shape
    n = w.shape[1]
    tm = min(512, s)

    def body(x_ref, g_ref, w_ref, b_ref, y_ref, h_ref):
        xv = x_ref[...]
        r = lax.rsqrt(jnp.mean(xv * xv, axis=-1, keepdims=True) + EPS)
        h = (xv * r * g_ref[...]).astype(BF16)
        h_ref[...] = h
        for c in range(0, n, 512):
            y_ref[:, c:c + 512] = _dot(h, w_ref[:, c:c + 512]) + b_ref[:, c:c + 512]

    return pl.pallas_call(
        body, name=name, grid=(s // tm,),
        in_specs=[pl.BlockSpec((tm, d), lambda i: (i, 0)), _full((1, d)), _full((d, n)), _full((1, n))],
        out_specs=[pl.BlockSpec((tm, n), lambda i: (i, 0)), pl.BlockSpec((tm, d), lambda i: (i, 0))],
        out_shape=[jax.ShapeDtypeStruct((s, n), F32), jax.ShapeDtypeStruct((s, d), BF16)],
        compiler_params=_params(("arbitrary",), 48),
    )(x, g, w, b)


def _matmul_nt(a, w, name):
    s, n = a.shape
    k = w.shape[0]
    tm = min(512, s)

    def body(a_ref, w_ref, o_ref):
        o_ref[...] = lax.dot_general(a_ref[...].astype(BF16), w_ref[...], NT_DIMS, preferred_element_type=F32)

    return pl.pallas_call(
        body, name=name, grid=(s // tm,),
        in_specs=[pl.BlockSpec((tm, n), lambda i: (i, 0)), _full((k, n))],
        out_specs=pl.BlockSpec((tm, k), lambda i: (i, 0)),
        out_shape=jax.ShapeDtypeStruct((s, k), F32),
        compiler_params=_params(("arbitrary",), 40),
    )(a, w)


def _matmul_tn(a, b, name):
    s, k = a.shape
    n = b.shape[1]
    ts = min(2048 if k <= 1024 else 1024, s)
    tn = 1024 if (n % 1024 == 0 and k <= 1024) else 512
    steps = s // ts

    def body(a_ref, b_ref, o_ref, acc):
        t = pl.program_id(1)

        @pl.when(t == 0)
        def _():
            acc[...] = jnp.zeros_like(acc)

        acc[...] += lax.dot_general(a_ref[...].astype(BF16), b_ref[...].astype(BF16), TN_DIMS,
                                    preferred_element_type=F32)

        @pl.when(t == steps - 1)
        def _():
            o_ref[...] = acc[...].astype(BF16)

    return pl.pallas_call(
        body, name=name, grid=(n // tn, steps),
        in_specs=[pl.BlockSpec((ts, k), lambda j, t: (t, 0)), pl.BlockSpec((ts, tn), lambda j, t: (t, j))],
        out_specs=pl.BlockSpec((k, tn), lambda j, t: (0, j)),
        out_shape=jax.ShapeDtypeStruct((k, n), BF16),
        scratch_shapes=[pltpu.VMEM((k, tn), F32)],
        compiler_params=_params(("arbitrary", "arbitrary"), 48),
    )(a, b)


def _nt_rms_bwd(dp, w, x, g, dres, name, send=()):
    s, n = dp.shape
    d = x.shape[1]
    tm = min(256, s)
    steps = s // tm

    def body(dp_ref, w_ref, x_ref, g_ref, dres_ref, dx_ref, dg_ref):
        @pl.when(pl.program_id(0) == 0)
        def _():
            dg_ref[...] = jnp.zeros_like(dg_ref)

        dh = lax.dot_general(dp_ref[...], w_ref[...], NT_DIMS, preferred_element_type=F32)
        xv = x_ref[...]
        r = lax.rsqrt(jnp.mean(xv * xv, axis=-1, keepdims=True) + EPS)
        xh = xv * r
        dg_ref[...] += jnp.sum(dh * xh, axis=0, keepdims=True)
        dn = dh * g_ref[...]
        dx_ref[...] = dres_ref[...] + r * (dn - xh * jnp.mean(dn * xh, axis=-1, keepdims=True))

    step_is = lambda i: lambda: pl.program_id(0) == i
    kern, ins, outs, shapes, sems = _carried(body, 5, 2, send, "scatter",
                                             {"start": step_is(0), "finish": step_is(steps - 1)})
    res = pl.pallas_call(
        kern, name=name, grid=(steps,),
        in_specs=[pl.BlockSpec((tm, n), lambda i: (i, 0)), _full((d, n)), pl.BlockSpec((tm, d), lambda i: (i, 0)),
                  _full((1, d)), pl.BlockSpec((tm, d), lambda i: (i, 0))] + ins,
        out_specs=[pl.BlockSpec((tm, d), lambda i: (i, 0)), _full((1, d))] + outs,
        out_shape=[jax.ShapeDtypeStruct((s, d), F32), jax.ShapeDtypeStruct((1, d), F32)] + shapes,
        scratch_shapes=sems,
        compiler_params=_params(("arbitrary",), 52),
    )(dp, w, x, g, dres, *send)
    return res[0], res[1], list(res[2:])


def _res_matmul(x, a, w, b, name):
    s, d = x.shape
    k = a.shape[1]
    tm = min(512, s)

    def body(x_ref, a_ref, w_ref, b_ref, o_ref):
        o_ref[...] = x_ref[...] + _dot(a_ref[...], w_ref[...]) + b_ref[...]

    return pl.pallas_call(
        body, name=name, grid=(s // tm,),
        in_specs=[pl.BlockSpec((tm, d), lambda i: (i, 0)), pl.BlockSpec((tm, k), lambda i: (i, 0)), _full((k, d)),
                  _full((1, d))],
        out_specs=pl.BlockSpec((tm, d), lambda i: (i, 0)),
        out_shape=jax.ShapeDtypeStruct((s, d), F32),
        compiler_params=_params(("arbitrary",), 32),
    )(x, a, w, b)


def _out_proj(x, o, gg, w, name):
    s, d = x.shape
    tm = min(512, s)

    def body(x_ref, o_ref, gg_ref, w_ref, y_ref, mix_ref):
        mix = jnp.concatenate([o_ref[...], gg_ref[...]], axis=1).astype(BF16)
        mix_ref[...] = mix
        y_ref[...] = x_ref[...] + _dot(mix, w_ref[...])

    return pl.pallas_call(
        body, name=name, grid=(s // tm,),
        in_specs=[pl.BlockSpec((tm, d), lambda i: (i, 0)), pl.BlockSpec((tm, SB_WIDTH), lambda i: (i, 0)),
                  pl.BlockSpec((tm, SG_WIDTH), lambda i: (i, 0)), _full((d, d))],
        out_specs=[pl.BlockSpec((tm, d), lambda i: (i, 0)), pl.BlockSpec((tm, d), lambda i: (i, 0))],
        out_shape=[jax.ShapeDtypeStruct((s, d), F32), jax.ShapeDtypeStruct((s, d), BF16)],
        compiler_params=_params(("arbitrary",), 32),
    )(x, o, gg, w)


def _qk_prep(proj, gq, gk, bd):
    s = proj.shape[0]
    tm = min(512, s)

    def body(q_ref, k_ref, v_ref, gq_ref, gk_ref, bd_ref, qn_ref, kn_ref, vb_ref):
        bdv = bd_ref[...]
        q = q_ref[...]
        k = k_ref[...]
        qn_ref[...] = (q * lax.rsqrt(_group_mean(q * q, bdv) + EPS) * gq_ref[...]).astype(BF16)
        kn_ref[...] = (k * lax.rsqrt(_group_mean(k * k, bdv) + EPS) * gk_ref[...]).astype(BF16)
        vb_ref[...] = v_ref[...].astype(BF16)

    col = lambda c: pl.BlockSpec((tm, SB_WIDTH), lambda i: (i, c))
    out = pl.BlockSpec((tm, SB_WIDTH), lambda i: (i, 0))
    return pl.pallas_call(
        body, name="qk_prep", grid=(s // tm,),
        in_specs=[col(0), col(1), col(2), _full((1, SB_WIDTH)), _full((1, SB_WIDTH)), _full((SB_WIDTH, SB_WIDTH))],
        out_specs=[out, out, out],
        out_shape=[jax.ShapeDtypeStruct((s, SB_WIDTH), BF16)] * 3,
        compiler_params=_params(("arbitrary",), 32),
    )(proj, proj, proj, gq, gk, bd)


def _qk_bwd(proj, dqn, dkn, dv, duz, gq, gk, bd):
    s = proj.shape[0]
    tm = min(512, s)

    def body(q_ref, k_ref, dq_ref, dk_ref, dv_ref, duz_ref, gq_ref, gk_ref, bd_ref, o_ref, dgq_ref, dgk_ref):
        @pl.when(pl.program_id(0) == 0)
        def _():
            dgq_ref[...] = jnp.zeros_like(dgq_ref)
            dgk_ref[...] = jnp.zeros_like(dgk_ref)

        bdv = bd_ref[...]

        def back(t, gain, dout, dg_ref):
            r = lax.rsqrt(_group_mean(t * t, bdv) + EPS)
            th = t * r
            dg_ref[...] += jnp.sum(dout * th, axis=0, keepdims=True)
            dn = dout * gain
            return r * (dn - th * _group_mean(dn * th, bdv))

        o_ref[:, 0:SB_WIDTH] = back(q_ref[...], gq_ref[...], dq_ref[...], dgq_ref).astype(BF16)
        o_ref[:, SB_WIDTH:2 * SB_WIDTH] = back(k_ref[...], gk_ref[...], dk_ref[...], dgk_ref).astype(BF16)
        o_ref[:, 2 * SB_WIDTH:3 * SB_WIDTH] = dv_ref[...].astype(BF16)
        o_ref[:, 3 * SB_WIDTH:IN_WIDTH] = duz_ref[...].astype(BF16)

    col = lambda c: pl.BlockSpec((tm, SB_WIDTH), lambda i: (i, c))
    row = pl.BlockSpec((tm, SB_WIDTH), lambda i: (i, 0))
    return pl.pallas_call(
        body, name="qk_bwd", grid=(s // tm,),
        in_specs=[col(0), col(1), row, row, row, pl.BlockSpec((tm, 2 * SG_WIDTH), lambda i: (i, 0)),
                  _full((1, SB_WIDTH)), _full((1, SB_WIDTH)), _full((SB_WIDTH, SB_WIDTH))],
        out_specs=[pl.BlockSpec((tm, IN_WIDTH), lambda i: (i, 0)), _full((1, SB_WIDTH)), _full((1, SB_WIDTH))],
        out_shape=[jax.ShapeDtypeStruct((s, IN_WIDTH), BF16), jax.ShapeDtypeStruct((1, SB_WIDTH), F32),
                   jax.ShapeDtypeStruct((1, SB_WIDTH), F32)],
        compiler_params=_params(("arbitrary",), 40),
    )(proj, proj, dqn, dkn, dv, duz, gq, gk, bd)


def _attn_masks(tq, tk):
    lane = lax.broadcasted_iota(jnp.int32, (tk, LANE), 1)
    heads = [(lane >= hh * HEAD_DIM) & (lane < (hh + 1) * HEAD_DIM) for hh in range(2)]
    urow = lax.broadcasted_iota(jnp.int32, (tk, tk), 0)
    ucol = lax.broadcasted_iota(jnp.int32, (tk, tk), 1)
    return heads, urow, ucol


Z2_MAX = 126.0


def _keep_cost(z2):
    return jnp.log(1.0 + jnp.exp2(z2))


def _at_rows(part, row0, total):
    rows, width = part.shape
    pieces = [jnp.zeros((row0, width), part.dtype)] if row0 else []
    pieces.append(part)
    if total - row0 - rows:
        pieces.append(jnp.zeros((total - row0 - rows, width), part.dtype))
    return pieces[0] if len(pieces) == 1 else jnp.concatenate(pieces, axis=0)


def _diagonal_specs(i, r, tq, tk, left_to_right):
    if r == 2:
        specs = [(2 * i + 1, tk, tk, True), (2 * i, tk, tk, False), (2 * i, 0, tk, True)]
    else:
        specs = [(r * i + r - 1 - d, 0, tq, True) for d in range(r)]
    return specs[::-1] if left_to_right else specs


def _sb_attn_fwd(q2, kn, vb, gather=()):
    s = q2.shape[0]
    tk = min(ATT_BLOCK, s)
    tq = min(2 * ATT_BLOCK, s)
    r = tq // tk
    nq = s // tq
    assert s // tk <= LANE

    def body(q_ref, k_ref, v_ref, o_ref, ls_ref):
        i = pl.program_id(1)
        heads, urow, ucol = _attn_masks(tq, tk)
        u_incl = (urow >= ucol).astype(BF16)
        row = lax.broadcasted_iota(jnp.int32, (tq, tk), 0)
        col = lax.broadcasted_iota(jnp.int32, (tq, tk), 1)
        qlane = lax.broadcasted_iota(jnp.int32, (tq, LANE), 1)
        q = q_ref[...]

        def blocks(specs, state):
            carry, acc, ls = list(state[0:2]), state[2], list(state[3:5])
            chains = [(d, hh) for d in range(len(specs)) for hh in range(2)]
            kblk, vblk, valid = {}, {}, {}
            for d, (kb, row0, rows, masked) in enumerate(specs):
                off = pl.multiple_of(kb * tk, tk)
                kfull = k_ref[pl.ds(off, tk), :]
                vfull = v_ref[pl.ds(off, tk), :]
                if masked:
                    valid[d] = (kb * tk + col[:rows]) < (i * tq + row0 + row[:rows])
                for hh in range(2):
                    kblk[d, hh] = jnp.where(heads[hh], kfull, jnp.zeros((), BF16))
                    vblk[d, hh] = jnp.where(heads[hh], vfull, jnp.zeros((), BF16))
            z2 = {(d, hh): jnp.minimum(lax.dot_general(q[specs[d][1]:specs[d][1] + specs[d][2]], kblk[d, hh], NT_DIMS,
                                                       preferred_element_type=F32), Z2_MAX) for d, hh in chains}
            cost = {}
            for d, hh in chains:
                cost[d, hh] = _keep_cost(z2[d, hh])
                if specs[d][3]:
                    cost[d, hh] = jnp.where(valid[d], cost[d, hh], 0.0)
            sums = {c: _dot(cost[c].astype(BF16), u_incl) for c in chains}
            a = {}
            for d, hh in chains:
                kb, row0, rows, masked = specs[d]
                rin = carry[hh][row0:row0 + rows] + sums[d, hh]
                a[d, hh] = jnp.exp2(z2[d, hh] - rin * LOG2E)
                if masked:
                    a[d, hh] = jnp.where(valid[d], a[d, hh], 0.0)
                rs = jnp.sum(_at_rows(cost[d, hh], row0, tq), axis=1, keepdims=True)
                ls[hh] = ls[hh] + jnp.where(qlane == kb, rs, 0.0)
                carry[hh] = carry[hh] + rs
            for d, hh in chains:
                acc = acc + _at_rows(_dot(a[d, hh].astype(BF16), vblk[d, hh]), specs[d][1], tq)
            return carry[0], carry[1], acc, ls[0], ls[1]

        zc = jnp.zeros((tq, 1), F32)
        zt = jnp.zeros((tq, LANE), F32)
        state = blocks(_diagonal_specs(i, r, tq, tk, False), (zc, zc, zt, zt, zt))
        pairs = lax.shift_right_logical(i, 1)
        state = lax.fori_loop(
            0, pairs,
            lambda n, st: blocks([(r * i - 1 - 2 * r * n - d, 0, tq, False) for d in range(2 * r)], st), state)
        state = lax.fori_loop(
            0, i - 2 * pairs, lambda n, st: blocks([(r - 1 - d, 0, tq, False) for d in range(r)], st), state)
        o_ref[...] = state[2]
        ls_ref[:, 0:LANE] = state[3]
        ls_ref[:, LANE:2 * LANE] = state[4]

    last_pair = SB_WIDTH // LANE - 1
    step_is = lambda p, i: lambda: (pl.program_id(0) == p) & (pl.program_id(1) == i)
    when = {"start": step_is(0, 0), "forward": step_is(last_pair, 0), "finish": step_is(last_pair, nq - 1)}
    kern, ins, outs, shapes, sems = _carried(body, 3, 2, gather, "gather", when)
    res = pl.pallas_call(
        kern, name="sb_attn_fwd", grid=(SB_WIDTH // LANE, nq),
        in_specs=[pl.BlockSpec((tq, LANE), lambda p, i: (i, p)), pl.BlockSpec((s, LANE), lambda p, i: (0, p)),
                  pl.BlockSpec((s, LANE), lambda p, i: (0, p))] + ins,
        out_specs=[pl.BlockSpec((tq, LANE), lambda p, i: (i, p)),
                   pl.BlockSpec((tq, 2 * LANE), lambda p, i: (i, p))] + outs,
        out_shape=[jax.ShapeDtypeStruct((s, SB_WIDTH), F32), jax.ShapeDtypeStruct((s, 2 * SB_WIDTH), F32)] + shapes,
        scratch_shapes=sems,
        compiler_params=_params(("arbitrary", "arbitrary"), 48),
    )(q2, kn, vb, *gather)
    return res[0], res[1], list(res[2:])


def _sb_attn_bwd(q2, kn, vb, lsum, dmix, send=()):
    s = q2.shape[0]
    tk = min(ATT_BLOCK, s)
    tq = min(2 * ATT_BLOCK, s)
    r = tq // tk
    nq = s // tq

    def body(q_ref, k_ref, v_ref, ls_ref, do_ref, dq_ref, dk_ref, dv_ref):
        i = pl.program_id(1)

        @pl.when(i == 0)
        def _():
            dk_ref[...] = jnp.zeros_like(dk_ref)
            dv_ref[...] = jnp.zeros_like(dv_ref)

        heads, urow, ucol = _attn_masks(tq, tk)
        u_incl = (urow >= ucol).astype(BF16)
        u_pre = (urow <= ucol).astype(BF16)
        lrow = lax.broadcasted_iota(jnp.int32, (LANE, LANE), 0)
        lcol = lax.broadcasted_iota(jnp.int32, (LANE, LANE), 1)
        u_after = (lrow > lcol).astype(BF16)
        row = lax.broadcasted_iota(jnp.int32, (tq, tk), 0)
        col = lax.broadcasted_iota(jnp.int32, (tq, tk), 1)
        qlane = lax.broadcasted_iota(jnp.int32, (tq, LANE), 1)
        qheads = [(qlane >= hh * HEAD_DIM) & (qlane < (hh + 1) * HEAD_DIM) for hh in range(2)]
        q = q_ref[...]
        dob = do_ref[...].astype(BF16)
        qm = [jnp.where(qheads[hh], q, jnp.zeros((), BF16)) for hh in range(2)]
        dom = [jnp.where(qheads[hh], dob, jnp.zeros((), BF16)) for hh in range(2)]
        after = []
        for hh in range(2):
            ls = ls_ref[:, hh * LANE:(hh + 1) * LANE]
            hi = ls.astype(BF16)
            mid = (ls - hi.astype(F32)).astype(BF16)
            lo = (ls - hi.astype(F32) - mid.astype(F32)).astype(BF16)
            after.append(_dot(hi, u_after) + _dot(mid, u_after) + _dot(lo, u_after))

        def blocks(specs, state):
            cp, dq = list(state[0:2]), state[2]
            chains = [(d, hh) for d in range(len(specs)) for hh in range(2)]
            rows_of = lambda d: slice(specs[d][1], specs[d][1] + specs[d][2])
            offs, kblk, vblk, valid = {}, {}, {}, {}
            for d, (kb, row0, rows, masked) in enumerate(specs):
                offs[d] = pl.multiple_of(kb * tk, tk)
                kfull = k_ref[pl.ds(offs[d], tk), :]
                vblk[d] = v_ref[pl.ds(offs[d], tk), :]
                if masked:
                    valid[d] = (kb * tk + col[:rows]) < (i * tq + row0 + row[:rows])
                for hh in range(2):
                    kblk[d, hh] = jnp.where(heads[hh], kfull, jnp.zeros((), BF16))
            z2 = {(d, hh): jnp.minimum(lax.dot_general(q[rows_of(d)], kblk[d, hh], NT_DIMS, preferred_element_type=F32),
                                       Z2_MAX) for d, hh in chains}
            da = {(d, hh): lax.dot_general(dom[hh][rows_of(d)], vblk[d], NT_DIMS, preferred_element_type=F32)
                  for d, hh in chains}
            cost, sig = {}, {}
            for d, hh in chains:
                cost[d, hh] = _keep_cost(z2[d, hh])
                sig[d, hh] = jnp.exp2(z2[d, hh] - cost[d, hh] * LOG2E)
                if specs[d][3]:
                    cost[d, hh] = jnp.where(valid[d], cost[d, hh], 0.0)
            sums = {c: _dot(cost[c].astype(BF16), u_incl) for c in chains}
            a, g = {}, {}
            for d, hh in chains:
                kb, row0, rows, masked = specs[d]
                cr = jnp.sum(jnp.where(qlane[:rows] == kb, after[hh][rows_of(d)], 0.0), axis=1, keepdims=True)
                a[d, hh] = jnp.exp2(z2[d, hh] - (cr + sums[d, hh]) * LOG2E)
                if masked:
                    a[d, hh] = jnp.where(valid[d], a[d, hh], 0.0)
                g[d, hh] = da[d, hh] * a[d, hh]
            pre = {c: _dot(g[c].astype(BF16), u_pre) for c in chains}
            dzb = {}
            for d, hh in chains:
                dz = g[d, hh] - sig[d, hh] * (cp[hh][rows_of(d)] + pre[d, hh])
                if specs[d][3]:
                    dz = jnp.where(valid[d], dz, 0.0)
                dzb[d, hh] = dz.astype(BF16)
                cp[hh] = cp[hh] + _at_rows(jnp.sum(g[d, hh], axis=1, keepdims=True), specs[d][1], tq)
            for d in range(len(specs)):
                dv_ref[pl.ds(offs[d], tk), :] += sum(
                    lax.dot_general(a[d, hh].astype(BF16), dom[hh][rows_of(d)], TN_DIMS, preferred_element_type=F32)
                    for hh in range(2))
                dk_ref[pl.ds(offs[d], tk), :] += sum(
                    lax.dot_general(dzb[d, hh], qm[hh][rows_of(d)], TN_DIMS, preferred_element_type=F32)
                    for hh in range(2))
            for d, hh in chains:
                dq = dq + _at_rows(_dot(dzb[d, hh], kblk[d, hh]), specs[d][1], tq)
            return cp[0], cp[1], dq

        zc = jnp.zeros((tq, 1), F32)
        pairs = lax.shift_right_logical(i, 1)
        state = lax.fori_loop(
            0, pairs, lambda n, st: blocks([(2 * r * n + d, 0, tq, False) for d in range(2 * r)], st),
            (zc, zc, jnp.zeros((tq, LANE), F32)))
        state = lax.fori_loop(
            0, i - 2 * pairs, lambda n, st: blocks([(2 * r * pairs + d, 0, tq, False) for d in range(r)], st), state)
        state = blocks(_diagonal_specs(i, r, tq, tk, True), state)
        dq_ref[...] = state[2]

    blk = pl.BlockSpec((tq, LANE), lambda p, i: (i, p))
    whole = pl.BlockSpec((s, LANE), lambda p, i: (0, p))
    last_pair = SB_WIDTH // LANE - 1
    step_is = lambda p, i: lambda: (pl.program_id(0) == p) & (pl.program_id(1) == i)
    kern, ins, outs, shapes, sems = _carried(body, 5, 3, send, "scatter",
                                             {"start": step_is(0, 0), "finish": step_is(last_pair, nq - 1)})
    res = pl.pallas_call(
        kern, name="sb_attn_bwd", grid=(SB_WIDTH // LANE, nq),
        in_specs=[blk, whole, whole, pl.BlockSpec((tq, 2 * LANE), lambda p, i: (i, p)), blk] + ins,
        out_specs=[blk, whole, whole] + outs,
        out_shape=[jax.ShapeDtypeStruct((s, SB_WIDTH), F32)] * 3 + shapes,
        scratch_shapes=sems,
        compiler_params=_params(("arbitrary", "arbitrary"), 56),
    )(q2, kn, vb, lsum, dmix, *send)
    return res[0], res[1], res[2], list(res[3:])


def _sgu_spatial(zn, wm_ref, lane, c):
    parts = []
    for p in range(SG_WIDTH // LANE):
        blk = zn[c * CHUNK:(c + 1) * CHUNK, p * LANE:(p + 1) * LANE].astype(BF16)
        lo = jnp.where(lane < HEAD_DIM, blk, jnp.zeros((), BF16))
        hi = jnp.where(lane >= HEAD_DIM, blk, jnp.zeros((), BF16))
        parts.append(_dot(wm_ref[2 * p], lo) + _dot(wm_ref[2 * p + 1], hi))
    return jnp.concatenate(parts, axis=1)


def _sgu_fwd(proj, gz, wm, bt, bd):
    s = proj.shape[0]
    tm = min(512, s)

    def body(u_ref, z_ref, gz_ref, wm_ref, bt_ref, bd_ref, o_ref):
        lane = lax.broadcasted_iota(jnp.int32, (CHUNK, LANE), 1)
        ug = _gelu(u_ref[...])
        zg = _gelu(z_ref[...])
        zn = zg * lax.rsqrt(_group_mean(zg * zg, bd_ref[...]) + EPS) * gz_ref[...]
        for c in range(tm // CHUNK):
            sp = _sgu_spatial(zn, wm_ref, lane, c) + bt_ref[...]
            o_ref[c * CHUNK:(c + 1) * CHUNK, :] = ug[c * CHUNK:(c + 1) * CHUNK, :] * sp

    col = lambda c: pl.BlockSpec((tm, SG_WIDTH), lambda i: (i, c))
    return pl.pallas_call(
        body, name="sgu_fwd", grid=(s // tm,),
        in_specs=[col(3), col(4), _full((1, SG_WIDTH)), _full((SG_GROUPS, CHUNK, CHUNK)), _full((CHUNK, SG_WIDTH)),
                  _full((SG_WIDTH, SG_WIDTH))],
        out_specs=pl.BlockSpec((tm, SG_WIDTH), lambda i: (i, 0)),
        out_shape=jax.ShapeDtypeStruct((s, SG_WIDTH), F32),
        compiler_params=_params(("arbitrary",), 32),
    )(proj, proj, gz, wm, bt, bd)


def _sgu_bwd(proj, dmix, gz, wm, wmt, bt, bd):
    s = proj.shape[0]
    tm = min(512, s)

    def body(u_ref, z_ref, dg_ref, gz_ref, wm_ref, wmt_ref, bt_ref, bd_ref, o_ref, dwm_ref, dbt_ref, dgz_ref):
        @pl.when(pl.program_id(0) == 0)
        def _():
            dwm_ref[...] = jnp.zeros_like(dwm_ref)
            dbt_ref[...] = jnp.zeros_like(dbt_ref)
            dgz_ref[...] = jnp.zeros_like(dgz_ref)

        lane = lax.broadcasted_iota(jnp.int32, (CHUNK, LANE), 1)
        bdv = bd_ref[...]
        u = u_ref[...]
        z = z_ref[...]
        ug = _gelu(u)
        zg = _gelu(z)
        r = lax.rsqrt(_group_mean(zg * zg, bdv) + EPS)
        zh = zg * r
        zn = zh * gz_ref[...]
        dzn_rows = []
        for c in range(tm // CHUNK):
            rows = slice(c * CHUNK, (c + 1) * CHUNK)
            sp = _sgu_spatial(zn, wm_ref, lane, c) + bt_ref[...]
            dgg = dg_ref[rows, :]
            ds = dgg * ug[rows, :]
            o_ref[rows, 0:SG_WIDTH] = dgg * sp * _gelu_grad(u[rows, :])
            dbt_ref[...] += ds
            parts = []
            for p in range(SG_WIDTH // LANE):
                dsb = ds[:, p * LANE:(p + 1) * LANE].astype(BF16)
                znb = zn[rows, p * LANE:(p + 1) * LANE].astype(BF16)
                acc = jnp.zeros((CHUNK, LANE), F32)
                for hh in range(2):
                    hm = (lane >= hh * HEAD_DIM) & (lane < (hh + 1) * HEAD_DIM)
                    dsm = jnp.where(hm, dsb, jnp.zeros((), BF16))
                    znm = jnp.where(hm, znb, jnp.zeros((), BF16))
                    acc = acc + _dot(wmt_ref[2 * p + hh], dsm)
                    dwm_ref[2 * p + hh] += lax.dot_general(dsm, znm, NT_DIMS, preferred_element_type=F32)
                parts.append(acc)
            dzn_rows.append(jnp.concatenate(parts, axis=1))
        dzn = jnp.concatenate(dzn_rows, axis=0)
        dgz_ref[...] += jnp.sum(dzn * zh, axis=0, keepdims=True)
        dn = dzn * gz_ref[...]
        o_ref[:, SG_WIDTH:2 * SG_WIDTH] = r * (dn - zh * _group_mean(dn * zh, bdv)) * _gelu_grad(z)

    col = lambda c: pl.BlockSpec((tm, SG_WIDTH), lambda i: (i, c))
    wspec = _full((SG_GROUPS, CHUNK, CHUNK))
    return pl.pallas_call(
        body, name="sgu_bwd", grid=(s // tm,),
        in_specs=[col(3), col(4), pl.BlockSpec((tm, SG_WIDTH), lambda i: (i, 1)), _full((1, SG_WIDTH)), wspec, wspec,
                  _full((CHUNK, SG_WIDTH)), _full((SG_WIDTH, SG_WIDTH))],
        out_specs=[pl.BlockSpec((tm, 2 * SG_WIDTH), lambda i: (i, 0)), wspec, _full((CHUNK, SG_WIDTH)),
                   _full((1, SG_WIDTH))],
        out_shape=[jax.ShapeDtypeStruct((s, 2 * SG_WIDTH), F32), jax.ShapeDtypeStruct((SG_GROUPS, CHUNK, CHUNK), F32),
                   jax.ShapeDtypeStruct((CHUNK, SG_WIDTH), F32), jax.ShapeDtypeStruct((1, SG_WIDTH), F32)],
        compiler_params=_params(("arbitrary",), 40),
    )(proj, proj, dmix, gz, wm, wmt, bt, bd)


def _shifted_copies(ext, sh):
    e = ext[...]
    sh[0] = e
    for b in range(1, 8):
        sh[b] = pltpu.roll(e, e.shape[0] - b, 0)


def _shifted_rows(sh, off, r0, cols):
    return sh[off % 8, pl.ds(r0 + (off - off % 8), CONV_ROWS), cols]


def _for_row_blocks(rows, block, looped):
    if not looped:
        for r0 in range(0, rows, CONV_ROWS):
            block(r0)
        return

    def step(rb, carry):
        block(pl.multiple_of(rb * CONV_ROWS, CONV_ROWS))
        return carry

    lax.fori_loop(0, rows // CONV_ROWS, step, 0)


def _dwconv(sh, w_ref, offsets, rows, store, looped):
    def block(r0):
        for c0 in range(0, sh.shape[2], CONV_LANES):
            cols = slice(c0, c0 + CONV_LANES)
            acc = jnp.zeros((CONV_ROWS, CONV_LANES), F32)
            for j, off in enumerate(offsets):
                acc = acc + w_ref[j:j + 1, cols] * _shifted_rows(sh, off, r0, cols)
            store(r0, cols, acc)

    _for_row_blocks(rows, block, looped)


def _conf_mid_fwd(p, wdw, bdw, lng, lnb):
    s = p.shape[0]
    c = p.shape[1] // 2
    tm = min(256, s)

    def body(a_ref, gt_ref, w_ref, b_ref, g_ref, beta_ref, yc_ref, y2_ref, ext, sh):
        i = pl.program_id(0)

        @pl.when(i == 0)
        def _():
            ext[0:HALO, :] = jnp.zeros((HALO, c), F32)

        @pl.when(i > 0)
        def _():
            ext[0:HALO, :] = ext[tm:tm + HALO, :]

        ext[HALO:HALO + tm, :] = a_ref[...] * _sigmoid(gt_ref[...])

        def store(r0, cols, block):
            yc_ref[pl.ds(r0, CONV_ROWS), cols] = block + b_ref[:, cols]

        _shifted_copies(ext, sh)
        _dwconv(sh, w_ref, [HALO - CONV_K + 1 + j for j in range(CONV_K)], tm, store, looped=False)
        acc = yc_ref[...]
        xc = acc - jnp.mean(acc, axis=-1, keepdims=True)
        ln = xc * lax.rsqrt(jnp.mean(xc * xc, axis=-1, keepdims=True) + EPS) * g_ref[...] + beta_ref[...]
        y2_ref[...] = (ln * _sigmoid(ln)).astype(BF16)

    vec = _full((1, c))
    return pl.pallas_call(
        body, name="conf_mid_fwd", grid=(s // tm,),
        in_specs=[pl.BlockSpec((tm, c), lambda i: (i, 0)), pl.BlockSpec((tm, c), lambda i: (i, 1)), _full((HALO, c)), vec,
                  vec, vec],
        out_specs=[pl.BlockSpec((tm, c), lambda i: (i, 0)), pl.BlockSpec((tm, c), lambda i: (i, 0))],
        out_shape=[jax.ShapeDtypeStruct((s, c), F32), jax.ShapeDtypeStruct((s, c), BF16)],
        scratch_shapes=[pltpu.VMEM((HALO + tm, c), F32), pltpu.VMEM((8, HALO + tm, c), F32)],
        compiler_params=_params(("arbitrary",), 40),
    )(p, p, wdw, bdw, lng, lnb)


def _conf_mid_bwd(p, yc, dout, x, gm, wdw, lng, lnb, w1, w2, send=()):
    s = p.shape[0]
    c = p.shape[1] // 2
    tm = min(256, s)
    n = s // tm

    def body(a_ref, gt_ref, yc_ref, dout_ref, x_ref, gm_ref, w_ref, g_ref, beta_ref, w1_hbm, w2_hbm,
             dp_ref, dw_ref, dbdw_ref, dlg_ref, dlb_ref, db1_ref, db2_ref, dx_ref, dgm_ref,
             yv, extd, dyv, dwacc, shd, w1_ref, w2_ref, wsem):
        i = pl.program_id(0)

        @pl.when(i == 0)
        def _():
            loads = [pltpu.make_async_copy(w1_hbm, w1_ref, wsem.at[0]),
                     pltpu.make_async_copy(w2_hbm, w2_ref, wsem.at[1])]
            for cp in loads:
                cp.start()
            extd[tm:tm + HALO, :] = jnp.zeros((HALO, c), F32)
            for ref in (dw_ref, dbdw_ref, dlg_ref, dlb_ref, db1_ref, db2_ref, dgm_ref, dwacc):
                ref[...] = jnp.zeros_like(ref)
            for cp in loads:
                cp.wait()

        @pl.when(i > 0)
        def _():
            extd[tm:tm + HALO, :] = extd[0:HALO, :]

        dout = dout_ref[...]
        dy2 = lax.dot_general(dout.astype(BF16), w2_ref[...], NT_DIMS, preferred_element_type=F32)
        a = a_ref[...]
        sg = _sigmoid(gt_ref[...])
        yv[...] = a * sg
        ycv = yc_ref[...]
        xc = ycv - jnp.mean(ycv, axis=-1, keepdims=True)
        rstd = lax.rsqrt(jnp.mean(xc * xc, axis=-1, keepdims=True) + EPS)
        xh = xc * rstd
        ln = xh * g_ref[...] + beta_ref[...]
        dln = dy2 * _silu_grad(ln, _sigmoid(ln))
        dlg_ref[...] += jnp.sum(dln * xh, axis=0, keepdims=True)
        dlb_ref[...] += jnp.sum(dln, axis=0, keepdims=True)
        dxh = dln * g_ref[...]
        dyc = rstd * (dxh - jnp.mean(dxh, axis=-1, keepdims=True) - xh * jnp.mean(dxh * xh, axis=-1, keepdims=True))
        extd[0:tm, :] = dyc
        dbdw_ref[...] += jnp.sum(dyc, axis=0, keepdims=True)
        db2_ref[...] += jnp.sum(dout, axis=0, keepdims=True)

        _shifted_copies(extd, shd)

        def conv_block(r0):
            for c0 in range(0, c, CONV_LANES):
                cols = slice(c0, c0 + CONV_LANES)
                ysub = yv[pl.ds(r0, CONV_ROWS), cols]
                acc = jnp.zeros((CONV_ROWS, CONV_LANES), F32)
                for j in range(CONV_K):
                    later = _shifted_rows(shd, CONV_K - 1 - j, r0, cols)
                    acc = acc + w_ref[j:j + 1, cols] * later
                    dwacc[8 * j:8 * j + 8, cols] += (ysub * later).reshape(CONV_ROWS // 8, 8, CONV_LANES).sum(axis=0)
                dyv[pl.ds(r0, CONV_ROWS), cols] = acc

        _for_row_blocks(tm, conv_block, looped=True)

        @pl.when(i == n - 1)
        def _():
            dw_ref[...] = dwacc[...].reshape(HALO, 8, c).sum(axis=1)

        dy = dyv[...]
        da = dy * sg
        dgt = dy * a * sg * (1.0 - sg)
        dab, dgtb = da.astype(BF16), dgt.astype(BF16)
        dp_ref[:, 0:c] = dab
        dp_ref[:, c:2 * c] = dgtb
        db1_ref[:, 0:c] += jnp.sum(da, axis=0, keepdims=True)
        db1_ref[:, c:2 * c] += jnp.sum(dgt, axis=0, keepdims=True)
        dh = (lax.dot_general(dab, w1_ref[:, 0:c], NT_DIMS, preferred_element_type=F32)
              + lax.dot_general(dgtb, w1_ref[:, c:2 * c], NT_DIMS, preferred_element_type=F32))
        xv = x_ref[...]
        r = lax.rsqrt(jnp.mean(xv * xv, axis=-1, keepdims=True) + EPS)
        xn = xv * r
        dgm_ref[...] += jnp.sum(dh * xn, axis=0, keepdims=True)
        dn = dh * gm_ref[...]
        dx_ref[...] = dout + r * (dn - xn * jnp.mean(dn * xn, axis=-1, keepdims=True))

    rev = lambda col: pl.BlockSpec((tm, c), lambda i: (n - 1 - i, col))
    vec = _full((1, c))
    step_is = lambda i: lambda: pl.program_id(0) == i
    kern, ins, outs, shapes, sems = _carried(body, 11, 9, send, "scatter",
                                             {"start": step_is(0), "finish": step_is(n - 1)})
    res = pl.pallas_call(
        kern, name="conf_mid_bwd", grid=(n,),
        in_specs=[rev(0), rev(1), rev(0), rev(0), rev(0), vec, _full((HALO, c)), vec, vec,
                  pl.BlockSpec(memory_space=pl.ANY), pl.BlockSpec(memory_space=pl.ANY)] + ins,
        out_specs=[pl.BlockSpec((tm, 2 * c), lambda i: (n - 1 - i, 0)), _full((HALO, c)), vec, vec, vec,
                   _full((1, 2 * c)), vec, rev(0), vec] + outs,
        out_shape=[jax.ShapeDtypeStruct((s, 2 * c), BF16), jax.ShapeDtypeStruct((HALO, c), F32),
                   jax.ShapeDtypeStruct((1, c), F32), jax.ShapeDtypeStruct((1, c), F32), jax.ShapeDtypeStruct((1, c), F32),
                   jax.ShapeDtypeStruct((1, 2 * c), F32), jax.ShapeDtypeStruct((1, c), F32),
                   jax.ShapeDtypeStruct((s, c), F32), jax.ShapeDtypeStruct((1, c), F32)] + shapes,
        scratch_shapes=[pltpu.VMEM((tm, c), F32), pltpu.VMEM((tm + HALO, c), F32), pltpu.VMEM((tm, c), F32),
                        pltpu.VMEM((8 * HALO, c), F32), pltpu.VMEM((8, tm + HALO, c), F32),
                        pltpu.VMEM((c, 2 * c), BF16), pltpu.VMEM((c, c), BF16),
                        pltpu.SemaphoreType.DMA((2,))] + sems,
        compiler_params=_params(("arbitrary",), 56),
    )(p, p, yc, dout, x, gm, wdw, lng, lnb, w1, w2, *send)
    return tuple(res[:9]) + (list(res[9:]),)


def _ffn_fwd(x, g, wup, wdw, bdw, wdn):
    s, d = x.shape
    ff = wdn.shape[0]
    tm = min(256, s)

    def body(x_ref, g_ref, wup_ref, wdw_ref, bdw_ref, wdn_ref, y_ref, h_ref, u_ref, carry):
        @pl.when(pl.program_id(0) == 0)
        def _():
            carry[...] = jnp.zeros_like(carry)

        xv = x_ref[...]
        r = lax.rsqrt(jnp.mean(xv * xv, axis=-1, keepdims=True) + EPS)
        h = (xv * r * g_ref[...]).astype(BF16)
        h_ref[...] = h
        acc = xv
        up = lambda c: (_dot(h, wup_ref[:, c:c + FF_CHUNK]), _dot(h, wup_ref[:, ff + c:ff + c + FF_CHUNK]))
        ahead = up(0)
        for c in range(0, ff, FF_CHUNK):
            cs = slice(c, c + FF_CHUNK)
            gp, val = ahead
            if c + FF_CHUNK < ff:
                ahead = up(c + FF_CHUNK)
            u_ref[:, cs] = gp.astype(BF16)
            u_ref[:, ff + c:ff + c + FF_CHUNK] = val.astype(BF16)
            prev = carry[:, cs]
            gate = (wdw_ref[0:1, cs] * _shift_down(gp, prev, 2) + wdw_ref[1:2, cs] * _shift_down(gp, prev, 1)
                    + wdw_ref[2:3, cs] * gp + bdw_ref[:, cs])
            act = gate * _sigmoid(gate) * val
            acc = acc + _dot(act.astype(BF16), wdn_ref[cs, :])
            carry[:, cs] = gp[tm - 8:tm, :]
        y_ref[...] = acc

    return pl.pallas_call(
        body, name="ffn_fwd", grid=(s // tm,),
        in_specs=[pl.BlockSpec((tm, d), lambda i: (i, 0)), _full((1, d)), _full((d, 2 * ff)), _full((8, ff)),
                  _full((1, ff)), _full((ff, d))],
        out_specs=[pl.BlockSpec((tm, d), lambda i: (i, 0)), pl.BlockSpec((tm, d), lambda i: (i, 0)),
                   pl.BlockSpec((tm, 2 * ff), lambda i: (i, 0))],
        out_shape=[jax.ShapeDtypeStruct((s, d), F32), jax.ShapeDtypeStruct((s, d), BF16),
                   jax.ShapeDtypeStruct((s, 2 * ff), BF16)],
        scratch_shapes=[pltpu.VMEM((8, ff), F32)],
        compiler_params=_params(("arbitrary",), 56),
    )(x, g, wup, wdw, bdw, wdn)


def _ffn_bwd(dy, u, x, g, wdw, bdw, wdn, wup, send=()):
    s, d = dy.shape
    ff = wdn.shape[0]
    tm = min(256, s)
    n = s // tm
    hb = tm // 16

    def body(dy_ref, u_hbm, uh_ref, x_ref, g_ref, wdw_ref, bdw_ref, wdn_hbm, wup_hbm,
             du_ref, act_ref, dw_ref, db_ref, dx_ref, dg_ref, carry, wdn_ref, wup_ref, wsem, ubuf, usem):
        i = pl.program_id(0)

        def fetch(step):
            slot = lax.rem(step, 3)
            rows = pl.ds(pl.multiple_of((n - 1 - step) * tm, tm), tm)
            return pltpu.make_async_copy(u_hbm.at[rows], ubuf.at[slot], usem.at[slot])

        @pl.when(i == 0)
        def _():
            fetch(0).start()
            if n > 1:
                fetch(1).start()

        @pl.when(i + 2 < n)
        def _():
            fetch(i + 2).start()

        fetch(i).wait()
        u_ref = ubuf.at[lax.rem(i, 3)]

        @pl.when(i == 0)
        def _():
            loads = [pltpu.make_async_copy(wdn_hbm, wdn_ref, wsem.at[0]),
                     pltpu.make_async_copy(wup_hbm, wup_ref, wsem.at[1])]
            for cp in loads:
                cp.start()
            carry[...] = jnp.zeros_like(carry)
            dw_ref[...] = jnp.zeros_like(dw_ref)
            db_ref[...] = jnp.zeros_like(db_ref)
            dg_ref[...] = jnp.zeros_like(dg_ref)
            for cp in loads:
                cp.wait()

        dyv = dy_ref[...]
        dyb = dyv.astype(BF16)
        down = lambda c: lax.dot_general(dyb, wdn_ref[c:c + FF_CHUNK, :], NT_DIMS, preferred_element_type=F32)
        ahead = down(0)
        dh = jnp.zeros((tm, d), F32)
        for c in range(0, ff, FF_CHUNK):
            cs = slice(c, c + FF_CHUNK)
            vs = slice(ff + c, ff + c + FF_CHUNK)
            da = ahead
            if c + FF_CHUNK < ff:
                ahead = down(c + FF_CHUNK)
            gp = u_ref[:, cs].astype(F32)
            val = u_ref[:, vs].astype(F32)
            prev = jnp.where(i < n - 1, uh_ref[:, cs].astype(F32)[8:16], 0.0)
            g1 = _shift_down(gp, prev, 1)
            g2 = _shift_down(gp, prev, 2)
            gate = wdw_ref[0:1, cs] * g2 + wdw_ref[1:2, cs] * g1 + wdw_ref[2:3, cs] * gp + bdw_ref[:, cs]
            sg = _sigmoid(gate)
            si = gate * sg
            act_ref[:, cs] = (si * val).astype(BF16)
            dgate = da * val * _silu_grad(gate, sg)
            nxt = carry[:, cs]
            dgp = (wdw_ref[2:3, cs] * dgate + wdw_ref[1:2, cs] * _shift_up(dgate, nxt, 1)
                   + wdw_ref[0:1, cs] * _shift_up(dgate, nxt, 2)).astype(BF16)
            dval = (da * si).astype(BF16)
            du_ref[:, cs] = dgp
            du_ref[:, vs] = dval
            dh = (dh + lax.dot_general(dgp, wup_ref[:, cs], NT_DIMS, preferred_element_type=F32)
                  + lax.dot_general(dval, wup_ref[:, vs], NT_DIMS, preferred_element_type=F32))
            dw_ref[:, cs] += jnp.concatenate(
                [jnp.sum(dgate * g2, axis=0, keepdims=True), jnp.sum(dgate * g1, axis=0, keepdims=True),
                 jnp.sum(dgate * gp, axis=0, keepdims=True), jnp.zeros((5, FF_CHUNK), F32)], axis=0)
            db_ref[:, cs] += jnp.sum(dgate, axis=0, keepdims=True)
            carry[:, cs] = dgate[0:8, :]
        xv = x_ref[...]
        r = lax.rsqrt(jnp.mean(xv * xv, axis=-1, keepdims=True) + EPS)
        xh = xv * r
        dg_ref[...] += jnp.sum(dh * xh, axis=0, keepdims=True)
        dn = dh * g_ref[...]
        dx_ref[...] = dyv + r * (dn - xh * jnp.mean(dn * xh, axis=-1, keepdims=True))

    step_is = lambda i: lambda: pl.program_id(0) == i
    kern, ins, outs, shapes, sems = _carried(body, 9, 6, send, "scatter",
                                             {"start": step_is(0), "finish": step_is(n - 1)})
    rev = lambda cols: pl.BlockSpec((tm, cols), lambda i: (n - 1 - i, 0))
    any_spec = pl.BlockSpec(memory_space=pl.ANY)
    res = pl.pallas_call(
        kern, name="ffn_bwd", grid=(n,),
        in_specs=[rev(d), any_spec,
                  pl.BlockSpec((16, 2 * ff), lambda i: (jnp.maximum((n - 1 - i) * hb - 1, 0), 0)),
                  rev(d), _full((1, d)), _full((8, ff)), _full((1, ff)), any_spec, any_spec] + ins,
        out_specs=[rev(2 * ff), rev(ff), _full((8, ff)), _full((1, ff)), rev(d), _full((1, d))] + outs,
        out_shape=[jax.ShapeDtypeStruct((s, 2 * ff), BF16), jax.ShapeDtypeStruct((s, ff), BF16),
                   jax.ShapeDtypeStruct((8, ff), F32), jax.ShapeDtypeStruct((1, ff), F32),
                   jax.ShapeDtypeStruct((s, d), F32), jax.ShapeDtypeStruct((1, d), F32)] + shapes,
        scratch_shapes=[pltpu.VMEM((8, ff), F32), pltpu.VMEM((ff, d), BF16), pltpu.VMEM((d, 2 * ff), BF16),
                        pltpu.SemaphoreType.DMA((2,)), pltpu.VMEM((3, tm, 2 * ff), BF16),
                        pltpu.SemaphoreType.DMA((3,))] + sems,
        compiler_params=_params(("arbitrary",), 56),
    )(dy, u, u, x, g, wdw, bdw, wdn, wup, *send)
    return tuple(res[:6]) + (list(res[6:]),)


def _loss_head(y, target):
    s, d = y.shape
    tm = min(512, s)

    def body(y_ref, t_ref, l_ref, dy_ref):
        @pl.when(pl.program_id(0) == 0)
        def _():
            l_ref[...] = jnp.zeros_like(l_ref)

        err = y_ref[...] - t_ref[...]
        dy_ref[...] = err * (1.0 / d)
        l_ref[...] += 0.5 * jnp.sum(jnp.mean(err * err, axis=-1, keepdims=True), axis=0, keepdims=True)

    return pl.pallas_call(
        body, name="loss_head", grid=(s // tm,),
        in_specs=[pl.BlockSpec((tm, d), lambda i: (i, 0)), pl.BlockSpec((tm, d), lambda i: (i, 0))],
        out_specs=[_full((8, LANE)), pl.BlockSpec((tm, d), lambda i: (i, 0))],
        out_shape=[jax.ShapeDtypeStruct((8, LANE), F32), jax.ShapeDtypeStruct((s, d), F32)],
        compiler_params=_params(("arbitrary",), 32),
    )(y, target)


def _row_tile(rows, limit=512):
    for cand in range(min(limit, rows) // 16 * 16, 0, -16):
        if rows % cand == 0:
            return cand
    return rows


def _reduce_adamw(parts, w, m, v, name):
    nl = len(parts)
    _, a, b = parts[0].shape
    ta = _row_tile(a, 256)

    def body(*refs):
        p_refs = refs[:nl]
        w_ref, m_ref, v_ref, g_ref, d_ref, mo_ref, vo_ref = refs[nl:]
        for layer in range(nl):
            @pl.when(pl.program_id(0) == layer)
            def _(p_ref=p_refs[layer]):
                g = p_ref[0].astype(F32)
                for k in range(1, N_DEV):
                    g = g + p_ref[k].astype(F32)
                g_ref[0] = g

        g = g_ref[0]
        mn = ADAM_B1 * m_ref[...] + (1.0 - ADAM_B1) * g
        vn = ADAM_B2 * v_ref[...] + (1.0 - ADAM_B2) * (g * g)
        mo_ref[...] = mn
        vo_ref[...] = vn
        m_hat = mn / (1.0 - ADAM_B1 ** ADAM_STEP)
        v_hat = vn / (1.0 - ADAM_B2 ** ADAM_STEP)
        d_ref[...] = -ADAM_LR * (m_hat / (jnp.sqrt(v_hat) + ADAM_EPS) + ADAM_WD * w_ref[...])

    blk = pl.BlockSpec((1, ta, b), lambda l, i: (l, i, 0))
    part = lambda layer: pl.BlockSpec((N_DEV, ta, b), lambda l, i: (0, jnp.where(l == layer, i, 0), 0))
    return pl.pallas_call(
        body, name=name, grid=(nl, a // ta),
        in_specs=[part(layer) for layer in range(nl)] + [blk, blk, blk],
        out_specs=[blk, blk, blk, blk],
        out_shape=[jax.ShapeDtypeStruct((nl, a, b), F32)] * 4,
        compiler_params=_params(("arbitrary", "arbitrary"), 56),
    )(*parts, w, m, v)


def _unshard(w8, layer, name):
    _, _, k, n = w8.shape
    tk = _row_tile(k, 256)

    def body(w_ref, o_ref):
        for d in range(N_DEV):
            o_ref[:, d * n:(d + 1) * n] = w_ref[d, 0]

    return pl.pallas_call(
        body, name=name, grid=(k // tk,),
        in_specs=[pl.BlockSpec((N_DEV, 1, tk, n), lambda i: (0, layer, i, 0))],
        out_specs=pl.BlockSpec((tk, N_DEV * n), lambda i: (i, 0)),
        out_shape=jax.ShapeDtypeStruct((k, N_DEV * n), w8.dtype),
        compiler_params=_params(("arbitrary",), 32),
    )(w8)


def _shard_cast(g, name):
    k, n8 = g.shape
    n = n8 // N_DEV
    tk = _row_tile(k, 256)

    def body(g_ref, o_ref):
        for d in range(N_DEV):
            o_ref[d] = g_ref[:, d * n:(d + 1) * n].astype(BF16)

    return pl.pallas_call(
        body, name=name, grid=(k // tk,),
        in_specs=[pl.BlockSpec((tk, n8), lambda i: (i, 0))],
        out_specs=pl.BlockSpec((N_DEV, tk, n), lambda i: (0, i, 0)),
        out_shape=jax.ShapeDtypeStruct((N_DEV, k, n), BF16),
        compiler_params=_params(("arbitrary",), 32),
    )(g)


def _row(v):
    return v.reshape(1, -1)


def _group_ones():
    idx = jnp.arange(SB_WIDTH) // HEAD_DIM
    return (idx[:, None] == idx[None, :]).astype(BF16)


def _pad_rows(w, rows):
    return jnp.concatenate([w, jnp.zeros((rows - w.shape[0], w.shape[1]), w.dtype)], axis=0)


def _local_step(x, target, wt, traffic):
    scale = HEAD_DIM ** -0.5
    bd = _group_ones()
    tril = jnp.tril(jnp.ones((CHUNK, CHUNK), dtype=bool))
    saved = []
    for i in range(DEPTH):
        j = i // 2
        lay = {"x_mix": x}
        if i % 2 == 0:
            proj, h = _rms_matmul(x, _row(wt["mix_norm_g"][i]), wt["sb_w_in"][j], jnp.zeros((1, IN_WIDTH), F32), "in_proj")
            gq = _row(jnp.tile(wt["sb_q_norm_g"][j], SB_WIDTH // HEAD_DIM)) * scale
            gk = _row(jnp.tile(wt["sb_k_norm_g"][j], SB_WIDTH // HEAD_DIM))
            qn, kn, vb = _qk_prep(proj, gq * LOG2E, gk, bd)
            o, lsum, gathered = _sb_attn_fwd(qn, kn, vb, gather=traffic.rest() if i == 0 else ())
            if i == 0:
                traffic.install(wt, gathered)
            wm = jnp.where(tril[None], wt["sg_w_spatial"][j], 0.0)
            wmb = wm.astype(BF16)
            wmt = jnp.swapaxes(wm, 1, 2).astype(BF16)
            bt = jnp.repeat(wt["sg_b_spatial"][j].T, HEAD_DIM, axis=1)
            gz = _row(wt["sg_z_norm_g"][j])
            gg = _sgu_fwd(proj, gz, wmb, bt, bd)
            x, mix = _out_proj(x, o, gg, wt["hyb_w_out"][j], "out_proj")
            lay.update(proj=proj, h=h, gq=gq, gk=gk, qn=qn, kn=kn, vb=vb, lsum=lsum, wmb=wmb, wmt=wmt, bt=bt, gz=gz, mix=mix)
        else:
            p, h = _rms_matmul(x, _row(wt["mix_norm_g"][i]), wt["cv_w_pw1"][j], _row(wt["cv_b_pw1"][j]), "conf_pw1")
            wdw = _pad_rows(wt["cv_w_dw"][j], HALO)
            yc, y2 = _conf_mid_fwd(p, wdw, _row(wt["cv_b_dw"][j]), _row(wt["cv_ln_g"][j]), _row(wt["cv_ln_b"][j]))
            x = _res_matmul(x, y2, wt["cv_w_pw2"][j], _row(wt["cv_b_pw2"][j]), "conf_pw2")
            lay.update(p=p, h=h, wdw=wdw, yc=yc, y2=y2)
        lay["x_ffn"] = x
        fdw = _pad_rows(wt["ffn_w_dw"][i], 8)
        x, hf, u = _ffn_fwd(x, _row(wt["ffn_norm_g"][i]), wt["ffn_w_up"][i], fdw, _row(wt["ffn_b_dw"][i]),
                            wt["ffn_w_down"][i])
        lay.update(hf=hf, u=u, fdw=fdw)
        saved.append(lay)

    lpart, dy = _loss_head(x, target)
    loss = lpart[0, 0]

    gr = {k: [None] * len(v) for k, v in wt.items()}

    def made(name, layer, grad):
        gr[name][layer] = grad
        traffic.ready(name, layer, grad)

    for i in reversed(range(DEPTH)):
        j = i // 2
        lay = saved[i]
        du, act, dfdw, dfb, dx, dgf, got = _ffn_bwd(
            dy, lay["u"], lay["x_ffn"], _row(wt["ffn_norm_g"][i]), lay["fdw"], _row(wt["ffn_b_dw"][i]),
            wt["ffn_w_down"][i], wt["ffn_w_up"][i], send=traffic.take())
        traffic.landed(got)
        made("ffn_w_down", i, _matmul_tn(act, dy, "ffn_dw_down"))
        made("ffn_w_up", i, _matmul_tn(lay["hf"], du, "ffn_dw_up"))
        gr["ffn_w_dw"][i] = dfdw[:FFN_K]
        gr["ffn_b_dw"][i] = dfb[0]
        gr["ffn_norm_g"][i] = dgf[0]
        dy = dx
        if i % 2 == 0:
            dmix = _matmul_nt(dy, wt["hyb_w_out"][j], "out_proj_dx")
            made("hyb_w_out", j, _matmul_tn(lay["mix"], dy, "out_proj_dw"))
            dqn, dkn, dv, got = _sb_attn_bwd(lay["qn"], lay["kn"], lay["vb"], lay["lsum"], dmix, send=traffic.take())
            traffic.landed(got)
            duz, dwm, dbt, dgz = _sgu_bwd(lay["proj"], dmix, lay["gz"], lay["wmb"], lay["wmt"], lay["bt"], bd)
            dproj, dgq, dgk = _qk_bwd(lay["proj"], dqn, dkn, dv, duz, lay["gq"], lay["gk"] * LN2, bd)
            made("sb_w_in", j, _matmul_tn(lay["h"], dproj, "in_proj_dw"))
            gr["sb_q_norm_g"][j] = dgq.reshape(SB_WIDTH // HEAD_DIM, HEAD_DIM).sum(0) * scale
            gr["sb_k_norm_g"][j] = dgk.reshape(SB_WIDTH // HEAD_DIM, HEAD_DIM).sum(0) * LN2
            gr["sg_z_norm_g"][j] = dgz[0]
            gr["sg_w_spatial"][j] = jnp.where(tril[None], dwm, 0.0)
            gr["sg_b_spatial"][j] = dbt.reshape(CHUNK, SG_GROUPS, HEAD_DIM).sum(-1).T
            dy, dgm, got = _nt_rms_bwd(dproj, wt["sb_w_in"][j], lay["x_mix"], _row(wt["mix_norm_g"][i]), dy,
                                       "in_proj_dx", send=traffic.take() if i == 0 else ())
            if i == 0:
                traffic.landed(got)
        else:
            made("cv_w_pw2", j, _matmul_tn(lay["y2"], dy, "conf_pw2_dw"))
            dp, dwdw, dbdw, dlg, dlb, db1, db2, dx, dgm, got = _conf_mid_bwd(
                lay["p"], lay["yc"], dy, lay["x_mix"], _row(wt["mix_norm_g"][i]), lay["wdw"],
                _row(wt["cv_ln_g"][j]), _row(wt["cv_ln_b"][j]), wt["cv_w_pw1"][j], wt["cv_w_pw2"][j],
                send=traffic.take())
            traffic.landed(got)
            dy = dx
            made("cv_w_pw1", j, _matmul_tn(lay["h"], dp, "conf_pw1_dw"))
            gr["cv_w_dw"][j] = dwdw[:CONV_K]
            gr["cv_b_dw"][j] = dbdw[0]
            gr["cv_ln_g"][j] = dlg[0]
            gr["cv_ln_b"][j] = dlb[0]
            gr["cv_b_pw1"][j] = db1[0]
            gr["cv_b_pw2"][j] = db2[0]
        gr["mix_norm_g"][i] = dgm[0]
    matmul_weights = ("sb_w_in", "hyb_w_out", "cv_w_pw1", "cv_w_pw2", "ffn_w_up", "ffn_w_down")
    grads = {k: (v if k in matmul_weights else jnp.stack(v)) for k, v in gr.items()}
    return loss, dy, grads


WEIGHTS = ["mix_norm_g", "sb_w_in", "sb_q_norm_g", "sb_k_norm_g", "sg_z_norm_g", "sg_w_spatial", "sg_b_spatial",
           "hyb_w_out", "cv_w_pw1", "cv_b_pw1", "cv_w_dw", "cv_b_dw", "cv_ln_g", "cv_ln_b", "cv_w_pw2", "cv_b_pw2",
           "ffn_norm_g", "ffn_w_up", "ffn_w_dw", "ffn_b_dw", "ffn_w_down"]
BIG = [("sb_w_in", "col"), ("hyb_w_out", "row"), ("cv_w_pw1", "col"), ("cv_w_pw2", "row"), ("ffn_w_up", "col"),
       ("ffn_w_down", "row")]
SMALL = ["cv_b_pw1", "cv_w_dw", "cv_b_dw", "cv_ln_g", "cv_ln_b", "cv_b_pw2", "ffn_w_dw"]
REPLICATED = ["mix_norm_g", "sb_q_norm_g", "sb_k_norm_g", "sg_z_norm_g", "sg_w_spatial", "sg_b_spatial", "ffn_norm_g",
              "ffn_b_dw"]


def _last_dim_blocks(full):
    t = jnp.moveaxis(full.reshape(full.shape[:-1] + (N_DEV, full.shape[-1] // N_DEV)), -2, 0)
    return t.reshape(N_DEV, -1)


def _from_last_dim_blocks(blocks, shard_shape):
    t = jnp.moveaxis(blocks.reshape((N_DEV,) + tuple(shard_shape)), 0, -2)
    return t.reshape(tuple(shard_shape[:-1]) + (N_DEV * shard_shape[-1],))


def _pack(arrays):
    lead = arrays[0].shape[:-1]
    flat = jnp.concatenate([a.astype(F32) for a in arrays], axis=-1)
    rows = -(-flat.shape[-1] // (16 * LANE)) * 16
    pad = rows * LANE - flat.shape[-1]
    if pad:
        flat = jnp.concatenate([flat, jnp.zeros(lead + (pad,), F32)], axis=-1)
    return flat.reshape(lead + (rows, LANE))


def _unpack(packed, shapes):
    flat = packed.reshape(-1)
    out, off = [], 0
    for shp in shapes:
        size = 1
        for dim in shp:
            size *= dim
        out.append(flat[off:off + size].reshape(shp))
        off += size
    return out


class _ShardTraffic:
    def __init__(self, w):
        self.w = w
        self.queue, self.flying, self.received = [], [], {}

    def rest(self):
        small = _pack([self.w[n].reshape(-1) for n in SMALL])
        return [self.w["sb_w_in"][1:].astype(BF16)] + [self.w[n].astype(BF16) for n, _ in BIG[1:]] + [small]

    def install(self, wt, gathered):
        wt["sb_w_in"].append(_unshard(gathered[0], 0, "unshard_sb_w_in"))
        for (n, kind), w8 in zip(BIG[1:], gathered[1:-1]):
            if kind == "col":
                wt[n] = [_unshard(w8, l, "unshard_" + n) for l in range(w8.shape[1])]
            else:
                wt[n] = [w8[:, l].reshape((N_DEV * w8.shape[2],) + w8.shape[3:]) for l in range(w8.shape[1])]
        sizes = [(self.w[n].size,) for n in SMALL]
        for n, parts in zip(SMALL, zip(*[_unpack(gathered[-1][d], sizes) for d in range(N_DEV)])):
            wt[n] = _from_last_dim_blocks(jnp.stack(parts), self.w[n].shape)

    def ready(self, name, layer, grad):
        if dict(BIG)[name] == "col":
            blocks = _shard_cast(grad, "shard_" + name)
        else:
            blocks = grad.reshape((N_DEV, grad.shape[0] // N_DEV) + grad.shape[1:])
        self.queue.append(((name, layer), blocks))

    def take(self):
        self.flying = [key for key, _ in self.queue]
        arrays = [blocks for _, blocks in self.queue]
        self.queue = []
        return arrays

    def landed(self, received):
        for key, blocks in zip(self.flying, received):
            self.received[key] = blocks
        self.flying = []


def kernel(x, mix_norm_g, sb_w_in, sb_q_norm_g, sb_k_norm_g, sg_z_norm_g, sg_w_spatial, sg_b_spatial, hyb_w_out, cv_w_pw1, cv_b_pw1, cv_w_dw, cv_b_dw, cv_ln_g, cv_ln_b, cv_w_pw2, cv_b_pw2, ffn_norm_g, ffn_w_up, ffn_w_dw, ffn_b_dw, ffn_w_down, loss_target, m_mix_norm_g, m_sb_w_in, m_sb_q_norm_g, m_sb_k_norm_g, m_sg_z_norm_g, m_sg_w_spatial, m_sg_b_spatial, m_hyb_w_out, m_cv_w_pw1, m_cv_b_pw1, m_cv_w_dw, m_cv_b_dw, m_cv_ln_g, m_cv_ln_b, m_cv_w_pw2, m_cv_b_pw2, m_ffn_norm_g, m_ffn_w_up, m_ffn_w_dw, m_ffn_b_dw, m_ffn_w_down, v_mix_norm_g, v_sb_w_in, v_sb_q_norm_g, v_sb_k_norm_g, v_sg_z_norm_g, v_sg_w_spatial, v_sg_b_spatial, v_hyb_w_out, v_cv_w_pw1, v_cv_b_pw1, v_cv_w_dw, v_cv_b_dw, v_cv_ln_g, v_cv_ln_b, v_cv_w_pw2, v_cv_b_pw2, v_ffn_norm_g, v_ffn_w_up, v_ffn_w_dw, v_ffn_b_dw, v_ffn_w_down):
    w = dict(zip(WEIGHTS, (mix_norm_g, sb_w_in, sb_q_norm_g, sb_k_norm_g, sg_z_norm_g, sg_w_spatial, sg_b_spatial,
                           hyb_w_out, cv_w_pw1, cv_b_pw1, cv_w_dw, cv_b_dw, cv_ln_g, cv_ln_b, cv_w_pw2, cv_b_pw2,
                           ffn_norm_g, ffn_w_up, ffn_w_dw, ffn_b_dw, ffn_w_down)))
    m = dict(zip(WEIGHTS, (m_mix_norm_g, m_sb_w_in, m_sb_q_norm_g, m_sb_k_norm_g, m_sg_z_norm_g, m_sg_w_spatial,
                           m_sg_b_spatial, m_hyb_w_out, m_cv_w_pw1, m_cv_b_pw1, m_cv_w_dw, m_cv_b_dw, m_cv_ln_g,
                           m_cv_ln_b, m_cv_w_pw2, m_cv_b_pw2, m_ffn_norm_g, m_ffn_w_up, m_ffn_w_dw, m_ffn_b_dw,
                           m_ffn_w_down)))
    v = dict(zip(WEIGHTS, (v_mix_norm_g, v_sb_w_in, v_sb_q_norm_g, v_sb_k_norm_g, v_sg_z_norm_g, v_sg_w_spatial,
                           v_sg_b_spatial, v_hyb_w_out, v_cv_w_pw1, v_cv_b_pw1, v_cv_w_dw, v_cv_b_dw, v_cv_ln_g,
                           v_cv_ln_b, v_cv_w_pw2, v_cv_b_pw2, v_ffn_norm_g, v_ffn_w_up, v_ffn_w_dw, v_ffn_b_dw,
                           v_ffn_w_down)))
    big_names = [n for n, _ in BIG]
    flat = lambda t, names: [t[n].reshape(-1) for n in names]

    first_in = _all_gather([w["sb_w_in"][0:1].astype(BF16)], "gather_first")[0]
    wt = {n: w[n] for n in REPLICATED}
    wt["sb_w_in"] = [_unshard(first_in, 0, "unshard_sb_w_in")]

    traffic = _ShardTraffic(w)
    loss, gx, grads = _local_step(x[0], loss_target[0], wt, traffic)
    assert not traffic.queue and not traffic.flying
    recv_big = [[traffic.received[n, l] for l in range(w[n].shape[0])] for n in big_names]

    grep = _pack(flat(grads, REPLICATED))
    gsmall = _pack([_last_dim_blocks(grads[n]) for n in SMALL])
    rep_rows, small_rows = grep.shape[0], gsmall.shape[1]
    vec = jnp.concatenate([grep, gsmall.reshape(N_DEV * small_rows, LANE)], axis=0).astype(BF16)
    vec_all = _all_gather([vec], "gather_small_grads")[0]
    me = 4 * lax.axis_index("x") + 2 * lax.axis_index("y") + lax.axis_index("c")
    recv_rep = vec_all[:, :rep_rows]
    recv_small = lax.dynamic_slice(vec_all, (0, rep_rows + small_rows * me, 0), (N_DEV, small_rows, LANE))

    out = {}
    kinds = ("grad", "delta", "new_m", "new_v")
    for n, parts in zip(big_names, recv_big):
        for kind, arr in zip(kinds, _reduce_adamw(parts, w[n], m[n], v[n], "adamw_" + n)):
            out[kind, n] = arr
    for names, recv, tag in ((SMALL, recv_small, "adamw_small"), (REPLICATED, recv_rep, "adamw_replicated")):
        res = _reduce_adamw([recv], _pack(flat(w, names))[None], _pack(flat(m, names))[None],
                            _pack(flat(v, names))[None], tag)
        shapes = [w[n].shape for n in names]
        for kind, packed in zip(kinds, res):
            for n, arr in zip(names, _unpack(packed[0], shapes)):
                out[kind, n] = arr

    loss = lax.psum(loss, ("x", "y", "c"))
    return (loss, gx[None], *[out[kind, n] for kind in kinds for n in WEIGHTS])
```

```python
import jax
import jax.numpy as jnp
from jax import lax
from jax.experimental import pallas as pl
from jax.experimental.pallas import tpu as pltpu

F32 = jnp.float32
BF16 = jnp.bfloat16

D_MODEL = 1024
HEAD_DIM = 64
SB_WIDTH = 512
SG_WIDTH = 512
SG_GROUPS = 8
IN_WIDTH = 3 * SB_WIDTH + 2 * SG_WIDTH
CHUNK = 128
CONV_K = 31
D_FF = 2816
FFN_K = 3
DEPTH = 4
EPS = 1e-6
N_DEV = 8
LANE = 128
HALO = 32
ATT_BLOCK = 256
FF_CHUNK = 256
CONV_ROWS = 32
CONV_LANES = 512
MIB = 2 ** 20

ADAM_LR = 0.001
ADAM_B1 = 0.9
ADAM_B2 = 0.999
ADAM_EPS = 1e-08
ADAM_WD = 0.01
ADAM_STEP = 10

LOG2E = 1.4426950408889634
LN2 = 0.6931471805599453

NT_DIMS = (((1,), (1,)), ((), ()))
TN_DIMS = (((0,), (0,)), ((), ()))


def _params(semantics, vmem_mib):
    return pltpu.CompilerParams(dimension_semantics=semantics, vmem_limit_bytes=vmem_mib * MIB)


def _full(shape):
    nd = len(shape)
    return pl.BlockSpec(shape, lambda *_: (0,) * nd)


def _sigmoid(x):
    return 1.0 / (1.0 + jnp.exp(-x))


def _gelu(x):
    return 0.5 * x * (1.0 + lax.erf(x * 0.7071067811865476))


def _gelu_grad(x):
    return 0.5 * (1.0 + lax.erf(x * 0.7071067811865476)) + x * jnp.exp(-0.5 * x * x) * 0.3989422804014327


def _silu_grad(x, s):
    return s * (1.0 + x * (1.0 - s))


def _dot(a, b):
    return jnp.dot(a, b, preferred_element_type=F32)


def _dot2(a, b):
    hi = a.astype(BF16)
    lo = (a - hi.astype(F32)).astype(BF16)
    return _dot(hi, b) + _dot(lo, b)


def _group_mean(t, bd):
    return _dot2(t, bd) * (1.0 / HEAD_DIM)


def _shift_down(v, prev8, s):
    top = pltpu.roll(jnp.concatenate([prev8, v[:8]], axis=0), s, 0)[8:16]
    return jnp.concatenate([top, pltpu.roll(v, s, 0)[8:]], axis=0)


def _shift_up(v, next8, s):
    n = v.shape[0]
    bottom = pltpu.roll(jnp.concatenate([v[n - 8:], next8], axis=0), 16 - s, 0)[0:8]
    return jnp.concatenate([pltpu.roll(v, n - s, 0)[: n - 8], bottom], axis=0)


def _mesh_pos():
    return lax.axis_index("x"), lax.axis_index("y"), lax.axis_index("c")


def _comm_scratch(n):
    return [pltpu.SemaphoreType.DMA((7 * n,)), pltpu.SemaphoreType.DMA((7 * n,)), pltpu.SemaphoreType.DMA((n,))]


class _Scatter:
    def __init__(self, src_refs, out_refs, send_sems, recv_sems, local_sems):
        x, y, cc = _mesh_pos()
        me = 4 * x + 2 * y + cc
        self.copies, self.mine = [], []
        for a, (src, out) in enumerate(zip(src_refs, out_refs)):
            self.mine.append(pltpu.make_async_copy(src.at[me], out.at[me], local_sems.at[a]))
            for k in range(1, N_DEV):
                px = 1 - x if k & 4 else x
                py = 1 - y if k & 2 else y
                pc = 1 - cc if k & 1 else cc
                self.copies.append(pltpu.make_async_remote_copy(
                    src_ref=src.at[4 * px + 2 * py + pc], dst_ref=out.at[me],
                    send_sem=send_sems.at[7 * a + k - 1], recv_sem=recv_sems.at[7 * a + k - 1],
                    device_id=(px, py, pc), device_id_type=pl.DeviceIdType.MESH))

    def start(self):
        for cp in self.mine + self.copies:
            cp.start()

    def finish(self):
        for cp in self.copies + self.mine:
            cp.wait()


class _Gather:
    def __init__(self, x_refs, out_refs, send_sems, recv_sems, local_sems):
        x, y, cc = _mesh_pos()
        self.n = len(x_refs)
        self.me, self.sibling, self.cc = (x, y, cc), (x, y, 1 - cc), cc
        self.chips = [(1 - x, y), (x, 1 - y), (1 - x, 1 - y)]
        self.x_refs, self.out_refs, self.send_sems, self.recv_sems = x_refs, out_refs, send_sems, recv_sems
        self.mine = [pltpu.make_async_copy(x_refs[a], out_refs[a].at[4 * x + 2 * y + cc], local_sems.at[a])
                     for a in range(self.n)]

    def copy(self, a, k, block, to, own=False):
        slot = self.out_refs[a].at[4 * block[0] + 2 * block[1] + block[2]]
        return pltpu.make_async_remote_copy(
            src_ref=self.x_refs[a] if own else slot, dst_ref=slot,
            send_sem=self.send_sems.at[7 * a + k], recv_sem=self.recv_sems.at[7 * a + k],
            device_id=to, device_id_type=pl.DeviceIdType.MESH)

    def first_hop(self, a):
        return [self.copy(a, 0, self.me, self.sibling, own=True)] + [
            self.copy(a, 1 + j, self.me, (*chip, self.cc), own=True) for j, chip in enumerate(self.chips)]

    def passed_on(self, a):
        return [self.copy(a, 4 + j, (*chip, self.cc), self.sibling) for j, chip in enumerate(self.chips)]

    def start(self):
        for a in range(self.n):
            self.mine[a].start()
        for a in range(self.n):
            for cp in self.first_hop(a):
                cp.start()

    def forward(self):
        for a in range(self.n):
            for j, chip in enumerate(self.chips):
                self.copy(a, 1 + j, (*chip, self.cc), self.me).wait_recv()
                self.copy(a, 4 + j, (*chip, self.cc), self.sibling).start()

    def finish(self):
        for a in range(self.n):
            self.copy(a, 0, self.sibling, self.me).wait_recv()
            for j, chip in enumerate(self.chips):
                self.copy(a, 4 + j, (*chip, 1 - self.cc), self.me).wait_recv()
        for a in range(self.n):
            for cp in self.first_hop(a) + self.passed_on(a):
                cp.wait_send()
        for cp in self.mine:
            cp.wait()


def _carried(body, n_in, n_out, arrays, kind, when):
    n = len(arrays)
    if n == 0:
        return body, [], [], [], []

    def wrapped(*refs):
        ins, srcs = refs[:n_in], refs[n_in:n_in + n]
        outs, landed = refs[n_in + n:n_in + n + n_out], refs[n_in + n + n_out:n_in + 2 * n + n_out]
        scratch = refs[n_in + 2 * n + n_out:]
        comm = (_Scatter if kind == "scatter" else _Gather)(srcs, landed, *scratch[-3:])
        pl.when(when["start"]())(comm.start)
        if kind == "gather":
            pl.when(when["forward"]())(comm.forward)
        body(*ins, *outs, *scratch[:-3])
        pl.when(when["finish"]())(comm.finish)

    any_spec = pl.BlockSpec(memory_space=pl.ANY)
    if kind == "scatter":
        shapes = [jax.ShapeDtypeStruct(a.shape, a.dtype) for a in arrays]
    else:
        shapes = [jax.ShapeDtypeStruct((N_DEV,) + a.shape, a.dtype) for a in arrays]
    return wrapped, [any_spec] * n, [any_spec] * n, shapes, _comm_scratch(n)


def _all_gather(shards, name):
    n = len(shards)

    def body(*refs):
        comm = _Gather(refs[:n], refs[n:2 * n], *refs[2 * n:])
        comm.start()
        comm.forward()
        comm.finish()

    any_spec = pl.BlockSpec(memory_space=pl.ANY)
    return pl.pallas_call(
        body, name=name, in_specs=[any_spec] * n, out_specs=[any_spec] * n,
        out_shape=[jax.ShapeDtypeStruct((N_DEV,) + s.shape, s.dtype) for s in shards],
        scratch_shapes=_comm_scratch(n),
    )(*shards)


def _rms_matmul(x, g, w, b, name):
    s, d = x.shape
    n = w.shape[1]
    tm = min(512, s)

    def body(x_ref, g_ref, w_ref, b_ref, y_ref, h_ref):
        xv = x_ref[...]
        r = lax.rsqrt(jnp.mean(xv * xv, axis=-1, keepdims=True) + EPS)
        h = (xv * r * g_ref[...]).astype(BF16)
        h_ref[...] = h
        for c in range(0, n, 512):
            y_ref[:, c:c + 512] = _dot(h, w_ref[:, c:c + 512]) + b_ref[:, c:c + 512]

    return pl.pallas_call(
        body, name=name, grid=(s // tm,),
        in_specs=[pl.BlockSpec((tm, d), lambda i: (i, 0)), _full((1, d)), _full((d, n)), _full((1, n))],
        out_specs=[pl.BlockSpec((tm, n), lambda i: (i, 0)), pl.BlockSpec((tm, d), lambda i: (i, 0))],
        out_shape=[jax.ShapeDtypeStruct((s, n), F32), jax.ShapeDtypeStruct((s, d), BF16)],
        compiler_params=_params(("arbitrary",), 48),
    )(x, g, w, b)


def _matmul_nt(a, w, name):
    s, n = a.shape
    k = w.shape[0]
    tm = min(512, s)

    def body(a_ref, w_ref, o_ref):
        o_ref[...] = lax.dot_general(a_ref[...].astype(BF16), w_ref[...], NT_DIMS, preferred_element_type=F32)

    return pl.pallas_call(
        body, name=name, grid=(s // tm,),
        in_specs=[pl.BlockSpec((tm, n), lambda i: (i, 0)), _full((k, n))],
        out_specs=pl.BlockSpec((tm, k), lambda i: (i, 0)),
        out_shape=jax.ShapeDtypeStruct((s, k), F32),
        compiler_params=_params(("arbitrary",), 40),
    )(a, w)


def _matmul_tn(a, b, name):
    s, k = a.shape
    n = b.shape[1]
    ts = min(2048 if k <= 1024 else 1024, s)
    tn = 1024 if (n % 1024 == 0 and k <= 1024) else 512
    steps = s // ts

    def body(a_ref, b_ref, o_ref, acc):
        t = pl.program_id(1)

        @pl.when(t == 0)
        def _():
            acc[...] = jnp.zeros_like(acc)

        acc[...] += lax.dot_general(a_ref[...].astype(BF16), b_ref[...].astype(BF16), TN_DIMS,
                                    preferred_element_type=F32)

        @pl.when(t == steps - 1)
        def _():
            o_ref[...] = acc[...].astype(BF16)

    return pl.pallas_call(
        body, name=name, grid=(n // tn, steps),
        in_specs=[pl.BlockSpec((ts, k), lambda j, t: (t, 0)), pl.BlockSpec((ts, tn), lambda j, t: (t, j))],
        out_specs=pl.BlockSpec((k, tn), lambda j, t: (0, j)),
        out_shape=jax.ShapeDtypeStruct((k, n), BF16),
        scratch_shapes=[pltpu.VMEM((k, tn), F32)],
        compiler_params=_params(("arbitrary", "arbitrary"), 48),
    )(a, b)


def _nt_rms_bwd(dp, w, x, g, dres, name, send=()):
    s, n = dp.shape
    d = x.shape[1]
    tm = min(256, s)
    steps = s // tm

    def body(dp_ref, w_ref, x_ref, g_ref, dres_ref, dx_ref, dg_ref):
        @pl.when(pl.program_id(0) == 0)
        def _():
            dg_ref[...] = jnp.zeros_like(dg_ref)

        dh = lax.dot_general(dp_ref[...], w_ref[...], NT_DIMS, preferred_element_type=F32)
        xv = x_ref[...]
        r = lax.rsqrt(jnp.mean(xv * xv, axis=-1, keepdims=True) + EPS)
        xh = xv * r
        dg_ref[...] += jnp.sum(dh * xh, axis=0, keepdims=True)
        dn = dh * g_ref[...]
        dx_ref[...] = dres_ref[...] + r * (dn - xh * jnp.mean(dn * xh, axis=-1, keepdims=True))

    step_is = lambda i: lambda: pl.program_id(0) == i
    kern, ins, outs, shapes, sems = _carried(body, 5, 2, send, "scatter",
                                             {"start": step_is(0), "finish": step_is(steps - 1)})
    res = pl.pallas_call(
        kern, name=name, grid=(steps,),
        in_specs=[pl.BlockSpec((tm, n), lambda i: (i, 0)), _full((d, n)), pl.BlockSpec((tm, d), lambda i: (i, 0)),
                  _full((1, d)), pl.BlockSpec((tm, d), lambda i: (i, 0))] + ins,
        out_specs=[pl.BlockSpec((tm, d), lambda i: (i, 0)), _full((1, d))] + outs,
        out_shape=[jax.ShapeDtypeStruct((s, d), F32), jax.ShapeDtypeStruct((1, d), F32)] + shapes,
        scratch_shapes=sems,
        compiler_params=_params(("arbitrary",), 52),
    )(dp, w, x, g, dres, *send)
    return res[0], res[1], list(res[2:])


def _res_matmul(x, a, w, b, name):
    s, d = x.shape
    k = a.shape[1]
    tm = min(512, s)

    def body(x_ref, a_ref, w_ref, b_ref, o_ref):
        o_ref[...] = x_ref[...] + _dot(a_ref[...], w_ref[...]) + b_ref[...]

    return pl.pallas_call(
        body, name=name, grid=(s // tm,),
        in_specs=[pl.BlockSpec((tm, d), lambda i: (i, 0)), pl.BlockSpec((tm, k), lambda i: (i, 0)), _full((k, d)),
                  _full((1, d))],
        out_specs=pl.BlockSpec((tm, d), lambda i: (i, 0)),
        out_shape=jax.ShapeDtypeStruct((s, d), F32),
        compiler_params=_params(("arbitrary",), 32),
    )(x, a, w, b)


def _out_proj(x, o, gg, w, name):
    s, d = x.shape
    tm = min(512, s)

    def body(x_ref, o_ref, gg_ref, w_ref, y_ref, mix_ref):
        mix = jnp.concatenate([o_ref[...], gg_ref[...]], axis=1).astype(BF16)
        mix_ref[...] = mix
        y_ref[...] = x_ref[...] + _dot(mix, w_ref[...])

    return pl.pallas_call(
        body, name=name, grid=(s // tm,),
        in_specs=[pl.BlockSpec((tm, d), lambda i: (i, 0)), pl.BlockSpec((tm, SB_WIDTH), lambda i: (i, 0)),
                  pl.BlockSpec((tm, SG_WIDTH), lambda i: (i, 0)), _full((d, d))],
        out_specs=[pl.BlockSpec((tm, d), lambda i: (i, 0)), pl.BlockSpec((tm, d), lambda i: (i, 0))],
        out_shape=[jax.ShapeDtypeStruct((s, d), F32), jax.ShapeDtypeStruct((s, d), BF16)],
        compiler_params=_params(("arbitrary",), 32),
    )(x, o, gg, w)


def _qk_prep(proj, gq, gk, bd):
    s = proj.shape[0]
    tm = min(512, s)

    def body(q_ref, k_ref, v_ref, gq_ref, gk_ref, bd_ref, qn_ref, kn_ref, vb_ref):
        bdv = bd_ref[...]
        q = q_ref[...]
        k = k_ref[...]
        qn_ref[...] = (q * lax.rsqrt(_group_mean(q * q, bdv) + EPS) * gq_ref[...]).astype(BF16)
        kn_ref[...] = (k * lax.rsqrt(_group_mean(k * k, bdv) + EPS) * gk_ref[...]).astype(BF16)
        vb_ref[...] = v_ref[...].astype(BF16)

    col = lambda c: pl.BlockSpec((tm, SB_WIDTH), lambda i: (i, c))
    out = pl.BlockSpec((tm, SB_WIDTH), lambda i: (i, 0))
    return pl.pallas_call(
        body, name="qk_prep", grid=(s // tm,),
        in_specs=[col(0), col(1), col(2), _full((1, SB_WIDTH)), _full((1, SB_WIDTH)), _full((SB_WIDTH, SB_WIDTH))],
        out_specs=[out, out, out],
        out_shape=[jax.ShapeDtypeStruct((s, SB_WIDTH), BF16)] * 3,
        compiler_params=_params(("arbitrary",), 32),
    )(proj, proj, proj, gq, gk, bd)


def _qk_bwd(proj, dqn, dkn, dv, duz, gq, gk, bd):
    s = proj.shape[0]
    tm = min(512, s)

    def body(q_ref, k_ref, dq_ref, dk_ref, dv_ref, duz_ref, gq_ref, gk_ref, bd_ref, o_ref, dgq_ref, dgk_ref):
        @pl.when(pl.program_id(0) == 0)
        def _():
            dgq_ref[...] = jnp.zeros_like(dgq_ref)
            dgk_ref[...] = jnp.zeros_like(dgk_ref)

        bdv = bd_ref[...]

        def back(t, gain, dout, dg_ref):
            r = lax.rsqrt(_group_mean(t * t, bdv) + EPS)
            th = t * r
            dg_ref[...] += jnp.sum(dout * th, axis=0, keepdims=True)
            dn = dout * gain
            return r * (dn - th * _group_mean(dn * th, bdv))

        o_ref[:, 0:SB_WIDTH] = back(q_ref[...], gq_ref[...], dq_ref[...], dgq_ref).astype(BF16)
        o_ref[:, SB_WIDTH:2 * SB_WIDTH] = back(k_ref[...], gk_ref[...], dk_ref[...], dgk_ref).astype(BF16)
        o_ref[:, 2 * SB_WIDTH:3 * SB_WIDTH] = dv_ref[...].astype(BF16)
        o_ref[:, 3 * SB_WIDTH:IN_WIDTH] = duz_ref[...].astype(BF16)

    col = lambda c: pl.BlockSpec((tm, SB_WIDTH), lambda i: (i, c))
    row = pl.BlockSpec((tm, SB_WIDTH), lambda i: (i, 0))
    return pl.pallas_call(
        body, name="qk_bwd", grid=(s // tm,),
        in_specs=[col(0), col(1), row, row, row, pl.BlockSpec((tm, 2 * SG_WIDTH), lambda i: (i, 0)),
                  _full((1, SB_WIDTH)), _full((1, SB_WIDTH)), _full((SB_WIDTH, SB_WIDTH))],
        out_specs=[pl.BlockSpec((tm, IN_WIDTH), lambda i: (i, 0)), _full((1, SB_WIDTH)), _full((1, SB_WIDTH))],
        out_shape=[jax.ShapeDtypeStruct((s, IN_WIDTH), BF16), jax.ShapeDtypeStruct((1, SB_WIDTH), F32),
                   jax.ShapeDtypeStruct((1, SB_WIDTH), F32)],
        compiler_params=_params(("arbitrary",), 40),
    )(proj, proj, dqn, dkn, dv, duz, gq, gk, bd)


def _attn_masks(tq, tk):
    lane = lax.broadcasted_iota(jnp.int32, (tk, LANE), 1)
    heads = [(lane >= hh * HEAD_DIM) & (lane < (hh + 1) * HEAD_DIM) for hh in range(2)]
    urow = lax.broadcasted_iota(jnp.int32, (tk, tk), 0)
    ucol = lax.broadcasted_iota(jnp.int32, (tk, tk), 1)
    return heads, urow, ucol


Z2_MAX = 126.0


def _keep_cost(z2):
    return jnp.log(1.0 + jnp.exp2(z2))


def _at_rows(part, row0, total):
    rows, width = part.shape
    pieces = [jnp.zeros((row0, width), part.dtype)] if row0 else []
    pieces.append(part)
    if total - row0 - rows:
        pieces.append(jnp.zeros((total - row0 - rows, width), part.dtype))
    return pieces[0] if len(pieces) == 1 else jnp.concatenate(pieces, axis=0)


def _diagonal_specs(i, r, tq, tk, left_to_right):
    if r == 2:
        specs = [(2 * i + 1, tk, tk, True), (2 * i, tk, tk, False), (2 * i, 0, tk, True)]
    else:
        specs = [(r * i + r - 1 - d, 0, tq, True) for d in range(r)]
    return specs[::-1] if left_to_right else specs


def _sb_attn_fwd(q2, kn, vb, gather=()):
    s = q2.shape[0]
    tk = min(ATT_BLOCK, s)
    tq = min(2 * ATT_BLOCK, s)
    r = tq // tk
    nq = s // tq
    assert s // tk <= LANE

    def body(q_ref, k_ref, v_ref, o_ref, ls_ref):
        i = pl.program_id(1)
        heads, urow, ucol = _attn_masks(tq, tk)
        u_incl = (urow >= ucol).astype(BF16)
        row = lax.broadcasted_iota(jnp.int32, (tq, tk), 0)
        col = lax.broadcasted_iota(jnp.int32, (tq, tk), 1)
        qlane = lax.broadcasted_iota(jnp.int32, (tq, LANE), 1)
        q = q_ref[...]

        def blocks(specs, state):
            carry, acc, ls = list(state[0:2]), state[2], list(state[3:5])
            chains = [(d, hh) for d in range(len(specs)) for hh in range(2)]
            kblk, vblk, valid = {}, {}, {}
            for d, (kb, row0, rows, masked) in enumerate(specs):
                off = pl.multiple_of(kb * tk, tk)
                kfull = k_ref[pl.ds(off, tk), :]
                vfull = v_ref[pl.ds(off, tk), :]
                if masked:
                    valid[d] = (kb * tk + col[:rows]) < (i * tq + row0 + row[:rows])
                for hh in range(2):
                    kblk[d, hh] = jnp.where(heads[hh], kfull, jnp.zeros((), BF16))
                    vblk[d, hh] = jnp.where(heads[hh], vfull, jnp.zeros((), BF16))
            z2 = {(d, hh): jnp.minimum(lax.dot_general(q[specs[d][1]:specs[d][1] + specs[d][2]], kblk[d, hh], NT_DIMS,
                                                       preferred_element_type=F32), Z2_MAX) for d, hh in chains}
            cost = {}
            for d, hh in chains:
                cost[d, hh] = _keep_cost(z2[d, hh])
                if specs[d][3]:
                    cost[d, hh] = jnp.where(valid[d], cost[d, hh], 0.0)
            sums = {c: _dot(cost[c].astype(BF16), u_incl) for c in chains}
            a = {}
            for d, hh in chains:
                kb, row0, rows, masked = specs[d]
                rin = carry[hh][row0:row0 + rows] + sums[d, hh]
                a[d, hh] = jnp.exp2(z2[d, hh] - rin * LOG2E)
                if masked:
                    a[d, hh] = jnp.where(valid[d], a[d, hh], 0.0)
                rs = jnp.sum(_at_rows(cost[d, hh], row0, tq), axis=1, keepdims=True)
                ls[hh] = ls[hh] + jnp.where(qlane == kb, rs, 0.0)
                carry[hh] = carry[hh] + rs
            for d, hh in chains:
                acc = acc + _at_rows(_dot(a[d, hh].astype(BF16), vblk[d, hh]), specs[d][1], tq)
            return carry[0], carry[1], acc, ls[0], ls[1]

        zc = jnp.zeros((tq, 1), F32)
        zt = jnp.zeros((tq, LANE), F32)
        state = blocks(_diagonal_specs(i, r, tq, tk, False), (zc, zc, zt, zt, zt))
        pairs = lax.shift_right_logical(i, 1)
        state = lax.fori_loop(
            0, pairs,
            lambda n, st: blocks([(r * i - 1 - 2 * r * n - d, 0, tq, False) for d in range(2 * r)], st), state)
        state = lax.fori_loop(
            0, i - 2 * pairs, lambda n, st: blocks([(r - 1 - d, 0, tq, False) for d in range(r)], st), state)
        o_ref[...] = state[2]
        ls_ref[:, 0:LANE] = state[3]
        ls_ref[:, LANE:2 * LANE] = state[4]

    last_pair = SB_WIDTH // LANE - 1
    step_is = lambda p, i: lambda: (pl.program_id(0) == p) & (pl.program_id(1) == i)
    when = {"start": step_is(0, 0), "forward": step_is(last_pair, 0), "finish": step_is(last_pair, nq - 1)}
    kern, ins, outs, shapes, sems = _carried(body, 3, 2, gather, "gather", when)
    res = pl.pallas_call(
        kern, name="sb_attn_fwd", grid=(SB_WIDTH // LANE, nq),
        in_specs=[pl.BlockSpec((tq, LANE), lambda p, i: (i, p)), pl.BlockSpec((s, LANE), lambda p, i: (0, p)),
                  pl.BlockSpec((s, LANE), lambda p, i: (0, p))] + ins,
        out_specs=[pl.BlockSpec((tq, LANE), lambda p, i: (i, p)),
                   pl.BlockSpec((tq, 2 * LANE), lambda p, i: (i, p))] + outs,
        out_shape=[jax.ShapeDtypeStruct((s, SB_WIDTH), F32), jax.ShapeDtypeStruct((s, 2 * SB_WIDTH), F32)] + shapes,
        scratch_shapes=sems,
        compiler_params=_params(("arbitrary", "arbitrary"), 48),
    )(q2, kn, vb, *gather)
    return res[0], res[1], list(res[2:])


def _sb_attn_bwd(q2, kn, vb, lsum, dmix, send=()):
    s = q2.shape[0]
    tk = min(ATT_BLOCK, s)
    tq = min(2 * ATT_BLOCK, s)
    r = tq // tk
    nq = s // tq

    def body(q_ref, k_ref, v_ref, ls_ref, do_ref, dq_ref, dk_ref, dv_ref):
        i = pl.program_id(1)

        @pl.when(i == 0)
        def _():
            dk_ref[...] = jnp.zeros_like(dk_ref)
            dv_ref[...] = jnp.zeros_like(dv_ref)

        heads, urow, ucol = _attn_masks(tq, tk)
        u_incl = (urow >= ucol).astype(BF16)
        u_pre = (urow <= ucol).astype(BF16)
        lrow = lax.broadcasted_iota(jnp.int32, (LANE, LANE), 0)
        lcol = lax.broadcasted_iota(jnp.int32, (LANE, LANE), 1)
        u_after = (lrow > lcol).astype(BF16)
        row = lax.broadcasted_iota(jnp.int32, (tq, tk), 0)
        col = lax.broadcasted_iota(jnp.int32, (tq, tk), 1)
        qlane = lax.broadcasted_iota(jnp.int32, (tq, LANE), 1)
        qheads = [(qlane >= hh * HEAD_DIM) & (qlane < (hh + 1) * HEAD_DIM) for hh in range(2)]
        q = q_ref[...]
        dob = do_ref[...].astype(BF16)
        qm = [jnp.where(qheads[hh], q, jnp.zeros((), BF16)) for hh in range(2)]
        dom = [jnp.where(qheads[hh], dob, jnp.zeros((), BF16)) for hh in range(2)]
        after = []
        for hh in range(2):
            ls = ls_ref[:, hh * LANE:(hh + 1) * LANE]
            hi = ls.astype(BF16)
            mid = (ls - hi.astype(F32)).astype(BF16)
            lo = (ls - hi.astype(F32) - mid.astype(F32)).astype(BF16)
            after.append(_dot(hi, u_after) + _dot(mid, u_after) + _dot(lo, u_after))

        def blocks(specs, state):
            cp, dq = list(state[0:2]), state[2]
            chains = [(d, hh) for d in range(len(specs)) for hh in range(2)]
            rows_of = lambda d: slice(specs[d][1], specs[d][1] + specs[d][2])
            offs, kblk, vblk, valid = {}, {}, {}, {}
            for d, (kb, row0, rows, masked) in enumerate(specs):
                offs[d] = pl.multiple_of(kb * tk, tk)
                kfull = k_ref[pl.ds(offs[d], tk), :]
                vblk[d] = v_ref[pl.ds(offs[d], tk), :]
                if masked:
                    valid[d] = (kb * tk + col[:rows]) < (i * tq + row0 + row[:rows])
                for hh in range(2):
                    kblk[d, hh] = jnp.where(heads[hh], kfull, jnp.zeros((), BF16))
            z2 = {(d, hh): jnp.minimum(lax.dot_general(q[rows_of(d)], kblk[d, hh], NT_DIMS, preferred_element_type=F32),
                                       Z2_MAX) for d, hh in chains}
            da = {(d, hh): lax.dot_general(dom[hh][rows_of(d)], vblk[d], NT_DIMS, preferred_element_type=F32)
                  for d, hh in chains}
            cost, sig = {}, {}
            for d, hh in chains:
                cost[d, hh] = _keep_cost(z2[d, hh])
                sig[d, hh] = jnp.exp2(z2[d, hh] - cost[d, hh] * LOG2E)
                if specs[d][3]:
                    cost[d, hh] = jnp.where(valid[d], cost[d, hh], 0.0)
            sums = {c: _dot(cost[c].astype(BF16), u_incl) for c in chains}
            a, g = {}, {}
            for d, hh in chains:
                kb, row0, rows, masked = specs[d]
                cr = jnp.sum(jnp.where(qlane[:rows] == kb, after[hh][rows_of(d)], 0.0), axis=1, keepdims=True)
                a[d, hh] = jnp.exp2(z2[d, hh] - (cr + sums[d, hh]) * LOG2E)
                if masked:
                    a[d, hh] = jnp.where(valid[d], a[d, hh], 0.0)
                g[d, hh] = da[d, hh] * a[d, hh]
            pre = {c: _dot(g[c].astype(BF16), u_pre) for c in chains}
            dzb = {}
            for d, hh in chains:
                dz = g[d, hh] - sig[d, hh] * (cp[hh][rows_of(d)] + pre[d, hh])
                if specs[d][3]:
                    dz = jnp.where(valid[d], dz, 0.0)
                dzb[d, hh] = dz.astype(BF16)
                cp[hh] = cp[hh] + _at_rows(jnp.sum(g[d, hh], axis=1, keepdims=True), specs[d][1], tq)
            for d in range(len(specs)):
                dv_ref[pl.ds(offs[d], tk), :] += sum(
                    lax.dot_general(a[d, hh].astype(BF16), dom[hh][rows_of(d)], TN_DIMS, preferred_element_type=F32)
                    for hh in range(2))
                dk_ref[pl.ds(offs[d], tk), :] += sum(
                    lax.dot_general(dzb[d, hh], qm[hh][rows_of(d)], TN_DIMS, preferred_element_type=F32)
                    for hh in range(2))
            for d, hh in chains:
                dq = dq + _at_rows(_dot(dzb[d, hh], kblk[d, hh]), specs[d][1], tq)
            return cp[0], cp[1], dq

        zc = jnp.zeros((tq, 1), F32)
        pairs = lax.shift_right_logical(i, 1)
        state = lax.fori_loop(
            0, pairs, lambda n, st: blocks([(2 * r * n + d, 0, tq, False) for d in range(2 * r)], st),
            (zc, zc, jnp.zeros((tq, LANE), F32)))
        state = lax.fori_loop(
            0, i - 2 * pairs, lambda n, st: blocks([(2 * r * pairs + d, 0, tq, False) for d in range(r)], st), state)
        state = blocks(_diagonal_specs(i, r, tq, tk, True), state)
        dq_ref[...] = state[2]

    blk = pl.BlockSpec((tq, LANE), lambda p, i: (i, p))
    whole = pl.BlockSpec((s, LANE), lambda p, i: (0, p))
    last_pair = SB_WIDTH // LANE - 1
    step_is = lambda p, i: lambda: (pl.program_id(0) == p) & (pl.program_id(1) == i)
    kern, ins, outs, shapes, sems = _carried(body, 5, 3, send, "scatter",
                                             {"start": step_is(0, 0), "finish": step_is(last_pair, nq - 1)})
    res = pl.pallas_call(
        kern, name="sb_attn_bwd", grid=(SB_WIDTH // LANE, nq),
        in_specs=[blk, whole, whole, pl.BlockSpec((tq, 2 * LANE), lambda p, i: (i, p)), blk] + ins,
        out_specs=[blk, whole, whole] + outs,
        out_shape=[jax.ShapeDtypeStruct((s, SB_WIDTH), F32)] * 3 + shapes,
        scratch_shapes=sems,
        compiler_params=_params(("arbitrary", "arbitrary"), 56),
    )(q2, kn, vb, lsum, dmix, *send)
    return res[0], res[1], res[2], list(res[3:])


def _sgu_spatial(zn, wm_ref, lane, c):
    parts = []
    for p in range(SG_WIDTH // LANE):
        blk = zn[c * CHUNK:(c + 1) * CHUNK, p * LANE:(p + 1) * LANE].astype(BF16)
        lo = jnp.where(lane < HEAD_DIM, blk, jnp.zeros((), BF16))
        hi = jnp.where(lane >= HEAD_DIM, blk, jnp.zeros((), BF16))
        parts.append(_dot(wm_ref[2 * p], lo) + _dot(wm_ref[2 * p + 1], hi))
    return jnp.concatenate(parts, axis=1)


def _sgu_fwd(proj, gz, wm, bt, bd):
    s = proj.shape[0]
    tm = min(512, s)

    def body(u_ref, z_ref, gz_ref, wm_ref, bt_ref, bd_ref, o_ref):
        lane = lax.broadcasted_iota(jnp.int32, (CHUNK, LANE), 1)
        ug = _gelu(u_ref[...])
        zg = _gelu(z_ref[...])
        zn = zg * lax.rsqrt(_group_mean(zg * zg, bd_ref[...]) + EPS) * gz_ref[...]
        for c in range(tm // CHUNK):
            sp = _sgu_spatial(zn, wm_ref, lane, c) + bt_ref[...]
            o_ref[c * CHUNK:(c + 1) * CHUNK, :] = ug[c * CHUNK:(c + 1) * CHUNK, :] * sp

    col = lambda c: pl.BlockSpec((tm, SG_WIDTH), lambda i: (i, c))
    return pl.pallas_call(
        body, name="sgu_fwd", grid=(s // tm,),
        in_specs=[col(3), col(4), _full((1, SG_WIDTH)), _full((SG_GROUPS, CHUNK, CHUNK)), _full((CHUNK, SG_WIDTH)),
                  _full((SG_WIDTH, SG_WIDTH))],
        out_specs=pl.BlockSpec((tm, SG_WIDTH), lambda i: (i, 0)),
        out_shape=jax.ShapeDtypeStruct((s, SG_WIDTH), F32),
        compiler_params=_params(("arbitrary",), 32),
    )(proj, proj, gz, wm, bt, bd)


def _sgu_bwd(proj, dmix, gz, wm, wmt, bt, bd):
    s = proj.shape[0]
    tm = min(512, s)

    def body(u_ref, z_ref, dg_ref, gz_ref, wm_ref, wmt_ref, bt_ref, bd_ref, o_ref, dwm_ref, dbt_ref, dgz_ref):
        @pl.when(pl.program_id(0) == 0)
        def _():
            dwm_ref[...] = jnp.zeros_like(dwm_ref)
            dbt_ref[...] = jnp.zeros_like(dbt_ref)
            dgz_ref[...] = jnp.zeros_like(dgz_ref)

        lane = lax.broadcasted_iota(jnp.int32, (CHUNK, LANE), 1)
        bdv = bd_ref[...]
        u = u_ref[...]
        z = z_ref[...]
        ug = _gelu(u)
        zg = _gelu(z)
        r = lax.rsqrt(_group_mean(zg * zg, bdv) + EPS)
        zh = zg * r
        zn = zh * gz_ref[...]
        dzn_rows = []
        for c in range(tm // CHUNK):
            rows = slice(c * CHUNK, (c + 1) * CHUNK)
            sp = _sgu_spatial(zn, wm_ref, lane, c) + bt_ref[...]
            dgg = dg_ref[rows, :]
            ds = dgg * ug[rows, :]
            o_ref[rows, 0:SG_WIDTH] = dgg * sp * _gelu_grad(u[rows, :])
            dbt_ref[...] += ds
            parts = []
            for p in range(SG_WIDTH // LANE):
                dsb = ds[:, p * LANE:(p + 1) * LANE].astype(BF16)
                znb = zn[rows, p * LANE:(p + 1) * LANE].astype(BF16)
                acc = jnp.zeros((CHUNK, LANE), F32)
                for hh in range(2):
                    hm = (lane >= hh * HEAD_DIM) & (lane < (hh + 1) * HEAD_DIM)
                    dsm = jnp.where(hm, dsb, jnp.zeros((), BF16))
                    znm = jnp.where(hm, znb, jnp.zeros((), BF16))
                    acc = acc + _dot(wmt_ref[2 * p + hh], dsm)
                    dwm_ref[2 * p + hh] += lax.dot_general(dsm, znm, NT_DIMS, preferred_element_type=F32)
                parts.append(acc)
            dzn_rows.append(jnp.concatenate(parts, axis=1))
        dzn = jnp.concatenate(dzn_rows, axis=0)
        dgz_ref[...] += jnp.sum(dzn * zh, axis=0, keepdims=True)
        dn = dzn * gz_ref[...]
        o_ref[:, SG_WIDTH:2 * SG_WIDTH] = r * (dn - zh * _group_mean(dn * zh, bdv)) * _gelu_grad(z)

    col = lambda c: pl.BlockSpec((tm, SG_WIDTH), lambda i: (i, c))
    wspec = _full((SG_GROUPS, CHUNK, CHUNK))
    return pl.pallas_call(
        body, name="sgu_bwd", grid=(s // tm,),
        in_specs=[col(3), col(4), pl.BlockSpec((tm, SG_WIDTH), lambda i: (i, 1)), _full((1, SG_WIDTH)), wspec, wspec,
                  _full((CHUNK, SG_WIDTH)), _full((SG_WIDTH, SG_WIDTH))],
        out_specs=[pl.BlockSpec((tm, 2 * SG_WIDTH), lambda i: (i, 0)), wspec, _full((CHUNK, SG_WIDTH)),
                   _full((1, SG_WIDTH))],
        out_shape=[jax.ShapeDtypeStruct((s, 2 * SG_WIDTH), F32), jax.ShapeDtypeStruct((SG_GROUPS, CHUNK, CHUNK), F32),
                   jax.ShapeDtypeStruct((CHUNK, SG_WIDTH), F32), jax.ShapeDtypeStruct((1, SG_WIDTH), F32)],
        compiler_params=_params(("arbitrary",), 40),
    )(proj, proj, dmix, gz, wm, wmt, bt, bd)


def _shifted_copies(ext, sh):
    e = ext[...]
    sh[0] = e
    for b in range(1, 8):
        sh[b] = pltpu.roll(e, e.shape[0] - b, 0)


def _shifted_rows(sh, off, r0, cols):
    return sh[off % 8, pl.ds(r0 + (off - off % 8), CONV_ROWS), cols]


def _for_row_blocks(rows, block, looped):
    if not looped:
        for r0 in range(0, rows, CONV_ROWS):
            block(r0)
        return

    def step(rb, carry):
        block(pl.multiple_of(rb * CONV_ROWS, CONV_ROWS))
        return carry

    lax.fori_loop(0, rows // CONV_ROWS, step, 0)


def _dwconv(sh, w_ref, offsets, rows, store, looped):
    def block(r0):
        for c0 in range(0, sh.shape[2], CONV_LANES):
            cols = slice(c0, c0 + CONV_LANES)
            acc = jnp.zeros((CONV_ROWS, CONV_LANES), F32)
            for j, off in enumerate(offsets):
                acc = acc + w_ref[j:j + 1, cols] * _shifted_rows(sh, off, r0, cols)
            store(r0, cols, acc)

    _for_row_blocks(rows, block, looped)


def _conf_mid_fwd(p, wdw, bdw, lng, lnb):
    s = p.shape[0]
    c = p.shape[1] // 2
    tm = min(256, s)

    def body(a_ref, gt_ref, w_ref, b_ref, g_ref, beta_ref, yc_ref, y2_ref, ext, sh):
        i = pl.program_id(0)

        @pl.when(i == 0)
        def _():
            ext[0:HALO, :] = jnp.zeros((HALO, c), F32)

        @pl.when(i > 0)
        def _():
            ext[0:HALO, :] = ext[tm:tm + HALO, :]

        ext[HALO:HALO + tm, :] = a_ref[...] * _sigmoid(gt_ref[...])

        def store(r0, cols, block):
            yc_ref[pl.ds(r0, CONV_ROWS), cols] = block + b_ref[:, cols]

        _shifted_copies(ext, sh)
        _dwconv(sh, w_ref, [HALO - CONV_K + 1 + j for j in range(CONV_K)], tm, store, looped=False)
        acc = yc_ref[...]
        xc = acc - jnp.mean(acc, axis=-1, keepdims=True)
        ln = xc * lax.rsqrt(jnp.mean(xc * xc, axis=-1, keepdims=True) + EPS) * g_ref[...] + beta_ref[...]
        y2_ref[...] = (ln * _sigmoid(ln)).astype(BF16)

    vec = _full((1, c))
    return pl.pallas_call(
        body, name="conf_mid_fwd", grid=(s // tm,),
        in_specs=[pl.BlockSpec((tm, c), lambda i: (i, 0)), pl.BlockSpec((tm, c), lambda i: (i, 1)), _full((HALO, c)), vec,
                  vec, vec],
        out_specs=[pl.BlockSpec((tm, c), lambda i: (i, 0)), pl.BlockSpec((tm, c), lambda i: (i, 0))],
        out_shape=[jax.ShapeDtypeStruct((s, c), F32), jax.ShapeDtypeStruct((s, c), BF16)],
        scratch_shapes=[pltpu.VMEM((HALO + tm, c), F32), pltpu.VMEM((8, HALO + tm, c), F32)],
        compiler_params=_params(("arbitrary",), 40),
    )(p, p, wdw, bdw, lng, lnb)


def _conf_mid_bwd(p, yc, dout, x, gm, wdw, lng, lnb, w1, w2, send=()):
    s = p.shape[0]
    c = p.shape[1] // 2
    tm = min(256, s)
    n = s // tm

    def body(a_ref, gt_ref, yc_ref, dout_ref, x_ref, gm_ref, w_ref, g_ref, beta_ref, w1_hbm, w2_hbm,
             dp_ref, dw_ref, dbdw_ref, dlg_ref, dlb_ref, db1_ref, db2_ref, dx_ref, dgm_ref,
             yv, extd, dyv, dwacc, shd, w1_ref, w2_ref, wsem):
        i = pl.program_id(0)

        @pl.when(i == 0)
        def _():
            loads = [pltpu.make_async_copy(w1_hbm, w1_ref, wsem.at[0]),
                     pltpu.make_async_copy(w2_hbm, w2_ref, wsem.at[1])]
            for cp in loads:
                cp.start()
            extd[tm:tm + HALO, :] = jnp.zeros((HALO, c), F32)
            for ref in (dw_ref, dbdw_ref, dlg_ref, dlb_ref, db1_ref, db2_ref, dgm_ref, dwacc):
                ref[...] = jnp.zeros_like(ref)
            for cp in loads:
                cp.wait()

        @pl.when(i > 0)
        def _():
            extd[tm:tm + HALO, :] = extd[0:HALO, :]

        dout = dout_ref[...]
        dy2 = lax.dot_general(dout.astype(BF16), w2_ref[...], NT_DIMS, preferred_element_type=F32)
        a = a_ref[...]
        sg = _sigmoid(gt_ref[...])
        yv[...] = a * sg
        ycv = yc_ref[...]
        xc = ycv - jnp.mean(ycv, axis=-1, keepdims=True)
        rstd = lax.rsqrt(jnp.mean(xc * xc, axis=-1, keepdims=True) + EPS)
        xh = xc * rstd
        ln = xh * g_ref[...] + beta_ref[...]
        dln = dy2 * _silu_grad(ln, _sigmoid(ln))
        dlg_ref[...] += jnp.sum(dln * xh, axis=0, keepdims=True)
        dlb_ref[...] += jnp.sum(dln, axis=0, keepdims=True)
        dxh = dln * g_ref[...]
        dyc = rstd * (dxh - jnp.mean(dxh, axis=-1, keepdims=True) - xh * jnp.mean(dxh * xh, axis=-1, keepdims=True))
        extd[0:tm, :] = dyc
        dbdw_ref[...] += jnp.sum(dyc, axis=0, keepdims=True)
        db2_ref[...] += jnp.sum(dout, axis=0, keepdims=True)

        _shifted_copies(extd, shd)

        def conv_block(r0):
            for c0 in range(0, c, CONV_LANES):
                cols = slice(c0, c0 + CONV_LANES)
                ysub = yv[pl.ds(r0, CONV_ROWS), cols]
                acc = jnp.zeros((CONV_ROWS, CONV_LANES), F32)
                for j in range(CONV_K):
                    later = _shifted_rows(shd, CONV_K - 1 - j, r0, cols)
                    acc = acc + w_ref[j:j + 1, cols] * later
                    dwacc[8 * j:8 * j + 8, cols] += (ysub * later).reshape(CONV_ROWS // 8, 8, CONV_LANES).sum(axis=0)
                dyv[pl.ds(r0, CONV_ROWS), cols] = acc

        _for_row_blocks(tm, conv_block, looped=True)

        @pl.when(i == n - 1)
        def _():
            dw_ref[...] = dwacc[...].reshape(HALO, 8, c).sum(axis=1)

        dy = dyv[...]
        da = dy * sg
        dgt = dy * a * sg * (1.0 - sg)
        dab, dgtb = da.astype(BF16), dgt.astype(BF16)
        dp_ref[:, 0:c] = dab
        dp_ref[:, c:2 * c] = dgtb
        db1_ref[:, 0:c] += jnp.sum(da, axis=0, keepdims=True)
        db1_ref[:, c:2 * c] += jnp.sum(dgt, axis=0, keepdims=True)
        dh = (lax.dot_general(dab, w1_ref[:, 0:c], NT_DIMS, preferred_element_type=F32)
              + lax.dot_general(dgtb, w1_ref[:, c:2 * c], NT_DIMS, preferred_element_type=F32))
        xv = x_ref[...]
        r = lax.rsqrt(jnp.mean(xv * xv, axis=-1, keepdims=True) + EPS)
        xn = xv * r
        dgm_ref[...] += jnp.sum(dh * xn, axis=0, keepdims=True)
        dn = dh * gm_ref[...]
        dx_ref[...] = dout + r * (dn - xn * jnp.mean(dn * xn, axis=-1, keepdims=True))

    rev = lambda col: pl.BlockSpec((tm, c), lambda i: (n - 1 - i, col))
    vec = _full((1, c))
    step_is = lambda i: lambda: pl.program_id(0) == i
    kern, ins, outs, shapes, sems = _carried(body, 11, 9, send, "scatter",
                                             {"start": step_is(0), "finish": step_is(n - 1)})
    res = pl.pallas_call(
        kern, name="conf_mid_bwd", grid=(n,),
        in_specs=[rev(0), rev(1), rev(0), rev(0), rev(0), vec, _full((HALO, c)), vec, vec,
                  pl.BlockSpec(memory_space=pl.ANY), pl.BlockSpec(memory_space=pl.ANY)] + ins,
        out_specs=[pl.BlockSpec((tm, 2 * c), lambda i: (n - 1 - i, 0)), _full((HALO, c)), vec, vec, vec,
                   _full((1, 2 * c)), vec, rev(0), vec] + outs,
        out_shape=[jax.ShapeDtypeStruct((s, 2 * c), BF16), jax.ShapeDtypeStruct((HALO, c), F32),
                   jax.ShapeDtypeStruct((1, c), F32), jax.ShapeDtypeStruct((1, c), F32), jax.ShapeDtypeStruct((1, c), F32),
                   jax.ShapeDtypeStruct((1, 2 * c), F32), jax.ShapeDtypeStruct((1, c), F32),
                   jax.ShapeDtypeStruct((s, c), F32), jax.ShapeDtypeStruct((1, c), F32)] + shapes,
        scratch_shapes=[pltpu.VMEM((tm, c), F32), pltpu.VMEM((tm + HALO, c), F32), pltpu.VMEM((tm, c), F32),
                        pltpu.VMEM((8 * HALO, c), F32), pltpu.VMEM((8, tm + HALO, c), F32),
                        pltpu.VMEM((c, 2 * c), BF16), pltpu.VMEM((c, c), BF16),
                        pltpu.SemaphoreType.DMA((2,))] + sems,
        compiler_params=_params(("arbitrary",), 56),
    )(p, p, yc, dout, x, gm, wdw, lng, lnb, w1, w2, *send)
    return tuple(res[:9]) + (list(res[9:]),)


def _ffn_fwd(x, g, wup, wdw, bdw, wdn):
    s, d = x.shape
    ff = wdn.shape[0]
    tm = min(256, s)

    def body(x_ref, g_ref, wup_ref, wdw_ref, bdw_ref, wdn_ref, y_ref, h_ref, u_ref, carry):
        @pl.when(pl.program_id(0) == 0)
        def _():
            carry[...] = jnp.zeros_like(carry)

        xv = x_ref[...]
        r = lax.rsqrt(jnp.mean(xv * xv, axis=-1, keepdims=True) + EPS)
        h = (xv * r * g_ref[...]).astype(BF16)
        h_ref[...] = h
        acc = xv
        up = lambda c: (_dot(h, wup_ref[:, c:c + FF_CHUNK]), _dot(h, wup_ref[:, ff + c:ff + c + FF_CHUNK]))
        ahead = up(0)
        for c in range(0, ff, FF_CHUNK):
            cs = slice(c, c + FF_CHUNK)
            gp, val = ahead
            if c + FF_CHUNK < ff:
                ahead = up(c + FF_CHUNK)
            u_ref[:, cs] = gp.astype(BF16)
            u_ref[:, ff + c:ff + c + FF_CHUNK] = val.astype(BF16)
            prev = carry[:, cs]
            gate = (wdw_ref[0:1, cs] * _shift_down(gp, prev, 2) + wdw_ref[1:2, cs] * _shift_down(gp, prev, 1)
                    + wdw_ref[2:3, cs] * gp + bdw_ref[:, cs])
            act = gate * _sigmoid(gate) * val
            acc = acc + _dot(act.astype(BF16), wdn_ref[cs, :])
            carry[:, cs] = gp[tm - 8:tm, :]
        y_ref[...] = acc

    return pl.pallas_call(
        body, name="ffn_fwd", grid=(s // tm,),
        in_specs=[pl.BlockSpec((tm, d), lambda i: (i, 0)), _full((1, d)), _full((d, 2 * ff)), _full((8, ff)),
                  _full((1, ff)), _full((ff, d))],
        out_specs=[pl.BlockSpec((tm, d), lambda i: (i, 0)), pl.BlockSpec((tm, d), lambda i: (i, 0)),
                   pl.BlockSpec((tm, 2 * ff), lambda i: (i, 0))],
        out_shape=[jax.ShapeDtypeStruct((s, d), F32), jax.ShapeDtypeStruct((s, d), BF16),
                   jax.ShapeDtypeStruct((s, 2 * ff), BF16)],
        scratch_shapes=[pltpu.VMEM((8, ff), F32)],
        compiler_params=_params(("arbitrary",), 56),
    )(x, g, wup, wdw, bdw, wdn)


def _ffn_bwd(dy, u, x, g, wdw, bdw, wdn, wup, send=()):
    s, d = dy.shape
    ff = wdn.shape[0]
    tm = min(256, s)
    n = s // tm
    hb = tm // 16

    def body(dy_ref, u_ref, uh_ref, x_ref, g_ref, wdw_ref, bdw_ref, wdn_hbm, wup_hbm,
             du_ref, act_ref, dw_ref, db_ref, dx_ref, dg_ref, carry, wdn_ref, wup_ref, wsem):
        i = pl.program_id(0)

        @pl.when(i == 0)
        def _():
            loads = [pltpu.make_async_copy(wdn_hbm, wdn_ref, wsem.at[0]),
                     pltpu.make_async_copy(wup_hbm, wup_ref, wsem.at[1])]
            for cp in loads:
                cp.start()
            carry[...] = jnp.zeros_like(carry)
            dw_ref[...] = jnp.zeros_like(dw_ref)
            db_ref[...] = jnp.zeros_like(db_ref)
            dg_ref[...] = jnp.zeros_like(dg_ref)
            for cp in loads:
                cp.wait()

        dyv = dy_ref[...]
        dyb = dyv.astype(BF16)
        down = lambda c: lax.dot_general(dyb, wdn_ref[c:c + FF_CHUNK, :], NT_DIMS, preferred_element_type=F32)
        ahead = down(0)
        dh = jnp.zeros((tm, d), F32)
        for c in range(0, ff, FF_CHUNK):
            cs = slice(c, c + FF_CHUNK)
            vs = slice(ff + c, ff + c + FF_CHUNK)
            da = ahead
            if c + FF_CHUNK < ff:
                ahead = down(c + FF_CHUNK)
            gp = u_ref[:, cs].astype(F32)
            val = u_ref[:, vs].astype(F32)
            prev = jnp.where(i < n - 1, uh_ref[:, cs].astype(F32)[8:16], 0.0)
            g1 = _shift_down(gp, prev, 1)
            g2 = _shift_down(gp, prev, 2)
            gate = wdw_ref[0:1, cs] * g2 + wdw_ref[1:2, cs] * g1 + wdw_ref[2:3, cs] * gp + bdw_ref[:, cs]
            sg = _sigmoid(gate)
            si = gate * sg
            act_ref[:, cs] = (si * val).astype(BF16)
            dgate = da * val * _silu_grad(gate, sg)
            nxt = carry[:, cs]
            dgp = (wdw_ref[2:3, cs] * dgate + wdw_ref[1:2, cs] * _shift_up(dgate, nxt, 1)
                   + wdw_ref[0:1, cs] * _shift_up(dgate, nxt, 2)).astype(BF16)
            dval = (da * si).astype(BF16)
            du_ref[:, cs] = dgp
            du_ref[:, vs] = dval
            dh = (dh + lax.dot_general(dgp, wup_ref[:, cs], NT_DIMS, preferred_element_type=F32)
                  + lax.dot_general(dval, wup_ref[:, vs], NT_DIMS, preferred_element_type=F32))
            dw_ref[:, cs] += jnp.concatenate(
                [jnp.sum(dgate * g2, axis=0, keepdims=True), jnp.sum(dgate * g1, axis=0, keepdims=True),
                 jnp.sum(dgate * gp, axis=0, keepdims=True), jnp.zeros((5, FF_CHUNK), F32)], axis=0)
            db_ref[:, cs] += jnp.sum(dgate, axis=0, keepdims=True)
            carry[:, cs] = dgate[0:8, :]
        xv = x_ref[...]
        r = lax.rsqrt(jnp.mean(xv * xv, axis=-1, keepdims=True) + EPS)
        xh = xv * r
        dg_ref[...] += jnp.sum(dh * xh, axis=0, keepdims=True)
        dn = dh * g_ref[...]
        dx_ref[...] = dyv + r * (dn - xh * jnp.mean(dn * xh, axis=-1, keepdims=True))

    step_is = lambda i: lambda: pl.program_id(0) == i
    kern, ins, outs, shapes, sems = _carried(body, 9, 6, send, "scatter",
                                             {"start": step_is(0), "finish": step_is(n - 1)})
    rev = lambda cols: pl.BlockSpec((tm, cols), lambda i: (n - 1 - i, 0))
    any_spec = pl.BlockSpec(memory_space=pl.ANY)
    res = pl.pallas_call(
        kern, name="ffn_bwd", grid=(n,),
        in_specs=[rev(d), rev(2 * ff),
                  pl.BlockSpec((16, 2 * ff), lambda i: (jnp.maximum((n - 1 - i) * hb - 1, 0), 0)),
                  rev(d), _full((1, d)), _full((8, ff)), _full((1, ff)), any_spec, any_spec] + ins,
        out_specs=[rev(2 * ff), rev(ff), _full((8, ff)), _full((1, ff)), rev(d), _full((1, d))] + outs,
        out_shape=[jax.ShapeDtypeStruct((s, 2 * ff), BF16), jax.ShapeDtypeStruct((s, ff), BF16),
                   jax.ShapeDtypeStruct((8, ff), F32), jax.ShapeDtypeStruct((1, ff), F32),
                   jax.ShapeDtypeStruct((s, d), F32), jax.ShapeDtypeStruct((1, d), F32)] + shapes,
        scratch_shapes=[pltpu.VMEM((8, ff), F32), pltpu.VMEM((ff, d), BF16), pltpu.VMEM((d, 2 * ff), BF16),
                        pltpu.SemaphoreType.DMA((2,))] + sems,
        compiler_params=_params(("arbitrary",), 56),
    )(dy, u, u, x, g, wdw, bdw, wdn, wup, *send)
    return tuple(res[:6]) + (list(res[6:]),)


def _loss_head(y, target):
    s, d = y.shape
    tm = min(512, s)

    def body(y_ref, t_ref, l_ref, dy_ref):
        @pl.when(pl.program_id(0) == 0)
        def _():
            l_ref[...] = jnp.zeros_like(l_ref)

        err = y_ref[...] - t_ref[...]
        dy_ref[...] = err * (1.0 / d)
        l_ref[...] += 0.5 * jnp.sum(jnp.mean(err * err, axis=-1, keepdims=True), axis=0, keepdims=True)

    return pl.pallas_call(
        body, name="loss_head", grid=(s // tm,),
        in_specs=[pl.BlockSpec((tm, d), lambda i: (i, 0)), pl.BlockSpec((tm, d), lambda i: (i, 0))],
        out_specs=[_full((8, LANE)), pl.BlockSpec((tm, d), lambda i: (i, 0))],
        out_shape=[jax.ShapeDtypeStruct((8, LANE), F32), jax.ShapeDtypeStruct((s, d), F32)],
        compiler_params=_params(("arbitrary",), 32),
    )(y, target)


def _row_tile(rows, limit=512):
    for cand in range(min(limit, rows) // 16 * 16, 0, -16):
        if rows % cand == 0:
            return cand
    return rows


def _reduce_adamw(parts, w, m, v, name, gather=()):
    nl = len(parts)
    _, a, b = parts[0].shape
    ta = _row_tile(a, 256)
    tiles = a // ta

    def body(*refs):
        p_refs = refs[:nl]
        w_ref, m_ref, v_ref, g_ref, d_ref, mo_ref, vo_ref = refs[nl:]
        for layer in range(nl):
            @pl.when(pl.program_id(0) == layer)
            def _(p_ref=p_refs[layer]):
                g = p_ref[0].astype(F32)
                for k in range(1, N_DEV):
                    g = g + p_ref[k].astype(F32)
                g_ref[0] = g

        g = g_ref[0]
        mn = ADAM_B1 * m_ref[...] + (1.0 - ADAM_B1) * g
        vn = ADAM_B2 * v_ref[...] + (1.0 - ADAM_B2) * (g * g)
        mo_ref[...] = mn
        vo_ref[...] = vn
        m_hat = mn / (1.0 - ADAM_B1 ** ADAM_STEP)
        v_hat = vn / (1.0 - ADAM_B2 ** ADAM_STEP)
        d_ref[...] = -ADAM_LR * (m_hat / (jnp.sqrt(v_hat) + ADAM_EPS) + ADAM_WD * w_ref[...])

    blk = pl.BlockSpec((1, ta, b), lambda l, i: (l, i, 0))
    part = lambda layer: pl.BlockSpec((N_DEV, ta, b), lambda l, i: (0, jnp.where(l == layer, i, 0), 0))
    step_is = lambda l, i: lambda: (pl.program_id(0) == l) & (pl.program_id(1) == i)
    when = {"start": step_is(0, 0), "forward": step_is(nl - 1, 0), "finish": step_is(nl - 1, tiles - 1)}
    kern, ins, outs, shapes, sems = _carried(body, nl + 3, 4, gather, "gather", when)
    res = pl.pallas_call(
        kern, name=name, grid=(nl, tiles),
        in_specs=[part(layer) for layer in range(nl)] + [blk, blk, blk] + ins,
        out_specs=[blk, blk, blk, blk] + outs,
        out_shape=[jax.ShapeDtypeStruct((nl, a, b), F32)] * 4 + shapes,
        scratch_shapes=sems,
        compiler_params=_params(("arbitrary", "arbitrary"), 56),
    )(*parts, w, m, v, *gather)
    return list(res[:4]) + ([list(res[4:])] if gather else [])


def _unshard(w8, layer, name):
    _, _, k, n = w8.shape
    tk = _row_tile(k, 256)

    def body(w_ref, o_ref):
        for d in range(N_DEV):
            o_ref[:, d * n:(d + 1) * n] = w_ref[d, 0]

    return pl.pallas_call(
        body, name=name, grid=(k // tk,),
        in_specs=[pl.BlockSpec((N_DEV, 1, tk, n), lambda i: (0, layer, i, 0))],
        out_specs=pl.BlockSpec((tk, N_DEV * n), lambda i: (i, 0)),
        out_shape=jax.ShapeDtypeStruct((k, N_DEV * n), w8.dtype),
        compiler_params=_params(("arbitrary",), 32),
    )(w8)


def _shard_cast(g, name):
    k, n8 = g.shape
    n = n8 // N_DEV
    tk = _row_tile(k, 256)

    def body(g_ref, o_ref):
        for d in range(N_DEV):
            o_ref[d] = g_ref[:, d * n:(d + 1) * n].astype(BF16)

    return pl.pallas_call(
        body, name=name, grid=(k // tk,),
        in_specs=[pl.BlockSpec((tk, n8), lambda i: (i, 0))],
        out_specs=pl.BlockSpec((N_DEV, tk, n), lambda i: (0, i, 0)),
        out_shape=jax.ShapeDtypeStruct((N_DEV, k, n), BF16),
        compiler_params=_params(("arbitrary",), 32),
    )(g)


def _row(v):
    return v.reshape(1, -1)


def _group_ones():
    idx = jnp.arange(SB_WIDTH) // HEAD_DIM
    return (idx[:, None] == idx[None, :]).astype(BF16)


def _pad_rows(w, rows):
    return jnp.concatenate([w, jnp.zeros((rows - w.shape[0], w.shape[1]), w.dtype)], axis=0)


def _local_step(x, target, wt, traffic):
    scale = HEAD_DIM ** -0.5
    bd = _group_ones()
    tril = jnp.tril(jnp.ones((CHUNK, CHUNK), dtype=bool))
    saved = []
    for i in range(DEPTH):
        j = i // 2
        lay = {"x_mix": x}
        if i % 2 == 0:
            proj, h = _rms_matmul(x, _row(wt["mix_norm_g"][i]), wt["sb_w_in"][j], jnp.zeros((1, IN_WIDTH), F32), "in_proj")
            gq = _row(jnp.tile(wt["sb_q_norm_g"][j], SB_WIDTH // HEAD_DIM)) * scale
            gk = _row(jnp.tile(wt["sb_k_norm_g"][j], SB_WIDTH // HEAD_DIM))
            qn, kn, vb = _qk_prep(proj, gq * LOG2E, gk, bd)
            o, lsum, gathered = _sb_attn_fwd(qn, kn, vb, gather=traffic.rest() if i == 0 else ())
            if i == 0:
                traffic.install(wt, gathered)
            wm = jnp.where(tril[None], wt["sg_w_spatial"][j], 0.0)
            wmb = wm.astype(BF16)
            wmt = jnp.swapaxes(wm, 1, 2).astype(BF16)
            bt = jnp.repeat(wt["sg_b_spatial"][j].T, HEAD_DIM, axis=1)
            gz = _row(wt["sg_z_norm_g"][j])
            gg = _sgu_fwd(proj, gz, wmb, bt, bd)
            x, mix = _out_proj(x, o, gg, wt["hyb_w_out"][j], "out_proj")
            lay.update(proj=proj, h=h, gq=gq, gk=gk, qn=qn, kn=kn, vb=vb, lsum=lsum, wmb=wmb, wmt=wmt, bt=bt, gz=gz, mix=mix)
        else:
            p, h = _rms_matmul(x, _row(wt["mix_norm_g"][i]), wt["cv_w_pw1"][j], _row(wt["cv_b_pw1"][j]), "conf_pw1")
            wdw = _pad_rows(wt["cv_w_dw"][j], HALO)
            yc, y2 = _conf_mid_fwd(p, wdw, _row(wt["cv_b_dw"][j]), _row(wt["cv_ln_g"][j]), _row(wt["cv_ln_b"][j]))
            x = _res_matmul(x, y2, wt["cv_w_pw2"][j], _row(wt["cv_b_pw2"][j]), "conf_pw2")
            lay.update(p=p, h=h, wdw=wdw, yc=yc, y2=y2)
        lay["x_ffn"] = x
        fdw = _pad_rows(wt["ffn_w_dw"][i], 8)
        x, hf, u = _ffn_fwd(x, _row(wt["ffn_norm_g"][i]), wt["ffn_w_up"][i], fdw, _row(wt["ffn_b_dw"][i]),
                            wt["ffn_w_down"][i])
        lay.update(hf=hf, u=u, fdw=fdw)
        saved.append(lay)

    lpart, dy = _loss_head(x, target)
    loss = lpart[0, 0]

    gr = {k: [None] * len(v) for k, v in wt.items()}

    def made(name, layer, grad):
        gr[name][layer] = grad
        traffic.ready(name, layer, grad)

    for i in reversed(range(DEPTH)):
        j = i // 2
        lay = saved[i]
        du, act, dfdw, dfb, dx, dgf, got = _ffn_bwd(
            dy, lay["u"], lay["x_ffn"], _row(wt["ffn_norm_g"][i]), lay["fdw"], _row(wt["ffn_b_dw"][i]),
            wt["ffn_w_down"][i], wt["ffn_w_up"][i], send=traffic.take())
        traffic.landed(got)
        made("ffn_w_down", i, _matmul_tn(act, dy, "ffn_dw_down"))
        made("ffn_w_up", i, _matmul_tn(lay["hf"], du, "ffn_dw_up"))
        gr["ffn_w_dw"][i] = dfdw[:FFN_K]
        gr["ffn_b_dw"][i] = dfb[0]
        gr["ffn_norm_g"][i] = dgf[0]
        dy = dx
        if i % 2 == 0:
            dmix = _matmul_nt(dy, wt["hyb_w_out"][j], "out_proj_dx")
            made("hyb_w_out", j, _matmul_tn(lay["mix"], dy, "out_proj_dw"))
            dqn, dkn, dv, got = _sb_attn_bwd(lay["qn"], lay["kn"], lay["vb"], lay["lsum"], dmix, send=traffic.take())
            traffic.landed(got)
            duz, dwm, dbt, dgz = _sgu_bwd(lay["proj"], dmix, lay["gz"], lay["wmb"], lay["wmt"], lay["bt"], bd)
            dproj, dgq, dgk = _qk_bwd(lay["proj"], dqn, dkn, dv, duz, lay["gq"], lay["gk"] * LN2, bd)
            made("sb_w_in", j, _matmul_tn(lay["h"], dproj, "in_proj_dw"))
            gr["sb_q_norm_g"][j] = dgq.reshape(SB_WIDTH // HEAD_DIM, HEAD_DIM).sum(0) * scale
            gr["sb_k_norm_g"][j] = dgk.reshape(SB_WIDTH // HEAD_DIM, HEAD_DIM).sum(0) * LN2
            gr["sg_z_norm_g"][j] = dgz[0]
            gr["sg_w_spatial"][j] = jnp.where(tril[None], dwm, 0.0)
            gr["sg_b_spatial"][j] = dbt.reshape(CHUNK, SG_GROUPS, HEAD_DIM).sum(-1).T
            dy, dgm, got = _nt_rms_bwd(dproj, wt["sb_w_in"][j], lay["x_mix"], _row(wt["mix_norm_g"][i]), dy,
                                       "in_proj_dx", send=traffic.take() if i == 0 else ())
            if i == 0:
                traffic.landed(got)
        else:
            made("cv_w_pw2", j, _matmul_tn(lay["y2"], dy, "conf_pw2_dw"))
            dp, dwdw, dbdw, dlg, dlb, db1, db2, dx, dgm, got = _conf_mid_bwd(
                lay["p"], lay["yc"], dy, lay["x_mix"], _row(wt["mix_norm_g"][i]), lay["wdw"],
                _row(wt["cv_ln_g"][j]), _row(wt["cv_ln_b"][j]), wt["cv_w_pw1"][j], wt["cv_w_pw2"][j],
                send=traffic.take())
            traffic.landed(got)
            dy = dx
            made("cv_w_pw1", j, _matmul_tn(lay["h"], dp, "conf_pw1_dw"))
            gr["cv_w_dw"][j] = dwdw[:CONV_K]
            gr["cv_b_dw"][j] = dbdw[0]
            gr["cv_ln_g"][j] = dlg[0]
            gr["cv_ln_b"][j] = dlb[0]
            gr["cv_b_pw1"][j] = db1[0]
            gr["cv_b_pw2"][j] = db2[0]
        gr["mix_norm_g"][i] = dgm[0]
    matmul_weights = ("sb_w_in", "hyb_w_out", "cv_w_pw1", "cv_w_pw2", "ffn_w_up", "ffn_w_down")
    grads = {k: (v if k in matmul_weights else jnp.stack(v)) for k, v in gr.items()}
    return loss, dy, grads


WEIGHTS = ["mix_norm_g", "sb_w_in", "sb_q_norm_g", "sb_k_norm_g", "sg_z_norm_g", "sg_w_spatial", "sg_b_spatial",
           "hyb_w_out", "cv_w_pw1", "cv_b_pw1", "cv_w_dw", "cv_b_dw", "cv_ln_g", "cv_ln_b", "cv_w_pw2", "cv_b_pw2",
           "ffn_norm_g", "ffn_w_up", "ffn_w_dw", "ffn_b_dw", "ffn_w_down"]
BIG = [("sb_w_in", "col"), ("hyb_w_out", "row"), ("cv_w_pw1", "col"), ("cv_w_pw2", "row"), ("ffn_w_up", "col"),
       ("ffn_w_down", "row")]
SMALL = ["cv_b_pw1", "cv_w_dw", "cv_b_dw", "cv_ln_g", "cv_ln_b", "cv_b_pw2", "ffn_w_dw"]
REPLICATED = ["mix_norm_g", "sb_q_norm_g", "sb_k_norm_g", "sg_z_norm_g", "sg_w_spatial", "sg_b_spatial", "ffn_norm_g",
              "ffn_b_dw"]


def _last_dim_blocks(full):
    t = jnp.moveaxis(full.reshape(full.shape[:-1] + (N_DEV, full.shape[-1] // N_DEV)), -2, 0)
    return t.reshape(N_DEV, -1)


def _from_last_dim_blocks(blocks, shard_shape):
    t = jnp.moveaxis(blocks.reshape((N_DEV,) + tuple(shard_shape)), 0, -2)
    return t.reshape(tuple(shard_shape[:-1]) + (N_DEV * shard_shape[-1],))


def _pack(arrays):
    lead = arrays[0].shape[:-1]
    flat = jnp.concatenate([a.astype(F32) for a in arrays], axis=-1)
    rows = -(-flat.shape[-1] // (16 * LANE)) * 16
    pad = rows * LANE - flat.shape[-1]
    if pad:
        flat = jnp.concatenate([flat, jnp.zeros(lead + (pad,), F32)], axis=-1)
    return flat.reshape(lead + (rows, LANE))


def _unpack(packed, shapes):
    flat = packed.reshape(-1)
    out, off = [], 0
    for shp in shapes:
        size = 1
        for dim in shp:
            size *= dim
        out.append(flat[off:off + size].reshape(shp))
        off += size
    return out


class _ShardTraffic:
    def __init__(self, w):
        self.w = w
        self.queue, self.flying, self.received = [], [], {}

    def rest(self):
        small = _pack([self.w[n].reshape(-1) for n in SMALL])
        return [self.w["sb_w_in"][1:].astype(BF16)] + [self.w[n].astype(BF16) for n, _ in BIG[1:]] + [small]

    def install(self, wt, gathered):
        wt["sb_w_in"].append(_unshard(gathered[0], 0, "unshard_sb_w_in"))
        for (n, kind), w8 in zip(BIG[1:], gathered[1:-1]):
            if kind == "col":
                wt[n] = [_unshard(w8, l, "unshard_" + n) for l in range(w8.shape[1])]
            else:
                wt[n] = [w8[:, l].reshape((N_DEV * w8.shape[2],) + w8.shape[3:]) for l in range(w8.shape[1])]
        sizes = [(self.w[n].size,) for n in SMALL]
        for n, parts in zip(SMALL, zip(*[_unpack(gathered[-1][d], sizes) for d in range(N_DEV)])):
            wt[n] = _from_last_dim_blocks(jnp.stack(parts), self.w[n].shape)

    def ready(self, name, layer, grad):
        if dict(BIG)[name] == "col":
            blocks = _shard_cast(grad, "shard_" + name)
        else:
            blocks = grad.reshape((N_DEV, grad.shape[0] // N_DEV) + grad.shape[1:])
        self.queue.append(((name, layer), blocks))

    def take(self):
        self.flying = [key for key, _ in self.queue]
        arrays = [blocks for _, blocks in self.queue]
        self.queue = []
        return arrays

    def landed(self, received):
        for key, blocks in zip(self.flying, received):
            self.received[key] = blocks
        self.flying = []


def kernel(x, mix_norm_g, sb_w_in, sb_q_norm_g, sb_k_norm_g, sg_z_norm_g, sg_w_spatial, sg_b_spatial, hyb_w_out, cv_w_pw1, cv_b_pw1, cv_w_dw, cv_b_dw, cv_ln_g, cv_ln_b, cv_w_pw2, cv_b_pw2, ffn_norm_g, ffn_w_up, ffn_w_dw, ffn_b_dw, ffn_w_down, loss_target, m_mix_norm_g, m_sb_w_in, m_sb_q_norm_g, m_sb_k_norm_g, m_sg_z_norm_g, m_sg_w_spatial, m_sg_b_spatial, m_hyb_w_out, m_cv_w_pw1, m_cv_b_pw1, m_cv_w_dw, m_cv_b_dw, m_cv_ln_g, m_cv_ln_b, m_cv_w_pw2, m_cv_b_pw2, m_ffn_norm_g, m_ffn_w_up, m_ffn_w_dw, m_ffn_b_dw, m_ffn_w_down, v_mix_norm_g, v_sb_w_in, v_sb_q_norm_g, v_sb_k_norm_g, v_sg_z_norm_g, v_sg_w_spatial, v_sg_b_spatial, v_hyb_w_out, v_cv_w_pw1, v_cv_b_pw1, v_cv_w_dw, v_cv_b_dw, v_cv_ln_g, v_cv_ln_b, v_cv_w_pw2, v_cv_b_pw2, v_ffn_norm_g, v_ffn_w_up, v_ffn_w_dw, v_ffn_b_dw, v_ffn_w_down):
    w = dict(zip(WEIGHTS, (mix_norm_g, sb_w_in, sb_q_norm_g, sb_k_norm_g, sg_z_norm_g, sg_w_spatial, sg_b_spatial,
                           hyb_w_out, cv_w_pw1, cv_b_pw1, cv_w_dw, cv_b_dw, cv_ln_g, cv_ln_b, cv_w_pw2, cv_b_pw2,
                           ffn_norm_g, ffn_w_up, ffn_w_dw, ffn_b_dw, ffn_w_down)))
    m = dict(zip(WEIGHTS, (m_mix_norm_g, m_sb_w_in, m_sb_q_norm_g, m_sb_k_norm_g, m_sg_z_norm_g, m_sg_w_spatial,
                           m_sg_b_spatial, m_hyb_w_out, m_cv_w_pw1, m_cv_b_pw1, m_cv_w_dw, m_cv_b_dw, m_cv_ln_g,
                           m_cv_ln_b, m_cv_w_pw2, m_cv_b_pw2, m_ffn_norm_g, m_ffn_w_up, m_ffn_w_dw, m_ffn_b_dw,
                           m_ffn_w_down)))
    v = dict(zip(WEIGHTS, (v_mix_norm_g, v_sb_w_in, v_sb_q_norm_g, v_sb_k_norm_g, v_sg_z_norm_g, v_sg_w_spatial,
                           v_sg_b_spatial, v_hyb_w_out, v_cv_w_pw1, v_cv_b_pw1, v_cv_w_dw, v_cv_b_dw, v_cv_ln_g,
                           v_cv_ln_b, v_cv_w_pw2, v_cv_b_pw2, v_ffn_norm_g, v_ffn_w_up, v_ffn_w_dw, v_ffn_b_dw,
                           v_ffn_w_down)))
    big_names = [n for n, _ in BIG]
    flat = lambda t, names: [t[n].reshape(-1) for n in names]

    first_in = _all_gather([w["sb_w_in"][0:1].astype(BF16)], "gather_first")[0]
    wt = {n: w[n] for n in REPLICATED}
    wt["sb_w_in"] = [_unshard(first_in, 0, "unshard_sb_w_in")]

    traffic = _ShardTraffic(w)
    loss, gx, grads = _local_step(x[0], loss_target[0], wt, traffic)
    assert not traffic.queue and not traffic.flying
    recv_big = [[traffic.received[n, l] for l in range(w[n].shape[0])] for n in big_names]

    grep = _pack(flat(grads, REPLICATED))
    gsmall = _pack([_last_dim_blocks(grads[n]) for n in SMALL])
    rep_rows, small_rows = grep.shape[0], gsmall.shape[1]
    vec = jnp.concatenate([grep, gsmall.reshape(N_DEV * small_rows, LANE)], axis=0).astype(BF16)
    out = {}
    kinds = ("grad", "delta", "new_m", "new_v")
    for n, parts in zip(big_names, recv_big):
        res = _reduce_adamw(parts, w[n], m[n], v[n], "adamw_" + n, gather=[vec] if n == "ffn_w_up" else ())
        if n == "ffn_w_up":
            vec_all = res[4][0]
        for kind, arr in zip(kinds, res[:4]):
            out[kind, n] = arr
    me = 4 * lax.axis_index("x") + 2 * lax.axis_index("y") + lax.axis_index("c")
    recv_rep = vec_all[:, :rep_rows]
    recv_small = lax.dynamic_slice(vec_all, (0, rep_rows + small_rows * me, 0), (N_DEV, small_rows, LANE))
    for names, recv, tag in ((SMALL, recv_small, "adamw_small"), (REPLICATED, recv_rep, "adamw_replicated")):
        res = _reduce_adamw([recv], _pack(flat(w, names))[None], _pack(flat(m, names))[None],
                            _pack(flat(v, names))[None], tag)
        shapes = [w[n].shape for n in names]
        for kind, packed in zip(kinds, res):
            for n, arr in zip(names, _unpack(packed[0], shapes)):
                out[kind, n] = arr

    loss = lax.psum(loss, ("x", "y", "c"))
    return (loss, gx[None], *[out[kind, n] for kind in kinds for n in WEIGHTS])
```
